```python
import jax, jax.numpy as jnp
from jax import lax
import numpy as np

D_MODEL = 1024
BATCH = 8
SEQ = 2048
DEPTH = 1

RW_HEAD = 64
RW_HEADS = 8
RW_WIDTH = RW_HEADS * RW_HEAD
DECAY_LORA = 64
AAA_LORA = 64
GATE_LORA = 128
RW_GN_EPS = 64e-5
RW_COLS = 3 * RW_WIDTH + DECAY_LORA + AAA_LORA + GATE_LORA
RW_SPLITS = [RW_WIDTH, 2 * RW_WIDTH, 3 * RW_WIDTH, 3 * RW_WIDTH + DECAY_LORA, 3 * RW_WIDTH + DECAY_LORA + AAA_LORA]

GLA_HEADS = 4
GLA_DK = 64
GLA_DV = 128
GLA_KW = GLA_HEADS * GLA_DK
GLA_VW = GLA_HEADS * GLA_DV
GLA_GATE_LORA = 16
GLA_TAU = 16.0
GLA_CHUNK = 64
GLA_NORM_EPS = 1e-5
GLA_COLS = 2 * GLA_KW + 2 * GLA_VW + GLA_GATE_LORA
GLA_SPLITS = [GLA_KW, 2 * GLA_KW, 2 * GLA_KW + GLA_VW, 2 * GLA_KW + 2 * GLA_VW]

N_IN = RW_COLS + GLA_COLS + 2 * D_MODEL

D_FF = ((-(-8 * D_MODEL // 3)) + 255) // 256 * 256

ALPHA = (2.0 * DEPTH) ** 0.25
BETA = (8.0 * DEPTH) ** -0.25
LN_EPS = 1e-5

kernel_name = "hybrid_rwkv7_gla_deepnorm_adaln_block"


def _layer_norm(x, eps):
    x32 = x.astype(jnp.float32)
    mu = jnp.mean(x32, -1, keepdims=True)
    var = jnp.mean(jnp.square(x32 - mu), -1, keepdims=True)
    return (x32 - mu) * lax.rsqrt(var + eps)


def _token_shift(p, mu):
    p_prev = jnp.pad(p, ((0, 0), (1, 0), (0, 0)))[:, :-1, :]
    return p + mu * (p_prev - p)


def _rwkv7_branch(p, mu, w0, w2, a0, a2, g2, k_k, k_a, r_k, gn_g, gn_b):
    B, T, _ = p.shape
    H, N = RW_HEADS, RW_HEAD
    f32 = jnp.float32
    p = _token_shift(p, mu)
    r, k, v, wd, ad, gd = jnp.split(p, RW_SPLITS, axis=-1)
    w = -jax.nn.softplus(-(w0 + jnp.tanh(wd) @ w2).astype(f32)) - 0.5
    decay = jnp.exp(-jnp.exp(w))
    a = jax.nn.sigmoid((a0 + ad @ a2).astype(f32))
    g = jax.nn.sigmoid(gd) @ g2
    kk = (k * k_k).astype(f32).reshape(B, T, H, N)
    kk = kk / jnp.maximum(jnp.sqrt(jnp.sum(kk * kk, -1, keepdims=True)), 1e-12)
    k = k.astype(f32) * (1.0 + (a - 1.0) * k_a.astype(f32))
    heads = lambda t: t.astype(f32).reshape(B, T, H, N)
    rh, kh, vh, wh, ah = heads(r), heads(k), heads(v), heads(decay), heads(a)
    bh = kk * ah

    def step(S, inp):
        r_t, w_t, k_t, v_t, kk_t, b_t = inp
        sa = jnp.einsum('bhvk,bhk->bhv', S, -kk_t)
        S = S * w_t[:, :, None, :] + sa[..., None] * b_t[:, :, None, :] + v_t[..., None] * k_t[:, :, None, :]
        return S, jnp.einsum('bhvk,bhk->bhv', S, r_t)

    xs = tuple(jnp.swapaxes(t, 0, 1) for t in (rh, wh, kh, vh, kk, bh))
    _, y = lax.scan(step, jnp.zeros((B, H, N, N), f32), xs)
    y = jnp.swapaxes(y, 0, 1)
    y = _layer_norm(y, RW_GN_EPS).reshape(B, T, RW_WIDTH) * gn_g.astype(f32) + gn_b.astype(f32)
    bonus = jnp.sum(rh * kh * r_k.astype(f32), -1, keepdims=True) * vh
    out = (y + bonus.reshape(B, T, RW_WIDTH)) * g.astype(f32)
    return out.astype(p.dtype)


def _gla_branch(p, a2, a_b, norm_g):
    B, T, _ = p.shape
    H, DK, DV, C = GLA_HEADS, GLA_DK, GLA_DV, GLA_CHUNK
    NC = T // C
    f32 = jnp.float32
    q, k, v, gg, ad = jnp.split(p, GLA_SPLITS, axis=-1)
    log_a = jax.nn.log_sigmoid((ad @ a2 + a_b).astype(f32)) / GLA_TAU

    def chunks(t, d):
        return t.astype(f32).reshape(B, NC, C, H, d).transpose(0, 3, 1, 2, 4)

    qc = chunks(q, DK) * (DK ** -0.5)
    kc, vc, lc = chunks(k, DK), chunks(v, DV), chunks(log_a, DK)
    b = jnp.cumsum(lc, axis=3)
    q_s = qc * jnp.exp(b)
    k_s = kc * jnp.exp(-b)
    causal = jnp.tril(jnp.ones((C, C), dtype=bool))
    att = jnp.where(causal, jnp.einsum('bhncd,bhnsd->bhncs', q_s, k_s), 0.0)
    o_intra = jnp.einsum('bhncs,bhnsv->bhncv', att, vc)
    b_last = b[:, :, :, -1:, :]
    chunk_kv = jnp.einsum('bhncd,bhncv->bhndv', kc * jnp.exp(b_last - b), vc)
    chunk_decay = jnp.exp(b_last[:, :, :, 0, :])

    def step(S, inp):
        dec, kv = inp
        return dec[..., None] * S + kv, S

    _, S_prev = lax.scan(step, jnp.zeros((B, H, DK, DV), f32),
                         (jnp.moveaxis(chunk_decay, 2, 0), jnp.moveaxis(chunk_kv, 2, 0)))
    S_prev = jnp.moveaxis(S_prev, 0, 2)
    o = o_intra + jnp.einsum('bhncd,bhndv->bhncv', q_s, S_prev)
    o = o.transpose(0, 2, 3, 1, 4).reshape(B, T, H, DV)
    o = o * lax.rsqrt(jnp.mean(o * o, -1, keepdims=True) + GLA_NORM_EPS) * norm_g.astype(f32)
    o = o.reshape(B, T, GLA_VW) * jax.nn.silu(gg.astype(f32))
    return o.astype(p.dtype)


def setup_inputs(seed: int = 0) -> dict:
    key = jax.random.key(seed)
    ks = jax.random.split(key, 32)
    L, D = DEPTH, D_MODEL
    nrm = lambda k, shape, s: jax.random.normal(k, shape, jnp.float32) * s
    rw_w0 = jnp.broadcast_to(jnp.linspace(-6.0, -1.0, RW_WIDTH, dtype=jnp.float32), (L, RW_WIDTH)) + nrm(ks[6], (L, RW_WIDTH), 0.1)
    return {
        "x": nrm(ks[0], (BATCH, SEQ, D), 1.0),
        "c": nrm(ks[1], (BATCH, D), 1.0),
        "w_ada": nrm(ks[2], (L, D, 6 * D), 0.5 * D ** -0.5),
        "b_ada": nrm(ks[3], (L, 6 * D), 0.02),
        "w_in": nrm(ks[4], (L, D, N_IN), D ** -0.5),
        "mu_rw": jax.random.uniform(ks[5], (L, RW_COLS), jnp.float32),
        "rw_w0": rw_w0,
        "rw_w2": nrm(ks[7], (L, DECAY_LORA, RW_WIDTH), 0.1 * DECAY_LORA ** -0.5),
        "rw_a0": nrm(ks[8], (L, RW_WIDTH), 0.1),
        "rw_a2": nrm(ks[9], (L, AAA_LORA, RW_WIDTH), 0.5 * AAA_LORA ** -0.5),
        "rw_g2": nrm(ks[10], (L, GATE_LORA, RW_WIDTH), GATE_LORA ** -0.5),
        "rw_k_k": 0.85 + nrm(ks[11], (L, RW_WIDTH), 0.02),
        "rw_k_a": 1.0 + nrm(ks[12], (L, RW_WIDTH), 0.02),
        "rw_r_k": nrm(ks[13], (L, RW_HEADS, RW_HEAD), 0.1),
        "rw_gn_g": 1.0 + nrm(ks[14], (L, RW_WIDTH), 0.05),
        "rw_gn_b": nrm(ks[15], (L, RW_WIDTH), 0.02),
        "gla_a2": nrm(ks[16], (L, GLA_GATE_LORA, GLA_KW), GLA_GATE_LORA ** -0.5),
        "gla_a_b": nrm(ks[17], (L, GLA_KW), 0.1),
        "gla_norm_g": 1.0 + nrm(ks[18], (L, GLA_DV), 0.05),
        "w_rw_branch": nrm(ks[19], (L, RW_WIDTH, D), BETA * RW_WIDTH ** -0.5),
        "w_gla_branch": nrm(ks[20], (L, GLA_VW, D), BETA * GLA_VW ** -0.5),
        "w_mix_out": nrm(ks[21], (L, D, D), BETA * D ** -0.5),
        "ln1_g": 1.0 + nrm(ks[22], (L, D), 0.05),
        "ln1_b": nrm(ks[23], (L, D), 0.02),
        "w_ffn_in": nrm(ks[24], (L, D, 2 * D_FF), D ** -0.5),
        "w_ffn_out": nrm(ks[25], (L, D_FF, D), BETA * D_FF ** -0.5),
        "ln2_g": 1.0 + nrm(ks[26], (L, D), 0.05),
        "ln2_b": nrm(ks[27], (L, D), 0.02),
    }


def reference(x, c, w_ada, b_ada, w_in, mu_rw, rw_w0, rw_w2, rw_a0, rw_a2, rw_g2, rw_k_k, rw_k_a,
              rw_r_k, rw_gn_g, rw_gn_b, gla_a2, gla_a_b, gla_norm_g, w_rw_branch, w_gla_branch,
              w_mix_out, ln1_g, ln1_b, w_ffn_in, w_ffn_out, ln2_g, ln2_b):
    for l in range(DEPTH):
        mod = (jax.nn.silu(c) @ w_ada[l] + b_ada[l])[:, None, :]
        shift1, scale1, gate1, shift2, scale2, gate2 = jnp.split(mod, 6, axis=-1)

        u = x * (1.0 + scale1) + shift1
        proj = u @ w_in[l]
        p_rw, p_gla, p_gate = jnp.split(proj, [RW_COLS, RW_COLS + GLA_COLS], axis=-1)
        gate_rw, gate_gla = jnp.split(p_gate, 2, axis=-1)
        o_rw = _rwkv7_branch(p_rw, mu_rw[l], rw_w0[l], rw_w2[l], rw_a0[l], rw_a2[l], rw_g2[l],
                             rw_k_k[l], rw_k_a[l], rw_r_k[l], rw_gn_g[l], rw_gn_b[l])
        o_gla = _gla_branch(p_gla, gla_a2[l], gla_a_b[l], gla_norm_g[l])
        merged = (jax.nn.sigmoid(gate_rw) * (o_rw @ w_rw_branch[l])
                  + jax.nn.sigmoid(gate_gla) * (o_gla @ w_gla_branch[l]))
        mix = merged @ w_mix_out[l]
        x = (_layer_norm(ALPHA * x + gate1 * mix, LN_EPS) * ln1_g[l] + ln1_b[l]).astype(x.dtype)

        u2 = x * (1.0 + scale2) + shift2
        h_gate, h_up = jnp.split(u2 @ w_ffn_in[l], 2, axis=-1)
        ffn = (jax.nn.silu(h_gate) * h_up) @ w_ffn_out[l]
        x = (_layer_norm(ALPHA * x + gate2 * ffn, LN_EPS) * ln2_g[l] + ln2_b[l]).astype(x.dtype)
    return x
```

```python
import functools

import jax
import jax.numpy as jnp
from jax import lax
from jax.experimental import pallas as pl
from jax.experimental.pallas import tpu as pltpu

F32 = jnp.float32
BF16 = jnp.bfloat16

RW_HEAD = 64
RW_HEADS = 8
RW_WIDTH = RW_HEADS * RW_HEAD
DECAY_LORA = 64
AAA_LORA = 64
GATE_LORA = 128
RW_GN_EPS = 64e-5
RW_COLS = 3 * RW_WIDTH + DECAY_LORA + AAA_LORA + GATE_LORA
GLA_HEADS = 4
GLA_DK = 64
GLA_DV = 128
GLA_KW = GLA_HEADS * GLA_DK
GLA_VW = GLA_HEADS * GLA_DV
GLA_GATE_LORA = 16
GLA_TAU = 16.0
GLA_NORM_EPS = 1e-5
GLA_MAIN = 2 * GLA_KW + 2 * GLA_VW
GLA_COLS = GLA_MAIN + GLA_GATE_LORA
LN_EPS = 1e-5
DEPTH = 1
ALPHA = (2.0 * DEPTH) ** 0.25

LANES = 128
CHUNK = 64
VMEM_LIMIT = 56 * 1024 * 1024

COL_RW = 0
COL_GLA = RW_COLS
COL_GATE = COL_GLA + GLA_MAIN


def _sigmoid(x):
    return 1.0 / (1.0 + jnp.exp(-x))


def _softplus(x):
    return jnp.maximum(x, 0.0) + jnp.log1p(jnp.exp(-jnp.abs(x)))


def _dot(a, b):
    return jnp.dot(a.astype(BF16), b.astype(BF16), preferred_element_type=F32)


def _dot_nt(a, b):
    return lax.dot_general(a.astype(BF16), b.astype(BF16), (((1,), (1,)), ((), ())),
                           preferred_element_type=F32)


def _split_hi_lo(x):
    hi = x.astype(BF16)
    lo = (x - hi.astype(F32)).astype(BF16)
    return hi, lo


def _dot_exact_rhs(x, w01):
    hi, lo = _split_hi_lo(x)
    return (jnp.dot(hi, w01, preferred_element_type=F32)
            + jnp.dot(lo, w01, preferred_element_type=F32))


def _dot_exact_lhs(w01, x):
    hi, lo = _split_hi_lo(x)
    return (jnp.dot(w01, hi, preferred_element_type=F32)
            + jnp.dot(w01, lo, preferred_element_type=F32))


def _layer_norm(x, eps):
    mu = jnp.mean(x, axis=-1, keepdims=True)
    xc = x - mu
    var = jnp.mean(xc * xc, axis=-1, keepdims=True)
    return xc * lax.rsqrt(var + eps)


def _mod_kernel(c_ref, w_ref, b_ref, o_ref):
    c = c_ref[...]
    sc = c * _sigmoid(c)
    o_ref[...] = _dot(sc, w_ref[...]) + b_ref[...]


def _mod_call(c, w_ada, b_ada):
    bsz, d = c.shape
    n = w_ada.shape[1]
    tn = 1536
    return pl.pallas_call(
        _mod_kernel,
        grid=(n // tn,),
        in_specs=[pl.BlockSpec((bsz, d), lambda j: (0, 0)),
                  pl.BlockSpec((d, tn), lambda j: (0, j)),
                  pl.BlockSpec((1, tn), lambda j: (0, j))],
        out_specs=pl.BlockSpec((bsz, tn), lambda j: (0, j)),
        out_shape=jax.ShapeDtypeStruct((bsz, n), F32),
        compiler_params=pltpu.CompilerParams(dimension_semantics=("arbitrary",),
                                             vmem_limit_bytes=VMEM_LIMIT),
        name="mod",
    )(c, w_ada, b_ada.reshape(1, n))


def _inproj_kernel(x_ref, shift_ref, scale_ref, wp_ref, mu_ref, w0_ref, w2_ref, a0_ref, a2_ref,
                   g2_ref, kk_ref, ka_ref, rk_ref, ga2_ref, gab_ref, bd_ref,
                   r_out, k_out, v_out, lw_out, nkk_out, b_out, g_out, bonus_out,
                   gq_out, gk_out, gv_out, gsg_out, gla_out, sgr_out, sgg_out,
                   carry_ref, *, tm, d_model):
    t = pl.program_id(1)

    @pl.when(t == 0)
    def _():
        carry_ref[...] = jnp.zeros_like(carry_ref)

    u = x_ref[...] * (1.0 + scale_ref[...]) + shift_ref[...]
    proj = _dot(u, wp_ref[...])

    p = proj[:, COL_RW:COL_RW + RW_COLS]
    prev = pltpu.roll(p, 1, 0)
    row = lax.broadcasted_iota(jnp.int32, p.shape, 0)
    prev = jnp.where(row == 0, carry_ref[...], prev)
    carry_ref[...] = p[tm - 1:tm, :]
    ps = p + mu_ref[...] * (prev - p)

    r = ps[:, 0:RW_WIDTH]
    k = ps[:, RW_WIDTH:2 * RW_WIDTH]
    v = ps[:, 2 * RW_WIDTH:3 * RW_WIDTH]
    z = ps[:, 3 * RW_WIDTH:3 * RW_WIDTH + LANES]
    gd = ps[:, 3 * RW_WIDTH + LANES:RW_COLS]

    wlin = w0_ref[...] + _dot(jnp.tanh(z), w2_ref[...])
    w = -_softplus(-wlin) - 0.5
    lw_out[...] = -jnp.exp(w)
    a = _sigmoid(a0_ref[...] + _dot(z, a2_ref[...]))
    g_out[...] = _dot(_sigmoid(gd), g2_ref[...])
    kkv = k * kk_ref[...]
    kp = k * (1.0 + (a - 1.0) * ka_ref[...])
    rkr = r * kp * rk_ref[...]
    bd = bd_ref[...]
    for j in range(RW_WIDTH // LANES):
        sl = slice(j * LANES, (j + 1) * LANES)
        kkj = kkv[:, sl]
        ssq = _dot_exact_rhs(kkj * kkj, bd)
        kkn = kkj / jnp.maximum(jnp.sqrt(ssq), 1e-12)
        nkk_out[:, sl] = -kkn
        b_out[:, sl] = kkn * a[:, sl]
        bonus_out[:, sl] = _dot_exact_rhs(rkr[:, sl], bd) * v[:, sl]
    r_out[...] = r
    k_out[...] = kp
    v_out[...] = v.astype(BF16)

    pg = proj[:, COL_GLA:COL_GLA + GLA_MAIN]
    gq_out[...] = pg[:, 0:GLA_KW] * (GLA_DK ** -0.5)
    gk_out[...] = pg[:, GLA_KW:2 * GLA_KW]
    gv_out[...] = pg[:, 2 * GLA_KW:2 * GLA_KW + GLA_VW].astype(BF16)
    gg = pg[:, 2 * GLA_KW + GLA_VW:GLA_MAIN]
    gsg_out[...] = gg * _sigmoid(gg)
    adg = proj[:, COL_GATE + 2 * d_model:COL_GATE + 2 * d_model + LANES]
    la = _dot(adg, ga2_ref[...]) + gab_ref[...]
    gla_out[...] = -_softplus(-la) / GLA_TAU

    sg = _sigmoid(proj[:, COL_GATE:COL_GATE + 2 * d_model])
    sgr_out[...] = sg[:, :d_model]
    sgg_out[...] = sg[:, d_model:]


def _inproj_call(x2, mod3, wp, small, bsz, seq, d_model):
    tm = 256
    nt = seq // tm
    n_tok = bsz * seq
    np_cols = wp.shape[1]
    (mu, w0, w2p, a0, a2p, g2, k_k, k_a, r_k, ga2p, gab, bd) = small

    def tok(n):
        return pl.BlockSpec((tm, n), lambda b, t: (b * nt + t, 0))

    def full(a):
        return pl.BlockSpec(a.shape, lambda b, t: (0,) * a.ndim)

    def modspec(idx):
        return pl.BlockSpec((None, 1, d_model), lambda b, t: (b, 0, idx))

    outs = [
        (RW_WIDTH, F32), (RW_WIDTH, F32), (RW_WIDTH, BF16), (RW_WIDTH, F32), (RW_WIDTH, F32),
        (RW_WIDTH, F32), (RW_WIDTH, F32), (RW_WIDTH, F32),
        (GLA_KW, F32), (GLA_KW, F32), (GLA_VW, BF16), (GLA_VW, F32), (GLA_KW, F32),
        (d_model, F32), (d_model, F32),
    ]
    return pl.pallas_call(
        functools.partial(_inproj_kernel, tm=tm, d_model=d_model),
        grid=(bsz, nt),
        in_specs=[tok(d_model), modspec(0), modspec(1), full(wp), full(mu), full(w0), full(w2p),
                  full(a0), full(a2p), full(g2), full(k_k), full(k_a), full(r_k), full(ga2p),
                  full(gab), full(bd)],
        out_specs=[tok(n) for n, _ in outs],
        out_shape=[jax.ShapeDtypeStruct((n_tok, n), dt) for n, dt in outs],
        scratch_shapes=[pltpu.VMEM((1, RW_COLS), F32)],
        compiler_params=pltpu.CompilerParams(dimension_semantics=("arbitrary", "arbitrary"),
                                             vmem_limit_bytes=VMEM_LIMIT),
        name="inproj",
    )(x2, mod3, mod3, wp, mu, w0, w2p, a0, a2p, g2, k_k, k_a, r_k, ga2p, gab, bd)


def _rwkv_kernel(r_ref, k_ref, v_ref, lw_ref, nkk_ref, b_ref, g_ref, bonus_ref, gng_ref, gnb_ref,
                 bd_ref, o_ref, h_ref, *, tb):
    c = CHUNK
    t = pl.program_id(2)

    @pl.when(t == 0)
    def _():
        h_ref[...] = jnp.zeros_like(h_ref)

    lane = lax.broadcasted_iota(jnp.int32, (1, LANES), 1)
    masks = ((lane < RW_HEAD).astype(F32), (lane >= RW_HEAD).astype(F32))
    ri = lax.broadcasted_iota(jnp.int32, (c, c), 0)
    ci_ = lax.broadcasted_iota(jnp.int32, (c, c), 1)
    strict = ci_ < ri
    incl = ci_ <= ri
    ltri = incl.astype(BF16)
    eye_c = (ci_ == ri).astype(F32)
    r2 = lax.broadcasted_iota(jnp.int32, (LANES, LANES), 0)
    c2 = lax.broadcasted_iota(jnp.int32, (LANES, LANES), 1)
    blockdiag = (r2 // RW_HEAD) == (c2 // RW_HEAD)
    eye_l = r2 == c2
    bd = bd_ref[...]
    gng = gng_ref[...]
    gnb = gnb_ref[...]

    def chunk(i, carry):
        rows = pl.ds(pl.multiple_of(i * c, c), c)
        hstate = h_ref[...]
        r = r_ref[rows, :]
        k = k_ref[rows, :]
        v = v_ref[rows, :]
        lw = lw_ref[rows, :]
        nkk = nkk_ref[rows, :]
        b = b_ref[rows, :]

        gcum = _dot_exact_lhs(ltri, lw)
        gend = gcum[c - 1:c, :]
        e_neg = jnp.exp(-gcum)
        rt = r * jnp.exp(gcum)
        at = nkk * jnp.exp(gcum - lw)
        kt = k * e_neg
        bt = b * e_neg
        dk = jnp.exp(gend - gcum)
        bh = b * dk
        kh = k * dk

        lhs = jnp.concatenate([at * masks[0], at * masks[1], rt * masks[0], rt * masks[1]], axis=0)
        xb = _dot_nt(lhs, bt)
        xk = _dot_nt(lhs, kt)

        ap = jnp.zeros((c, LANES), F32)
        rp = jnp.zeros((c, LANES), F32)
        uloc = jnp.zeros((c, LANES), F32)
        yloc = jnp.zeros((c, LANES), F32)
        for h in range(2):
            a_ab = jnp.where(strict, xb[h * c:(h + 1) * c], 0.0)
            a_ak = jnp.where(strict, xk[h * c:(h + 1) * c], 0.0)
            a_rb = jnp.where(incl, xb[(2 + h) * c:(3 + h) * c], 0.0)
            a_rk = jnp.where(incl, xk[(2 + h) * c:(3 + h) * c], 0.0)
            tm_ = eye_c + a_ab
            pw = a_ab
            for _ in range(5):
                pw = _dot(pw, pw)
                tm_ = tm_ + _dot(tm_, pw)
            ap_h = _dot(tm_, at * masks[h])
            ul_h = _dot(tm_, _dot(a_ak, v))
            ap = ap + ap_h
            rp = rp + rt * masks[h] + _dot(a_rb, ap_h)
            uloc = uloc + masks[h] * ul_h
            yloc = yloc + masks[h] * (_dot(a_rb, ul_h) + _dot(a_rk, v))

        bh_t = bh.T
        m = jnp.where(eye_l, jnp.exp(gend), 0.0) + jnp.where(blockdiag, _dot(bh_t, ap), 0.0)
        hloc = jnp.where(blockdiag, _dot(bh_t, uloc) + _dot(kh.T, v), 0.0)

        y = _dot(rp, hstate) + yloc
        h_ref[...] = _dot(m, hstate) + hloc

        mean = _dot_exact_rhs(y, bd) * (1.0 / RW_HEAD)
        yc = y - mean
        var = _dot_exact_rhs(yc * yc, bd) * (1.0 / RW_HEAD)
        yn = yc * lax.rsqrt(var + RW_GN_EPS)
        out = (yn * gng + gnb + bonus_ref[rows, :]) * g_ref[rows, :]
        o_ref[rows, :] = out.astype(o_ref.dtype)
        return carry

    lax.fori_loop(0, tb // c, chunk, 0)


def _rwkv_call(r, k, v, lw, nkk, b, g, bonus, gng, gnb, bd, bsz, seq):
    tb = 512
    nt = seq // tb
    n_tok = bsz * seq
    npair = RW_WIDTH // LANES

    def tok():
        return pl.BlockSpec((tb, LANES), lambda bb, hp, t: (bb * nt + t, hp))

    def par():
        return pl.BlockSpec((1, LANES), lambda bb, hp, t: (0, hp))

    return pl.pallas_call(
        functools.partial(_rwkv_kernel, tb=tb),
        grid=(bsz, npair, nt),
        in_specs=[tok() for _ in range(8)] + [par(), par(),
                                               pl.BlockSpec((LANES, LANES), lambda bb, hp, t: (0, 0))],
        out_specs=tok(),
        out_shape=jax.ShapeDtypeStruct((n_tok, RW_WIDTH), BF16),
        scratch_shapes=[pltpu.VMEM((LANES, LANES), F32)],
        compiler_params=pltpu.CompilerParams(
            dimension_semantics=("arbitrary", "arbitrary", "arbitrary"),
            vmem_limit_bytes=VMEM_LIMIT),
        name="rwkv",
    )(r, k, v, lw, nkk, b, g, bonus, gng, gnb, bd)


def _gla_kernel(q_ref, k_ref, v_ref, sg_ref, la_ref, ng_ref, o_ref, st_ref, *, tb):
    c = CHUNK
    h = pl.program_id(1)
    t = pl.program_id(2)

    @pl.when(t == 0)
    def _():
        st_ref[...] = jnp.zeros_like(st_ref)

    lane = lax.broadcasted_iota(jnp.int32, (1, LANES), 1)
    hm = ((lane // GLA_DK) == (h % 2)).astype(F32)
    ri = lax.broadcasted_iota(jnp.int32, (c, c), 0)
    ci_ = lax.broadcasted_iota(jnp.int32, (c, c), 1)
    causal = ci_ <= ri
    ltri = causal.astype(BF16)
    ng = ng_ref[...]

    def chunk(i, carry):
        rows = pl.ds(pl.multiple_of(i * c, c), c)
        st = st_ref[...]
        q = q_ref[rows, :]
        k = k_ref[rows, :]
        v = v_ref[rows, :]
        la = la_ref[rows, :]
        bcum = _dot_exact_lhs(ltri, la)
        blast = bcum[c - 1:c, :]
        q_s = q * jnp.exp(bcum) * hm
        k_s = k * jnp.exp(-bcum)
        att = jnp.where(causal, _dot_nt(q_s, k_s), 0.0)
        o = _dot(att, v) + _dot_nt(q_s, st)
        kdec = k * jnp.exp(blast - bcum) * hm
        st_ref[...] = st * jnp.exp(blast) + _dot(v.astype(F32).T, kdec)
        o = o * lax.rsqrt(jnp.mean(o * o, axis=-1, keepdims=True) + GLA_NORM_EPS) * ng
        o_ref[rows, :] = (o * sg_ref[rows, :]).astype(o_ref.dtype)
        return carry

    lax.fori_loop(0, tb // c, chunk, 0)


def _gla_call(q, k, v, sg, la, ng, bsz, seq):
    tb = 512
    nt = seq // tb
    n_tok = bsz * seq

    def tok_pair():
        return pl.BlockSpec((tb, LANES), lambda bb, h, t: (bb * nt + t, h // 2))

    def tok_head():
        return pl.BlockSpec((tb, LANES), lambda bb, h, t: (bb * nt + t, h))

    return pl.pallas_call(
        functools.partial(_gla_kernel, tb=tb),
        grid=(bsz, GLA_HEADS, nt),
        in_specs=[tok_pair(), tok_pair(), tok_head(), tok_head(), tok_pair(),
                  pl.BlockSpec((1, LANES), lambda bb, h, t: (0, 0))],
        out_specs=tok_head(),
        out_shape=jax.ShapeDtypeStruct((n_tok, GLA_VW), BF16),
        scratch_shapes=[pltpu.VMEM((GLA_DV, LANES), F32)],
        compiler_params=pltpu.CompilerParams(
            dimension_semantics=("arbitrary", "arbitrary", "arbitrary"),
            vmem_limit_bytes=VMEM_LIMIT),
        name="gla",
    )(q, k, v, sg, la, ng)


def _mix_kernel(x_ref, orw_ref, ogla_ref, sgr_ref, sgg_ref, gate_ref, wrb_ref, wgb_ref, wmix_ref,
                lng_ref, lnb_ref, o_ref):
    merged = (sgr_ref[...] * jnp.dot(orw_ref[...], wrb_ref[...], preferred_element_type=F32)
              + sgg_ref[...] * jnp.dot(ogla_ref[...], wgb_ref[...], preferred_element_type=F32))
    mix = _dot(merged, wmix_ref[...])
    y = _layer_norm(ALPHA * x_ref[...] + gate_ref[...] * mix, LN_EPS)
    o_ref[...] = y * lng_ref[...] + lnb_ref[...]


def _mix_call(x2, o_rw, o_gla, sgr, sgg, mod3, wrb, wgb, wmix, lng, lnb, bsz, seq, d_model):
    tm = 512
    nt = seq // tm
    n_tok = bsz * seq

    def tok(n):
        return pl.BlockSpec((tm, n), lambda b, t: (b * nt + t, 0))

    def full(a):
        return pl.BlockSpec(a.shape, lambda b, t: (0,) * a.ndim)

    return pl.pallas_call(
        _mix_kernel,
        grid=(bsz, nt),
        in_specs=[tok(d_model), tok(RW_WIDTH), tok(GLA_VW), tok(d_model), tok(d_model),
                  pl.BlockSpec((None, 1, d_model), lambda b, t: (b, 0, 2)),
                  full(wrb), full(wgb), full(wmix), full(lng), full(lnb)],
        out_specs=tok(d_model),
        out_shape=jax.ShapeDtypeStruct((n_tok, d_model), F32),
        compiler_params=pltpu.CompilerParams(dimension_semantics=("arbitrary", "arbitrary"),
                                             vmem_limit_bytes=VMEM_LIMIT),
        name="mix",
    )(x2, o_rw, o_gla, sgr, sgg, mod3, wrb, wgb, wmix, lng, lnb)


def _ffn_kernel(x_ref, shift_ref, scale_ref, gate_ref, win_ref, wout_ref, lng_ref, lnb_ref, o_ref,
                *, d_ff):
    x = x_ref[...]
    u = x * (1.0 + scale_ref[...]) + shift_ref[...]
    h = _dot(u, win_ref[...])
    hg = h[:, :d_ff]
    act = hg * _sigmoid(hg) * h[:, d_ff:]
    ffn = _dot(act, wout_ref[...])
    y = _layer_norm(ALPHA * x + gate_ref[...] * ffn, LN_EPS)
    o_ref[...] = y * lng_ref[...] + lnb_ref[...]


def _ffn_call(x1, mod3, win, wout, lng, lnb, bsz, seq, d_model):
    tm = 256
    nt = seq // tm
    n_tok = bsz * seq
    d_ff = wout.shape[0]

    def tok(n):
        return pl.BlockSpec((tm, n), lambda b, t: (b * nt + t, 0))

    def full(a):
        return pl.BlockSpec(a.shape, lambda b, t: (0,) * a.ndim)

    def modspec(idx):
        return pl.BlockSpec((None, 1, d_model), lambda b, t: (b, 0, idx))

    return pl.pallas_call(
        functools.partial(_ffn_kernel, d_ff=d_ff),
        grid=(bsz, nt),
        in_specs=[tok(d_model), modspec(3), modspec(4), modspec(5), full(win), full(wout),
                  full(lng), full(lnb)],
        out_specs=tok(d_model),
        out_shape=jax.ShapeDtypeStruct((n_tok, d_model), F32),
        compiler_params=pltpu.CompilerParams(dimension_semantics=("arbitrary", "arbitrary"),
                                             vmem_limit_bytes=VMEM_LIMIT),
        name="ffn",
    )(x1, mod3, mod3, mod3, win, wout, lng, lnb)


def kernel(x, c, w_ada, b_ada, w_in, mu_rw, rw_w0, rw_w2, rw_a0, rw_a2, rw_g2, rw_k_k, rw_k_a,
           rw_r_k, rw_gn_g, rw_gn_b, gla_a2, gla_a_b, gla_norm_g, w_rw_branch, w_gla_branch,
           w_mix_out, ln1_g, ln1_b, w_ffn_in, w_ffn_out, ln2_g, ln2_b):
    bsz, seq, d_model = x.shape
    assert w_ada.shape[0] == DEPTH and seq % 512 == 0
    n_tok = bsz * seq
    l = 0

    wi = w_in[l]
    gla_ad_cols = wi[:, RW_COLS + GLA_MAIN:RW_COLS + GLA_COLS]
    wp = jnp.concatenate(
        [wi[:, :RW_COLS + GLA_MAIN], wi[:, RW_COLS + GLA_COLS:], gla_ad_cols,
         jnp.zeros((d_model, LANES - GLA_GATE_LORA), wi.dtype)], axis=1).astype(BF16)
    zeros_lora = jnp.zeros((DECAY_LORA, RW_WIDTH), F32)
    w2p = jnp.concatenate([rw_w2[l], zeros_lora], axis=0).astype(BF16)
    a2p = jnp.concatenate([zeros_lora, rw_a2[l]], axis=0).astype(BF16)
    ga2p = jnp.concatenate(
        [gla_a2[l], jnp.zeros((LANES - GLA_GATE_LORA, GLA_KW), F32)], axis=0).astype(BF16)
    row = lambda a: a.reshape(1, -1)
    hid = jnp.arange(LANES) // RW_HEAD
    bd = (hid[:, None] == hid[None, :]).astype(BF16)
    small = (row(mu_rw[l]), row(rw_w0[l]), w2p, row(rw_a0[l]), a2p, rw_g2[l].astype(BF16),
             row(rw_k_k[l]), row(rw_k_a[l]), row(rw_r_k[l]), ga2p, row(gla_a_b[l]), bd)

    x2 = x.reshape(n_tok, d_model)
    mod = _mod_call(c, w_ada[l], b_ada[l])
    mod3 = mod.reshape(bsz, 1, 6 * d_model)

    (r, k, v, lw, nkk, b, g, bonus, gq, gk, gv, gsg, gla, sgr, sgg) = _inproj_call(
        x2, mod3, wp, small, bsz, seq, d_model)

    o_rw = _rwkv_call(r, k, v, lw, nkk, b, g, bonus, row(rw_gn_g[l]), row(rw_gn_b[l]), bd, bsz, seq)
    o_gla = _gla_call(gq, gk, gv, gsg, gla, row(gla_norm_g[l]), bsz, seq)

    x1 = _mix_call(x2, o_rw, o_gla, sgr, sgg, mod3, w_rw_branch[l].astype(BF16),
                   w_gla_branch[l].astype(BF16), w_mix_out[l].astype(BF16),
                   row(ln1_g[l]), row(ln1_b[l]), bsz, seq, d_model)
    out = _ffn_call(x1, mod3, w_ffn_in[l].astype(BF16), w_ffn_out[l].astype(BF16),
                    row(ln2_g[l]), row(ln2_b[l]), bsz, seq, d_model)
    return out.reshape(bsz, seq, d_model)
```

```python
import functools

import jax
import jax.numpy as jnp
from jax import lax
from jax.experimental import pallas as pl
from jax.experimental.pallas import tpu as pltpu

F32 = jnp.float32
BF16 = jnp.bfloat16

RW_HEAD = 64
RW_HEADS = 8
RW_WIDTH = RW_HEADS * RW_HEAD
DECAY_LORA = 64
AAA_LORA = 64
GATE_LORA = 128
RW_GN_EPS = 64e-5
RW_COLS = 3 * RW_WIDTH + DECAY_LORA + AAA_LORA + GATE_LORA
GLA_HEADS = 4
GLA_DK = 64
GLA_DV = 128
GLA_KW = GLA_HEADS * GLA_DK
GLA_VW = GLA_HEADS * GLA_DV
GLA_GATE_LORA = 16
GLA_TAU = 16.0
GLA_NORM_EPS = 1e-5
GLA_MAIN = 2 * GLA_KW + 2 * GLA_VW
GLA_COLS = GLA_MAIN + GLA_GATE_LORA
LN_EPS = 1e-5
DEPTH = 1
ALPHA = (2.0 * DEPTH) ** 0.25

LANES = 128
CHUNK = 64
RWKV_UNROLL = 2
GLA_UNROLL = 2
REC_TILE = 512
VMEM_LIMIT = 56 * 1024 * 1024

COL_RW = 0
COL_GLA = RW_COLS
COL_GATE = COL_GLA + GLA_MAIN


def _sigmoid(x):
    return 1.0 / (1.0 + jnp.exp(-x))


def _softplus(x):
    return jnp.maximum(x, 0.0) + jnp.log1p(jnp.exp(-jnp.abs(x)))


def _dot(a, b):
    return jnp.dot(a.astype(BF16), b.astype(BF16), preferred_element_type=F32)


def _dot_nt(a, b):
    return lax.dot_general(a.astype(BF16), b.astype(BF16), (((1,), (1,)), ((), ())),
                           preferred_element_type=F32)


def _split_hi_lo(x):
    hi = x.astype(BF16)
    lo = (x - hi.astype(F32)).astype(BF16)
    return hi, lo


def _dot_exact_rhs(x, w01):
    hi, lo = _split_hi_lo(x)
    return (jnp.dot(hi, w01, preferred_element_type=F32)
            + jnp.dot(lo, w01, preferred_element_type=F32))


def _dot_exact_lhs(w01, x):
    hi, lo = _split_hi_lo(x)
    return (jnp.dot(w01, hi, preferred_element_type=F32)
            + jnp.dot(w01, lo, preferred_element_type=F32))


def _layer_norm(x, eps):
    mu = jnp.mean(x, axis=-1, keepdims=True)
    xc = x - mu
    var = jnp.mean(xc * xc, axis=-1, keepdims=True)
    return xc * lax.rsqrt(var + eps)


def _mod_kernel(c_ref, w_ref, b_ref, o_ref):
    c = c_ref[...]
    sc = c * _sigmoid(c)
    o_ref[...] = _dot(sc, w_ref[...]) + b_ref[...]


def _mod_call(c, w_ada, b_ada):
    bsz, d = c.shape
    n = w_ada.shape[1]
    tn = 1536
    return pl.pallas_call(
        _mod_kernel,
        grid=(n // tn,),
        in_specs=[pl.BlockSpec((bsz, d), lambda j: (0, 0)),
                  pl.BlockSpec((d, tn), lambda j: (0, j)),
                  pl.BlockSpec((1, tn), lambda j: (0, j))],
        out_specs=pl.BlockSpec((bsz, tn), lambda j: (0, j)),
        out_shape=jax.ShapeDtypeStruct((bsz, n), F32),
        compiler_params=pltpu.CompilerParams(dimension_semantics=("arbitrary",),
                                             vmem_limit_bytes=VMEM_LIMIT),
        name="mod",
    )(c, w_ada, b_ada.reshape(1, n))


def _inproj_kernel(x_ref, shift_ref, scale_ref, wp_ref, mu_ref, w0_ref, w2_ref, a0_ref, a2_ref,
                   g2_ref, kk_ref, ka_ref, rk_ref, ga2_ref, gab_ref, bd_ref,
                   r_out, k_out, v_out, lw_out, nkk_out, b_out, g_out, bonus_out,
                   gq_out, gk_out, gv_out, gsg_out, gla_out, sgr_out, sgg_out,
                   carry_ref, *, tm, d_model):
    t = pl.program_id(1)

    @pl.when(t == 0)
    def _():
        carry_ref[...] = jnp.zeros_like(carry_ref)

    u = x_ref[...] * (1.0 + scale_ref[...]) + shift_ref[...]
    proj = _dot(u, wp_ref[...])

    p = proj[:, COL_RW:COL_RW + RW_COLS]
    prev = pltpu.roll(p, 1, 0)
    row = lax.broadcasted_iota(jnp.int32, p.shape, 0)
    prev = jnp.where(row == 0, carry_ref[...], prev)
    carry_ref[...] = p[tm - 1:tm, :]
    ps = p + mu_ref[...] * (prev - p)

    r = ps[:, 0:RW_WIDTH]
    k = ps[:, RW_WIDTH:2 * RW_WIDTH]
    v = ps[:, 2 * RW_WIDTH:3 * RW_WIDTH]
    z = ps[:, 3 * RW_WIDTH:3 * RW_WIDTH + LANES]
    gd = ps[:, 3 * RW_WIDTH + LANES:RW_COLS]

    wlin = w0_ref[...] + _dot(jnp.tanh(z), w2_ref[...])
    w = -_softplus(-wlin) - 0.5
    lw_out[...] = -jnp.exp(w)
    a = _sigmoid(a0_ref[...] + _dot(z, a2_ref[...]))
    g_out[...] = _dot(_sigmoid(gd), g2_ref[...])
    kkv = k * kk_ref[...]
    kp = k * (1.0 + (a - 1.0) * ka_ref[...])
    rkr = r * kp * rk_ref[...]
    bd = bd_ref[...]
    for j in range(RW_WIDTH // LANES):
        sl = slice(j * LANES, (j + 1) * LANES)
        kkj = kkv[:, sl]
        ssq = _dot_exact_rhs(kkj * kkj, bd)
        kkn = kkj / jnp.maximum(jnp.sqrt(ssq), 1e-12)
        nkk_out[:, sl] = -kkn
        b_out[:, sl] = kkn * a[:, sl]
        bonus_out[:, sl] = _dot_exact_rhs(rkr[:, sl], bd) * v[:, sl]
    r_out[...] = r
    k_out[...] = kp
    v_out[...] = v.astype(BF16)

    pg = proj[:, COL_GLA:COL_GLA + GLA_MAIN]
    gq_out[...] = pg[:, 0:GLA_KW] * (GLA_DK ** -0.5)
    gk_out[...] = pg[:, GLA_KW:2 * GLA_KW]
    gv_out[...] = pg[:, 2 * GLA_KW:2 * GLA_KW + GLA_VW].astype(BF16)
    gg = pg[:, 2 * GLA_KW + GLA_VW:GLA_MAIN]
    gsg_out[...] = gg * _sigmoid(gg)
    adg = proj[:, COL_GATE + 2 * d_model:COL_GATE + 2 * d_model + LANES]
    la = _dot(adg, ga2_ref[...]) + gab_ref[...]
    gla_out[...] = -_softplus(-la) / GLA_TAU

    sg = _sigmoid(proj[:, COL_GATE:COL_GATE + 2 * d_model])
    sgr_out[...] = sg[:, :d_model]
    sgg_out[...] = sg[:, d_model:]


def _inproj_call(x2, mod3, wp, small, bsz, seq, d_model):
    tm = 256
    nt = seq // tm
    n_tok = bsz * seq
    (mu, w0, w2p, a0, a2p, g2, k_k, k_a, r_k, ga2p, gab, bd) = small

    def tok(n):
        return pl.BlockSpec((tm, n), lambda b, t: (b * nt + t, 0))

    def full(a):
        return pl.BlockSpec(a.shape, lambda b, t: (0,) * a.ndim)

    def modspec(idx):
        return pl.BlockSpec((None, 1, d_model), lambda b, t: (b, 0, idx))

    outs = [
        (RW_WIDTH, F32), (RW_WIDTH, F32), (RW_WIDTH, BF16), (RW_WIDTH, F32), (RW_WIDTH, F32),
        (RW_WIDTH, F32), (RW_WIDTH, F32), (RW_WIDTH, F32),
        (GLA_KW, F32), (GLA_KW, F32), (GLA_VW, BF16), (GLA_VW, F32), (GLA_KW, F32),
        (d_model, F32), (d_model, F32),
    ]
    return pl.pallas_call(
        functools.partial(_inproj_kernel, tm=tm, d_model=d_model),
        grid=(bsz, nt),
        in_specs=[tok(d_model), modspec(0), modspec(1), full(wp), full(mu), full(w0), full(w2p),
                  full(a0), full(a2p), full(g2), full(k_k), full(k_a), full(r_k), full(ga2p),
                  full(gab), full(bd)],
        out_specs=[tok(n) for n, _ in outs],
        out_shape=[jax.ShapeDtypeStruct((n_tok, n), dt) for n, dt in outs],
        scratch_shapes=[pltpu.VMEM((1, RW_COLS), F32)],
        compiler_params=pltpu.CompilerParams(dimension_semantics=("arbitrary", "arbitrary"),
                                             vmem_limit_bytes=VMEM_LIMIT),
        name="inproj",
    )(x2, mod3, mod3, wp, mu, w0, w2p, a0, a2p, g2, k_k, k_a, r_k, ga2p, gab, bd)


def _rwkv_kernel(r_ref, k_ref, v_ref, lw_ref, nkk_ref, b_ref, g_ref, bonus_ref, gng_ref, gnb_ref,
                 bd_ref, o_ref, h_ref, rp_s, yl_s, m_s, hl_s, *, tb, unroll):
    c = CHUNK
    nchunk = tb // c
    npair = RW_WIDTH // LANES
    t = pl.program_id(1)

    @pl.when(t == 0)
    def _():
        h_ref[...] = jnp.zeros_like(h_ref)

    lane = lax.broadcasted_iota(jnp.int32, (1, LANES), 1)
    masks = ((lane < RW_HEAD).astype(F32), (lane >= RW_HEAD).astype(F32))
    ri = lax.broadcasted_iota(jnp.int32, (c, c), 0)
    ci_ = lax.broadcasted_iota(jnp.int32, (c, c), 1)
    strict = ci_ < ri
    incl = ci_ <= ri
    ltri = incl.astype(BF16)
    eye_c = (ci_ == ri).astype(F32)
    r2 = lax.broadcasted_iota(jnp.int32, (LANES, LANES), 0)
    c2 = lax.broadcasted_iota(jnp.int32, (LANES, LANES), 1)
    blockdiag = (r2 // RW_HEAD) == (c2 // RW_HEAD)
    eye_l = r2 == c2
    bd = bd_ref[...]

    def local_group(gi, carry):
        units = [(u, p) for u in range(unroll) for p in range(npair)]
        heads = [(s, h) for s in range(len(units)) for h in range(2)]
        idxs = [gi * unroll + u for u in range(unroll)]
        rows = [pl.ds(pl.multiple_of(i * c, c), c) for i in idxs]

        def ld(ref, u, p):
            return ref[rows[u], p * LANES:(p + 1) * LANES]

        lw = [ld(lw_ref, u, p) for u, p in units]
        gcum = [_dot_exact_lhs(ltri, x) for x in lw]
        gend, rt, at, kt, bt, bh, kh, v = [], [], [], [], [], [], [], []
        for s, (u, p) in enumerate(units):
            g_ = gcum[s]
            ge = g_[c - 1:c, :]
            r = ld(r_ref, u, p)
            k = ld(k_ref, u, p)
            nkk = ld(nkk_ref, u, p)
            b = ld(b_ref, u, p)
            e_neg = jnp.exp(-g_)
            dk = jnp.exp(ge - g_)
            gend.append(ge)
            rt.append(r * jnp.exp(g_))
            at.append(nkk * jnp.exp(g_ - lw[s]))
            kt.append(k * e_neg)
            bt.append(b * e_neg)
            bh.append(b * dk)
            kh.append(k * dk)
            v.append(ld(v_ref, u, p))

        lhs = [jnp.concatenate([at[s] * masks[0], at[s] * masks[1],
                                rt[s] * masks[0], rt[s] * masks[1]], axis=0).astype(BF16)
               for s in range(len(units))]
        xb = [_dot_nt(lhs[s], bt[s]) for s in range(len(units))]
        xk = [_dot_nt(lhs[s], kt[s]) for s in range(len(units))]

        a_ab = [jnp.where(strict, xb[s][h * c:(h + 1) * c], 0.0) for s, h in heads]
        a_ak = [jnp.where(strict, xk[s][h * c:(h + 1) * c], 0.0) for s, h in heads]
        a_rb = [jnp.where(incl, xb[s][(2 + h) * c:(3 + h) * c], 0.0) for s, h in heads]
        a_rk = [jnp.where(incl, xk[s][(2 + h) * c:(3 + h) * c], 0.0) for s, h in heads]

        tinv = [eye_c + a for a in a_ab]
        pw = a_ab
        for _ in range(5):
            pw = [_dot(x, x) for x in pw]
            tinv = [tm_ + _dot(tm_, x) for tm_, x in zip(tinv, pw)]

        ap_h = [_dot(tinv[i], at[s] * masks[h]) for i, (s, h) in enumerate(heads)]
        akv = [_dot(a_ak[i], v[s]) for i, (s, h) in enumerate(heads)]
        ul_h = [_dot(tinv[i], akv[i]) for i in range(len(heads))]
        rb_ap = [_dot(a_rb[i], ap_h[i]) for i in range(len(heads))]
        rk_v = [_dot(a_rk[i], v[s]) for i, (s, h) in enumerate(heads)]
        rb_ul = [_dot(a_rb[i], ul_h[i]) for i in range(len(heads))]

        ap, uloc = [], []
        for s, (u, p) in enumerate(units):
            i0, i1 = 2 * s, 2 * s + 1
            lanes = slice(p * LANES, (p + 1) * LANES)
            ap.append(ap_h[i0] + ap_h[i1])
            uloc.append(masks[0] * ul_h[i0] + masks[1] * ul_h[i1])
            rp_s[idxs[u], :, lanes] = rt[s] + rb_ap[i0] + rb_ap[i1]
            yl_s[rows[u], lanes] = (masks[0] * (rb_ul[i0] + rk_v[i0])
                                    + masks[1] * (rb_ul[i1] + rk_v[i1]))

        bh_t = [x.T.astype(BF16) for x in bh]
        kh_t = [x.T.astype(BF16) for x in kh]
        m1 = [_dot(bh_t[s], ap[s]) for s in range(len(units))]
        h1 = [_dot(bh_t[s], uloc[s]) for s in range(len(units))]
        h2 = [_dot(kh_t[s], v[s]) for s in range(len(units))]
        for s, (u, p) in enumerate(units):
            m_s[idxs[u], p] = (jnp.where(eye_l, jnp.exp(gend[s]), 0.0)
                               + jnp.where(blockdiag, m1[s], 0.0))
            hl_s[idxs[u], p] = jnp.where(blockdiag, h1[s] + h2[s], 0.0)
        return carry

    lax.fori_loop(0, nchunk // unroll, local_group, 0)

    hstate = [h_ref[p] for p in range(npair)]
    for ci in range(nchunk):
        rows = slice(ci * c, (ci + 1) * c)
        rp = rp_s[ci]
        ys = [_dot(rp[:, p * LANES:(p + 1) * LANES], hstate[p]) for p in range(npair)]
        hn = [_dot(m_s[ci, p], hstate[p]) for p in range(npair)]
        for p in range(npair):
            lanes = slice(p * LANES, (p + 1) * LANES)
            yl_s[rows, lanes] = ys[p] + yl_s[rows, lanes]
            hstate[p] = hn[p] + hl_s[ci, p]
    for p in range(npair):
        h_ref[p] = hstate[p]

    for p in range(npair):
        lanes = slice(p * LANES, (p + 1) * LANES)
        y = yl_s[:, lanes]
        mean = _dot_exact_rhs(y, bd) * (1.0 / RW_HEAD)
        yc = y - mean
        var = _dot_exact_rhs(yc * yc, bd) * (1.0 / RW_HEAD)
        yn = yc * lax.rsqrt(var + RW_GN_EPS)
        out = (yn * gng_ref[:, lanes] + gnb_ref[:, lanes] + bonus_ref[:, lanes]) * g_ref[:, lanes]
        o_ref[:, lanes] = out.astype(o_ref.dtype)


def _rwkv_call(r, k, v, lw, nkk, b, g, bonus, gng, gnb, bd, bsz, seq):
    tb = REC_TILE
    nt = seq // tb
    n_tok = bsz * seq
    npair = RW_WIDTH // LANES
    nchunk = tb // CHUNK

    def tok():
        return pl.BlockSpec((tb, RW_WIDTH), lambda bb, t: (bb * nt + t, 0))

    def full(a):
        return pl.BlockSpec(a.shape, lambda bb, t: (0,) * a.ndim)

    return pl.pallas_call(
        functools.partial(_rwkv_kernel, tb=tb, unroll=RWKV_UNROLL),
        grid=(bsz, nt),
        in_specs=[tok() for _ in range(8)] + [full(gng), full(gnb), full(bd)],
        out_specs=tok(),
        out_shape=jax.ShapeDtypeStruct((n_tok, RW_WIDTH), BF16),
        scratch_shapes=[pltpu.VMEM((npair, LANES, LANES), F32),
                        pltpu.VMEM((nchunk, CHUNK, RW_WIDTH), F32),
                        pltpu.VMEM((tb, RW_WIDTH), F32),
                        pltpu.VMEM((nchunk, npair, LANES, LANES), F32),
                        pltpu.VMEM((nchunk, npair, LANES, LANES), F32)],
        compiler_params=pltpu.CompilerParams(
            dimension_semantics=("arbitrary", "arbitrary"),
            vmem_limit_bytes=VMEM_LIMIT),
        name="rwkv",
    )(r, k, v, lw, nkk, b, g, bonus, gng, gnb, bd)


def _gla_kernel(q_ref, k_ref, v_ref, sg_ref, la_ref, ng_ref, o_ref, st_ref, qs_s, oi_s, kv_s, eb_s,
                *, tb, unroll):
    c = CHUNK
    nchunk = tb // c
    t = pl.program_id(1)

    @pl.when(t == 0)
    def _():
        st_ref[...] = jnp.zeros_like(st_ref)

    lane = lax.broadcasted_iota(jnp.int32, (1, LANES), 1)
    hmask = ((lane < GLA_DK).astype(F32), (lane >= GLA_DK).astype(F32))
    ri = lax.broadcasted_iota(jnp.int32, (c, c), 0)
    ci_ = lax.broadcasted_iota(jnp.int32, (c, c), 1)
    causal = ci_ <= ri
    ltri = causal.astype(BF16)

    def local_group(gi, carry):
        units = [(u, h) for u in range(unroll) for h in range(GLA_HEADS)]
        idxs = [gi * unroll + u for u in range(unroll)]
        rows = [pl.ds(pl.multiple_of(i * c, c), c) for i in idxs]

        def qk_lanes(h):
            return slice((h // 2) * LANES, (h // 2 + 1) * LANES)

        la = [la_ref[rows[u], qk_lanes(h)] for u, h in units]
        bcum = [_dot_exact_lhs(ltri, x) for x in la]
        q_s, k_s, kdec, v = [], [], [], []
        for s, (u, h) in enumerate(units):
            bc = bcum[s]
            blast = bc[c - 1:c, :]
            q = q_ref[rows[u], qk_lanes(h)]
            k = k_ref[rows[u], qk_lanes(h)]
            hm = hmask[h % 2]
            q_s.append((q * jnp.exp(bc) * hm).astype(BF16))
            k_s.append(k * jnp.exp(-bc))
            kdec.append(k * jnp.exp(blast - bc) * hm)
            v.append(v_ref[rows[u], h * LANES:(h + 1) * LANES])
            eb_s[idxs[u], h] = jnp.exp(blast)
        att = [jnp.where(causal, _dot_nt(q_s[s], k_s[s]), 0.0) for s in range(len(units))]
        oi = [_dot(att[s], v[s]) for s in range(len(units))]
        v_t = [x.astype(F32).T.astype(BF16) for x in v]
        kv = [_dot(v_t[s], kdec[s]) for s in range(len(units))]
        for s, (u, h) in enumerate(units):
            qs_s[idxs[u], h] = q_s[s]
            oi_s[rows[u], h * LANES:(h + 1) * LANES] = oi[s]
            kv_s[idxs[u], h] = kv[s]
        return carry

    lax.fori_loop(0, nchunk // unroll, local_group, 0)

    st = [st_ref[h] for h in range(GLA_HEADS)]
    for ci in range(nchunk):
        rows = slice(ci * c, (ci + 1) * c)
        inter = [_dot_nt(qs_s[ci, h], st[h]) for h in range(GLA_HEADS)]
        for h in range(GLA_HEADS):
            lanes = slice(h * LANES, (h + 1) * LANES)
            oi_s[rows, lanes] = oi_s[rows, lanes] + inter[h]
            st[h] = st[h] * eb_s[ci, h] + kv_s[ci, h]
    for h in range(GLA_HEADS):
        st_ref[h] = st[h]

    ng = ng_ref[...]
    for h in range(GLA_HEADS):
        lanes = slice(h * LANES, (h + 1) * LANES)
        o = oi_s[:, lanes]
        o = o * lax.rsqrt(jnp.mean(o * o, axis=-1, keepdims=True) + GLA_NORM_EPS) * ng
        o_ref[:, lanes] = (o * sg_ref[:, lanes]).astype(o_ref.dtype)


def _gla_call(q, k, v, sg, la, ng, bsz, seq):
    tb = REC_TILE
    nt = seq // tb
    n_tok = bsz * seq
    nchunk = tb // CHUNK

    def tok(n):
        return pl.BlockSpec((tb, n), lambda bb, t: (bb * nt + t, 0))

    return pl.pallas_call(
        functools.partial(_gla_kernel, tb=tb, unroll=GLA_UNROLL),
        grid=(bsz, nt),
        in_specs=[tok(GLA_KW), tok(GLA_KW), tok(GLA_VW), tok(GLA_VW), tok(GLA_KW),
                  pl.BlockSpec((1, LANES), lambda bb, t: (0, 0))],
        out_specs=tok(GLA_VW),
        out_shape=jax.ShapeDtypeStruct((n_tok, GLA_VW), BF16),
        scratch_shapes=[pltpu.VMEM((GLA_HEADS, GLA_DV, LANES), F32),
                        pltpu.VMEM((nchunk, GLA_HEADS, CHUNK, LANES), BF16),
                        pltpu.VMEM((tb, GLA_VW), F32),
                        pltpu.VMEM((nchunk, GLA_HEADS, GLA_DV, LANES), F32),
                        pltpu.VMEM((nchunk, GLA_HEADS, 1, LANES), F32)],
        compiler_params=pltpu.CompilerParams(
            dimension_semantics=("arbitrary", "arbitrary"),
            vmem_limit_bytes=VMEM_LIMIT),
        name="gla",
    )(q, k, v, sg, la, ng)


def _mix_kernel(x_ref, orw_ref, ogla_ref, sgr_ref, sgg_ref, gate_ref, wrb_ref, wgb_ref, wmix_ref,
                lng_ref, lnb_ref, o_ref):
    merged = (sgr_ref[...] * jnp.dot(orw_ref[...], wrb_ref[...], preferred_element_type=F32)
              + sgg_ref[...] * jnp.dot(ogla_ref[...], wgb_ref[...], preferred_element_type=F32))
    mix = _dot(merged, wmix_ref[...])
    y = _layer_norm(ALPHA * x_ref[...] + gate_ref[...] * mix, LN_EPS)
    o_ref[...] = y * lng_ref[...] + lnb_ref[...]


def _mix_call(x2, o_rw, o_gla, sgr, sgg, mod3, wrb, wgb, wmix, lng, lnb, bsz, seq, d_model):
    tm = 512
    nt = seq // tm
    n_tok = bsz * seq

    def tok(n):
        return pl.BlockSpec((tm, n), lambda b, t: (b * nt + t, 0))

    def full(a):
        return pl.BlockSpec(a.shape, lambda b, t: (0,) * a.ndim)

    return pl.pallas_call(
        _mix_kernel,
        grid=(bsz, nt),
        in_specs=[tok(d_model), tok(RW_WIDTH), tok(GLA_VW), tok(d_model), tok(d_model),
                  pl.BlockSpec((None, 1, d_model), lambda b, t: (b, 0, 2)),
                  full(wrb), full(wgb), full(wmix), full(lng), full(lnb)],
        out_specs=tok(d_model),
        out_shape=jax.ShapeDtypeStruct((n_tok, d_model), F32),
        compiler_params=pltpu.CompilerParams(dimension_semantics=("arbitrary", "arbitrary"),
                                             vmem_limit_bytes=VMEM_LIMIT),
        name="mix",
    )(x2, o_rw, o_gla, sgr, sgg, mod3, wrb, wgb, wmix, lng, lnb)


def _ffn_kernel(x_ref, shift_ref, scale_ref, gate_ref, win_ref, wout_ref, lng_ref, lnb_ref, o_ref,
                *, d_ff):
    x = x_ref[...]
    u = x * (1.0 + scale_ref[...]) + shift_ref[...]
    h = _dot(u, win_ref[...])
    hg = h[:, :d_ff]
    act = hg * _sigmoid(hg) * h[:, d_ff:]
    ffn = _dot(act, wout_ref[...])
    y = _layer_norm(ALPHA * x + gate_ref[...] * ffn, LN_EPS)
    o_ref[...] = y * lng_ref[...] + lnb_ref[...]


def _ffn_call(x1, mod3, win, wout, lng, lnb, bsz, seq, d_model):
    tm = 256
    nt = seq // tm
    n_tok = bsz * seq
    d_ff = wout.shape[0]

    def tok(n):
        return pl.BlockSpec((tm, n), lambda b, t: (b * nt + t, 0))

    def full(a):
        return pl.BlockSpec(a.shape, lambda b, t: (0,) * a.ndim)

    def modspec(idx):
        return pl.BlockSpec((None, 1, d_model), lambda b, t: (b, 0, idx))

    return pl.pallas_call(
        functools.partial(_ffn_kernel, d_ff=d_ff),
        grid=(bsz, nt),
        in_specs=[tok(d_model), modspec(3), modspec(4), modspec(5), full(win), full(wout),
                  full(lng), full(lnb)],
        out_specs=tok(d_model),
        out_shape=jax.ShapeDtypeStruct((n_tok, d_model), F32),
        compiler_params=pltpu.CompilerParams(dimension_semantics=("arbitrary", "arbitrary"),
                                             vmem_limit_bytes=VMEM_LIMIT),
        name="ffn",
    )(x1, mod3, mod3, mod3, win, wout, lng, lnb)


def kernel(x, c, w_ada, b_ada, w_in, mu_rw, rw_w0, rw_w2, rw_a0, rw_a2, rw_g2, rw_k_k, rw_k_a,
           rw_r_k, rw_gn_g, rw_gn_b, gla_a2, gla_a_b, gla_norm_g, w_rw_branch, w_gla_branch,
           w_mix_out, ln1_g, ln1_b, w_ffn_in, w_ffn_out, ln2_g, ln2_b):
    bsz, seq, d_model = x.shape
    assert w_ada.shape[0] == DEPTH and seq % REC_TILE == 0
    n_tok = bsz * seq
    l = 0

    wi = w_in[l]
    gla_ad_cols = wi[:, RW_COLS + GLA_MAIN:RW_COLS + GLA_COLS]
    wp = jnp.concatenate(
        [wi[:, :RW_COLS + GLA_MAIN], wi[:, RW_COLS + GLA_COLS:], gla_ad_cols,
         jnp.zeros((d_model, LANES - GLA_GATE_LORA), wi.dtype)], axis=1).astype(BF16)
    zeros_lora = jnp.zeros((DECAY_LORA, RW_WIDTH), F32)
    w2p = jnp.concatenate([rw_w2[l], zeros_lora], axis=0).astype(BF16)
    a2p = jnp.concatenate([zeros_lora, rw_a2[l]], axis=0).astype(BF16)
    ga2p = jnp.concatenate(
        [gla_a2[l], jnp.zeros((LANES - GLA_GATE_LORA, GLA_KW), F32)], axis=0).astype(BF16)
    row = lambda a: a.reshape(1, -1)
    hid = jnp.arange(LANES) // RW_HEAD
    bd = (hid[:, None] == hid[None, :]).astype(BF16)
    small = (row(mu_rw[l]), row(rw_w0[l]), w2p, row(rw_a0[l]), a2p, rw_g2[l].astype(BF16),
             row(rw_k_k[l]), row(rw_k_a[l]), row(rw_r_k[l]), ga2p, row(gla_a_b[l]), bd)

    x2 = x.reshape(n_tok, d_model)
    mod = _mod_call(c, w_ada[l], b_ada[l])
    mod3 = mod.reshape(bsz, 1, 6 * d_model)

    (r, k, v, lw, nkk, b, g, bonus, gq, gk, gv, gsg, gla, sgr, sgg) = _inproj_call(
        x2, mod3, wp, small, bsz, seq, d_model)

    o_rw = _rwkv_call(r, k, v, lw, nkk, b, g, bonus, row(rw_gn_g[l]), row(rw_gn_b[l]), bd, bsz, seq)
    o_gla = _gla_call(gq, gk, gv, gsg, gla, row(gla_norm_g[l]), bsz, seq)

    x1 = _mix_call(x2, o_rw, o_gla, sgr, sgg, mod3, w_rw_branch[l].astype(BF16),
                   w_gla_branch[l].astype(BF16), w_mix_out[l].astype(BF16),
                   row(ln1_g[l]), row(ln1_b[l]), bsz, seq, d_model)
    out = _ffn_call(x1, mod3, w_ffn_in[l].astype(BF16), w_ffn_out[l].astype(BF16),
                    row(ln2_g[l]), row(ln2_b[l]), bsz, seq, d_model)
    return out.reshape(bsz, seq, d_model)
```

```python
import functools

import jax
import jax.numpy as jnp
from jax import lax
from jax.experimental import pallas as pl
from jax.experimental.pallas import tpu as pltpu

F32 = jnp.float32
BF16 = jnp.bfloat16

RW_HEAD = 64
RW_HEADS = 8
RW_WIDTH = RW_HEADS * RW_HEAD
DECAY_LORA = 64
AAA_LORA = 64
GATE_LORA = 128
RW_GN_EPS = 64e-5
RW_COLS = 3 * RW_WIDTH + DECAY_LORA + AAA_LORA + GATE_LORA
GLA_HEADS = 4
GLA_DK = 64
GLA_DV = 128
GLA_KW = GLA_HEADS * GLA_DK
GLA_VW = GLA_HEADS * GLA_DV
GLA_GATE_LORA = 16
GLA_TAU = 16.0
GLA_NORM_EPS = 1e-5
GLA_MAIN = 2 * GLA_KW + 2 * GLA_VW
GLA_COLS = GLA_MAIN + GLA_GATE_LORA
LN_EPS = 1e-5
DEPTH = 1
ALPHA = (2.0 * DEPTH) ** 0.25

LANES = 128
CHUNK = 64
RWKV_UNROLL = 2
GLA_UNROLL = 2
REC_TILE = 512
VMEM_LIMIT = 56 * 1024 * 1024

COL_RW = 0
COL_GLA = RW_COLS
COL_GATE = COL_GLA + GLA_MAIN


def _sigmoid(x):
    return 1.0 / (1.0 + jnp.exp(-x))


def _softplus(x):
    return jnp.maximum(x, 0.0) + jnp.log1p(jnp.exp(-jnp.abs(x)))


def _dot(a, b):
    return jnp.dot(a.astype(BF16), b.astype(BF16), preferred_element_type=F32)


def _dot_nt(a, b):
    return lax.dot_general(a.astype(BF16), b.astype(BF16), (((1,), (1,)), ((), ())),
                           preferred_element_type=F32)


def _split_hi_lo(x):
    hi = x.astype(BF16)
    lo = (x - hi.astype(F32)).astype(BF16)
    return hi, lo


def _dot_exact_rhs(x, w01):
    hi, lo = _split_hi_lo(x)
    return (jnp.dot(hi, w01, preferred_element_type=F32)
            + jnp.dot(lo, w01, preferred_element_type=F32))


def _dot_exact_lhs(w01, x):
    hi, lo = _split_hi_lo(x)
    return (jnp.dot(w01, hi, preferred_element_type=F32)
            + jnp.dot(w01, lo, preferred_element_type=F32))


def _layer_norm(x, eps):
    mu = jnp.mean(x, axis=-1, keepdims=True)
    xc = x - mu
    var = jnp.mean(xc * xc, axis=-1, keepdims=True)
    return xc * lax.rsqrt(var + eps)


def _mod_kernel(c_ref, w_ref, b_ref, o_ref):
    c = c_ref[...]
    sc = c * _sigmoid(c)
    o_ref[...] = _dot(sc, w_ref[...]) + b_ref[...]


def _mod_call(c, w_ada, b_ada):
    bsz, d = c.shape
    n = w_ada.shape[1]
    tn = 1536
    return pl.pallas_call(
        _mod_kernel,
        grid=(n // tn,),
        in_specs=[pl.BlockSpec((bsz, d), lambda j: (0, 0)),
                  pl.BlockSpec((d, tn), lambda j: (0, j)),
                  pl.BlockSpec((1, tn), lambda j: (0, j))],
        out_specs=pl.BlockSpec((bsz, tn), lambda j: (0, j)),
        out_shape=jax.ShapeDtypeStruct((bsz, n), F32),
        compiler_params=pltpu.CompilerParams(dimension_semantics=("arbitrary",),
                                             vmem_limit_bytes=VMEM_LIMIT),
        name="mod",
    )(c, w_ada, b_ada.reshape(1, n))


def _inproj_kernel(x_ref, shift_ref, scale_ref, wp_ref, mu_ref, w0_ref, w2_ref, a0_ref, a2_ref,
                   g2_ref, kk_ref, ka_ref, rk_ref, ga2_ref, gab_ref, bd_ref,
                   r_out, k_out, v_out, lw_out, nkk_out, b_out, g_out, bonus_out,
                   gq_out, gk_out, gv_out, gsg_out, gla_out, sgr_out, sgg_out,
                   carry_ref, *, tm, d_model):
    t = pl.program_id(1)

    @pl.when(t == 0)
    def _():
        carry_ref[...] = jnp.zeros_like(carry_ref)

    u = x_ref[...] * (1.0 + scale_ref[...]) + shift_ref[...]
    proj = _dot(u, wp_ref[...])

    p = proj[:, COL_RW:COL_RW + RW_COLS]
    prev = pltpu.roll(p, 1, 0)
    row = lax.broadcasted_iota(jnp.int32, p.shape, 0)
    prev = jnp.where(row == 0, carry_ref[...], prev)
    carry_ref[...] = p[tm - 1:tm, :]
    ps = p + mu_ref[...] * (prev - p)

    r = ps[:, 0:RW_WIDTH]
    k = ps[:, RW_WIDTH:2 * RW_WIDTH]
    v = ps[:, 2 * RW_WIDTH:3 * RW_WIDTH]
    z = ps[:, 3 * RW_WIDTH:3 * RW_WIDTH + LANES]
    gd = ps[:, 3 * RW_WIDTH + LANES:RW_COLS]

    wlin = w0_ref[...] + _dot(jnp.tanh(z), w2_ref[...])
    w = -_softplus(-wlin) - 0.5
    lw_out[...] = -jnp.exp(w)
    a = _sigmoid(a0_ref[...] + _dot(z, a2_ref[...]))
    g_out[...] = _dot(_sigmoid(gd), g2_ref[...])
    kkv = k * kk_ref[...]
    kp = k * (1.0 + (a - 1.0) * ka_ref[...])
    rkr = r * kp * rk_ref[...]
    bd = bd_ref[...]
    for j in range(RW_WIDTH // LANES):
        sl = slice(j * LANES, (j + 1) * LANES)
        kkj = kkv[:, sl]
        ssq = _dot_exact_rhs(kkj * kkj, bd)
        kkn = kkj / jnp.maximum(jnp.sqrt(ssq), 1e-12)
        nkk_out[:, sl] = -kkn
        b_out[:, sl] = kkn * a[:, sl]
        bonus_out[:, sl] = _dot_exact_rhs(rkr[:, sl], bd) * v[:, sl]
    r_out[...] = r
    k_out[...] = kp
    v_out[...] = v.astype(BF16)

    pg = proj[:, COL_GLA:COL_GLA + GLA_MAIN]
    gq_out[...] = pg[:, 0:GLA_KW] * (GLA_DK ** -0.5)
    gk_out[...] = pg[:, GLA_KW:2 * GLA_KW]
    gv_out[...] = pg[:, 2 * GLA_KW:2 * GLA_KW + GLA_VW].astype(BF16)
    gg = pg[:, 2 * GLA_KW + GLA_VW:GLA_MAIN]
    gsg_out[...] = gg * _sigmoid(gg)
    adg = proj[:, COL_GATE + 2 * d_model:COL_GATE + 2 * d_model + LANES]
    la = _dot(adg, ga2_ref[...]) + gab_ref[...]
    gla_out[...] = -_softplus(-la) / GLA_TAU

    sg = _sigmoid(proj[:, COL_GATE:COL_GATE + 2 * d_model])
    sgr_out[...] = sg[:, :d_model]
    sgg_out[...] = sg[:, d_model:]


def _inproj_call(x2, mod3, wp, small, bsz, seq, d_model):
    tm = 256
    nt = seq // tm
    n_tok = bsz * seq
    (mu, w0, w2p, a0, a2p, g2, k_k, k_a, r_k, ga2p, gab, bd) = small

    def tok(n):
        return pl.BlockSpec((tm, n), lambda b, t: (b * nt + t, 0))

    def full(a):
        return pl.BlockSpec(a.shape, lambda b, t: (0,) * a.ndim)

    def modspec(idx):
        return pl.BlockSpec((None, 1, d_model), lambda b, t: (b, 0, idx))

    outs = [
        (RW_WIDTH, F32), (RW_WIDTH, F32), (RW_WIDTH, BF16), (RW_WIDTH, F32), (RW_WIDTH, F32),
        (RW_WIDTH, F32), (RW_WIDTH, F32), (RW_WIDTH, F32),
        (GLA_KW, F32), (GLA_KW, F32), (GLA_VW, BF16), (GLA_VW, F32), (GLA_KW, F32),
        (d_model, F32), (d_model, F32),
    ]
    return pl.pallas_call(
        functools.partial(_inproj_kernel, tm=tm, d_model=d_model),
        grid=(bsz, nt),
        in_specs=[tok(d_model), modspec(0), modspec(1), full(wp), full(mu), full(w0), full(w2p),
                  full(a0), full(a2p), full(g2), full(k_k), full(k_a), full(r_k), full(ga2p),
                  full(gab), full(bd)],
        out_specs=[tok(n) for n, _ in outs],
        out_shape=[jax.ShapeDtypeStruct((n_tok, n), dt) for n, dt in outs],
        scratch_shapes=[pltpu.VMEM((1, RW_COLS), F32)],
        compiler_params=pltpu.CompilerParams(dimension_semantics=("arbitrary", "arbitrary"),
                                             vmem_limit_bytes=VMEM_LIMIT),
        name="inproj",
    )(x2, mod3, mod3, wp, mu, w0, w2p, a0, a2p, g2, k_k, k_a, r_k, ga2p, gab, bd)


def _rwkv_kernel(r_ref, k_ref, v_ref, lw_ref, nkk_ref, b_ref, g_ref, bonus_ref, gng_ref, gnb_ref,
                 bd_ref, o_ref, h_ref, rp_s, yl_s, m_s, hl_s, *, tb, unroll):
    c = CHUNK
    nchunk = tb // c
    npair = RW_WIDTH // LANES
    t = pl.program_id(1)

    @pl.when(t == 0)
    def _():
        h_ref[...] = jnp.zeros_like(h_ref)

    lane = lax.broadcasted_iota(jnp.int32, (1, LANES), 1)
    masks = ((lane < RW_HEAD).astype(F32), (lane >= RW_HEAD).astype(F32))
    ri = lax.broadcasted_iota(jnp.int32, (c, c), 0)
    ci_ = lax.broadcasted_iota(jnp.int32, (c, c), 1)
    strict = ci_ < ri
    incl = ci_ <= ri
    ltri = incl.astype(BF16)
    eye_c = (ci_ == ri).astype(F32)
    r2 = lax.broadcasted_iota(jnp.int32, (LANES, LANES), 0)
    c2 = lax.broadcasted_iota(jnp.int32, (LANES, LANES), 1)
    blockdiag = (r2 // RW_HEAD) == (c2 // RW_HEAD)
    eye_l = r2 == c2
    bd = bd_ref[...]

    def local_group(gi, carry):
        units = [(u, p) for u in range(unroll) for p in range(npair)]
        heads = [(s, h) for s in range(len(units)) for h in range(2)]
        idxs = [gi * unroll + u for u in range(unroll)]
        rows = [pl.ds(pl.multiple_of(i * c, c), c) for i in idxs]

        def ld(ref, u, p):
            return ref[rows[u], p * LANES:(p + 1) * LANES]

        lw = [ld(lw_ref, u, p) for u, p in units]
        gcum = [_dot_exact_lhs(ltri, x) for x in lw]
        gend, rt, at, kt, bt, bh, kh, v = [], [], [], [], [], [], [], []
        for s, (u, p) in enumerate(units):
            g_ = gcum[s]
            ge = g_[c - 1:c, :]
            r = ld(r_ref, u, p)
            k = ld(k_ref, u, p)
            nkk = ld(nkk_ref, u, p)
            b = ld(b_ref, u, p)
            e_neg = jnp.exp(-g_)
            dk = jnp.exp(ge - g_)
            gend.append(ge)
            rt.append(r * jnp.exp(g_))
            at.append(nkk * jnp.exp(g_ - lw[s]))
            kt.append(k * e_neg)
            bt.append(b * e_neg)
            bh.append(b * dk)
            kh.append(k * dk)
            v.append(ld(v_ref, u, p))

        lhs = [jnp.concatenate([at[s] * masks[0], at[s] * masks[1],
                                rt[s] * masks[0], rt[s] * masks[1]], axis=0).astype(BF16)
               for s in range(len(units))]
        xb = [_dot_nt(lhs[s], bt[s]) for s in range(len(units))]
        xk = [_dot_nt(lhs[s], kt[s]) for s in range(len(units))]

        a_ab = [jnp.where(strict, xb[s][h * c:(h + 1) * c], 0.0) for s, h in heads]
        a_ak = [jnp.where(strict, xk[s][h * c:(h + 1) * c], 0.0) for s, h in heads]
        a_rb = [jnp.where(incl, xb[s][(2 + h) * c:(3 + h) * c], 0.0) for s, h in heads]
        a_rk = [jnp.where(incl, xk[s][(2 + h) * c:(3 + h) * c], 0.0) for s, h in heads]

        tinv = [eye_c + a for a in a_ab]
        pw = a_ab
        for _ in range(5):
            pw = [_dot(x, x) for x in pw]
            tinv = [tm_ + _dot(tm_, x) for tm_, x in zip(tinv, pw)]

        ap_h = [_dot(tinv[i], at[s] * masks[h]) for i, (s, h) in enumerate(heads)]
        akv = [_dot(a_ak[i], v[s]) for i, (s, h) in enumerate(heads)]
        ul_h = [_dot(tinv[i], akv[i]) for i in range(len(heads))]
        rb_ap = [_dot(a_rb[i], ap_h[i]) for i in range(len(heads))]
        rk_v = [_dot(a_rk[i], v[s]) for i, (s, h) in enumerate(heads)]
        rb_ul = [_dot(a_rb[i], ul_h[i]) for i in range(len(heads))]

        ap, uloc = [], []
        for s, (u, p) in enumerate(units):
            i0, i1 = 2 * s, 2 * s + 1
            lanes = slice(p * LANES, (p + 1) * LANES)
            ap.append(ap_h[i0] + ap_h[i1])
            uloc.append(masks[0] * ul_h[i0] + masks[1] * ul_h[i1])
            rp_s[idxs[u], :, lanes] = rt[s] + rb_ap[i0] + rb_ap[i1]
            yl_s[rows[u], lanes] = (masks[0] * (rb_ul[i0] + rk_v[i0])
                                    + masks[1] * (rb_ul[i1] + rk_v[i1]))

        bh_t = [x.T.astype(BF16) for x in bh]
        kh_t = [x.T.astype(BF16) for x in kh]
        m1 = [_dot(bh_t[s], ap[s]) for s in range(len(units))]
        h1 = [_dot(bh_t[s], uloc[s]) for s in range(len(units))]
        h2 = [_dot(kh_t[s], v[s]) for s in range(len(units))]
        for s, (u, p) in enumerate(units):
            m_s[idxs[u], p] = (jnp.where(eye_l, jnp.exp(gend[s]), 0.0)
                               + jnp.where(blockdiag, m1[s], 0.0))
            hl_s[idxs[u], p] = jnp.where(blockdiag, h1[s] + h2[s], 0.0)
        return carry

    lax.fori_loop(0, nchunk // unroll, local_group, 0)

    hstate = [h_ref[p] for p in range(npair)]
    for ci in range(nchunk):
        rows = slice(ci * c, (ci + 1) * c)
        rp = rp_s[ci]
        ys = [_dot(rp[:, p * LANES:(p + 1) * LANES], hstate[p]) for p in range(npair)]
        hn = [_dot(m_s[ci, p], hstate[p]) for p in range(npair)]
        for p in range(npair):
            lanes = slice(p * LANES, (p + 1) * LANES)
            yl_s[rows, lanes] = ys[p] + yl_s[rows, lanes]
            hstate[p] = hn[p] + hl_s[ci, p]
    for p in range(npair):
        h_ref[p] = hstate[p]

    for p in range(npair):
        lanes = slice(p * LANES, (p + 1) * LANES)
        y = yl_s[:, lanes]
        mean = _dot_exact_rhs(y, bd) * (1.0 / RW_HEAD)
        yc = y - mean
        var = _dot_exact_rhs(yc * yc, bd) * (1.0 / RW_HEAD)
        yn = yc * lax.rsqrt(var + RW_GN_EPS)
        out = (yn * gng_ref[:, lanes] + gnb_ref[:, lanes] + bonus_ref[:, lanes]) * g_ref[:, lanes]
        o_ref[:, lanes] = out.astype(o_ref.dtype)


def _rwkv_call(r, k, v, lw, nkk, b, g, bonus, gng, gnb, bd, bsz, seq):
    tb = REC_TILE
    nt = seq // tb
    n_tok = bsz * seq
    npair = RW_WIDTH // LANES
    nchunk = tb // CHUNK

    def tok():
        return pl.BlockSpec((tb, RW_WIDTH), lambda bb, t: (bb * nt + t, 0))

    def full(a):
        return pl.BlockSpec(a.shape, lambda bb, t: (0,) * a.ndim)

    return pl.pallas_call(
        functools.partial(_rwkv_kernel, tb=tb, unroll=RWKV_UNROLL),
        grid=(bsz, nt),
        in_specs=[tok() for _ in range(8)] + [full(gng), full(gnb), full(bd)],
        out_specs=tok(),
        out_shape=jax.ShapeDtypeStruct((n_tok, RW_WIDTH), BF16),
        scratch_shapes=[pltpu.VMEM((npair, LANES, LANES), F32),
                        pltpu.VMEM((nchunk, CHUNK, RW_WIDTH), F32),
                        pltpu.VMEM((tb, RW_WIDTH), F32),
                        pltpu.VMEM((nchunk, npair, LANES, LANES), F32),
                        pltpu.VMEM((nchunk, npair, LANES, LANES), F32)],
        compiler_params=pltpu.CompilerParams(
            dimension_semantics=("arbitrary", "arbitrary"),
            vmem_limit_bytes=VMEM_LIMIT),
        name="rwkv",
    )(r, k, v, lw, nkk, b, g, bonus, gng, gnb, bd)


def _gla_kernel(q_ref, k_ref, v_ref, sg_ref, la_ref, ng_ref, o_ref, st_ref, qs_s, oi_s, kv_s, eb_s,
                *, tb, unroll):
    c = CHUNK
    nchunk = tb // c
    t = pl.program_id(1)

    @pl.when(t == 0)
    def _():
        st_ref[...] = jnp.zeros_like(st_ref)

    lane = lax.broadcasted_iota(jnp.int32, (1, LANES), 1)
    hmask = ((lane < GLA_DK).astype(F32), (lane >= GLA_DK).astype(F32))
    ri = lax.broadcasted_iota(jnp.int32, (c, c), 0)
    ci_ = lax.broadcasted_iota(jnp.int32, (c, c), 1)
    causal = ci_ <= ri
    ltri = causal.astype(BF16)

    def local_group(gi, carry):
        units = [(u, h) for u in range(unroll) for h in range(GLA_HEADS)]
        idxs = [gi * unroll + u for u in range(unroll)]
        rows = [pl.ds(pl.multiple_of(i * c, c), c) for i in idxs]

        def qk_lanes(h):
            return slice((h // 2) * LANES, (h // 2 + 1) * LANES)

        la = [la_ref[rows[u], qk_lanes(h)] for u, h in units]
        bcum = [_dot_exact_lhs(ltri, x) for x in la]
        q_s, k_s, kdec, v = [], [], [], []
        for s, (u, h) in enumerate(units):
            bc = bcum[s]
            blast = bc[c - 1:c, :]
            q = q_ref[rows[u], qk_lanes(h)]
            k = k_ref[rows[u], qk_lanes(h)]
            hm = hmask[h % 2]
            q_s.append((q * jnp.exp(bc) * hm).astype(BF16))
            k_s.append(k * jnp.exp(-bc))
            kdec.append(k * jnp.exp(blast - bc) * hm)
            v.append(v_ref[rows[u], h * LANES:(h + 1) * LANES])
            eb_s[idxs[u], h] = jnp.exp(blast)
        att = [jnp.where(causal, _dot_nt(q_s[s], k_s[s]), 0.0) for s in range(len(units))]
        oi = [_dot(att[s], v[s]) for s in range(len(units))]
        v_t = [x.astype(F32).T.astype(BF16) for x in v]
        kv = [_dot(v_t[s], kdec[s]) for s in range(len(units))]
        for s, (u, h) in enumerate(units):
            qs_s[idxs[u], h] = q_s[s]
            oi_s[rows[u], h * LANES:(h + 1) * LANES] = oi[s]
            kv_s[idxs[u], h] = kv[s]
        return carry

    lax.fori_loop(0, nchunk // unroll, local_group, 0)

    st = [st_ref[h] for h in range(GLA_HEADS)]
    for ci in range(nchunk):
        rows = slice(ci * c, (ci + 1) * c)
        inter = [_dot_nt(qs_s[ci, h], st[h]) for h in range(GLA_HEADS)]
        for h in range(GLA_HEADS):
            lanes = slice(h * LANES, (h + 1) * LANES)
            oi_s[rows, lanes] = oi_s[rows, lanes] + inter[h]
            st[h] = st[h] * eb_s[ci, h] + kv_s[ci, h]
    for h in range(GLA_HEADS):
        st_ref[h] = st[h]

    ng = ng_ref[...]
    for h in range(GLA_HEADS):
        lanes = slice(h * LANES, (h + 1) * LANES)
        o = oi_s[:, lanes]
        o = o * lax.rsqrt(jnp.mean(o * o, axis=-1, keepdims=True) + GLA_NORM_EPS) * ng
        o_ref[:, lanes] = (o * sg_ref[:, lanes]).astype(o_ref.dtype)


def _gla_call(q, k, v, sg, la, ng, bsz, seq):
    tb = REC_TILE
    nt = seq // tb
    n_tok = bsz * seq
    nchunk = tb // CHUNK

    def tok(n):
        return pl.BlockSpec((tb, n), lambda bb, t: (bb * nt + t, 0))

    return pl.pallas_call(
        functools.partial(_gla_kernel, tb=tb, unroll=GLA_UNROLL),
        grid=(bsz, nt),
        in_specs=[tok(GLA_KW), tok(GLA_KW), tok(GLA_VW), tok(GLA_VW), tok(GLA_KW),
                  pl.BlockSpec((1, LANES), lambda bb, t: (0, 0))],
        out_specs=tok(GLA_VW),
        out_shape=jax.ShapeDtypeStruct((n_tok, GLA_VW), BF16),
        scratch_shapes=[pltpu.VMEM((GLA_HEADS, GLA_DV, LANES), F32),
                        pltpu.VMEM((nchunk, GLA_HEADS, CHUNK, LANES), BF16),
                        pltpu.VMEM((tb, GLA_VW), F32),
                        pltpu.VMEM((nchunk, GLA_HEADS, GLA_DV, LANES), F32),
                        pltpu.VMEM((nchunk, GLA_HEADS, 1, LANES), F32)],
        compiler_params=pltpu.CompilerParams(
            dimension_semantics=("arbitrary", "arbitrary"),
            vmem_limit_bytes=VMEM_LIMIT),
        name="gla",
    )(q, k, v, sg, la, ng)


def _tail_kernel(x_ref, orw_ref, ogla_ref, sgr_ref, sgg_ref, gate1_ref, shift2_ref, scale2_ref,
                 gate2_ref, wrb_ref, wgb_ref, wmix_ref, win_ref, wout_ref, ln1g_ref, ln1b_ref,
                 ln2g_ref, ln2b_ref, o_ref, *, d_ff):
    merged = (sgr_ref[...] * jnp.dot(orw_ref[...], wrb_ref[...], preferred_element_type=F32)
              + sgg_ref[...] * jnp.dot(ogla_ref[...], wgb_ref[...], preferred_element_type=F32))
    mix = _dot(merged, wmix_ref[...])
    x1 = _layer_norm(ALPHA * x_ref[...] + gate1_ref[...] * mix, LN_EPS) * ln1g_ref[...] + ln1b_ref[...]

    u = x1 * (1.0 + scale2_ref[...]) + shift2_ref[...]
    h = _dot(u, win_ref[...])
    hg = h[:, :d_ff]
    act = hg * _sigmoid(hg) * h[:, d_ff:]
    ffn = _dot(act, wout_ref[...])
    y = _layer_norm(ALPHA * x1 + gate2_ref[...] * ffn, LN_EPS)
    o_ref[...] = y * ln2g_ref[...] + ln2b_ref[...]


def _tail_call(x2, o_rw, o_gla, sgr, sgg, mod3, wrb, wgb, wmix, win, wout, ln1g, ln1b, ln2g, ln2b,
               bsz, seq, d_model):
    tm = 256
    nt = seq // tm
    n_tok = bsz * seq
    d_ff = wout.shape[0]

    def tok(n):
        return pl.BlockSpec((tm, n), lambda b, t: (b * nt + t, 0))

    def const(a):
        return pl.BlockSpec(a.shape, lambda b, t: (0,) * a.ndim, pipeline_mode=pl.Buffered(1))

    def modspec(idx):
        return pl.BlockSpec((None, 1, d_model), lambda b, t: (b, 0, idx))

    return pl.pallas_call(
        functools.partial(_tail_kernel, d_ff=d_ff),
        grid=(bsz, nt),
        in_specs=[tok(d_model), tok(RW_WIDTH), tok(GLA_VW), tok(d_model), tok(d_model),
                  modspec(2), modspec(3), modspec(4), modspec(5),
                  const(wrb), const(wgb), const(wmix), const(win), const(wout),
                  const(ln1g), const(ln1b), const(ln2g), const(ln2b)],
        out_specs=tok(d_model),
        out_shape=jax.ShapeDtypeStruct((n_tok, d_model), F32),
        compiler_params=pltpu.CompilerParams(dimension_semantics=("arbitrary", "arbitrary"),
                                             vmem_limit_bytes=VMEM_LIMIT),
        name="tail",
    )(x2, o_rw, o_gla, sgr, sgg, mod3, mod3, mod3, mod3, wrb, wgb, wmix, win, wout,
      ln1g, ln1b, ln2g, ln2b)


def kernel(x, c, w_ada, b_ada, w_in, mu_rw, rw_w0, rw_w2, rw_a0, rw_a2, rw_g2, rw_k_k, rw_k_a,
           rw_r_k, rw_gn_g, rw_gn_b, gla_a2, gla_a_b, gla_norm_g, w_rw_branch, w_gla_branch,
           w_mix_out, ln1_g, ln1_b, w_ffn_in, w_ffn_out, ln2_g, ln2_b):
    bsz, seq, d_model = x.shape
    assert w_ada.shape[0] == DEPTH and seq % REC_TILE == 0
    n_tok = bsz * seq
    l = 0

    wi = w_in[l]
    gla_ad_cols = wi[:, RW_COLS + GLA_MAIN:RW_COLS + GLA_COLS]
    wp = jnp.concatenate(
        [wi[:, :RW_COLS + GLA_MAIN], wi[:, RW_COLS + GLA_COLS:], gla_ad_cols,
         jnp.zeros((d_model, LANES - GLA_GATE_LORA), wi.dtype)], axis=1).astype(BF16)
    zeros_lora = jnp.zeros((DECAY_LORA, RW_WIDTH), F32)
    w2p = jnp.concatenate([rw_w2[l], zeros_lora], axis=0).astype(BF16)
    a2p = jnp.concatenate([zeros_lora, rw_a2[l]], axis=0).astype(BF16)
    ga2p = jnp.concatenate(
        [gla_a2[l], jnp.zeros((LANES - GLA_GATE_LORA, GLA_KW), F32)], axis=0).astype(BF16)
    row = lambda a: a.reshape(1, -1)
    hid = jnp.arange(LANES) // RW_HEAD
    bd = (hid[:, None] == hid[None, :]).astype(BF16)
    small = (row(mu_rw[l]), row(rw_w0[l]), w2p, row(rw_a0[l]), a2p, rw_g2[l].astype(BF16),
             row(rw_k_k[l]), row(rw_k_a[l]), row(rw_r_k[l]), ga2p, row(gla_a_b[l]), bd)

    x2 = x.reshape(n_tok, d_model)
    mod = _mod_call(c, w_ada[l], b_ada[l])
    mod3 = mod.reshape(bsz, 1, 6 * d_model)

    (r, k, v, lw, nkk, b, g, bonus, gq, gk, gv, gsg, gla, sgr, sgg) = _inproj_call(
        x2, mod3, wp, small, bsz, seq, d_model)

    o_rw = _rwkv_call(r, k, v, lw, nkk, b, g, bonus, row(rw_gn_g[l]), row(rw_gn_b[l]), bd, bsz, seq)
    o_gla = _gla_call(gq, gk, gv, gsg, gla, row(gla_norm_g[l]), bsz, seq)

    out = _tail_call(x2, o_rw, o_gla, sgr, sgg, mod3, w_rw_branch[l].astype(BF16),
                     w_gla_branch[l].astype(BF16), w_mix_out[l].astype(BF16),
                     w_ffn_in[l].astype(BF16), w_ffn_out[l].astype(BF16),
                     row(ln1_g[l]), row(ln1_b[l]), row(ln2_g[l]), row(ln2_b[l]), bsz, seq, d_model)
    return out.reshape(bsz, seq, d_model)
```

```python
import functools

import jax
import jax.numpy as jnp
from jax import lax
from jax.experimental import pallas as pl
from jax.experimental.pallas import tpu as pltpu

F32 = jnp.float32
BF16 = jnp.bfloat16

RW_HEAD = 64
RW_HEADS = 8
RW_WIDTH = RW_HEADS * RW_HEAD
DECAY_LORA = 64
AAA_LORA = 64
GATE_LORA = 128
RW_GN_EPS = 64e-5
RW_COLS = 3 * RW_WIDTH + DECAY_LORA + AAA_LORA + GATE_LORA
GLA_HEADS = 4
GLA_DK = 64
GLA_DV = 128
GLA_KW = GLA_HEADS * GLA_DK
GLA_VW = GLA_HEADS * GLA_DV
GLA_GATE_LORA = 16
GLA_TAU = 16.0
GLA_NORM_EPS = 1e-5
GLA_MAIN = 2 * GLA_KW + 2 * GLA_VW
GLA_COLS = GLA_MAIN + GLA_GATE_LORA
LN_EPS = 1e-5
DEPTH = 1
ALPHA = (2.0 * DEPTH) ** 0.25

LANES = 128
MXU_K = 256
CHUNK = 64
RWKV_UNROLL = 8
RW_GROUP = 4
GLA_UNROLL = 2
REC_TILE = 512
TAIL_TILE = 512
FFN_CHUNKS = 2
VMEM_LIMIT = 58 * 1024 * 1024

COL_RW = 0
COL_GLA = RW_COLS
COL_GATE = COL_GLA + GLA_MAIN


def _sigmoid(x):
    return 1.0 / (1.0 + jnp.exp(-x))


def _softplus(x):
    return jnp.maximum(x, 0.0) + jnp.log1p(jnp.exp(-jnp.abs(x)))


def _dot(a, b):
    return jnp.dot(a.astype(BF16), b.astype(BF16), preferred_element_type=F32)


def _dot_nt(a, b):
    return lax.dot_general(a.astype(BF16), b.astype(BF16), (((1,), (1,)), ((), ())),
                           preferred_element_type=F32)


def _split_hi_lo(x):
    hi = x.astype(BF16)
    lo = (x - hi.astype(F32)).astype(BF16)
    return hi, lo


def _dot_exact_rhs(x, w01):
    hi, lo = _split_hi_lo(x)
    if 2 * x.shape[1] <= MXU_K:
        return jnp.dot(jnp.concatenate([hi, lo], axis=1), jnp.concatenate([w01, w01], axis=0),
                       preferred_element_type=F32)
    return (jnp.dot(hi, w01, preferred_element_type=F32)
            + jnp.dot(lo, w01, preferred_element_type=F32))


def _dot_exact_lhs(w01, x):
    hi, lo = _split_hi_lo(x)
    if 2 * x.shape[0] <= MXU_K:
        return jnp.dot(jnp.concatenate([w01, w01], axis=1), jnp.concatenate([hi, lo], axis=0),
                       preferred_element_type=F32)
    return (jnp.dot(w01, hi, preferred_element_type=F32)
            + jnp.dot(w01, lo, preferred_element_type=F32))


def _layer_norm(x, eps):
    mu = jnp.mean(x, axis=-1, keepdims=True)
    xc = x - mu
    var = jnp.mean(xc * xc, axis=-1, keepdims=True)
    return xc * lax.rsqrt(var + eps)


def _mod_kernel(c_ref, w_ref, b_ref, o_ref):
    c = c_ref[...]
    sc = c * _sigmoid(c)
    o_ref[...] = _dot(sc, w_ref[...]) + b_ref[...]


def _mod_call(c, w_ada, b_ada):
    bsz, d = c.shape
    n = w_ada.shape[1]
    tn = 1536
    return pl.pallas_call(
        _mod_kernel,
        grid=(n // tn,),
        in_specs=[pl.BlockSpec((bsz, d), lambda j: (0, 0)),
                  pl.BlockSpec((d, tn), lambda j: (0, j)),
                  pl.BlockSpec((1, tn), lambda j: (0, j))],
        out_specs=pl.BlockSpec((bsz, tn), lambda j: (0, j)),
        out_shape=jax.ShapeDtypeStruct((bsz, n), F32),
        compiler_params=pltpu.CompilerParams(dimension_semantics=("arbitrary",),
                                             vmem_limit_bytes=VMEM_LIMIT),
        name="mod",
    )(c, w_ada, b_ada.reshape(1, n))


def _wprep_kernel(w_ref, o_ref, *, d_model):
    main = RW_COLS + GLA_MAIN
    w = w_ref[...]
    o_ref[:, :main] = w[:, :main].astype(BF16)
    o_ref[:, main:main + 2 * d_model] = w[:, RW_COLS + GLA_COLS:].astype(BF16)
    pad = jnp.zeros((w.shape[0], LANES - GLA_GATE_LORA), F32)
    o_ref[:, main + 2 * d_model:] = jnp.concatenate(
        [w[:, main:RW_COLS + GLA_COLS], pad], axis=1).astype(BF16)


def _wprep_call(w_in, d_model):
    rows = 128
    n_in = w_in.shape[1]
    n_out = RW_COLS + GLA_MAIN + 2 * d_model + LANES
    return pl.pallas_call(
        functools.partial(_wprep_kernel, d_model=d_model),
        grid=(d_model // rows,),
        in_specs=[pl.BlockSpec((rows, n_in), lambda i: (i, 0))],
        out_specs=pl.BlockSpec((rows, n_out), lambda i: (i, 0)),
        out_shape=jax.ShapeDtypeStruct((d_model, n_out), BF16),
        compiler_params=pltpu.CompilerParams(dimension_semantics=("arbitrary",),
                                             vmem_limit_bytes=VMEM_LIMIT),
        name="wprep",
    )(w_in)


def _inproj_kernel(x_ref, shift_ref, scale_ref, wp_ref, mu_ref, w0_ref, w2_ref, a0_ref, a2_ref,
                   g2_ref, kk_ref, ka_ref, rk_ref, ga2_ref, gab_ref, bd_ref,
                   r_out, k_out, v_out, lw_out, nkk_out, b_out, g_out, bonus_out,
                   gq_out, gk_out, gv_out, gsg_out, gla_out, sgr_out, sgg_out,
                   carry_ref, *, tm, d_model):
    t = pl.program_id(1)

    @pl.when(t == 0)
    def _():
        carry_ref[...] = jnp.zeros_like(carry_ref)

    u = x_ref[...] * (1.0 + scale_ref[...]) + shift_ref[...]
    proj = _dot(u, wp_ref[...])

    p = proj[:, COL_RW:COL_RW + RW_COLS]
    prev = pltpu.roll(p, 1, 0)
    row = lax.broadcasted_iota(jnp.int32, p.shape, 0)
    prev = jnp.where(row == 0, carry_ref[...], prev)
    carry_ref[...] = p[tm - 1:tm, :]
    ps = p + mu_ref[...] * (prev - p)

    r = ps[:, 0:RW_WIDTH]
    k = ps[:, RW_WIDTH:2 * RW_WIDTH]
    v = ps[:, 2 * RW_WIDTH:3 * RW_WIDTH]
    z = ps[:, 3 * RW_WIDTH:3 * RW_WIDTH + LANES]
    gd = ps[:, 3 * RW_WIDTH + LANES:RW_COLS]

    wlin = w0_ref[...] + _dot(jnp.tanh(z), w2_ref[...])
    w = -_softplus(-wlin) - 0.5
    lw_out[...] = -jnp.exp(w)
    a = _sigmoid(a0_ref[...] + _dot(z, a2_ref[...]))
    g_out[...] = _dot(_sigmoid(gd), g2_ref[...])
    kkv = k * kk_ref[...]
    kp = k * (1.0 + (a - 1.0) * ka_ref[...])
    rkr = r * kp * rk_ref[...]
    bd = bd_ref[...]
    for j in range(RW_WIDTH // LANES):
        sl = slice(j * LANES, (j + 1) * LANES)
        kkj = kkv[:, sl]
        ssq = _dot_exact_rhs(kkj * kkj, bd)
        kkn = kkj / jnp.maximum(jnp.sqrt(ssq), 1e-12)
        nkk_out[:, sl] = -kkn
        b_out[:, sl] = kkn * a[:, sl]
        bonus_out[:, sl] = _dot_exact_rhs(rkr[:, sl], bd) * v[:, sl]
    r_out[...] = r
    k_out[...] = kp
    v_out[...] = v.astype(BF16)

    pg = proj[:, COL_GLA:COL_GLA + GLA_MAIN]
    gq_out[...] = pg[:, 0:GLA_KW] * (GLA_DK ** -0.5)
    gk_out[...] = pg[:, GLA_KW:2 * GLA_KW]
    gv_out[...] = pg[:, 2 * GLA_KW:2 * GLA_KW + GLA_VW].astype(BF16)
    gg = pg[:, 2 * GLA_KW + GLA_VW:GLA_MAIN]
    gsg_out[...] = gg * _sigmoid(gg)
    adg = proj[:, COL_GATE + 2 * d_model:COL_GATE + 2 * d_model + LANES]
    la = _dot(adg, ga2_ref[...]) + gab_ref[...]
    gla_out[...] = -_softplus(-la) / GLA_TAU

    sg = _sigmoid(proj[:, COL_GATE:COL_GATE + 2 * d_model])
    sgr_out[...] = sg[:, :d_model]
    sgg_out[...] = sg[:, d_model:]


def _inproj_call(x2, mod3, wp, small, bsz, seq, d_model):
    tm = 256
    nt = seq // tm
    n_tok = bsz * seq
    (mu, w0, w2p, a0, a2p, g2, k_k, k_a, r_k, ga2p, gab, bd) = small

    def tok(n):
        return pl.BlockSpec((tm, n), lambda b, t: (b * nt + t, 0))

    def full(a):
        return pl.BlockSpec(a.shape, lambda b, t: (0,) * a.ndim)

    def modspec(idx):
        return pl.BlockSpec((None, 1, d_model), lambda b, t: (b, 0, idx))

    outs = [
        (RW_WIDTH, F32), (RW_WIDTH, F32), (RW_WIDTH, BF16), (RW_WIDTH, F32), (RW_WIDTH, F32),
        (RW_WIDTH, F32), (RW_WIDTH, F32), (RW_WIDTH, F32),
        (GLA_KW, F32), (GLA_KW, F32), (GLA_VW, BF16), (GLA_VW, F32), (GLA_KW, F32),
        (d_model, F32), (d_model, F32),
    ]
    return pl.pallas_call(
        functools.partial(_inproj_kernel, tm=tm, d_model=d_model),
        grid=(bsz, nt),
        in_specs=[tok(d_model), modspec(0), modspec(1), full(wp), full(mu), full(w0), full(w2p),
                  full(a0), full(a2p), full(g2), full(k_k), full(k_a), full(r_k), full(ga2p),
                  full(gab), full(bd)],
        out_specs=[tok(n) for n, _ in outs],
        out_shape=[jax.ShapeDtypeStruct((n_tok, n), dt) for n, dt in outs],
        scratch_shapes=[pltpu.VMEM((1, RW_COLS), F32)],
        compiler_params=pltpu.CompilerParams(dimension_semantics=("arbitrary", "arbitrary"),
                                             vmem_limit_bytes=VMEM_LIMIT),
        name="inproj",
    )(x2, mod3, mod3, wp, mu, w0, w2p, a0, a2p, g2, k_k, k_a, r_k, ga2p, gab, bd)


def _rwkv_kernel(r_ref, k_ref, v_ref, lw_ref, nkk_ref, b_ref, g_ref, bonus_ref, gng_ref, gnb_ref,
                 bd_ref, o_ref, h_ref, rp_s, yl_s, m_s, hl_s, *, tb, unroll):
    c = CHUNK
    gw = RW_GROUP * RW_HEAD
    nchunk = tb // c
    ngroup = RW_WIDTH // gw
    t = pl.program_id(1)

    @pl.when(t == 0)
    def _():
        h_ref[...] = jnp.zeros_like(h_ref)

    lane = lax.broadcasted_iota(jnp.int32, (c, gw), 1)
    rowi = lax.broadcasted_iota(jnp.int32, (c, gw), 0)
    head_of_lane = lane // RW_HEAD
    lmask = [head_of_lane == h for h in range(RW_GROUP)]
    strict = (lane % RW_HEAD) < rowi
    incl = (lane % RW_HEAD) <= rowi
    eye = (lane % RW_HEAD) == rowi
    eye_f = eye.astype(F32)
    eye_b = eye.astype(BF16)
    ri = lax.broadcasted_iota(jnp.int32, (c, c), 0)
    ci_ = lax.broadcasted_iota(jnp.int32, (c, c), 1)
    ltri = (ci_ <= ri).astype(BF16)

    def stack(x):
        xb = x.astype(BF16)
        zero = jnp.zeros_like(xb)
        return jnp.concatenate([jnp.where(lmask[h], xb, zero) for h in range(RW_GROUP)], axis=0)

    def local_group(gi, carry):
        units = [(u, g) for u in range(unroll) for g in range(ngroup)]
        n = len(units)
        idxs = [gi * unroll + u for u in range(unroll)]
        rows = [pl.ds(pl.multiple_of(i * c, c), c) for i in idxs]

        def ld(ref, u, g):
            return ref[rows[u], g * gw:(g + 1) * gw]

        lw = [ld(lw_ref, u, g) for u, g in units]
        gcum = [_dot_exact_lhs(ltri, x) for x in lw]
        gend, rt, at, kt, bt, bh, kh, v = [], [], [], [], [], [], [], []
        for s, (u, g) in enumerate(units):
            g_ = gcum[s]
            ge = g_[c - 1:c, :]
            r = ld(r_ref, u, g)
            k = ld(k_ref, u, g)
            nkk = ld(nkk_ref, u, g)
            b = ld(b_ref, u, g)
            e_neg = jnp.exp(-g_)
            dk = jnp.exp(ge - g_)
            gend.append(ge)
            rt.append(r * jnp.exp(g_))
            at.append(nkk * jnp.exp(g_ - lw[s]))
            kt.append(k * e_neg)
            bt.append(b * e_neg)
            bh.append(b * dk)
            kh.append(k * dk)
            v.append(ld(v_ref, u, g))

        lhs = [jnp.concatenate([at[s], rt[s]], axis=0).astype(BF16) for s in range(n)]
        xb = [_dot_nt(lhs[s], stack(bt[s])) for s in range(n)]
        xk = [_dot_nt(lhs[s], stack(kt[s])) for s in range(n)]
        a_ab = [jnp.where(strict, x[:c], 0.0) for x in xb]
        a_rb = [jnp.where(incl, x[c:], 0.0) for x in xb]
        a_ak = [jnp.where(strict, x[:c], 0.0) for x in xk]
        a_rk = [jnp.where(incl, x[c:], 0.0) for x in xk]

        tinv = [eye_f + a for a in a_ab]
        pw = a_ab
        for _ in range(5):
            pw = [_dot(x, stack(x)) for x in pw]
            tinv = [tm_ + _dot(tm_, stack(x)) for tm_, x in zip(tinv, pw)]

        vst = [stack(x) for x in v]
        ap = [_dot(tinv[s], stack(at[s])) for s in range(n)]
        akv = [_dot(a_ak[s], vst[s]) for s in range(n)]
        uloc = [_dot(tinv[s], stack(akv[s])) for s in range(n)]
        rb_ap = [_dot(a_rb[s], stack(ap[s])) for s in range(n)]
        rk_v = [_dot(a_rk[s], vst[s]) for s in range(n)]
        ulst = [stack(x) for x in uloc]
        rb_ul = [_dot(a_rb[s], ulst[s]) for s in range(n)]
        bht = [_dot_nt(eye_b, stack(x)) for x in bh]
        kht = [_dot_nt(eye_b, stack(x)) for x in kh]
        m1 = [_dot(bht[s], stack(ap[s])) for s in range(n)]
        h1 = [_dot(bht[s], ulst[s]) for s in range(n)]
        h2 = [_dot(kht[s], vst[s]) for s in range(n)]
        for s, (u, g) in enumerate(units):
            lanes = slice(g * gw, (g + 1) * gw)
            rp_s[idxs[u], :, lanes] = rt[s] + rb_ap[s]
            yl_s[rows[u], lanes] = rb_ul[s] + rk_v[s]
            m_s[idxs[u], g] = jnp.where(eye, jnp.exp(gend[s]), 0.0) + m1[s]
            hl_s[idxs[u], g] = h1[s] + h2[s]
        return carry

    lax.fori_loop(0, nchunk // unroll, local_group, 0)

    hstate = [h_ref[g] for g in range(ngroup)]
    for ci in range(nchunk):
        rows = slice(ci * c, (ci + 1) * c)
        rp = rp_s[ci]
        hst = [stack(hstate[g]) for g in range(ngroup)]
        ys = [_dot(rp[:, g * gw:(g + 1) * gw], hst[g]) for g in range(ngroup)]
        hn = [_dot(m_s[ci, g], hst[g]) for g in range(ngroup)]
        for g in range(ngroup):
            lanes = slice(g * gw, (g + 1) * gw)
            yl_s[rows, lanes] = ys[g] + yl_s[rows, lanes]
            hstate[g] = hn[g] + hl_s[ci, g]
    for g in range(ngroup):
        h_ref[g] = hstate[g]

    bd = bd_ref[...]
    for p in range(RW_WIDTH // LANES):
        lanes = slice(p * LANES, (p + 1) * LANES)
        y = yl_s[:, lanes]
        mean = _dot_exact_rhs(y, bd) * (1.0 / RW_HEAD)
        yc = y - mean
        var = _dot_exact_rhs(yc * yc, bd) * (1.0 / RW_HEAD)
        yn = yc * lax.rsqrt(var + RW_GN_EPS)
        out = (yn * gng_ref[:, lanes] + gnb_ref[:, lanes] + bonus_ref[:, lanes]) * g_ref[:, lanes]
        o_ref[:, lanes] = out.astype(o_ref.dtype)


def _rwkv_call(r, k, v, lw, nkk, b, g, bonus, gng, gnb, bd, bsz, seq):
    tb = REC_TILE
    nt = seq // tb
    n_tok = bsz * seq
    gw = RW_GROUP * RW_HEAD
    ngroup = RW_WIDTH // gw
    nchunk = tb // CHUNK

    def tok():
        return pl.BlockSpec((tb, RW_WIDTH), lambda bb, t: (bb * nt + t, 0))

    def full(a):
        return pl.BlockSpec(a.shape, lambda bb, t: (0,) * a.ndim)

    return pl.pallas_call(
        functools.partial(_rwkv_kernel, tb=tb, unroll=RWKV_UNROLL),
        grid=(bsz, nt),
        in_specs=[tok() for _ in range(8)] + [full(gng), full(gnb), full(bd)],
        out_specs=tok(),
        out_shape=jax.ShapeDtypeStruct((n_tok, RW_WIDTH), BF16),
        scratch_shapes=[pltpu.VMEM((ngroup, RW_HEAD, gw), F32),
                        pltpu.VMEM((nchunk, CHUNK, RW_WIDTH), F32),
                        pltpu.VMEM((tb, RW_WIDTH), F32),
                        pltpu.VMEM((nchunk, ngroup, RW_HEAD, gw), F32),
                        pltpu.VMEM((nchunk, ngroup, RW_HEAD, gw), F32)],
        compiler_params=pltpu.CompilerParams(
            dimension_semantics=("arbitrary", "arbitrary"),
            vmem_limit_bytes=VMEM_LIMIT),
        name="rwkv",
    )(r, k, v, lw, nkk, b, g, bonus, gng, gnb, bd)


def _gla_kernel(q_ref, k_ref, v_ref, sg_ref, la_ref, ng_ref, o_ref, st_ref, qs_s, oi_s, kv_s, eb_s,
                *, tb, unroll):
    c = CHUNK
    nchunk = tb // c
    t = pl.program_id(1)

    @pl.when(t == 0)
    def _():
        st_ref[...] = jnp.zeros_like(st_ref)

    lane = lax.broadcasted_iota(jnp.int32, (1, LANES), 1)
    hmask = ((lane < GLA_DK).astype(F32), (lane >= GLA_DK).astype(F32))
    ri = lax.broadcasted_iota(jnp.int32, (c, c), 0)
    ci_ = lax.broadcasted_iota(jnp.int32, (c, c), 1)
    causal = ci_ <= ri
    ltri = causal.astype(BF16)

    def local_group(gi, carry):
        units = [(u, h) for u in range(unroll) for h in range(GLA_HEADS)]
        idxs = [gi * unroll + u for u in range(unroll)]
        rows = [pl.ds(pl.multiple_of(i * c, c), c) for i in idxs]

        def qk_lanes(h):
            return slice((h // 2) * LANES, (h // 2 + 1) * LANES)

        la = [la_ref[rows[u], qk_lanes(h)] for u, h in units]
        bcum = [_dot_exact_lhs(ltri, x) for x in la]
        q_s, k_s, kdec, v = [], [], [], []
        for s, (u, h) in enumerate(units):
            bc = bcum[s]
            blast = bc[c - 1:c, :]
            q = q_ref[rows[u], qk_lanes(h)]
            k = k_ref[rows[u], qk_lanes(h)]
            hm = hmask[h % 2]
            q_s.append((q * jnp.exp(bc) * hm).astype(BF16))
            k_s.append(k * jnp.exp(-bc))
            kdec.append(k * jnp.exp(blast - bc) * hm)
            v.append(v_ref[rows[u], h * LANES:(h + 1) * LANES])
            eb_s[idxs[u], h] = jnp.exp(blast)
        att = [jnp.where(causal, _dot_nt(q_s[s], k_s[s]), 0.0) for s in range(len(units))]
        oi = [_dot(att[s], v[s]) for s in range(len(units))]
        v_t = [x.astype(F32).T.astype(BF16) for x in v]
        kv = [_dot(v_t[s], kdec[s]) for s in range(len(units))]
        for s, (u, h) in enumerate(units):
            qs_s[idxs[u], h] = q_s[s]
            oi_s[rows[u], h * LANES:(h + 1) * LANES] = oi[s]
            kv_s[idxs[u], h] = kv[s]
        return carry

    lax.fori_loop(0, nchunk // unroll, local_group, 0)

    st = [st_ref[h] for h in range(GLA_HEADS)]
    for ci in range(nchunk):
        rows = slice(ci * c, (ci + 1) * c)
        inter = [_dot_nt(qs_s[ci, h], st[h]) for h in range(GLA_HEADS)]
        for h in range(GLA_HEADS):
            lanes = slice(h * LANES, (h + 1) * LANES)
            oi_s[rows, lanes] = oi_s[rows, lanes] + inter[h]
            st[h] = st[h] * eb_s[ci, h] + kv_s[ci, h]
    for h in range(GLA_HEADS):
        st_ref[h] = st[h]

    ng = ng_ref[...]
    for h in range(GLA_HEADS):
        lanes = slice(h * LANES, (h + 1) * LANES)
        o = oi_s[:, lanes]
        o = o * lax.rsqrt(jnp.mean(o * o, axis=-1, keepdims=True) + GLA_NORM_EPS) * ng
        o_ref[:, lanes] = (o * sg_ref[:, lanes]).astype(o_ref.dtype)


def _gla_call(q, k, v, sg, la, ng, bsz, seq):
    tb = REC_TILE
    nt = seq // tb
    n_tok = bsz * seq
    nchunk = tb // CHUNK

    def tok(n):
        return pl.BlockSpec((tb, n), lambda bb, t: (bb * nt + t, 0))

    return pl.pallas_call(
        functools.partial(_gla_kernel, tb=tb, unroll=GLA_UNROLL),
        grid=(bsz, nt),
        in_specs=[tok(GLA_KW), tok(GLA_KW), tok(GLA_VW), tok(GLA_VW), tok(GLA_KW),
                  pl.BlockSpec((1, LANES), lambda bb, t: (0, 0))],
        out_specs=tok(GLA_VW),
        out_shape=jax.ShapeDtypeStruct((n_tok, GLA_VW), BF16),
        scratch_shapes=[pltpu.VMEM((GLA_HEADS, GLA_DV, LANES), F32),
                        pltpu.VMEM((nchunk, GLA_HEADS, CHUNK, LANES), BF16),
                        pltpu.VMEM((tb, GLA_VW), F32),
                        pltpu.VMEM((nchunk, GLA_HEADS, GLA_DV, LANES), F32),
                        pltpu.VMEM((nchunk, GLA_HEADS, 1, LANES), F32)],
        compiler_params=pltpu.CompilerParams(
            dimension_semantics=("arbitrary", "arbitrary"),
            vmem_limit_bytes=VMEM_LIMIT),
        name="gla",
    )(q, k, v, sg, la, ng)


def _tail_kernel(x_ref, orw_ref, ogla_ref, sgr_ref, sgg_ref, gate1_ref, shift2_ref, scale2_ref,
                 gate2_ref, wrb_ref, wgb_ref, wmix_ref, win_ref, wout_ref, ln1g_ref, ln1b_ref,
                 ln2g_ref, ln2b_ref, o_ref, *, d_ff):
    merged = (sgr_ref[...] * jnp.dot(orw_ref[...], wrb_ref[...], preferred_element_type=F32)
              + sgg_ref[...] * jnp.dot(ogla_ref[...], wgb_ref[...], preferred_element_type=F32))
    mix = _dot(merged, wmix_ref[...])
    x1 = _layer_norm(ALPHA * x_ref[...] + gate1_ref[...] * mix, LN_EPS) * ln1g_ref[...] + ln1b_ref[...]

    u = (x1 * (1.0 + scale2_ref[...]) + shift2_ref[...]).astype(BF16)
    ffn = jnp.zeros_like(x1)
    ntile = d_ff // MXU_K
    bounds = [MXU_K * ((ntile * j) // FFN_CHUNKS) for j in range(FFN_CHUNKS)] + [d_ff]
    for lo, hi in zip(bounds[:-1], bounds[1:]):
        hg = jnp.dot(u, win_ref[:, lo:hi], preferred_element_type=F32)
        hu = jnp.dot(u, win_ref[:, d_ff + lo:d_ff + hi], preferred_element_type=F32)
        act = hg * _sigmoid(hg) * hu
        ffn = ffn + _dot(act, wout_ref[lo:hi, :])
    y = _layer_norm(ALPHA * x1 + gate2_ref[...] * ffn, LN_EPS)
    o_ref[...] = y * ln2g_ref[...] + ln2b_ref[...]


def _tail_call(x2, o_rw, o_gla, sgr, sgg, mod3, wrb, wgb, wmix, win, wout, ln1g, ln1b, ln2g, ln2b,
               bsz, seq, d_model):
    tm = TAIL_TILE
    nt = seq // tm
    n_tok = bsz * seq
    d_ff = wout.shape[0]
    assert d_ff % MXU_K == 0

    def tok(n):
        return pl.BlockSpec((tm, n), lambda b, t: (b * nt + t, 0))

    def const(a):
        return pl.BlockSpec(a.shape, lambda b, t: (0,) * a.ndim, pipeline_mode=pl.Buffered(1))

    def modspec(idx):
        return pl.BlockSpec((None, 1, d_model), lambda b, t: (b, 0, idx))

    return pl.pallas_call(
        functools.partial(_tail_kernel, d_ff=d_ff),
        grid=(bsz, nt),
        in_specs=[tok(d_model), tok(RW_WIDTH), tok(GLA_VW), tok(d_model), tok(d_model),
                  modspec(2), modspec(3), modspec(4), modspec(5),
                  const(wrb), const(wgb), const(wmix), const(win), const(wout),
                  const(ln1g), const(ln1b), const(ln2g), const(ln2b)],
        out_specs=tok(d_model),
        out_shape=jax.ShapeDtypeStruct((n_tok, d_model), F32),
        compiler_params=pltpu.CompilerParams(dimension_semantics=("arbitrary", "arbitrary"),
                                             vmem_limit_bytes=VMEM_LIMIT),
        name="tail",
    )(x2, o_rw, o_gla, sgr, sgg, mod3, mod3, mod3, mod3, wrb, wgb, wmix, win, wout,
      ln1g, ln1b, ln2g, ln2b)


def kernel(x, c, w_ada, b_ada, w_in, mu_rw, rw_w0, rw_w2, rw_a0, rw_a2, rw_g2, rw_k_k, rw_k_a,
           rw_r_k, rw_gn_g, rw_gn_b, gla_a2, gla_a_b, gla_norm_g, w_rw_branch, w_gla_branch,
           w_mix_out, ln1_g, ln1_b, w_ffn_in, w_ffn_out, ln2_g, ln2_b):
    bsz, seq, d_model = x.shape
    assert w_ada.shape[0] == DEPTH and seq % REC_TILE == 0
    n_tok = bsz * seq
    l = 0

    wp = _wprep_call(w_in[l], d_model)
    zeros_lora = jnp.zeros((DECAY_LORA, RW_WIDTH), F32)
    w2p = jnp.concatenate([rw_w2[l], zeros_lora], axis=0).astype(BF16)
    a2p = jnp.concatenate([zeros_lora, rw_a2[l]], axis=0).astype(BF16)
    ga2p = jnp.concatenate(
        [gla_a2[l], jnp.zeros((LANES - GLA_GATE_LORA, GLA_KW), F32)], axis=0).astype(BF16)
    row = lambda a: a.reshape(1, -1)
    hid = jnp.arange(LANES) // RW_HEAD
    bd = (hid[:, None] == hid[None, :]).astype(BF16)
    small = (row(mu_rw[l]), row(rw_w0[l]), w2p, row(rw_a0[l]), a2p, rw_g2[l].astype(BF16),
             row(rw_k_k[l]), row(rw_k_a[l]), row(rw_r_k[l]), ga2p, row(gla_a_b[l]), bd)

    x2 = x.reshape(n_tok, d_model)
    mod = _mod_call(c, w_ada[l], b_ada[l])
    mod3 = mod.reshape(bsz, 1, 6 * d_model)

    (r, k, v, lw, nkk, b, g, bonus, gq, gk, gv, gsg, gla, sgr, sgg) = _inproj_call(
        x2, mod3, wp, small, bsz, seq, d_model)

    o_rw = _rwkv_call(r, k, v, lw, nkk, b, g, bonus, row(rw_gn_g[l]), row(rw_gn_b[l]), bd, bsz, seq)
    o_gla = _gla_call(gq, gk, gv, gsg, gla, row(gla_norm_g[l]), bsz, seq)

    out = _tail_call(x2, o_rw, o_gla, sgr, sgg, mod3, w_rw_branch[l].astype(BF16),
                     w_gla_branch[l].astype(BF16), w_mix_out[l].astype(BF16),
                     w_ffn_in[l].astype(BF16), w_ffn_out[l].astype(BF16),
                     row(ln1_g[l]), row(ln1_b[l]), row(ln2_g[l]), row(ln2_b[l]), bsz, seq, d_model)
    return out.reshape(bsz, seq, d_model)
```

```python
import functools

import jax
import jax.numpy as jnp
from jax import lax
from jax.experimental import pallas as pl
from jax.experimental.pallas import tpu as pltpu

F32 = jnp.float32
BF16 = jnp.bfloat16

RW_HEAD = 64
RW_HEADS = 8
RW_WIDTH = RW_HEADS * RW_HEAD
DECAY_LORA = 64
AAA_LORA = 64
GATE_LORA = 128
RW_GN_EPS = 64e-5
RW_COLS = 3 * RW_WIDTH + DECAY_LORA + AAA_LORA + GATE_LORA
GLA_HEADS = 4
GLA_DK = 64
GLA_DV = 128
GLA_KW = GLA_HEADS * GLA_DK
GLA_VW = GLA_HEADS * GLA_DV
GLA_GATE_LORA = 16
GLA_TAU = 16.0
GLA_NORM_EPS = 1e-5
GLA_MAIN = 2 * GLA_KW + 2 * GLA_VW
GLA_COLS = GLA_MAIN + GLA_GATE_LORA
LN_EPS = 1e-5
DEPTH = 1
ALPHA = (2.0 * DEPTH) ** 0.25

LANES = 128
MXU_K = 256
CHUNK = 64
RWKV_UNROLL = 8
RW_GROUP = 4
GLA_UNROLL = 2
REC_TILE = 512
TAIL_TILE = 512
FFN_CHUNKS = 2
VMEM_LIMIT = 58 * 1024 * 1024

COL_RW = 0
COL_GLA = RW_COLS
COL_GATE = COL_GLA + GLA_MAIN


def _sigmoid(x):
    return 1.0 / (1.0 + jnp.exp(-x))


def _softplus(x):
    return jnp.maximum(x, 0.0) + jnp.log1p(jnp.exp(-jnp.abs(x)))


def _dot(a, b):
    return jnp.dot(a.astype(BF16), b.astype(BF16), preferred_element_type=F32)


def _dot_nt(a, b):
    return lax.dot_general(a.astype(BF16), b.astype(BF16), (((1,), (1,)), ((), ())),
                           preferred_element_type=F32)


def _split_hi_lo(x):
    hi = x.astype(BF16)
    lo = (x - hi.astype(F32)).astype(BF16)
    return hi, lo


def _dot_exact_rhs(x, w01):
    hi, lo = _split_hi_lo(x)
    if 2 * x.shape[1] <= MXU_K:
        return jnp.dot(jnp.concatenate([hi, lo], axis=1), jnp.concatenate([w01, w01], axis=0),
                       preferred_element_type=F32)
    return (jnp.dot(hi, w01, preferred_element_type=F32)
            + jnp.dot(lo, w01, preferred_element_type=F32))


def _dot_exact_lhs(w01, x):
    hi, lo = _split_hi_lo(x)
    if 2 * x.shape[0] <= MXU_K:
        return jnp.dot(jnp.concatenate([w01, w01], axis=1), jnp.concatenate([hi, lo], axis=0),
                       preferred_element_type=F32)
    return (jnp.dot(w01, hi, preferred_element_type=F32)
            + jnp.dot(w01, lo, preferred_element_type=F32))


def _layer_norm(x, eps):
    mu = jnp.mean(x, axis=-1, keepdims=True)
    xc = x - mu
    var = jnp.mean(xc * xc, axis=-1, keepdims=True)
    return xc * lax.rsqrt(var + eps)


def _mod_kernel(c_ref, w_ref, b_ref, o_ref):
    c = c_ref[...]
    sc = c * _sigmoid(c)
    o_ref[...] = _dot(sc, w_ref[...]) + b_ref[...]


def _mod_call(c, w_ada, b_ada):
    bsz, d = c.shape
    n = w_ada.shape[1]
    tn = 1536
    return pl.pallas_call(
        _mod_kernel,
        grid=(n // tn,),
        in_specs=[pl.BlockSpec((bsz, d), lambda j: (0, 0)),
                  pl.BlockSpec((d, tn), lambda j: (0, j)),
                  pl.BlockSpec((1, tn), lambda j: (0, j))],
        out_specs=pl.BlockSpec((bsz, tn), lambda j: (0, j)),
        out_shape=jax.ShapeDtypeStruct((bsz, n), F32),
        compiler_params=pltpu.CompilerParams(dimension_semantics=("arbitrary",),
                                             vmem_limit_bytes=VMEM_LIMIT),
        name="mod",
    )(c, w_ada, b_ada.reshape(1, n))


def _wprep_kernel(w_ref, o_ref, *, d_model):
    main = RW_COLS + GLA_MAIN
    o_ref[:main, :] = w_ref[:main, :].astype(BF16)
    o_ref[main:main + 2 * d_model, :] = w_ref[RW_COLS + GLA_COLS:, :].astype(BF16)
    o_ref[main + 2 * d_model:main + 2 * d_model + GLA_GATE_LORA, :] = (
        w_ref[main:RW_COLS + GLA_COLS, :].astype(BF16))
    o_ref[main + 2 * d_model + GLA_GATE_LORA:, :] = jnp.zeros(
        (LANES - GLA_GATE_LORA, o_ref.shape[1]), BF16)


def _wprep_call(w_in_t, d_model):
    n_in = w_in_t.shape[0]
    n_out = RW_COLS + GLA_MAIN + 2 * d_model + LANES
    cols = 256
    return pl.pallas_call(
        functools.partial(_wprep_kernel, d_model=d_model),
        grid=(d_model // cols,),
        in_specs=[pl.BlockSpec((n_in, cols), lambda i: (0, i))],
        out_specs=pl.BlockSpec((n_out, cols), lambda i: (0, i)),
        out_shape=jax.ShapeDtypeStruct((n_out, d_model), BF16),
        compiler_params=pltpu.CompilerParams(dimension_semantics=("arbitrary",),
                                             vmem_limit_bytes=VMEM_LIMIT),
        name="wprep",
    )(w_in_t)


def _inproj_kernel(x_ref, shift_ref, scale_ref, wp_ref, mu_ref, w0_ref, w2_ref, a0_ref, a2_ref,
                   g2_ref, kk_ref, ka_ref, rk_ref, ga2_ref, gab_ref, bd_ref,
                   r_out, k_out, v_out, lw_out, nkk_out, b_out, g_out, bonus_out,
                   gq_out, gk_out, gv_out, gsg_out, gla_out, sgr_out, sgg_out,
                   carry_ref, *, tm, d_model):
    t = pl.program_id(1)

    @pl.when(t == 0)
    def _():
        carry_ref[...] = jnp.zeros_like(carry_ref)

    u = x_ref[...] * (1.0 + scale_ref[...]) + shift_ref[...]
    proj = _dot_nt(u, wp_ref[...])

    p = proj[:, COL_RW:COL_RW + RW_COLS]
    prev = pltpu.roll(p, 1, 0)
    row = lax.broadcasted_iota(jnp.int32, p.shape, 0)
    prev = jnp.where(row == 0, carry_ref[...], prev)
    carry_ref[...] = p[tm - 1:tm, :]
    ps = p + mu_ref[...] * (prev - p)

    r = ps[:, 0:RW_WIDTH]
    k = ps[:, RW_WIDTH:2 * RW_WIDTH]
    v = ps[:, 2 * RW_WIDTH:3 * RW_WIDTH]
    z = ps[:, 3 * RW_WIDTH:3 * RW_WIDTH + LANES]
    gd = ps[:, 3 * RW_WIDTH + LANES:RW_COLS]

    wlin = w0_ref[...] + _dot(jnp.tanh(z), w2_ref[...])
    w = -_softplus(-wlin) - 0.5
    lw_out[...] = -jnp.exp(w)
    a = _sigmoid(a0_ref[...] + _dot(z, a2_ref[...]))
    g_out[...] = _dot(_sigmoid(gd), g2_ref[...])
    kkv = k * kk_ref[...]
    kp = k * (1.0 + (a - 1.0) * ka_ref[...])
    rkr = r * kp * rk_ref[...]
    bd = bd_ref[...]
    for j in range(RW_WIDTH // LANES):
        sl = slice(j * LANES, (j + 1) * LANES)
        kkj = kkv[:, sl]
        ssq = _dot_exact_rhs(kkj * kkj, bd)
        kkn = kkj / jnp.maximum(jnp.sqrt(ssq), 1e-12)
        nkk_out[:, sl] = -kkn
        b_out[:, sl] = kkn * a[:, sl]
        bonus_out[:, sl] = _dot_exact_rhs(rkr[:, sl], bd) * v[:, sl]
    r_out[...] = r
    k_out[...] = kp
    v_out[...] = v.astype(BF16)

    pg = proj[:, COL_GLA:COL_GLA + GLA_MAIN]
    gq_out[...] = pg[:, 0:GLA_KW] * (GLA_DK ** -0.5)
    gk_out[...] = pg[:, GLA_KW:2 * GLA_KW]
    gv_out[...] = pg[:, 2 * GLA_KW:2 * GLA_KW + GLA_VW].astype(BF16)
    gg = pg[:, 2 * GLA_KW + GLA_VW:GLA_MAIN]
    gsg_out[...] = gg * _sigmoid(gg)
    adg = proj[:, COL_GATE + 2 * d_model:COL_GATE + 2 * d_model + LANES]
    la = _dot(adg, ga2_ref[...]) + gab_ref[...]
    gla_out[...] = -_softplus(-la) / GLA_TAU

    sg = _sigmoid(proj[:, COL_GATE:COL_GATE + 2 * d_model])
    sgr_out[...] = sg[:, :d_model]
    sgg_out[...] = sg[:, d_model:]


def _inproj_call(x2, mod3, wp, small, bsz, seq, d_model):
    tm = 256
    nt = seq // tm
    n_tok = bsz * seq
    (mu, w0, w2p, a0, a2p, g2, k_k, k_a, r_k, ga2p, gab, bd) = small

    def tok(n):
        return pl.BlockSpec((tm, n), lambda b, t: (b * nt + t, 0))

    def full(a):
        return pl.BlockSpec(a.shape, lambda b, t: (0,) * a.ndim)

    def modspec(idx):
        return pl.BlockSpec((None, 1, d_model), lambda b, t: (b, 0, idx))

    outs = [
        (RW_WIDTH, F32), (RW_WIDTH, F32), (RW_WIDTH, BF16), (RW_WIDTH, F32), (RW_WIDTH, F32),
        (RW_WIDTH, F32), (RW_WIDTH, F32), (RW_WIDTH, F32),
        (GLA_KW, F32), (GLA_KW, F32), (GLA_VW, BF16), (GLA_VW, F32), (GLA_KW, F32),
        (d_model, F32), (d_model, F32),
    ]
    return pl.pallas_call(
        functools.partial(_inproj_kernel, tm=tm, d_model=d_model),
        grid=(bsz, nt),
        in_specs=[tok(d_model), modspec(0), modspec(1), full(wp), full(mu), full(w0), full(w2p),
                  full(a0), full(a2p), full(g2), full(k_k), full(k_a), full(r_k), full(ga2p),
                  full(gab), full(bd)],
        out_specs=[tok(n) for n, _ in outs],
        out_shape=[jax.ShapeDtypeStruct((n_tok, n), dt) for n, dt in outs],
        scratch_shapes=[pltpu.VMEM((1, RW_COLS), F32)],
        compiler_params=pltpu.CompilerParams(dimension_semantics=("arbitrary", "arbitrary"),
                                             vmem_limit_bytes=VMEM_LIMIT),
        name="inproj",
    )(x2, mod3, mod3, wp, mu, w0, w2p, a0, a2p, g2, k_k, k_a, r_k, ga2p, gab, bd)


def _rwkv_kernel(r_ref, k_ref, v_ref, lw_ref, nkk_ref, b_ref, g_ref, bonus_ref, gng_ref, gnb_ref,
                 bd_ref, o_ref, h_ref, rp_s, yl_s, m_s, hl_s, *, tb, unroll):
    c = CHUNK
    gw = RW_GROUP * RW_HEAD
    nchunk = tb // c
    ngroup = RW_WIDTH // gw
    t = pl.program_id(1)

    @pl.when(t == 0)
    def _():
        h_ref[...] = jnp.zeros_like(h_ref)

    lane = lax.broadcasted_iota(jnp.int32, (c, gw), 1)
    rowi = lax.broadcasted_iota(jnp.int32, (c, gw), 0)
    head_of_lane = lane // RW_HEAD
    lmask = [head_of_lane == h for h in range(RW_GROUP)]
    strict = (lane % RW_HEAD) < rowi
    incl = (lane % RW_HEAD) <= rowi
    eye = (lane % RW_HEAD) == rowi
    eye_f = eye.astype(F32)
    eye_b = eye.astype(BF16)
    ri = lax.broadcasted_iota(jnp.int32, (c, c), 0)
    ci_ = lax.broadcasted_iota(jnp.int32, (c, c), 1)
    ltri = (ci_ <= ri).astype(BF16)

    def stack(x):
        xb = x.astype(BF16)
        zero = jnp.zeros_like(xb)
        return jnp.concatenate([jnp.where(lmask[h], xb, zero) for h in range(RW_GROUP)], axis=0)

    def local_group(gi, carry):
        units = [(u, g) for u in range(unroll) for g in range(ngroup)]
        n = len(units)
        idxs = [gi * unroll + u for u in range(unroll)]
        rows = [pl.ds(pl.multiple_of(i * c, c), c) for i in idxs]

        def ld(ref, u, g):
            return ref[rows[u], g * gw:(g + 1) * gw]

        lw = [ld(lw_ref, u, g) for u, g in units]
        gcum = [_dot_exact_lhs(ltri, x) for x in lw]
        gend, rt, at, kt, bt, bh, kh, v = [], [], [], [], [], [], [], []
        for s, (u, g) in enumerate(units):
            g_ = gcum[s]
            ge = g_[c - 1:c, :]
            r = ld(r_ref, u, g)
            k = ld(k_ref, u, g)
            nkk = ld(nkk_ref, u, g)
            b = ld(b_ref, u, g)
            e_neg = jnp.exp(-g_)
            dk = jnp.exp(ge - g_)
            gend.append(ge)
            rt.append(r * jnp.exp(g_))
            at.append(nkk * jnp.exp(g_ - lw[s]))
            kt.append(k * e_neg)
            bt.append(b * e_neg)
            bh.append(b * dk)
            kh.append(k * dk)
            v.append(ld(v_ref, u, g))

        lhs = [jnp.concatenate([at[s], rt[s]], axis=0).astype(BF16) for s in range(n)]
        xb = [_dot_nt(lhs[s], stack(bt[s])) for s in range(n)]
        xk = [_dot_nt(lhs[s], stack(kt[s])) for s in range(n)]
        a_ab = [jnp.where(strict, x[:c], 0.0) for x in xb]
        a_rb = [jnp.where(incl, x[c:], 0.0) for x in xb]
        a_ak = [jnp.where(strict, x[:c], 0.0) for x in xk]
        a_rk = [jnp.where(incl, x[c:], 0.0) for x in xk]

        tinv = [eye_f + a for a in a_ab]
        pw = a_ab
        for _ in range(5):
            pw = [_dot(x, stack(x)) for x in pw]
            tinv = [tm_ + _dot(tm_, stack(x)) for tm_, x in zip(tinv, pw)]

        vst = [stack(x) for x in v]
        ap = [_dot(tinv[s], stack(at[s])) for s in range(n)]
        akv = [_dot(a_ak[s], vst[s]) for s in range(n)]
        uloc = [_dot(tinv[s], stack(akv[s])) for s in range(n)]
        rb_ap = [_dot(a_rb[s], stack(ap[s])) for s in range(n)]
        rk_v = [_dot(a_rk[s], vst[s]) for s in range(n)]
        ulst = [stack(x) for x in uloc]
        rb_ul = [_dot(a_rb[s], ulst[s]) for s in range(n)]
        bht = [_dot_nt(eye_b, stack(x)) for x in bh]
        kht = [_dot_nt(eye_b, stack(x)) for x in kh]
        m1 = [_dot(bht[s], stack(ap[s])) for s in range(n)]
        h1 = [_dot(bht[s], ulst[s]) for s in range(n)]
        h2 = [_dot(kht[s], vst[s]) for s in range(n)]
        for s, (u, g) in enumerate(units):
            lanes = slice(g * gw, (g + 1) * gw)
            rp_s[idxs[u], :, lanes] = rt[s] + rb_ap[s]
            yl_s[rows[u], lanes] = rb_ul[s] + rk_v[s]
            m_s[idxs[u], g] = jnp.where(eye, jnp.exp(gend[s]), 0.0) + m1[s]
            hl_s[idxs[u], g] = h1[s] + h2[s]
        return carry

    lax.fori_loop(0, nchunk // unroll, local_group, 0)

    hstate = [h_ref[g] for g in range(ngroup)]
    for ci in range(nchunk):
        rows = slice(ci * c, (ci + 1) * c)
        rp = rp_s[ci]
        hst = [stack(hstate[g]) for g in range(ngroup)]
        ys = [_dot(rp[:, g * gw:(g + 1) * gw], hst[g]) for g in range(ngroup)]
        hn = [_dot(m_s[ci, g], hst[g]) for g in range(ngroup)]
        for g in range(ngroup):
            lanes = slice(g * gw, (g + 1) * gw)
            yl_s[rows, lanes] = ys[g] + yl_s[rows, lanes]
            hstate[g] = hn[g] + hl_s[ci, g]
    for g in range(ngroup):
        h_ref[g] = hstate[g]

    bd = bd_ref[...]
    for p in range(RW_WIDTH // LANES):
        lanes = slice(p * LANES, (p + 1) * LANES)
        y = yl_s[:, lanes]
        mean = _dot_exact_rhs(y, bd) * (1.0 / RW_HEAD)
        yc = y - mean
        var = _dot_exact_rhs(yc * yc, bd) * (1.0 / RW_HEAD)
        yn = yc * lax.rsqrt(var + RW_GN_EPS)
        out = (yn * gng_ref[:, lanes] + gnb_ref[:, lanes] + bonus_ref[:, lanes]) * g_ref[:, lanes]
        o_ref[:, lanes] = out.astype(o_ref.dtype)


def _rwkv_call(r, k, v, lw, nkk, b, g, bonus, gng, gnb, bd, bsz, seq):
    tb = REC_TILE
    nt = seq // tb
    n_tok = bsz * seq
    gw = RW_GROUP * RW_HEAD
    ngroup = RW_WIDTH // gw
    nchunk = tb // CHUNK

    def tok():
        return pl.BlockSpec((tb, RW_WIDTH), lambda bb, t: (bb * nt + t, 0))

    def full(a):
        return pl.BlockSpec(a.shape, lambda bb, t: (0,) * a.ndim)

    return pl.pallas_call(
        functools.partial(_rwkv_kernel, tb=tb, unroll=RWKV_UNROLL),
        grid=(bsz, nt),
        in_specs=[tok() for _ in range(8)] + [full(gng), full(gnb), full(bd)],
        out_specs=tok(),
        out_shape=jax.ShapeDtypeStruct((n_tok, RW_WIDTH), BF16),
        scratch_shapes=[pltpu.VMEM((ngroup, RW_HEAD, gw), F32),
                        pltpu.VMEM((nchunk, CHUNK, RW_WIDTH), F32),
                        pltpu.VMEM((tb, RW_WIDTH), F32),
                        pltpu.VMEM((nchunk, ngroup, RW_HEAD, gw), F32),
                        pltpu.VMEM((nchunk, ngroup, RW_HEAD, gw), F32)],
        compiler_params=pltpu.CompilerParams(
            dimension_semantics=("arbitrary", "arbitrary"),
            vmem_limit_bytes=VMEM_LIMIT),
        name="rwkv",
    )(r, k, v, lw, nkk, b, g, bonus, gng, gnb, bd)


def _gla_kernel(q_ref, k_ref, v_ref, sg_ref, la_ref, ng_ref, o_ref, st_ref, qs_s, oi_s, kv_s, eb_s,
                *, tb, unroll):
    c = CHUNK
    nchunk = tb // c
    t = pl.program_id(1)

    @pl.when(t == 0)
    def _():
        st_ref[...] = jnp.zeros_like(st_ref)

    lane = lax.broadcasted_iota(jnp.int32, (1, LANES), 1)
    hmask = ((lane < GLA_DK).astype(F32), (lane >= GLA_DK).astype(F32))
    ri = lax.broadcasted_iota(jnp.int32, (c, c), 0)
    ci_ = lax.broadcasted_iota(jnp.int32, (c, c), 1)
    causal = ci_ <= ri
    ltri = causal.astype(BF16)

    def local_group(gi, carry):
        units = [(u, h) for u in range(unroll) for h in range(GLA_HEADS)]
        idxs = [gi * unroll + u for u in range(unroll)]
        rows = [pl.ds(pl.multiple_of(i * c, c), c) for i in idxs]

        def qk_lanes(h):
            return slice((h // 2) * LANES, (h // 2 + 1) * LANES)

        la = [la_ref[rows[u], qk_lanes(h)] for u, h in units]
        bcum = [_dot_exact_lhs(ltri, x) for x in la]
        q_s, k_s, kdec, v = [], [], [], []
        for s, (u, h) in enumerate(units):
            bc = bcum[s]
            blast = bc[c - 1:c, :]
            q = q_ref[rows[u], qk_lanes(h)]
            k = k_ref[rows[u], qk_lanes(h)]
            hm = hmask[h % 2]
            q_s.append((q * jnp.exp(bc) * hm).astype(BF16))
            k_s.append(k * jnp.exp(-bc))
            kdec.append(k * jnp.exp(blast - bc) * hm)
            v.append(v_ref[rows[u], h * LANES:(h + 1) * LANES])
            eb_s[idxs[u], h] = jnp.exp(blast)
        att = [jnp.where(causal, _dot_nt(q_s[s], k_s[s]), 0.0) for s in range(len(units))]
        oi = [_dot(att[s], v[s]) for s in range(len(units))]
        v_t = [x.astype(F32).T.astype(BF16) for x in v]
        kv = [_dot(v_t[s], kdec[s]) for s in range(len(units))]
        for s, (u, h) in enumerate(units):
            qs_s[idxs[u], h] = q_s[s]
            oi_s[rows[u], h * LANES:(h + 1) * LANES] = oi[s]
            kv_s[idxs[u], h] = kv[s]
        return carry

    lax.fori_loop(0, nchunk // unroll, local_group, 0)

    st = [st_ref[h] for h in range(GLA_HEADS)]
    for ci in range(nchunk):
        rows = slice(ci * c, (ci + 1) * c)
        inter = [_dot_nt(qs_s[ci, h], st[h]) for h in range(GLA_HEADS)]
        for h in range(GLA_HEADS):
            lanes = slice(h * LANES, (h + 1) * LANES)
            oi_s[rows, lanes] = oi_s[rows, lanes] + inter[h]
            st[h] = st[h] * eb_s[ci, h] + kv_s[ci, h]
    for h in range(GLA_HEADS):
        st_ref[h] = st[h]

    ng = ng_ref[...]
    for h in range(GLA_HEADS):
        lanes = slice(h * LANES, (h + 1) * LANES)
        o = oi_s[:, lanes]
        o = o * lax.rsqrt(jnp.mean(o * o, axis=-1, keepdims=True) + GLA_NORM_EPS) * ng
        o_ref[:, lanes] = (o * sg_ref[:, lanes]).astype(o_ref.dtype)


def _gla_call(q, k, v, sg, la, ng, bsz, seq):
    tb = REC_TILE
    nt = seq // tb
    n_tok = bsz * seq
    nchunk = tb // CHUNK

    def tok(n):
        return pl.BlockSpec((tb, n), lambda bb, t: (bb * nt + t, 0))

    return pl.pallas_call(
        functools.partial(_gla_kernel, tb=tb, unroll=GLA_UNROLL),
        grid=(bsz, nt),
        in_specs=[tok(GLA_KW), tok(GLA_KW), tok(GLA_VW), tok(GLA_VW), tok(GLA_KW),
                  pl.BlockSpec((1, LANES), lambda bb, t: (0, 0))],
        out_specs=tok(GLA_VW),
        out_shape=jax.ShapeDtypeStruct((n_tok, GLA_VW), BF16),
        scratch_shapes=[pltpu.VMEM((GLA_HEADS, GLA_DV, LANES), F32),
                        pltpu.VMEM((nchunk, GLA_HEADS, CHUNK, LANES), BF16),
                        pltpu.VMEM((tb, GLA_VW), F32),
                        pltpu.VMEM((nchunk, GLA_HEADS, GLA_DV, LANES), F32),
                        pltpu.VMEM((nchunk, GLA_HEADS, 1, LANES), F32)],
        compiler_params=pltpu.CompilerParams(
            dimension_semantics=("arbitrary", "arbitrary"),
            vmem_limit_bytes=VMEM_LIMIT),
        name="gla",
    )(q, k, v, sg, la, ng)


def _tail_kernel(x_ref, orw_ref, ogla_ref, sgr_ref, sgg_ref, gate1_ref, shift2_ref, scale2_ref,
                 gate2_ref, wrb_ref, wgb_ref, wmix_ref, win_ref, wout_ref, ln1g_ref, ln1b_ref,
                 ln2g_ref, ln2b_ref, o_ref, *, d_ff):
    merged = (sgr_ref[...] * jnp.dot(orw_ref[...], wrb_ref[...], preferred_element_type=F32)
              + sgg_ref[...] * jnp.dot(ogla_ref[...], wgb_ref[...], preferred_element_type=F32))
    mix = _dot(merged, wmix_ref[...])
    x1 = _layer_norm(ALPHA * x_ref[...] + gate1_ref[...] * mix, LN_EPS) * ln1g_ref[...] + ln1b_ref[...]

    u = (x1 * (1.0 + scale2_ref[...]) + shift2_ref[...]).astype(BF16)
    ffn = jnp.zeros_like(x1)
    ntile = d_ff // MXU_K
    bounds = [MXU_K * ((ntile * j) // FFN_CHUNKS) for j in range(FFN_CHUNKS)] + [d_ff]
    for lo, hi in zip(bounds[:-1], bounds[1:]):
        hg = jnp.dot(u, win_ref[:, lo:hi], preferred_element_type=F32)
        hu = jnp.dot(u, win_ref[:, d_ff + lo:d_ff + hi], preferred_element_type=F32)
        act = hg * _sigmoid(hg) * hu
        ffn = ffn + _dot(act, wout_ref[lo:hi, :])
    y = _layer_norm(ALPHA * x1 + gate2_ref[...] * ffn, LN_EPS)
    o_ref[...] = y * ln2g_ref[...] + ln2b_ref[...]


def _tail_call(x2, o_rw, o_gla, sgr, sgg, mod3, wrb, wgb, wmix, win, wout, ln1g, ln1b, ln2g, ln2b,
               bsz, seq, d_model):
    tm = TAIL_TILE
    nt = seq // tm
    n_tok = bsz * seq
    d_ff = wout.shape[0]
    assert d_ff % MXU_K == 0

    def tok(n):
        return pl.BlockSpec((tm, n), lambda b, t: (b * nt + t, 0))

    def const(a):
        return pl.BlockSpec(a.shape, lambda b, t: (0,) * a.ndim, pipeline_mode=pl.Buffered(1))

    def modspec(idx):
        return pl.BlockSpec((None, 1, d_model), lambda b, t: (b, 0, idx))

    return pl.pallas_call(
        functools.partial(_tail_kernel, d_ff=d_ff),
        grid=(bsz, nt),
        in_specs=[tok(d_model), tok(RW_WIDTH), tok(GLA_VW), tok(d_model), tok(d_model),
                  modspec(2), modspec(3), modspec(4), modspec(5),
                  const(wrb), const(wgb), const(wmix), const(win), const(wout),
                  const(ln1g), const(ln1b), const(ln2g), const(ln2b)],
        out_specs=tok(d_model),
        out_shape=jax.ShapeDtypeStruct((n_tok, d_model), F32),
        compiler_params=pltpu.CompilerParams(dimension_semantics=("arbitrary", "arbitrary"),
                                             vmem_limit_bytes=VMEM_LIMIT),
        name="tail",
    )(x2, o_rw, o_gla, sgr, sgg, mod3, mod3, mod3, mod3, wrb, wgb, wmix, win, wout,
      ln1g, ln1b, ln2g, ln2b)


def kernel(x, c, w_ada, b_ada, w_in, mu_rw, rw_w0, rw_w2, rw_a0, rw_a2, rw_g2, rw_k_k, rw_k_a,
           rw_r_k, rw_gn_g, rw_gn_b, gla_a2, gla_a_b, gla_norm_g, w_rw_branch, w_gla_branch,
           w_mix_out, ln1_g, ln1_b, w_ffn_in, w_ffn_out, ln2_g, ln2_b):
    bsz, seq, d_model = x.shape
    assert w_ada.shape[0] == DEPTH and seq % REC_TILE == 0
    n_tok = bsz * seq
    l = 0

    wp = _wprep_call(jnp.swapaxes(w_in[l], 0, 1), d_model)
    zeros_lora = jnp.zeros((DECAY_LORA, RW_WIDTH), F32)
    w2p = jnp.concatenate([rw_w2[l], zeros_lora], axis=0).astype(BF16)
    a2p = jnp.concatenate([zeros_lora, rw_a2[l]], axis=0).astype(BF16)
    ga2p = jnp.concatenate(
        [gla_a2[l], jnp.zeros((LANES - GLA_GATE_LORA, GLA_KW), F32)], axis=0).astype(BF16)
    row = lambda a: a.reshape(1, -1)
    hid = jnp.arange(LANES) // RW_HEAD
    bd = (hid[:, None] == hid[None, :]).astype(BF16)
    small = (row(mu_rw[l]), row(rw_w0[l]), w2p, row(rw_a0[l]), a2p, rw_g2[l].astype(BF16),
             row(rw_k_k[l]), row(rw_k_a[l]), row(rw_r_k[l]), ga2p, row(gla_a_b[l]), bd)

    x2 = x.reshape(n_tok, d_model)
    mod = _mod_call(c, w_ada[l], b_ada[l])
    mod3 = mod.reshape(bsz, 1, 6 * d_model)

    (r, k, v, lw, nkk, b, g, bonus, gq, gk, gv, gsg, gla, sgr, sgg) = _inproj_call(
        x2, mod3, wp, small, bsz, seq, d_model)

    o_rw = _rwkv_call(r, k, v, lw, nkk, b, g, bonus, row(rw_gn_g[l]), row(rw_gn_b[l]), bd, bsz, seq)
    o_gla = _gla_call(gq, gk, gv, gsg, gla, row(gla_norm_g[l]), bsz, seq)

    out = _tail_call(x2, o_rw, o_gla, sgr, sgg, mod3, w_rw_branch[l].astype(BF16),
                     w_gla_branch[l].astype(BF16), w_mix_out[l].astype(BF16),
                     w_ffn_in[l].astype(BF16), w_ffn_out[l].astype(BF16),
                     row(ln1_g[l]), row(ln1_b[l]), row(ln2_g[l]), row(ln2_b[l]), bsz, seq, d_model)
    return out.reshape(bsz, seq, d_model)
```

```python
import functools

import jax
import jax.numpy as jnp
from jax import lax
from jax.experimental import pallas as pl
from jax.experimental.pallas import tpu as pltpu

F32 = jnp.float32
BF16 = jnp.bfloat16

RW_HEAD = 64
RW_HEADS = 8
RW_WIDTH = RW_HEADS * RW_HEAD
DECAY_LORA = 64
AAA_LORA = 64
GATE_LORA = 128
RW_GN_EPS = 64e-5
RW_COLS = 3 * RW_WIDTH + DECAY_LORA + AAA_LORA + GATE_LORA
GLA_HEADS = 4
GLA_DK = 64
GLA_DV = 128
GLA_KW = GLA_HEADS * GLA_DK
GLA_VW = GLA_HEADS * GLA_DV
GLA_GATE_LORA = 16
GLA_TAU = 16.0
GLA_NORM_EPS = 1e-5
GLA_MAIN = 2 * GLA_KW + 2 * GLA_VW
GLA_COLS = GLA_MAIN + GLA_GATE_LORA
LN_EPS = 1e-5
DEPTH = 1
ALPHA = (2.0 * DEPTH) ** 0.25

LANES = 128
MXU_K = 256
CHUNK = 64
RWKV_UNROLL = 4
RW_GROUP = 4
GLA_UNROLL = 8
REC_TILE = 512
TAIL_TILE = 512
FFN_CHUNKS = 2
VMEM_LIMIT = 58 * 1024 * 1024

COL_RW = 0
COL_GLA = RW_COLS
COL_GATE = COL_GLA + GLA_MAIN


def _sigmoid(x):
    return 1.0 / (1.0 + jnp.exp(-x))


def _softplus(x):
    return jnp.maximum(x, 0.0) + jnp.log1p(jnp.exp(-jnp.abs(x)))


def _dot(a, b):
    return jnp.dot(a.astype(BF16), b.astype(BF16), preferred_element_type=F32)


def _dot_nt(a, b):
    return lax.dot_general(a.astype(BF16), b.astype(BF16), (((1,), (1,)), ((), ())),
                           preferred_element_type=F32)


def _split_hi_lo(x):
    hi = x.astype(BF16)
    lo = (x - hi.astype(F32)).astype(BF16)
    return hi, lo


def _dot_exact_rhs(x, w01):
    hi, lo = _split_hi_lo(x)
    if 2 * x.shape[1] <= MXU_K:
        return jnp.dot(jnp.concatenate([hi, lo], axis=1), jnp.concatenate([w01, w01], axis=0),
                       preferred_element_type=F32)
    return (jnp.dot(hi, w01, preferred_element_type=F32)
            + jnp.dot(lo, w01, preferred_element_type=F32))


def _dot_exact_lhs(w01, x):
    hi, lo = _split_hi_lo(x)
    if 2 * x.shape[0] <= MXU_K:
        return jnp.dot(jnp.concatenate([w01, w01], axis=1), jnp.concatenate([hi, lo], axis=0),
                       preferred_element_type=F32)
    return (jnp.dot(w01, hi, preferred_element_type=F32)
            + jnp.dot(w01, lo, preferred_element_type=F32))


def _layer_norm(x, eps):
    mu = jnp.mean(x, axis=-1, keepdims=True)
    xc = x - mu
    var = jnp.mean(xc * xc, axis=-1, keepdims=True)
    return xc * lax.rsqrt(var + eps)


def _mod_kernel(c_ref, w_ref, b_ref, o_ref):
    c = c_ref[...]
    sc = c * _sigmoid(c)
    o_ref[...] = _dot(sc, w_ref[...]) + b_ref[...]


def _mod_call(c, w_ada, b_ada):
    bsz, d = c.shape
    n = w_ada.shape[1]
    tn = 1536
    return pl.pallas_call(
        _mod_kernel,
        grid=(n // tn,),
        in_specs=[pl.BlockSpec((bsz, d), lambda j: (0, 0)),
                  pl.BlockSpec((d, tn), lambda j: (0, j)),
                  pl.BlockSpec((1, tn), lambda j: (0, j))],
        out_specs=pl.BlockSpec((bsz, tn), lambda j: (0, j)),
        out_shape=jax.ShapeDtypeStruct((bsz, n), F32),
        compiler_params=pltpu.CompilerParams(dimension_semantics=("arbitrary",),
                                             vmem_limit_bytes=VMEM_LIMIT),
        name="mod",
    )(c, w_ada, b_ada.reshape(1, n))


def _wprep_kernel(w_ref, o_ref, *, d_model):
    main = RW_COLS + GLA_MAIN
    o_ref[:main, :] = w_ref[:main, :].astype(BF16)
    o_ref[main:main + 2 * d_model, :] = w_ref[RW_COLS + GLA_COLS:, :].astype(BF16)
    o_ref[main + 2 * d_model:main + 2 * d_model + GLA_GATE_LORA, :] = (
        w_ref[main:RW_COLS + GLA_COLS, :].astype(BF16))
    o_ref[main + 2 * d_model + GLA_GATE_LORA:, :] = jnp.zeros(
        (LANES - GLA_GATE_LORA, o_ref.shape[1]), BF16)


def _wprep_call(w_in_t, d_model):
    n_in = w_in_t.shape[0]
    n_out = RW_COLS + GLA_MAIN + 2 * d_model + LANES
    cols = 256
    return pl.pallas_call(
        functools.partial(_wprep_kernel, d_model=d_model),
        grid=(d_model // cols,),
        in_specs=[pl.BlockSpec((n_in, cols), lambda i: (0, i))],
        out_specs=pl.BlockSpec((n_out, cols), lambda i: (0, i)),
        out_shape=jax.ShapeDtypeStruct((n_out, d_model), BF16),
        compiler_params=pltpu.CompilerParams(dimension_semantics=("arbitrary",),
                                             vmem_limit_bytes=VMEM_LIMIT),
        name="wprep",
    )(w_in_t)


def _inproj_kernel(x_ref, shift_ref, scale_ref, wp_ref, mu_ref, w0_ref, w2_ref, a0_ref, a2_ref,
                   g2_ref, kk_ref, ka_ref, rk_ref, ga2_ref, gab_ref, bd_ref,
                   r_out, k_out, v_out, lw_out, nkk_out, b_out, g_out, bonus_out,
                   gq_out, gk_out, gv_out, gsg_out, gla_out, sgr_out, sgg_out,
                   carry_ref, *, tm, d_model):
    t = pl.program_id(1)

    @pl.when(t == 0)
    def _():
        carry_ref[...] = jnp.zeros_like(carry_ref)

    u = x_ref[...] * (1.0 + scale_ref[...]) + shift_ref[...]
    proj = _dot_nt(u, wp_ref[...])

    p = proj[:, COL_RW:COL_RW + RW_COLS]
    prev = pltpu.roll(p, 1, 0)
    row = lax.broadcasted_iota(jnp.int32, p.shape, 0)
    prev = jnp.where(row == 0, carry_ref[...], prev)
    carry_ref[...] = p[tm - 1:tm, :]
    ps = p + mu_ref[...] * (prev - p)

    r = ps[:, 0:RW_WIDTH]
    k = ps[:, RW_WIDTH:2 * RW_WIDTH]
    v = ps[:, 2 * RW_WIDTH:3 * RW_WIDTH]
    z = ps[:, 3 * RW_WIDTH:3 * RW_WIDTH + LANES]
    gd = ps[:, 3 * RW_WIDTH + LANES:RW_COLS]

    wlin = w0_ref[...] + _dot(jnp.tanh(z), w2_ref[...])
    w = -_softplus(-wlin) - 0.5
    lw_out[...] = -jnp.exp(w)
    a = _sigmoid(a0_ref[...] + _dot(z, a2_ref[...]))
    g_out[...] = _dot(_sigmoid(gd), g2_ref[...])
    kkv = k * kk_ref[...]
    kp = k * (1.0 + (a - 1.0) * ka_ref[...])
    rkr = r * kp * rk_ref[...]
    bd = bd_ref[...]
    for j in range(RW_WIDTH // LANES):
        sl = slice(j * LANES, (j + 1) * LANES)
        kkj = kkv[:, sl]
        ssq = _dot_exact_rhs(kkj * kkj, bd)
        kkn = kkj / jnp.maximum(jnp.sqrt(ssq), 1e-12)
        nkk_out[:, sl] = -kkn
        b_out[:, sl] = kkn * a[:, sl]
        bonus_out[:, sl] = _dot_exact_rhs(rkr[:, sl], bd) * v[:, sl]
    r_out[...] = r
    k_out[...] = kp
    v_out[...] = v.astype(BF16)

    pg = proj[:, COL_GLA:COL_GLA + GLA_MAIN]
    gq_out[...] = pg[:, 0:GLA_KW] * (GLA_DK ** -0.5)
    gk_out[...] = pg[:, GLA_KW:2 * GLA_KW]
    gv_out[...] = pg[:, 2 * GLA_KW:2 * GLA_KW + GLA_VW].astype(BF16)
    gg = pg[:, 2 * GLA_KW + GLA_VW:GLA_MAIN]
    gsg_out[...] = gg * _sigmoid(gg)
    adg = proj[:, COL_GATE + 2 * d_model:COL_GATE + 2 * d_model + LANES]
    la = _dot(adg, ga2_ref[...]) + gab_ref[...]
    gla_out[...] = -_softplus(-la) / GLA_TAU

    sg = _sigmoid(proj[:, COL_GATE:COL_GATE + 2 * d_model])
    sgr_out[...] = sg[:, :d_model]
    sgg_out[...] = sg[:, d_model:]


def _inproj_call(x2, mod3, wp, small, bsz, seq, d_model):
    tm = 256
    nt = seq // tm
    n_tok = bsz * seq
    (mu, w0, w2p, a0, a2p, g2, k_k, k_a, r_k, ga2p, gab, bd) = small

    def tok(n):
        return pl.BlockSpec((tm, n), lambda b, t: (b * nt + t, 0))

    def full(a):
        return pl.BlockSpec(a.shape, lambda b, t: (0,) * a.ndim)

    def modspec(idx):
        return pl.BlockSpec((None, 1, d_model), lambda b, t: (b, 0, idx))

    outs = [
        (RW_WIDTH, F32), (RW_WIDTH, F32), (RW_WIDTH, BF16), (RW_WIDTH, F32), (RW_WIDTH, F32),
        (RW_WIDTH, F32), (RW_WIDTH, F32), (RW_WIDTH, F32),
        (GLA_KW, F32), (GLA_KW, F32), (GLA_VW, BF16), (GLA_VW, F32), (GLA_KW, F32),
        (d_model, F32), (d_model, F32),
    ]
    return pl.pallas_call(
        functools.partial(_inproj_kernel, tm=tm, d_model=d_model),
        grid=(bsz, nt),
        in_specs=[tok(d_model), modspec(0), modspec(1), full(wp), full(mu), full(w0), full(w2p),
                  full(a0), full(a2p), full(g2), full(k_k), full(k_a), full(r_k), full(ga2p),
                  full(gab), full(bd)],
        out_specs=[tok(n) for n, _ in outs],
        out_shape=[jax.ShapeDtypeStruct((n_tok, n), dt) for n, dt in outs],
        scratch_shapes=[pltpu.VMEM((1, RW_COLS), F32)],
        compiler_params=pltpu.CompilerParams(dimension_semantics=("arbitrary", "arbitrary"),
                                             vmem_limit_bytes=VMEM_LIMIT),
        name="inproj",
    )(x2, mod3, mod3, wp, mu, w0, w2p, a0, a2p, g2, k_k, k_a, r_k, ga2p, gab, bd)


def _rwkv_kernel(r_ref, k_ref, v_ref, lw_ref, nkk_ref, b_ref, g_ref, bonus_ref, gng_ref, gnb_ref,
                 bd_ref, o_ref, h_ref, rp_s, yl_s, m_s, hl_s, *, tb, unroll):
    c = CHUNK
    gw = RW_GROUP * RW_HEAD
    nchunk = tb // c
    ngroup = RW_WIDTH // gw
    t = pl.program_id(1)

    @pl.when(t == 0)
    def _():
        h_ref[...] = jnp.zeros_like(h_ref)

    lane = lax.broadcasted_iota(jnp.int32, (c, gw), 1)
    rowi = lax.broadcasted_iota(jnp.int32, (c, gw), 0)
    head_of_lane = lane // RW_HEAD
    lmask = [head_of_lane == h for h in range(RW_GROUP)]
    strict = (lane % RW_HEAD) < rowi
    incl = (lane % RW_HEAD) <= rowi
    eye = (lane % RW_HEAD) == rowi
    eye_f = eye.astype(F32)
    eye_b = eye.astype(BF16)
    ri = lax.broadcasted_iota(jnp.int32, (c, c), 0)
    ci_ = lax.broadcasted_iota(jnp.int32, (c, c), 1)
    ltri = (ci_ <= ri).astype(BF16)

    def stack(x):
        xb = x.astype(BF16)
        zero = jnp.zeros_like(xb)
        return jnp.concatenate([jnp.where(lmask[h], xb, zero) for h in range(RW_GROUP)], axis=0)

    def local_group(gi, carry):
        units = [(u, g) for u in range(unroll) for g in range(ngroup)]
        n = len(units)
        idxs = [gi * unroll + u for u in range(unroll)]
        rows = [pl.ds(pl.multiple_of(i * c, c), c) for i in idxs]

        def ld(ref, u, g):
            return ref[rows[u], g * gw:(g + 1) * gw]

        lw = [ld(lw_ref, u, g) for u, g in units]
        gcum = [_dot_exact_lhs(ltri, x) for x in lw]
        gend, rt, at, kt, bt, bh, kh, v = [], [], [], [], [], [], [], []
        for s, (u, g) in enumerate(units):
            g_ = gcum[s]
            ge = g_[c - 1:c, :]
            r = ld(r_ref, u, g)
            k = ld(k_ref, u, g)
            nkk = ld(nkk_ref, u, g)
            b = ld(b_ref, u, g)
            e_neg = jnp.exp(-g_)
            dk = jnp.exp(ge - g_)
            gend.append(ge)
            rt.append(r * jnp.exp(g_))
            at.append(nkk * jnp.exp(g_ - lw[s]))
            kt.append(k * e_neg)
            bt.append(b * e_neg)
            bh.append(b * dk)
            kh.append(k * dk)
            v.append(ld(v_ref, u, g))

        lhs = [jnp.concatenate([at[s], rt[s]], axis=0).astype(BF16) for s in range(n)]
        xb = [_dot_nt(lhs[s], stack(bt[s])) for s in range(n)]
        xk = [_dot_nt(lhs[s], stack(kt[s])) for s in range(n)]
        a_ab = [jnp.where(strict, x[:c], 0.0) for x in xb]
        a_rb = [jnp.where(incl, x[c:], 0.0) for x in xb]
        a_ak = [jnp.where(strict, x[:c], 0.0) for x in xk]
        a_rk = [jnp.where(incl, x[c:], 0.0) for x in xk]

        tinv = [eye_f + a for a in a_ab]
        pw = a_ab
        for _ in range(5):
            pw = [_dot(x, stack(x)) for x in pw]
            tinv = [tm_ + _dot(tm_, stack(x)) for tm_, x in zip(tinv, pw)]

        vst = [stack(x) for x in v]
        ap = [_dot(tinv[s], stack(at[s])) for s in range(n)]
        akv = [_dot(a_ak[s], vst[s]) for s in range(n)]
        uloc = [_dot(tinv[s], stack(akv[s])) for s in range(n)]
        rb_ap = [_dot(a_rb[s], stack(ap[s])) for s in range(n)]
        rk_v = [_dot(a_rk[s], vst[s]) for s in range(n)]
        ulst = [stack(x) for x in uloc]
        rb_ul = [_dot(a_rb[s], ulst[s]) for s in range(n)]
        bht = [_dot_nt(eye_b, stack(x)) for x in bh]
        kht = [_dot_nt(eye_b, stack(x)) for x in kh]
        m1 = [_dot(bht[s], stack(ap[s])) for s in range(n)]
        h1 = [_dot(bht[s], ulst[s]) for s in range(n)]
        h2 = [_dot(kht[s], vst[s]) for s in range(n)]
        for s, (u, g) in enumerate(units):
            lanes = slice(g * gw, (g + 1) * gw)
            rp_s[idxs[u], :, lanes] = rt[s] + rb_ap[s]
            yl_s[rows[u], lanes] = rb_ul[s] + rk_v[s]
            m_s[idxs[u], g] = jnp.where(eye, jnp.exp(gend[s]), 0.0) + m1[s]
            hl_s[idxs[u], g] = h1[s] + h2[s]
        return carry

    lax.fori_loop(0, nchunk // unroll, local_group, 0)

    hstate = [h_ref[g] for g in range(ngroup)]
    for ci in range(nchunk):
        rows = slice(ci * c, (ci + 1) * c)
        rp = rp_s[ci]
        hst = [stack(hstate[g]) for g in range(ngroup)]
        ys = [_dot(rp[:, g * gw:(g + 1) * gw], hst[g]) for g in range(ngroup)]
        hn = [_dot(m_s[ci, g], hst[g]) for g in range(ngroup)]
        for g in range(ngroup):
            lanes = slice(g * gw, (g + 1) * gw)
            yl_s[rows, lanes] = ys[g] + yl_s[rows, lanes]
            hstate[g] = hn[g] + hl_s[ci, g]
    for g in range(ngroup):
        h_ref[g] = hstate[g]

    bd = bd_ref[...]
    for p in range(RW_WIDTH // LANES):
        lanes = slice(p * LANES, (p + 1) * LANES)
        y = yl_s[:, lanes]
        mean = _dot_exact_rhs(y, bd) * (1.0 / RW_HEAD)
        yc = y - mean
        var = _dot_exact_rhs(yc * yc, bd) * (1.0 / RW_HEAD)
        yn = yc * lax.rsqrt(var + RW_GN_EPS)
        out = (yn * gng_ref[:, lanes] + gnb_ref[:, lanes] + bonus_ref[:, lanes]) * g_ref[:, lanes]
        o_ref[:, lanes] = out.astype(o_ref.dtype)


def _rwkv_call(r, k, v, lw, nkk, b, g, bonus, gng, gnb, bd, bsz, seq):
    tb = REC_TILE
    nt = seq // tb
    n_tok = bsz * seq
    gw = RW_GROUP * RW_HEAD
    ngroup = RW_WIDTH // gw
    nchunk = tb // CHUNK

    def tok():
        return pl.BlockSpec((tb, RW_WIDTH), lambda bb, t: (bb * nt + t, 0))

    def full(a):
        return pl.BlockSpec(a.shape, lambda bb, t: (0,) * a.ndim)

    return pl.pallas_call(
        functools.partial(_rwkv_kernel, tb=tb, unroll=RWKV_UNROLL),
        grid=(bsz, nt),
        in_specs=[tok() for _ in range(8)] + [full(gng), full(gnb), full(bd)],
        out_specs=tok(),
        out_shape=jax.ShapeDtypeStruct((n_tok, RW_WIDTH), BF16),
        scratch_shapes=[pltpu.VMEM((ngroup, RW_HEAD, gw), F32),
                        pltpu.VMEM((nchunk, CHUNK, RW_WIDTH), F32),
                        pltpu.VMEM((tb, RW_WIDTH), F32),
                        pltpu.VMEM((nchunk, ngroup, RW_HEAD, gw), F32),
                        pltpu.VMEM((nchunk, ngroup, RW_HEAD, gw), F32)],
        compiler_params=pltpu.CompilerParams(
            dimension_semantics=("arbitrary", "arbitrary"),
            vmem_limit_bytes=VMEM_LIMIT),
        name="rwkv",
    )(r, k, v, lw, nkk, b, g, bonus, gng, gnb, bd)


def _gla_kernel(q_ref, k_ref, v_ref, sg_ref, la_ref, ng_ref, o_ref, st_ref, qs_s, oi_s, kv_s, eb_s,
                *, tb, unroll):
    c = CHUNK
    nchunk = tb // c
    t = pl.program_id(1)

    @pl.when(t == 0)
    def _():
        st_ref[...] = jnp.zeros_like(st_ref)

    lane = lax.broadcasted_iota(jnp.int32, (1, LANES), 1)
    hmask = ((lane < GLA_DK).astype(F32), (lane >= GLA_DK).astype(F32))
    ri = lax.broadcasted_iota(jnp.int32, (c, c), 0)
    ci_ = lax.broadcasted_iota(jnp.int32, (c, c), 1)
    causal = ci_ <= ri
    ltri = causal.astype(BF16)

    def local_group(gi, carry):
        units = [(u, h) for u in range(unroll) for h in range(GLA_HEADS)]
        idxs = [gi * unroll + u for u in range(unroll)]
        rows = [pl.ds(pl.multiple_of(i * c, c), c) for i in idxs]

        def qk_lanes(h):
            return slice((h // 2) * LANES, (h // 2 + 1) * LANES)

        la = [la_ref[rows[u], qk_lanes(h)] for u, h in units]
        bcum = [_dot_exact_lhs(ltri, x) for x in la]
        q_s, k_s, kdec, v = [], [], [], []
        for s, (u, h) in enumerate(units):
            bc = bcum[s]
            blast = bc[c - 1:c, :]
            q = q_ref[rows[u], qk_lanes(h)]
            k = k_ref[rows[u], qk_lanes(h)]
            hm = hmask[h % 2]
            q_s.append((q * jnp.exp(bc) * hm).astype(BF16))
            k_s.append(k * jnp.exp(-bc))
            kdec.append(k * jnp.exp(blast - bc) * hm)
            v.append(v_ref[rows[u], h * LANES:(h + 1) * LANES])
            eb_s[idxs[u], h] = jnp.exp(blast)
        att = [jnp.where(causal, _dot_nt(q_s[s], k_s[s]), 0.0) for s in range(len(units))]
        oi = [_dot(att[s], v[s]) for s in range(len(units))]
        v_t = [x.astype(F32).T.astype(BF16) for x in v]
        kv = [_dot(v_t[s], kdec[s]) for s in range(len(units))]
        for s, (u, h) in enumerate(units):
            qs_s[idxs[u], h] = q_s[s]
            oi_s[rows[u], h * LANES:(h + 1) * LANES] = oi[s]
            kv_s[idxs[u], h] = kv[s]
        return carry

    lax.fori_loop(0, nchunk // unroll, local_group, 0)

    st = [st_ref[h] for h in range(GLA_HEADS)]
    for ci in range(nchunk):
        rows = slice(ci * c, (ci + 1) * c)
        inter = [_dot_nt(qs_s[ci, h], st[h]) for h in range(GLA_HEADS)]
        for h in range(GLA_HEADS):
            lanes = slice(h * LANES, (h + 1) * LANES)
            oi_s[rows, lanes] = oi_s[rows, lanes] + inter[h]
            st[h] = st[h] * eb_s[ci, h] + kv_s[ci, h]
    for h in range(GLA_HEADS):
        st_ref[h] = st[h]

    ng = ng_ref[...]
    for h in range(GLA_HEADS):
        lanes = slice(h * LANES, (h + 1) * LANES)
        o = oi_s[:, lanes]
        o = o * lax.rsqrt(jnp.mean(o * o, axis=-1, keepdims=True) + GLA_NORM_EPS) * ng
        o_ref[:, lanes] = (o * sg_ref[:, lanes]).astype(o_ref.dtype)


def _gla_call(q, k, v, sg, la, ng, bsz, seq):
    tb = REC_TILE
    nt = seq // tb
    n_tok = bsz * seq
    nchunk = tb // CHUNK

    def tok(n):
        return pl.BlockSpec((tb, n), lambda bb, t: (bb * nt + t, 0))

    return pl.pallas_call(
        functools.partial(_gla_kernel, tb=tb, unroll=GLA_UNROLL),
        grid=(bsz, nt),
        in_specs=[tok(GLA_KW), tok(GLA_KW), tok(GLA_VW), tok(GLA_VW), tok(GLA_KW),
                  pl.BlockSpec((1, LANES), lambda bb, t: (0, 0))],
        out_specs=tok(GLA_VW),
        out_shape=jax.ShapeDtypeStruct((n_tok, GLA_VW), BF16),
        scratch_shapes=[pltpu.VMEM((GLA_HEADS, GLA_DV, LANES), F32),
                        pltpu.VMEM((nchunk, GLA_HEADS, CHUNK, LANES), BF16),
                        pltpu.VMEM((tb, GLA_VW), F32),
                        pltpu.VMEM((nchunk, GLA_HEADS, GLA_DV, LANES), F32),
                        pltpu.VMEM((nchunk, GLA_HEADS, 1, LANES), F32)],
        compiler_params=pltpu.CompilerParams(
            dimension_semantics=("arbitrary", "arbitrary"),
            vmem_limit_bytes=VMEM_LIMIT),
        name="gla",
    )(q, k, v, sg, la, ng)


def _tail_kernel(x_ref, orw_ref, ogla_ref, sgr_ref, sgg_ref, gate1_ref, shift2_ref, scale2_ref,
                 gate2_ref, wrb_ref, wgb_ref, wmix_ref, win_ref, wout_ref, ln1g_ref, ln1b_ref,
                 ln2g_ref, ln2b_ref, o_ref, *, d_ff):
    merged = (sgr_ref[...] * jnp.dot(orw_ref[...], wrb_ref[...], preferred_element_type=F32)
              + sgg_ref[...] * jnp.dot(ogla_ref[...], wgb_ref[...], preferred_element_type=F32))
    mix = _dot(merged, wmix_ref[...])
    x1 = _layer_norm(ALPHA * x_ref[...] + gate1_ref[...] * mix, LN_EPS) * ln1g_ref[...] + ln1b_ref[...]

    u = (x1 * (1.0 + scale2_ref[...]) + shift2_ref[...]).astype(BF16)
    ffn = jnp.zeros_like(x1)
    ntile = d_ff // MXU_K
    bounds = [MXU_K * ((ntile * j) // FFN_CHUNKS) for j in range(FFN_CHUNKS)] + [d_ff]
    for lo, hi in zip(bounds[:-1], bounds[1:]):
        hg = jnp.dot(u, win_ref[:, lo:hi], preferred_element_type=F32)
        hu = jnp.dot(u, win_ref[:, d_ff + lo:d_ff + hi], preferred_element_type=F32)
        act = hg * _sigmoid(hg) * hu
        ffn = ffn + _dot(act, wout_ref[lo:hi, :])
    y = _layer_norm(ALPHA * x1 + gate2_ref[...] * ffn, LN_EPS)
    o_ref[...] = y * ln2g_ref[...] + ln2b_ref[...]


def _tail_call(x2, o_rw, o_gla, sgr, sgg, mod3, wrb, wgb, wmix, win, wout, ln1g, ln1b, ln2g, ln2b,
               bsz, seq, d_model):
    tm = TAIL_TILE
    nt = seq // tm
    n_tok = bsz * seq
    d_ff = wout.shape[0]
    assert d_ff % MXU_K == 0

    def tok(n):
        return pl.BlockSpec((tm, n), lambda b, t: (b * nt + t, 0))

    def const(a):
        return pl.BlockSpec(a.shape, lambda b, t: (0,) * a.ndim, pipeline_mode=pl.Buffered(1))

    def modspec(idx):
        return pl.BlockSpec((None, 1, d_model), lambda b, t: (b, 0, idx))

    return pl.pallas_call(
        functools.partial(_tail_kernel, d_ff=d_ff),
        grid=(bsz, nt),
        in_specs=[tok(d_model), tok(RW_WIDTH), tok(GLA_VW), tok(d_model), tok(d_model),
                  modspec(2), modspec(3), modspec(4), modspec(5),
                  const(wrb), const(wgb), const(wmix), const(win), const(wout),
                  const(ln1g), const(ln1b), const(ln2g), const(ln2b)],
        out_specs=tok(d_model),
        out_shape=jax.ShapeDtypeStruct((n_tok, d_model), F32),
        compiler_params=pltpu.CompilerParams(dimension_semantics=("arbitrary", "arbitrary"),
                                             vmem_limit_bytes=VMEM_LIMIT),
        name="tail",
    )(x2, o_rw, o_gla, sgr, sgg, mod3, mod3, mod3, mod3, wrb, wgb, wmix, win, wout,
      ln1g, ln1b, ln2g, ln2b)


def kernel(x, c, w_ada, b_ada, w_in, mu_rw, rw_w0, rw_w2, rw_a0, rw_a2, rw_g2, rw_k_k, rw_k_a,
           rw_r_k, rw_gn_g, rw_gn_b, gla_a2, gla_a_b, gla_norm_g, w_rw_branch, w_gla_branch,
           w_mix_out, ln1_g, ln1_b, w_ffn_in, w_ffn_out, ln2_g, ln2_b):
    bsz, seq, d_model = x.shape
    assert w_ada.shape[0] == DEPTH and seq % REC_TILE == 0
    n_tok = bsz * seq
    l = 0

    wp = _wprep_call(jnp.swapaxes(w_in[l], 0, 1), d_model)
    zeros_lora = jnp.zeros((DECAY_LORA, RW_WIDTH), F32)
    w2p = jnp.concatenate([rw_w2[l], zeros_lora], axis=0).astype(BF16)
    a2p = jnp.concatenate([zeros_lora, rw_a2[l]], axis=0).astype(BF16)
    ga2p = jnp.concatenate(
        [gla_a2[l], jnp.zeros((LANES - GLA_GATE_LORA, GLA_KW), F32)], axis=0).astype(BF16)
    row = lambda a: a.reshape(1, -1)
    hid = jnp.arange(LANES) // RW_HEAD
    bd = (hid[:, None] == hid[None, :]).astype(BF16)
    small = (row(mu_rw[l]), row(rw_w0[l]), w2p, row(rw_a0[l]), a2p, rw_g2[l].astype(BF16),
             row(rw_k_k[l]), row(rw_k_a[l]), row(rw_r_k[l]), ga2p, row(gla_a_b[l]), bd)

    x2 = x.reshape(n_tok, d_model)
    mod = _mod_call(c, w_ada[l], b_ada[l])
    mod3 = mod.reshape(bsz, 1, 6 * d_model)

    (r, k, v, lw, nkk, b, g, bonus, gq, gk, gv, gsg, gla, sgr, sgg) = _inproj_call(
        x2, mod3, wp, small, bsz, seq, d_model)

    o_rw = _rwkv_call(r, k, v, lw, nkk, b, g, bonus, row(rw_gn_g[l]), row(rw_gn_b[l]), bd, bsz, seq)
    o_gla = _gla_call(gq, gk, gv, gsg, gla, row(gla_norm_g[l]), bsz, seq)

    out = _tail_call(x2, o_rw, o_gla, sgr, sgg, mod3, w_rw_branch[l].astype(BF16),
                     w_gla_branch[l].astype(BF16), w_mix_out[l].astype(BF16),
                     w_ffn_in[l].astype(BF16), w_ffn_out[l].astype(BF16),
                     row(ln1_g[l]), row(ln1_b[l]), row(ln2_g[l]), row(ln2_b[l]), bsz, seq, d_model)
    return out.reshape(bsz, seq, d_model)
```

```python
import functools

import jax
import jax.numpy as jnp
from jax import lax
from jax.experimental import pallas as pl
from jax.experimental.pallas import tpu as pltpu

F32 = jnp.float32
BF16 = jnp.bfloat16

RW_HEAD = 64
RW_HEADS = 8
RW_WIDTH = RW_HEADS * RW_HEAD
DECAY_LORA = 64
AAA_LORA = 64
GATE_LORA = 128
RW_GN_EPS = 64e-5
RW_COLS = 3 * RW_WIDTH + DECAY_LORA + AAA_LORA + GATE_LORA
GLA_HEADS = 4
GLA_DK = 64
GLA_DV = 128
GLA_KW = GLA_HEADS * GLA_DK
GLA_VW = GLA_HEADS * GLA_DV
GLA_GATE_LORA = 16
GLA_TAU = 16.0
GLA_NORM_EPS = 1e-5
GLA_MAIN = 2 * GLA_KW + 2 * GLA_VW
GLA_COLS = GLA_MAIN + GLA_GATE_LORA
LN_EPS = 1e-5
DEPTH = 1
ALPHA = (2.0 * DEPTH) ** 0.25

LANES = 128
MXU_K = 256
CHUNK = 64
RWKV_UNROLL = 4
RW_GROUP = 2
GLA_UNROLL = 8
REC_TILE = 512
TAIL_TILE = 512
FFN_CHUNKS = 2
VMEM_LIMIT = 58 * 1024 * 1024

COL_RW = 0
COL_GLA = RW_COLS
COL_GATE = COL_GLA + GLA_MAIN


def _sigmoid(x):
    return 1.0 / (1.0 + jnp.exp(-x))


def _softplus(x):
    return jnp.maximum(x, 0.0) + jnp.log1p(jnp.exp(-jnp.abs(x)))


def _dot(a, b):
    return jnp.dot(a.astype(BF16), b.astype(BF16), preferred_element_type=F32)


def _dot_nt(a, b):
    return lax.dot_general(a.astype(BF16), b.astype(BF16), (((1,), (1,)), ((), ())),
                           preferred_element_type=F32)


def _split_hi_lo(x):
    hi = x.astype(BF16)
    lo = (x - hi.astype(F32)).astype(BF16)
    return hi, lo


def _dot_exact_rhs(x, w01):
    hi, lo = _split_hi_lo(x)
    if 2 * x.shape[1] <= MXU_K:
        return jnp.dot(jnp.concatenate([hi, lo], axis=1), jnp.concatenate([w01, w01], axis=0),
                       preferred_element_type=F32)
    return (jnp.dot(hi, w01, preferred_element_type=F32)
            + jnp.dot(lo, w01, preferred_element_type=F32))


def _dot_exact_lhs(w01, x):
    hi, lo = _split_hi_lo(x)
    if 2 * x.shape[0] <= MXU_K:
        return jnp.dot(jnp.concatenate([w01, w01], axis=1), jnp.concatenate([hi, lo], axis=0),
                       preferred_element_type=F32)
    return (jnp.dot(w01, hi, preferred_element_type=F32)
            + jnp.dot(w01, lo, preferred_element_type=F32))


def _layer_norm(x, eps):
    mu = jnp.mean(x, axis=-1, keepdims=True)
    xc = x - mu
    var = jnp.mean(xc * xc, axis=-1, keepdims=True)
    return xc * lax.rsqrt(var + eps)


def _mod_kernel(c_ref, w_ref, b_ref, o_ref):
    c = c_ref[...]
    sc = c * _sigmoid(c)
    o_ref[...] = _dot(sc, w_ref[...]) + b_ref[...]


def _mod_call(c, w_ada, b_ada):
    bsz, d = c.shape
    n = w_ada.shape[1]
    tn = 1536
    return pl.pallas_call(
        _mod_kernel,
        grid=(n // tn,),
        in_specs=[pl.BlockSpec((bsz, d), lambda j: (0, 0)),
                  pl.BlockSpec((d, tn), lambda j: (0, j)),
                  pl.BlockSpec((1, tn), lambda j: (0, j))],
        out_specs=pl.BlockSpec((bsz, tn), lambda j: (0, j)),
        out_shape=jax.ShapeDtypeStruct((bsz, n), F32),
        compiler_params=pltpu.CompilerParams(dimension_semantics=("arbitrary",),
                                             vmem_limit_bytes=VMEM_LIMIT),
        name="mod",
    )(c, w_ada, b_ada.reshape(1, n))


def _wprep_kernel(w_ref, o_ref, *, d_model):
    main = RW_COLS + GLA_MAIN
    o_ref[:main, :] = w_ref[:main, :].astype(BF16)
    o_ref[main:main + 2 * d_model, :] = w_ref[RW_COLS + GLA_COLS:, :].astype(BF16)
    o_ref[main + 2 * d_model:main + 2 * d_model + GLA_GATE_LORA, :] = (
        w_ref[main:RW_COLS + GLA_COLS, :].astype(BF16))
    o_ref[main + 2 * d_model + GLA_GATE_LORA:, :] = jnp.zeros(
        (LANES - GLA_GATE_LORA, o_ref.shape[1]), BF16)


def _wprep_call(w_in_t, d_model):
    n_in = w_in_t.shape[0]
    n_out = RW_COLS + GLA_MAIN + 2 * d_model + LANES
    cols = 256
    return pl.pallas_call(
        functools.partial(_wprep_kernel, d_model=d_model),
        grid=(d_model // cols,),
        in_specs=[pl.BlockSpec((n_in, cols), lambda i: (0, i))],
        out_specs=pl.BlockSpec((n_out, cols), lambda i: (0, i)),
        out_shape=jax.ShapeDtypeStruct((n_out, d_model), BF16),
        compiler_params=pltpu.CompilerParams(dimension_semantics=("arbitrary",),
                                             vmem_limit_bytes=VMEM_LIMIT),
        name="wprep",
    )(w_in_t)


def _inproj_kernel(x_ref, shift_ref, scale_ref, wp_ref, mu_ref, w0_ref, w2_ref, a0_ref, a2_ref,
                   g2_ref, kk_ref, ka_ref, rk_ref, ga2_ref, gab_ref, bd_ref,
                   r_out, k_out, v_out, lw_out, nkk_out, b_out, g_out, bonus_out,
                   gq_out, gk_out, gv_out, gsg_out, gla_out, sgr_out, sgg_out,
                   carry_ref, *, tm, d_model):
    t = pl.program_id(1)

    @pl.when(t == 0)
    def _():
        carry_ref[...] = jnp.zeros_like(carry_ref)

    u = x_ref[...] * (1.0 + scale_ref[...]) + shift_ref[...]
    proj = _dot_nt(u, wp_ref[...])

    p = proj[:, COL_RW:COL_RW + RW_COLS]
    prev = pltpu.roll(p, 1, 0)
    row = lax.broadcasted_iota(jnp.int32, p.shape, 0)
    prev = jnp.where(row == 0, carry_ref[...], prev)
    carry_ref[...] = p[tm - 1:tm, :]
    ps = p + mu_ref[...] * (prev - p)

    r = ps[:, 0:RW_WIDTH]
    k = ps[:, RW_WIDTH:2 * RW_WIDTH]
    v = ps[:, 2 * RW_WIDTH:3 * RW_WIDTH]
    z = ps[:, 3 * RW_WIDTH:3 * RW_WIDTH + LANES]
    gd = ps[:, 3 * RW_WIDTH + LANES:RW_COLS]

    wlin = w0_ref[...] + _dot(jnp.tanh(z), w2_ref[...])
    w = -_softplus(-wlin) - 0.5
    lw_out[...] = -jnp.exp(w)
    a = _sigmoid(a0_ref[...] + _dot(z, a2_ref[...]))
    g_out[...] = _dot(_sigmoid(gd), g2_ref[...])
    kkv = k * kk_ref[...]
    kp = k * (1.0 + (a - 1.0) * ka_ref[...])
    rkr = r * kp * rk_ref[...]
    bd = bd_ref[...]
    for j in range(RW_WIDTH // LANES):
        sl = slice(j * LANES, (j + 1) * LANES)
        kkj = kkv[:, sl]
        ssq = _dot_exact_rhs(kkj * kkj, bd)
        kkn = kkj / jnp.maximum(jnp.sqrt(ssq), 1e-12)
        nkk_out[:, sl] = -kkn
        b_out[:, sl] = kkn * a[:, sl]
        bonus_out[:, sl] = _dot_exact_rhs(rkr[:, sl], bd) * v[:, sl]
    r_out[...] = r
    k_out[...] = kp
    v_out[...] = v.astype(BF16)

    pg = proj[:, COL_GLA:COL_GLA + GLA_MAIN]
    gq_out[...] = pg[:, 0:GLA_KW] * (GLA_DK ** -0.5)
    gk_out[...] = pg[:, GLA_KW:2 * GLA_KW]
    gv_out[...] = pg[:, 2 * GLA_KW:2 * GLA_KW + GLA_VW].astype(BF16)
    gg = pg[:, 2 * GLA_KW + GLA_VW:GLA_MAIN]
    gsg_out[...] = gg * _sigmoid(gg)
    adg = proj[:, COL_GATE + 2 * d_model:COL_GATE + 2 * d_model + LANES]
    la = _dot(adg, ga2_ref[...]) + gab_ref[...]
    gla_out[...] = -_softplus(-la) / GLA_TAU

    sgr_out[...] = proj[:, COL_GATE:COL_GATE + d_model]
    sgg_out[...] = proj[:, COL_GATE + d_model:COL_GATE + 2 * d_model]


def _inproj_call(x2, mod3, wp, small, bsz, seq, d_model):
    tm = 256
    nt = seq // tm
    n_tok = bsz * seq
    (mu, w0, w2p, a0, a2p, g2, k_k, k_a, r_k, ga2p, gab, bd) = small

    def tok(n):
        return pl.BlockSpec((tm, n), lambda b, t: (b * nt + t, 0))

    def full(a):
        return pl.BlockSpec(a.shape, lambda b, t: (0,) * a.ndim)

    def modspec(idx):
        return pl.BlockSpec((None, 1, d_model), lambda b, t: (b, 0, idx))

    outs = [
        (RW_WIDTH, F32), (RW_WIDTH, F32), (RW_WIDTH, BF16), (RW_WIDTH, F32), (RW_WIDTH, F32),
        (RW_WIDTH, F32), (RW_WIDTH, F32), (RW_WIDTH, F32),
        (GLA_KW, F32), (GLA_KW, F32), (GLA_VW, BF16), (GLA_VW, F32), (GLA_KW, F32),
        (d_model, F32), (d_model, F32),
    ]
    return pl.pallas_call(
        functools.partial(_inproj_kernel, tm=tm, d_model=d_model),
        grid=(bsz, nt),
        in_specs=[tok(d_model), modspec(0), modspec(1), full(wp), full(mu), full(w0), full(w2p),
                  full(a0), full(a2p), full(g2), full(k_k), full(k_a), full(r_k), full(ga2p),
                  full(gab), full(bd)],
        out_specs=[tok(n) for n, _ in outs],
        out_shape=[jax.ShapeDtypeStruct((n_tok, n), dt) for n, dt in outs],
        scratch_shapes=[pltpu.VMEM((1, RW_COLS), F32)],
        compiler_params=pltpu.CompilerParams(dimension_semantics=("arbitrary", "arbitrary"),
                                             vmem_limit_bytes=VMEM_LIMIT),
        name="inproj",
    )(x2, mod3, mod3, wp, mu, w0, w2p, a0, a2p, g2, k_k, k_a, r_k, ga2p, gab, bd)


def _rwkv_kernel(r_ref, k_ref, v_ref, lw_ref, nkk_ref, b_ref, g_ref, bonus_ref, gng_ref, gnb_ref,
                 bd_ref, o_ref, h_ref, rp_s, yl_s, m_s, hl_s, *, tb, unroll):
    c = CHUNK
    gw = RW_GROUP * RW_HEAD
    nchunk = tb // c
    ngroup = RW_WIDTH // gw
    t = pl.program_id(1)

    @pl.when(t == 0)
    def _():
        h_ref[...] = jnp.zeros_like(h_ref)

    lane = lax.broadcasted_iota(jnp.int32, (c, gw), 1)
    rowi = lax.broadcasted_iota(jnp.int32, (c, gw), 0)
    head_of_lane = lane // RW_HEAD
    lmask = [head_of_lane == h for h in range(RW_GROUP)]
    strict = (lane % RW_HEAD) < rowi
    incl = (lane % RW_HEAD) <= rowi
    eye = (lane % RW_HEAD) == rowi
    eye_f = eye.astype(F32)
    eye_b = eye.astype(BF16)
    ri = lax.broadcasted_iota(jnp.int32, (c, c), 0)
    ci_ = lax.broadcasted_iota(jnp.int32, (c, c), 1)
    ltri = (ci_ <= ri).astype(BF16)

    def stack(x):
        xb = x.astype(BF16)
        zero = jnp.zeros_like(xb)
        return jnp.concatenate([jnp.where(lmask[h], xb, zero) for h in range(RW_GROUP)], axis=0)

    def local_group(gi, carry):
        units = [(u, g) for u in range(unroll) for g in range(ngroup)]
        n = len(units)
        idxs = [gi * unroll + u for u in range(unroll)]
        rows = [pl.ds(pl.multiple_of(i * c, c), c) for i in idxs]

        def ld(ref, u, g):
            return ref[rows[u], g * gw:(g + 1) * gw]

        lw = [ld(lw_ref, u, g) for u, g in units]
        gcum = [_dot_exact_lhs(ltri, x) for x in lw]
        gend, rt, at, kt, bt, bh, kh, v = [], [], [], [], [], [], [], []
        for s, (u, g) in enumerate(units):
            g_ = gcum[s]
            ge = g_[c - 1:c, :]
            r = ld(r_ref, u, g)
            k = ld(k_ref, u, g)
            nkk = ld(nkk_ref, u, g)
            b = ld(b_ref, u, g)
            e_neg = jnp.exp(-g_)
            dk = jnp.exp(ge - g_)
            gend.append(ge)
            rt.append(r * jnp.exp(g_))
            at.append(nkk * jnp.exp(g_ - lw[s]))
            kt.append(k * e_neg)
            bt.append(b * e_neg)
            bh.append(b * dk)
            kh.append(k * dk)
            v.append(ld(v_ref, u, g))

        lhs = [jnp.concatenate([at[s], rt[s]], axis=0).astype(BF16) for s in range(n)]
        xb = [_dot_nt(lhs[s], stack(bt[s])) for s in range(n)]
        xk = [_dot_nt(lhs[s], stack(kt[s])) for s in range(n)]
        a_ab = [jnp.where(strict, x[:c], 0.0) for x in xb]
        a_rb = [jnp.where(incl, x[c:], 0.0) for x in xb]
        a_ak = [jnp.where(strict, x[:c], 0.0) for x in xk]
        a_rk = [jnp.where(incl, x[c:], 0.0) for x in xk]

        tinv = [eye_f + a for a in a_ab]
        pw = a_ab
        for _ in range(5):
            pw = [_dot(x, stack(x)) for x in pw]
            tinv = [tm_ + _dot(tm_, stack(x)) for tm_, x in zip(tinv, pw)]

        vst = [stack(x) for x in v]
        ap = [_dot(tinv[s], stack(at[s])) for s in range(n)]
        akv = [_dot(a_ak[s], vst[s]) for s in range(n)]
        uloc = [_dot(tinv[s], stack(akv[s])) for s in range(n)]
        rb_ap = [_dot(a_rb[s], stack(ap[s])) for s in range(n)]
        rk_v = [_dot(a_rk[s], vst[s]) for s in range(n)]
        ulst = [stack(x) for x in uloc]
        rb_ul = [_dot(a_rb[s], ulst[s]) for s in range(n)]
        bht = [_dot_nt(eye_b, stack(x)) for x in bh]
        kht = [_dot_nt(eye_b, stack(x)) for x in kh]
        m1 = [_dot(bht[s], stack(ap[s])) for s in range(n)]
        h1 = [_dot(bht[s], ulst[s]) for s in range(n)]
        h2 = [_dot(kht[s], vst[s]) for s in range(n)]
        for s, (u, g) in enumerate(units):
            lanes = slice(g * gw, (g + 1) * gw)
            rp_s[idxs[u], :, lanes] = rt[s] + rb_ap[s]
            yl_s[rows[u], lanes] = rb_ul[s] + rk_v[s]
            m_s[idxs[u], g] = jnp.where(eye, jnp.exp(gend[s]), 0.0) + m1[s]
            hl_s[idxs[u], g] = h1[s] + h2[s]
        return carry

    lax.fori_loop(0, nchunk // unroll, local_group, 0)

    hstate = [h_ref[g] for g in range(ngroup)]
    for ci in range(nchunk):
        rows = slice(ci * c, (ci + 1) * c)
        rp = rp_s[ci]
        hst = [stack(hstate[g]) for g in range(ngroup)]
        ys = [_dot(rp[:, g * gw:(g + 1) * gw], hst[g]) for g in range(ngroup)]
        hn = [_dot(m_s[ci, g], hst[g]) for g in range(ngroup)]
        for g in range(ngroup):
            lanes = slice(g * gw, (g + 1) * gw)
            yl_s[rows, lanes] = ys[g] + yl_s[rows, lanes]
            hstate[g] = hn[g] + hl_s[ci, g]
    for g in range(ngroup):
        h_ref[g] = hstate[g]

    bd = bd_ref[...]
    for p in range(RW_WIDTH // LANES):
        lanes = slice(p * LANES, (p + 1) * LANES)
        y = yl_s[:, lanes]
        mean = _dot_exact_rhs(y, bd) * (1.0 / RW_HEAD)
        yc = y - mean
        var = _dot_exact_rhs(yc * yc, bd) * (1.0 / RW_HEAD)
        yn = yc * lax.rsqrt(var + RW_GN_EPS)
        out = (yn * gng_ref[:, lanes] + gnb_ref[:, lanes] + bonus_ref[:, lanes]) * g_ref[:, lanes]
        o_ref[:, lanes] = out.astype(o_ref.dtype)


def _rwkv_call(r, k, v, lw, nkk, b, g, bonus, gng, gnb, bd, bsz, seq):
    tb = REC_TILE
    nt = seq // tb
    n_tok = bsz * seq
    gw = RW_GROUP * RW_HEAD
    ngroup = RW_WIDTH // gw
    nchunk = tb // CHUNK

    def tok():
        return pl.BlockSpec((tb, RW_WIDTH), lambda bb, t: (bb * nt + t, 0))

    def full(a):
        return pl.BlockSpec(a.shape, lambda bb, t: (0,) * a.ndim)

    return pl.pallas_call(
        functools.partial(_rwkv_kernel, tb=tb, unroll=RWKV_UNROLL),
        grid=(bsz, nt),
        in_specs=[tok() for _ in range(8)] + [full(gng), full(gnb), full(bd)],
        out_specs=tok(),
        out_shape=jax.ShapeDtypeStruct((n_tok, RW_WIDTH), BF16),
        scratch_shapes=[pltpu.VMEM((ngroup, RW_HEAD, gw), F32),
                        pltpu.VMEM((nchunk, CHUNK, RW_WIDTH), F32),
                        pltpu.VMEM((tb, RW_WIDTH), F32),
                        pltpu.VMEM((nchunk, ngroup, RW_HEAD, gw), F32),
                        pltpu.VMEM((nchunk, ngroup, RW_HEAD, gw), F32)],
        compiler_params=pltpu.CompilerParams(
            dimension_semantics=("arbitrary", "arbitrary"),
            vmem_limit_bytes=VMEM_LIMIT),
        name="rwkv",
    )(r, k, v, lw, nkk, b, g, bonus, gng, gnb, bd)


def _gla_kernel(q_ref, k_ref, v_ref, sg_ref, la_ref, ng_ref, o_ref, st_ref, qs_s, oi_s, kv_s, eb_s,
                *, tb, unroll):
    c = CHUNK
    nchunk = tb // c
    t = pl.program_id(1)

    @pl.when(t == 0)
    def _():
        st_ref[...] = jnp.zeros_like(st_ref)

    lane = lax.broadcasted_iota(jnp.int32, (1, LANES), 1)
    hmask = ((lane < GLA_DK).astype(F32), (lane >= GLA_DK).astype(F32))
    ri = lax.broadcasted_iota(jnp.int32, (c, c), 0)
    ci_ = lax.broadcasted_iota(jnp.int32, (c, c), 1)
    causal = ci_ <= ri
    ltri = causal.astype(BF16)

    def local_group(gi, carry):
        units = [(u, h) for u in range(unroll) for h in range(GLA_HEADS)]
        idxs = [gi * unroll + u for u in range(unroll)]
        rows = [pl.ds(pl.multiple_of(i * c, c), c) for i in idxs]

        def qk_lanes(h):
            return slice((h // 2) * LANES, (h // 2 + 1) * LANES)

        la = [la_ref[rows[u], qk_lanes(h)] for u, h in units]
        bcum = [_dot_exact_lhs(ltri, x) for x in la]
        q_s, k_s, kdec, v = [], [], [], []
        for s, (u, h) in enumerate(units):
            bc = bcum[s]
            blast = bc[c - 1:c, :]
            q = q_ref[rows[u], qk_lanes(h)]
            k = k_ref[rows[u], qk_lanes(h)]
            hm = hmask[h % 2]
            q_s.append((q * jnp.exp(bc) * hm).astype(BF16))
            k_s.append(k * jnp.exp(-bc))
            kdec.append(k * jnp.exp(blast - bc) * hm)
            v.append(v_ref[rows[u], h * LANES:(h + 1) * LANES])
            eb_s[idxs[u], h] = jnp.exp(blast)
        att = [jnp.where(causal, _dot_nt(q_s[s], k_s[s]), 0.0) for s in range(len(units))]
        oi = [_dot(att[s], v[s]) for s in range(len(units))]
        v_t = [x.astype(F32).T.astype(BF16) for x in v]
        kv = [_dot(v_t[s], kdec[s]) for s in range(len(units))]
        for s, (u, h) in enumerate(units):
            qs_s[idxs[u], h] = q_s[s]
            oi_s[rows[u], h * LANES:(h + 1) * LANES] = oi[s]
            kv_s[idxs[u], h] = kv[s]
        return carry

    lax.fori_loop(0, nchunk // unroll, local_group, 0)

    st = [st_ref[h] for h in range(GLA_HEADS)]
    for ci in range(nchunk):
        rows = slice(ci * c, (ci + 1) * c)
        inter = [_dot_nt(qs_s[ci, h], st[h]) for h in range(GLA_HEADS)]
        for h in range(GLA_HEADS):
            lanes = slice(h * LANES, (h + 1) * LANES)
            oi_s[rows, lanes] = oi_s[rows, lanes] + inter[h]
            st[h] = st[h] * eb_s[ci, h] + kv_s[ci, h]
    for h in range(GLA_HEADS):
        st_ref[h] = st[h]

    ng = ng_ref[...]
    for h in range(GLA_HEADS):
        lanes = slice(h * LANES, (h + 1) * LANES)
        o = oi_s[:, lanes]
        o = o * lax.rsqrt(jnp.mean(o * o, axis=-1, keepdims=True) + GLA_NORM_EPS) * ng
        o_ref[:, lanes] = (o * sg_ref[:, lanes]).astype(o_ref.dtype)


def _gla_call(q, k, v, sg, la, ng, bsz, seq):
    tb = REC_TILE
    nt = seq // tb
    n_tok = bsz * seq
    nchunk = tb // CHUNK

    def tok(n):
        return pl.BlockSpec((tb, n), lambda bb, t: (bb * nt + t, 0))

    return pl.pallas_call(
        functools.partial(_gla_kernel, tb=tb, unroll=GLA_UNROLL),
        grid=(bsz, nt),
        in_specs=[tok(GLA_KW), tok(GLA_KW), tok(GLA_VW), tok(GLA_VW), tok(GLA_KW),
                  pl.BlockSpec((1, LANES), lambda bb, t: (0, 0))],
        out_specs=tok(GLA_VW),
        out_shape=jax.ShapeDtypeStruct((n_tok, GLA_VW), BF16),
        scratch_shapes=[pltpu.VMEM((GLA_HEADS, GLA_DV, LANES), F32),
                        pltpu.VMEM((nchunk, GLA_HEADS, CHUNK, LANES), BF16),
                        pltpu.VMEM((tb, GLA_VW), F32),
                        pltpu.VMEM((nchunk, GLA_HEADS, GLA_DV, LANES), F32),
                        pltpu.VMEM((nchunk, GLA_HEADS, 1, LANES), F32)],
        compiler_params=pltpu.CompilerParams(
            dimension_semantics=("arbitrary", "arbitrary"),
            vmem_limit_bytes=VMEM_LIMIT),
        name="gla",
    )(q, k, v, sg, la, ng)


def _tail_kernel(x_ref, orw_ref, ogla_ref, sgr_ref, sgg_ref, gate1_ref, shift2_ref, scale2_ref,
                 gate2_ref, wrb_ref, wgb_ref, wmix_ref, win_ref, wout_ref, ln1g_ref, ln1b_ref,
                 ln2g_ref, ln2b_ref, o_ref, *, d_ff):
    merged = (_sigmoid(sgr_ref[...]) * jnp.dot(orw_ref[...], wrb_ref[...], preferred_element_type=F32)
              + _sigmoid(sgg_ref[...]) * jnp.dot(ogla_ref[...], wgb_ref[...],
                                                 preferred_element_type=F32))
    mix = _dot(merged, wmix_ref[...])
    x1 = _layer_norm(ALPHA * x_ref[...] + gate1_ref[...] * mix, LN_EPS) * ln1g_ref[...] + ln1b_ref[...]

    u = (x1 * (1.0 + scale2_ref[...]) + shift2_ref[...]).astype(BF16)
    ffn = jnp.zeros_like(x1)
    ntile = d_ff // MXU_K
    bounds = [MXU_K * ((ntile * j) // FFN_CHUNKS) for j in range(FFN_CHUNKS)] + [d_ff]
    for lo, hi in zip(bounds[:-1], bounds[1:]):
        hg = jnp.dot(u, win_ref[:, lo:hi], preferred_element_type=F32)
        hu = jnp.dot(u, win_ref[:, d_ff + lo:d_ff + hi], preferred_element_type=F32)
        act = hg * _sigmoid(hg) * hu
        ffn = ffn + _dot(act, wout_ref[lo:hi, :])
    y = _layer_norm(ALPHA * x1 + gate2_ref[...] * ffn, LN_EPS)
    o_ref[...] = y * ln2g_ref[...] + ln2b_ref[...]


def _tail_call(x2, o_rw, o_gla, sgr, sgg, mod3, wrb, wgb, wmix, win, wout, ln1g, ln1b, ln2g, ln2b,
               bsz, seq, d_model):
    tm = TAIL_TILE
    nt = seq // tm
    n_tok = bsz * seq
    d_ff = wout.shape[0]
    assert d_ff % MXU_K == 0

    def tok(n):
        return pl.BlockSpec((tm, n), lambda b, t: (b * nt + t, 0))

    def const(a):
        return pl.BlockSpec(a.shape, lambda b, t: (0,) * a.ndim, pipeline_mode=pl.Buffered(1))

    def modspec(idx):
        return pl.BlockSpec((None, 1, d_model), lambda b, t: (b, 0, idx))

    return pl.pallas_call(
        functools.partial(_tail_kernel, d_ff=d_ff),
        grid=(bsz, nt),
        in_specs=[tok(d_model), tok(RW_WIDTH), tok(GLA_VW), tok(d_model), tok(d_model),
                  modspec(2), modspec(3), modspec(4), modspec(5),
                  const(wrb), const(wgb), const(wmix), const(win), const(wout),
                  const(ln1g), const(ln1b), const(ln2g), const(ln2b)],
        out_specs=tok(d_model),
        out_shape=jax.ShapeDtypeStruct((n_tok, d_model), F32),
        compiler_params=pltpu.CompilerParams(dimension_semantics=("arbitrary", "arbitrary"),
                                             vmem_limit_bytes=VMEM_LIMIT),
        name="tail",
    )(x2, o_rw, o_gla, sgr, sgg, mod3, mod3, mod3, mod3, wrb, wgb, wmix, win, wout,
      ln1g, ln1b, ln2g, ln2b)


def kernel(x, c, w_ada, b_ada, w_in, mu_rw, rw_w0, rw_w2, rw_a0, rw_a2, rw_g2, rw_k_k, rw_k_a,
           rw_r_k, rw_gn_g, rw_gn_b, gla_a2, gla_a_b, gla_norm_g, w_rw_branch, w_gla_branch,
           w_mix_out, ln1_g, ln1_b, w_ffn_in, w_ffn_out, ln2_g, ln2_b):
    bsz, seq, d_model = x.shape
    assert w_ada.shape[0] == DEPTH and seq % REC_TILE == 0
    n_tok = bsz * seq
    l = 0

    wp = _wprep_call(jnp.swapaxes(w_in[l], 0, 1), d_model)
    zeros_lora = jnp.zeros((DECAY_LORA, RW_WIDTH), F32)
    w2p = jnp.concatenate([rw_w2[l], zeros_lora], axis=0).astype(BF16)
    a2p = jnp.concatenate([zeros_lora, rw_a2[l]], axis=0).astype(BF16)
    ga2p = jnp.concatenate(
        [gla_a2[l], jnp.zeros((LANES - GLA_GATE_LORA, GLA_KW), F32)], axis=0).astype(BF16)
    row = lambda a: a.reshape(1, -1)
    hid = jnp.arange(LANES) // RW_HEAD
    bd = (hid[:, None] == hid[None, :]).astype(BF16)
    small = (row(mu_rw[l]), row(rw_w0[l]), w2p, row(rw_a0[l]), a2p, rw_g2[l].astype(BF16),
             row(rw_k_k[l]), row(rw_k_a[l]), row(rw_r_k[l]), ga2p, row(gla_a_b[l]), bd)

    x2 = x.reshape(n_tok, d_model)
    mod = _mod_call(c, w_ada[l], b_ada[l])
    mod3 = mod.reshape(bsz, 1, 6 * d_model)

    (r, k, v, lw, nkk, b, g, bonus, gq, gk, gv, gsg, gla, sgr, sgg) = _inproj_call(
        x2, mod3, wp, small, bsz, seq, d_model)

    o_rw = _rwkv_call(r, k, v, lw, nkk, b, g, bonus, row(rw_gn_g[l]), row(rw_gn_b[l]), bd, bsz, seq)
    o_gla = _gla_call(gq, gk, gv, gsg, gla, row(gla_norm_g[l]), bsz, seq)

    out = _tail_call(x2, o_rw, o_gla, sgr, sgg, mod3, w_rw_branch[l].astype(BF16),
                     w_gla_branch[l].astype(BF16), w_mix_out[l].astype(BF16),
                     w_ffn_in[l].astype(BF16), w_ffn_out[l].astype(BF16),
                     row(ln1_g[l]), row(ln1_b[l]), row(ln2_g[l]), row(ln2_b[l]), bsz, seq, d_model)
    return out.reshape(bsz, seq, d_model)
```

```python
import functools

import jax
import jax.numpy as jnp
from jax import lax
from jax.experimental import pallas as pl
from jax.experimental.pallas import tpu as pltpu

F32 = jnp.float32
BF16 = jnp.bfloat16

RW_HEAD = 64
RW_HEADS = 8
RW_WIDTH = RW_HEADS * RW_HEAD
DECAY_LORA = 64
AAA_LORA = 64
GATE_LORA = 128
RW_GN_EPS = 64e-5
RW_COLS = 3 * RW_WIDTH + DECAY_LORA + AAA_LORA + GATE_LORA
GLA_HEADS = 4
GLA_DK = 64
GLA_DV = 128
GLA_KW = GLA_HEADS * GLA_DK
GLA_VW = GLA_HEADS * GLA_DV
GLA_GATE_LORA = 16
GLA_TAU = 16.0
GLA_NORM_EPS = 1e-5
GLA_MAIN = 2 * GLA_KW + 2 * GLA_VW
GLA_COLS = GLA_MAIN + GLA_GATE_LORA
LN_EPS = 1e-5
DEPTH = 1
ALPHA = (2.0 * DEPTH) ** 0.25

LANES = 128
MXU_K = 256
CHUNK = 64
RWKV_UNROLL = 4
RW_GROUP = 2
GLA_UNROLL = 8
REC_TILE = 512
TAIL_TILE = 512
TAIL_SPLIT = 2
FFN_CHUNKS = 2
VMEM_LIMIT = 58 * 1024 * 1024

COL_RW = 0
COL_GLA = RW_COLS
COL_GATE = COL_GLA + GLA_MAIN


def _sigmoid(x):
    return 1.0 / (1.0 + jnp.exp(-x))


def _softplus(x):
    return jnp.maximum(x, 0.0) + jnp.log1p(jnp.exp(-jnp.abs(x)))


def _dot(a, b):
    return jnp.dot(a.astype(BF16), b.astype(BF16), preferred_element_type=F32)


def _dot_nt(a, b):
    return lax.dot_general(a.astype(BF16), b.astype(BF16), (((1,), (1,)), ((), ())),
                           preferred_element_type=F32)


def _split_hi_lo(x):
    hi = x.astype(BF16)
    lo = (x - hi.astype(F32)).astype(BF16)
    return hi, lo


def _dot_exact_rhs(x, w01):
    hi, lo = _split_hi_lo(x)
    if 2 * x.shape[1] <= MXU_K:
        return jnp.dot(jnp.concatenate([hi, lo], axis=1), jnp.concatenate([w01, w01], axis=0),
                       preferred_element_type=F32)
    return (jnp.dot(hi, w01, preferred_element_type=F32)
            + jnp.dot(lo, w01, preferred_element_type=F32))


def _dot_exact_lhs(w01, x):
    hi, lo = _split_hi_lo(x)
    if 2 * x.shape[0] <= MXU_K:
        return jnp.dot(jnp.concatenate([w01, w01], axis=1), jnp.concatenate([hi, lo], axis=0),
                       preferred_element_type=F32)
    return (jnp.dot(w01, hi, preferred_element_type=F32)
            + jnp.dot(w01, lo, preferred_element_type=F32))


def _layer_norm(x, eps):
    mu = jnp.mean(x, axis=-1, keepdims=True)
    xc = x - mu
    var = jnp.mean(xc * xc, axis=-1, keepdims=True)
    return xc * lax.rsqrt(var + eps)


def _mod_kernel(c_ref, w_ref, b_ref, o_ref):
    c = c_ref[...]
    sc = c * _sigmoid(c)
    o_ref[...] = _dot(sc, w_ref[...]) + b_ref[...]


def _mod_call(c, w_ada, b_ada):
    bsz, d = c.shape
    n = w_ada.shape[1]
    tn = 1536
    return pl.pallas_call(
        _mod_kernel,
        grid=(n // tn,),
        in_specs=[pl.BlockSpec((bsz, d), lambda j: (0, 0)),
                  pl.BlockSpec((d, tn), lambda j: (0, j)),
                  pl.BlockSpec((1, tn), lambda j: (0, j))],
        out_specs=pl.BlockSpec((bsz, tn), lambda j: (0, j)),
        out_shape=jax.ShapeDtypeStruct((bsz, n), F32),
        compiler_params=pltpu.CompilerParams(dimension_semantics=("arbitrary",),
                                             vmem_limit_bytes=VMEM_LIMIT),
        name="mod",
    )(c, w_ada, b_ada.reshape(1, n))


def _wprep_kernel(w_ref, o_ref, *, d_model):
    main = RW_COLS + GLA_MAIN
    o_ref[:main, :] = w_ref[:main, :].astype(BF16)
    o_ref[main:main + 2 * d_model, :] = w_ref[RW_COLS + GLA_COLS:, :].astype(BF16)
    o_ref[main + 2 * d_model:main + 2 * d_model + GLA_GATE_LORA, :] = (
        w_ref[main:RW_COLS + GLA_COLS, :].astype(BF16))
    o_ref[main + 2 * d_model + GLA_GATE_LORA:, :] = jnp.zeros(
        (LANES - GLA_GATE_LORA, o_ref.shape[1]), BF16)


def _wprep_call(w_in_t, d_model):
    n_in = w_in_t.shape[0]
    n_out = RW_COLS + GLA_MAIN + 2 * d_model + LANES
    cols = 256
    return pl.pallas_call(
        functools.partial(_wprep_kernel, d_model=d_model),
        grid=(d_model // cols,),
        in_specs=[pl.BlockSpec((n_in, cols), lambda i: (0, i))],
        out_specs=pl.BlockSpec((n_out, cols), lambda i: (0, i)),
        out_shape=jax.ShapeDtypeStruct((n_out, d_model), BF16),
        compiler_params=pltpu.CompilerParams(dimension_semantics=("arbitrary",),
                                             vmem_limit_bytes=VMEM_LIMIT),
        name="wprep",
    )(w_in_t)


def _inproj_kernel(x_ref, shift_ref, scale_ref, wp_ref, mu_ref, w0_ref, w2_ref, a0_ref, a2_ref,
                   g2_ref, kk_ref, ka_ref, rk_ref, ga2_ref, gab_ref, bd_ref,
                   r_out, k_out, v_out, lw_out, nkk_out, b_out, g_out, bonus_out,
                   gq_out, gk_out, gv_out, gsg_out, gla_out, sgr_out, sgg_out,
                   carry_ref, *, tm, d_model):
    t = pl.program_id(1)

    @pl.when(t == 0)
    def _():
        carry_ref[...] = jnp.zeros_like(carry_ref)

    u = x_ref[...] * (1.0 + scale_ref[...]) + shift_ref[...]
    proj = _dot_nt(u, wp_ref[...])

    p = proj[:, COL_RW:COL_RW + RW_COLS]
    prev = pltpu.roll(p, 1, 0)
    row = lax.broadcasted_iota(jnp.int32, p.shape, 0)
    prev = jnp.where(row == 0, carry_ref[...], prev)
    carry_ref[...] = p[tm - 1:tm, :]
    ps = p + mu_ref[...] * (prev - p)

    r = ps[:, 0:RW_WIDTH]
    k = ps[:, RW_WIDTH:2 * RW_WIDTH]
    v = ps[:, 2 * RW_WIDTH:3 * RW_WIDTH]
    z = ps[:, 3 * RW_WIDTH:3 * RW_WIDTH + LANES]
    gd = ps[:, 3 * RW_WIDTH + LANES:RW_COLS]

    wlin = w0_ref[...] + _dot(jnp.tanh(z), w2_ref[...])
    w = -_softplus(-wlin) - 0.5
    lw_out[...] = -jnp.exp(w)
    a = _sigmoid(a0_ref[...] + _dot(z, a2_ref[...]))
    g_out[...] = _dot(_sigmoid(gd), g2_ref[...])
    kkv = k * kk_ref[...]
    kp = k * (1.0 + (a - 1.0) * ka_ref[...])
    rkr = r * kp * rk_ref[...]
    bd = bd_ref[...]
    for j in range(RW_WIDTH // LANES):
        sl = slice(j * LANES, (j + 1) * LANES)
        kkj = kkv[:, sl]
        ssq = _dot_exact_rhs(kkj * kkj, bd)
        kkn = kkj / jnp.maximum(jnp.sqrt(ssq), 1e-12)
        nkk_out[:, sl] = -kkn
        b_out[:, sl] = kkn * a[:, sl]
        bonus_out[:, sl] = _dot_exact_rhs(rkr[:, sl], bd) * v[:, sl]
    r_out[...] = r
    k_out[...] = kp
    v_out[...] = v.astype(BF16)

    pg = proj[:, COL_GLA:COL_GLA + GLA_MAIN]
    gq_out[...] = pg[:, 0:GLA_KW] * (GLA_DK ** -0.5)
    gk_out[...] = pg[:, GLA_KW:2 * GLA_KW]
    gv_out[...] = pg[:, 2 * GLA_KW:2 * GLA_KW + GLA_VW].astype(BF16)
    gg = pg[:, 2 * GLA_KW + GLA_VW:GLA_MAIN]
    gsg_out[...] = gg * _sigmoid(gg)
    adg = proj[:, COL_GATE + 2 * d_model:COL_GATE + 2 * d_model + LANES]
    la = _dot(adg, ga2_ref[...]) + gab_ref[...]
    gla_out[...] = -_softplus(-la) / GLA_TAU

    sgr_out[...] = proj[:, COL_GATE:COL_GATE + d_model]
    sgg_out[...] = proj[:, COL_GATE + d_model:COL_GATE + 2 * d_model]


def _inproj_call(x2, mod3, wp, small, bsz, seq, d_model):
    tm = 256
    nt = seq // tm
    n_tok = bsz * seq
    (mu, w0, w2p, a0, a2p, g2, k_k, k_a, r_k, ga2p, gab, bd) = small

    def tok(n):
        return pl.BlockSpec((tm, n), lambda b, t: (b * nt + t, 0))

    def full(a):
        return pl.BlockSpec(a.shape, lambda b, t: (0,) * a.ndim)

    def modspec(idx):
        return pl.BlockSpec((None, 1, d_model), lambda b, t: (b, 0, idx))

    outs = [
        (RW_WIDTH, F32), (RW_WIDTH, F32), (RW_WIDTH, BF16), (RW_WIDTH, F32), (RW_WIDTH, F32),
        (RW_WIDTH, F32), (RW_WIDTH, F32), (RW_WIDTH, F32),
        (GLA_KW, F32), (GLA_KW, F32), (GLA_VW, BF16), (GLA_VW, F32), (GLA_KW, F32),
        (d_model, F32), (d_model, F32),
    ]
    return pl.pallas_call(
        functools.partial(_inproj_kernel, tm=tm, d_model=d_model),
        grid=(bsz, nt),
        in_specs=[tok(d_model), modspec(0), modspec(1), full(wp), full(mu), full(w0), full(w2p),
                  full(a0), full(a2p), full(g2), full(k_k), full(k_a), full(r_k), full(ga2p),
                  full(gab), full(bd)],
        out_specs=[tok(n) for n, _ in outs],
        out_shape=[jax.ShapeDtypeStruct((n_tok, n), dt) for n, dt in outs],
        scratch_shapes=[pltpu.VMEM((1, RW_COLS), F32)],
        compiler_params=pltpu.CompilerParams(dimension_semantics=("arbitrary", "arbitrary"),
                                             vmem_limit_bytes=VMEM_LIMIT),
        name="inproj",
    )(x2, mod3, mod3, wp, mu, w0, w2p, a0, a2p, g2, k_k, k_a, r_k, ga2p, gab, bd)


def _rwkv_kernel(r_ref, k_ref, v_ref, lw_ref, nkk_ref, b_ref, g_ref, bonus_ref, gng_ref, gnb_ref,
                 bd_ref, o_ref, h_ref, rp_s, yl_s, m_s, hl_s, *, tb, unroll):
    c = CHUNK
    gw = RW_GROUP * RW_HEAD
    nchunk = tb // c
    ngroup = RW_WIDTH // gw
    t = pl.program_id(1)

    @pl.when(t == 0)
    def _():
        h_ref[...] = jnp.zeros_like(h_ref)

    lane = lax.broadcasted_iota(jnp.int32, (c, gw), 1)
    rowi = lax.broadcasted_iota(jnp.int32, (c, gw), 0)
    head_of_lane = lane // RW_HEAD
    lmask = [head_of_lane == h for h in range(RW_GROUP)]
    strict = (lane % RW_HEAD) < rowi
    incl = (lane % RW_HEAD) <= rowi
    eye = (lane % RW_HEAD) == rowi
    eye_f = eye.astype(F32)
    eye_b = eye.astype(BF16)
    ri = lax.broadcasted_iota(jnp.int32, (c, c), 0)
    ci_ = lax.broadcasted_iota(jnp.int32, (c, c), 1)
    ltri = (ci_ <= ri).astype(BF16)

    def stack(x):
        xb = x.astype(BF16)
        zero = jnp.zeros_like(xb)
        return jnp.concatenate([jnp.where(lmask[h], xb, zero) for h in range(RW_GROUP)], axis=0)

    def local_group(gi, carry):
        units = [(u, g) for u in range(unroll) for g in range(ngroup)]
        n = len(units)
        idxs = [gi * unroll + u for u in range(unroll)]
        rows = [pl.ds(pl.multiple_of(i * c, c), c) for i in idxs]

        def ld(ref, u, g):
            return ref[rows[u], g * gw:(g + 1) * gw]

        lw = [ld(lw_ref, u, g) for u, g in units]
        gcum = [_dot_exact_lhs(ltri, x) for x in lw]
        gend, rt, at, kt, bt, bh, kh, v = [], [], [], [], [], [], [], []
        for s, (u, g) in enumerate(units):
            g_ = gcum[s]
            ge = g_[c - 1:c, :]
            r = ld(r_ref, u, g)
            k = ld(k_ref, u, g)
            nkk = ld(nkk_ref, u, g)
            b = ld(b_ref, u, g)
            e_neg = jnp.exp(-g_)
            dk = jnp.exp(ge - g_)
            gend.append(ge)
            rt.append(r * jnp.exp(g_))
            at.append(nkk * jnp.exp(g_ - lw[s]))
            kt.append(k * e_neg)
            bt.append(b * e_neg)
            bh.append(b * dk)
            kh.append(k * dk)
            v.append(ld(v_ref, u, g))

        lhs = [jnp.concatenate([at[s], rt[s]], axis=0).astype(BF16) for s in range(n)]
        xb = [_dot_nt(lhs[s], stack(bt[s])) for s in range(n)]
        xk = [_dot_nt(lhs[s], stack(kt[s])) for s in range(n)]
        a_ab = [jnp.where(strict, x[:c], 0.0) for x in xb]
        a_rb = [jnp.where(incl, x[c:], 0.0) for x in xb]
        a_ak = [jnp.where(strict, x[:c], 0.0) for x in xk]
        a_rk = [jnp.where(incl, x[c:], 0.0) for x in xk]

        tinv = [eye_f + a for a in a_ab]
        pw = a_ab
        for _ in range(5):
            pw = [_dot(x, stack(x)) for x in pw]
            tinv = [tm_ + _dot(tm_, stack(x)) for tm_, x in zip(tinv, pw)]

        vst = [stack(x) for x in v]
        ap = [_dot(tinv[s], stack(at[s])) for s in range(n)]
        akv = [_dot(a_ak[s], vst[s]) for s in range(n)]
        uloc = [_dot(tinv[s], stack(akv[s])) for s in range(n)]
        rb_ap = [_dot(a_rb[s], stack(ap[s])) for s in range(n)]
        rk_v = [_dot(a_rk[s], vst[s]) for s in range(n)]
        ulst = [stack(x) for x in uloc]
        rb_ul = [_dot(a_rb[s], ulst[s]) for s in range(n)]
        bht = [_dot_nt(eye_b, stack(x)) for x in bh]
        kht = [_dot_nt(eye_b, stack(x)) for x in kh]
        m1 = [_dot(bht[s], stack(ap[s])) for s in range(n)]
        h1 = [_dot(bht[s], ulst[s]) for s in range(n)]
        h2 = [_dot(kht[s], vst[s]) for s in range(n)]
        for s, (u, g) in enumerate(units):
            lanes = slice(g * gw, (g + 1) * gw)
            rp_s[idxs[u], :, lanes] = rt[s] + rb_ap[s]
            yl_s[rows[u], lanes] = rb_ul[s] + rk_v[s]
            m_s[idxs[u], g] = jnp.where(eye, jnp.exp(gend[s]), 0.0) + m1[s]
            hl_s[idxs[u], g] = h1[s] + h2[s]
        return carry

    lax.fori_loop(0, nchunk // unroll, local_group, 0)

    hstate = [h_ref[g] for g in range(ngroup)]
    for ci in range(nchunk):
        rows = slice(ci * c, (ci + 1) * c)
        rp = rp_s[ci]
        hst = [stack(hstate[g]) for g in range(ngroup)]
        ys = [_dot(rp[:, g * gw:(g + 1) * gw], hst[g]) for g in range(ngroup)]
        hn = [_dot(m_s[ci, g], hst[g]) for g in range(ngroup)]
        for g in range(ngroup):
            lanes = slice(g * gw, (g + 1) * gw)
            yl_s[rows, lanes] = ys[g] + yl_s[rows, lanes]
            hstate[g] = hn[g] + hl_s[ci, g]
    for g in range(ngroup):
        h_ref[g] = hstate[g]

    bd = bd_ref[...]
    for p in range(RW_WIDTH // LANES):
        lanes = slice(p * LANES, (p + 1) * LANES)
        y = yl_s[:, lanes]
        mean = _dot_exact_rhs(y, bd) * (1.0 / RW_HEAD)
        yc = y - mean
        var = _dot_exact_rhs(yc * yc, bd) * (1.0 / RW_HEAD)
        yn = yc * lax.rsqrt(var + RW_GN_EPS)
        out = (yn * gng_ref[:, lanes] + gnb_ref[:, lanes] + bonus_ref[:, lanes]) * g_ref[:, lanes]
        o_ref[:, lanes] = out.astype(o_ref.dtype)


def _rwkv_call(r, k, v, lw, nkk, b, g, bonus, gng, gnb, bd, bsz, seq):
    tb = REC_TILE
    nt = seq // tb
    n_tok = bsz * seq
    gw = RW_GROUP * RW_HEAD
    ngroup = RW_WIDTH // gw
    nchunk = tb // CHUNK

    def tok():
        return pl.BlockSpec((tb, RW_WIDTH), lambda bb, t: (bb * nt + t, 0))

    def full(a):
        return pl.BlockSpec(a.shape, lambda bb, t: (0,) * a.ndim)

    return pl.pallas_call(
        functools.partial(_rwkv_kernel, tb=tb, unroll=RWKV_UNROLL),
        grid=(bsz, nt),
        in_specs=[tok() for _ in range(8)] + [full(gng), full(gnb), full(bd)],
        out_specs=tok(),
        out_shape=jax.ShapeDtypeStruct((n_tok, RW_WIDTH), BF16),
        scratch_shapes=[pltpu.VMEM((ngroup, RW_HEAD, gw), F32),
                        pltpu.VMEM((nchunk, CHUNK, RW_WIDTH), F32),
                        pltpu.VMEM((tb, RW_WIDTH), F32),
                        pltpu.VMEM((nchunk, ngroup, RW_HEAD, gw), F32),
                        pltpu.VMEM((nchunk, ngroup, RW_HEAD, gw), F32)],
        compiler_params=pltpu.CompilerParams(
            dimension_semantics=("arbitrary", "arbitrary"),
            vmem_limit_bytes=VMEM_LIMIT),
        name="rwkv",
    )(r, k, v, lw, nkk, b, g, bonus, gng, gnb, bd)


def _gla_kernel(q_ref, k_ref, v_ref, sg_ref, la_ref, ng_ref, o_ref, st_ref, qs_s, oi_s, kv_s, eb_s,
                *, tb, unroll):
    c = CHUNK
    nchunk = tb // c
    t = pl.program_id(1)

    @pl.when(t == 0)
    def _():
        st_ref[...] = jnp.zeros_like(st_ref)

    lane = lax.broadcasted_iota(jnp.int32, (1, LANES), 1)
    hmask = ((lane < GLA_DK).astype(F32), (lane >= GLA_DK).astype(F32))
    ri = lax.broadcasted_iota(jnp.int32, (c, c), 0)
    ci_ = lax.broadcasted_iota(jnp.int32, (c, c), 1)
    causal = ci_ <= ri
    ltri = causal.astype(BF16)

    def local_group(gi, carry):
        units = [(u, h) for u in range(unroll) for h in range(GLA_HEADS)]
        idxs = [gi * unroll + u for u in range(unroll)]
        rows = [pl.ds(pl.multiple_of(i * c, c), c) for i in idxs]

        def qk_lanes(h):
            return slice((h // 2) * LANES, (h // 2 + 1) * LANES)

        la = [la_ref[rows[u], qk_lanes(h)] for u, h in units]
        bcum = [_dot_exact_lhs(ltri, x) for x in la]
        q_s, k_s, kdec, v = [], [], [], []
        for s, (u, h) in enumerate(units):
            bc = bcum[s]
            blast = bc[c - 1:c, :]
            q = q_ref[rows[u], qk_lanes(h)]
            k = k_ref[rows[u], qk_lanes(h)]
            hm = hmask[h % 2]
            q_s.append((q * jnp.exp(bc) * hm).astype(BF16))
            k_s.append(k * jnp.exp(-bc))
            kdec.append(k * jnp.exp(blast - bc) * hm)
            v.append(v_ref[rows[u], h * LANES:(h + 1) * LANES])
            eb_s[idxs[u], h] = jnp.exp(blast)
        att = [jnp.where(causal, _dot_nt(q_s[s], k_s[s]), 0.0) for s in range(len(units))]
        oi = [_dot(att[s], v[s]) for s in range(len(units))]
        v_t = [x.astype(F32).T.astype(BF16) for x in v]
        kv = [_dot(v_t[s], kdec[s]) for s in range(len(units))]
        for s, (u, h) in enumerate(units):
            qs_s[idxs[u], h] = q_s[s]
            oi_s[rows[u], h * LANES:(h + 1) * LANES] = oi[s]
            kv_s[idxs[u], h] = kv[s]
        return carry

    lax.fori_loop(0, nchunk // unroll, local_group, 0)

    st = [st_ref[h] for h in range(GLA_HEADS)]
    for ci in range(nchunk):
        rows = slice(ci * c, (ci + 1) * c)
        inter = [_dot_nt(qs_s[ci, h], st[h]) for h in range(GLA_HEADS)]
        for h in range(GLA_HEADS):
            lanes = slice(h * LANES, (h + 1) * LANES)
            oi_s[rows, lanes] = oi_s[rows, lanes] + inter[h]
            st[h] = st[h] * eb_s[ci, h] + kv_s[ci, h]
    for h in range(GLA_HEADS):
        st_ref[h] = st[h]

    ng = ng_ref[...]
    for h in range(GLA_HEADS):
        lanes = slice(h * LANES, (h + 1) * LANES)
        o = oi_s[:, lanes]
        o = o * lax.rsqrt(jnp.mean(o * o, axis=-1, keepdims=True) + GLA_NORM_EPS) * ng
        o_ref[:, lanes] = (o * sg_ref[:, lanes]).astype(o_ref.dtype)


def _gla_call(q, k, v, sg, la, ng, bsz, seq):
    tb = REC_TILE
    nt = seq // tb
    n_tok = bsz * seq
    nchunk = tb // CHUNK

    def tok(n):
        return pl.BlockSpec((tb, n), lambda bb, t: (bb * nt + t, 0))

    return pl.pallas_call(
        functools.partial(_gla_kernel, tb=tb, unroll=GLA_UNROLL),
        grid=(bsz, nt),
        in_specs=[tok(GLA_KW), tok(GLA_KW), tok(GLA_VW), tok(GLA_VW), tok(GLA_KW),
                  pl.BlockSpec((1, LANES), lambda bb, t: (0, 0))],
        out_specs=tok(GLA_VW),
        out_shape=jax.ShapeDtypeStruct((n_tok, GLA_VW), BF16),
        scratch_shapes=[pltpu.VMEM((GLA_HEADS, GLA_DV, LANES), F32),
                        pltpu.VMEM((nchunk, GLA_HEADS, CHUNK, LANES), BF16),
                        pltpu.VMEM((tb, GLA_VW), F32),
                        pltpu.VMEM((nchunk, GLA_HEADS, GLA_DV, LANES), F32),
                        pltpu.VMEM((nchunk, GLA_HEADS, 1, LANES), F32)],
        compiler_params=pltpu.CompilerParams(
            dimension_semantics=("arbitrary", "arbitrary"),
            vmem_limit_bytes=VMEM_LIMIT),
        name="gla",
    )(q, k, v, sg, la, ng)


def _tail_kernel(x_ref, orw_ref, ogla_ref, sgr_ref, sgg_ref, gate1_ref, shift2_ref, scale2_ref,
                 gate2_ref, wrb_ref, wgb_ref, wmix_ref, win_ref, wout_ref, ln1g_ref, ln1b_ref,
                 ln2g_ref, ln2b_ref, o_ref, *, d_ff):
    hrows = x_ref.shape[0] // TAIL_SPLIT
    rs = [slice(i * hrows, (i + 1) * hrows) for i in range(TAIL_SPLIT)]
    nsp = range(TAIL_SPLIT)
    b_rw = [jnp.dot(orw_ref[s, :], wrb_ref[...], preferred_element_type=F32) for s in rs]
    b_gla = [jnp.dot(ogla_ref[s, :], wgb_ref[...], preferred_element_type=F32) for s in rs]
    mix = [_dot(_sigmoid(sgr_ref[rs[i], :]) * b_rw[i] + _sigmoid(sgg_ref[rs[i], :]) * b_gla[i],
                wmix_ref[...]) for i in nsp]
    x1 = [_layer_norm(ALPHA * x_ref[rs[i], :] + gate1_ref[...] * mix[i], LN_EPS) * ln1g_ref[...]
          + ln1b_ref[...] for i in nsp]
    u = [(x1[i] * (1.0 + scale2_ref[...]) + shift2_ref[...]).astype(BF16) for i in nsp]
    ffn = [jnp.zeros_like(x1[i]) for i in nsp]
    ntile = d_ff // MXU_K
    bounds = [MXU_K * ((ntile * j) // FFN_CHUNKS) for j in range(FFN_CHUNKS)] + [d_ff]
    for lo, hi in zip(bounds[:-1], bounds[1:]):
        hg, hu = [], []
        for i in nsp:
            hg.append(jnp.dot(u[i], win_ref[:, lo:hi], preferred_element_type=F32))
            hu.append(jnp.dot(u[i], win_ref[:, d_ff + lo:d_ff + hi], preferred_element_type=F32))
        act = [hg[i] * _sigmoid(hg[i]) * hu[i] for i in nsp]
        ffn = [ffn[i] + _dot(act[i], wout_ref[lo:hi, :]) for i in nsp]
    for i in nsp:
        y = _layer_norm(ALPHA * x1[i] + gate2_ref[...] * ffn[i], LN_EPS)
        o_ref[rs[i], :] = y * ln2g_ref[...] + ln2b_ref[...]


def _tail_call(x2, o_rw, o_gla, sgr, sgg, mod3, wrb, wgb, wmix, win, wout, ln1g, ln1b, ln2g, ln2b,
               bsz, seq, d_model):
    tm = TAIL_TILE
    nt = seq // tm
    n_tok = bsz * seq
    d_ff = wout.shape[0]
    assert d_ff % MXU_K == 0

    def tok(n):
        return pl.BlockSpec((tm, n), lambda b, t: (b * nt + t, 0))

    def const(a):
        return pl.BlockSpec(a.shape, lambda b, t: (0,) * a.ndim, pipeline_mode=pl.Buffered(1))

    def modspec(idx):
        return pl.BlockSpec((None, 1, d_model), lambda b, t: (b, 0, idx))

    return pl.pallas_call(
        functools.partial(_tail_kernel, d_ff=d_ff),
        grid=(bsz, nt),
        in_specs=[tok(d_model), tok(RW_WIDTH), tok(GLA_VW), tok(d_model), tok(d_model),
                  modspec(2), modspec(3), modspec(4), modspec(5),
                  const(wrb), const(wgb), const(wmix), const(win), const(wout),
                  const(ln1g), const(ln1b), const(ln2g), const(ln2b)],
        out_specs=tok(d_model),
        out_shape=jax.ShapeDtypeStruct((n_tok, d_model), F32),
        compiler_params=pltpu.CompilerParams(dimension_semantics=("arbitrary", "arbitrary"),
                                             vmem_limit_bytes=VMEM_LIMIT),
        name="tail",
    )(x2, o_rw, o_gla, sgr, sgg, mod3, mod3, mod3, mod3, wrb, wgb, wmix, win, wout,
      ln1g, ln1b, ln2g, ln2b)


def kernel(x, c, w_ada, b_ada, w_in, mu_rw, rw_w0, rw_w2, rw_a0, rw_a2, rw_g2, rw_k_k, rw_k_a,
           rw_r_k, rw_gn_g, rw_gn_b, gla_a2, gla_a_b, gla_norm_g, w_rw_branch, w_gla_branch,
           w_mix_out, ln1_g, ln1_b, w_ffn_in, w_ffn_out, ln2_g, ln2_b):
    bsz, seq, d_model = x.shape
    assert w_ada.shape[0] == DEPTH and seq % REC_TILE == 0
    n_tok = bsz * seq
    l = 0

    wp = _wprep_call(jnp.swapaxes(w_in[l], 0, 1), d_model)
    zeros_lora = jnp.zeros((DECAY_LORA, RW_WIDTH), F32)
    w2p = jnp.concatenate([rw_w2[l], zeros_lora], axis=0).astype(BF16)
    a2p = jnp.concatenate([zeros_lora, rw_a2[l]], axis=0).astype(BF16)
    ga2p = jnp.concatenate(
        [gla_a2[l], jnp.zeros((LANES - GLA_GATE_LORA, GLA_KW), F32)], axis=0).astype(BF16)
    row = lambda a: a.reshape(1, -1)
    hid = jnp.arange(LANES) // RW_HEAD
    bd = (hid[:, None] == hid[None, :]).astype(BF16)
    small = (row(mu_rw[l]), row(rw_w0[l]), w2p, row(rw_a0[l]), a2p, rw_g2[l].astype(BF16),
             row(rw_k_k[l]), row(rw_k_a[l]), row(rw_r_k[l]), ga2p, row(gla_a_b[l]), bd)

    x2 = x.reshape(n_tok, d_model)
    mod = _mod_call(c, w_ada[l], b_ada[l])
    mod3 = mod.reshape(bsz, 1, 6 * d_model)

    (r, k, v, lw, nkk, b, g, bonus, gq, gk, gv, gsg, gla, sgr, sgg) = _inproj_call(
        x2, mod3, wp, small, bsz, seq, d_model)

    o_rw = _rwkv_call(r, k, v, lw, nkk, b, g, bonus, row(rw_gn_g[l]), row(rw_gn_b[l]), bd, bsz, seq)
    o_gla = _gla_call(gq, gk, gv, gsg, gla, row(gla_norm_g[l]), bsz, seq)

    out = _tail_call(x2, o_rw, o_gla, sgr, sgg, mod3, w_rw_branch[l].astype(BF16),
                     w_gla_branch[l].astype(BF16), w_mix_out[l].astype(BF16),
                     w_ffn_in[l].astype(BF16), w_ffn_out[l].astype(BF16),
                     row(ln1_g[l]), row(ln1_b[l]), row(ln2_g[l]), row(ln2_b[l]), bsz, seq, d_model)
    return out.reshape(bsz, seq, d_model)
```

```python
import functools

import jax
import jax.numpy as jnp
from jax import lax
from jax.experimental import pallas as pl
from jax.experimental.pallas import tpu as pltpu

F32 = jnp.float32
BF16 = jnp.bfloat16

RW_HEAD = 64
RW_HEADS = 8
RW_WIDTH = RW_HEADS * RW_HEAD
DECAY_LORA = 64
AAA_LORA = 64
GATE_LORA = 128
RW_GN_EPS = 64e-5
RW_COLS = 3 * RW_WIDTH + DECAY_LORA + AAA_LORA + GATE_LORA
GLA_HEADS = 4
GLA_DK = 64
GLA_DV = 128
GLA_KW = GLA_HEADS * GLA_DK
GLA_VW = GLA_HEADS * GLA_DV
GLA_GATE_LORA = 16
GLA_TAU = 16.0
GLA_NORM_EPS = 1e-5
GLA_MAIN = 2 * GLA_KW + 2 * GLA_VW
GLA_COLS = GLA_MAIN + GLA_GATE_LORA
LN_EPS = 1e-5
DEPTH = 1
ALPHA = (2.0 * DEPTH) ** 0.25

LANES = 128
MXU_K = 256
CHUNK = 64
RWKV_UNROLL = 4
RW_GROUP = 2
GLA_UNROLL = 8
REC_TILE = 512
TAIL_TILE = 512
TAIL_SPLIT = 2
FFN_CHUNKS = 2
VMEM_LIMIT = 58 * 1024 * 1024

COL_RW = 0
COL_GLA = RW_COLS
COL_GATE = COL_GLA + GLA_MAIN


def _sigmoid(x):
    return 1.0 / (1.0 + jnp.exp(-x))


def _softplus(x):
    return jnp.maximum(x, 0.0) + jnp.log1p(jnp.exp(-jnp.abs(x)))


def _dot(a, b):
    return jnp.dot(a.astype(BF16), b.astype(BF16), preferred_element_type=F32)


def _dot_nt(a, b):
    return lax.dot_general(a.astype(BF16), b.astype(BF16), (((1,), (1,)), ((), ())),
                           preferred_element_type=F32)


def _split_hi_lo(x):
    hi = x.astype(BF16)
    lo = (x - hi.astype(F32)).astype(BF16)
    return hi, lo


def _dot_exact_rhs(x, w01):
    hi, lo = _split_hi_lo(x)
    if 2 * x.shape[1] <= MXU_K:
        return jnp.dot(jnp.concatenate([hi, lo], axis=1), jnp.concatenate([w01, w01], axis=0),
                       preferred_element_type=F32)
    return (jnp.dot(hi, w01, preferred_element_type=F32)
            + jnp.dot(lo, w01, preferred_element_type=F32))


def _dot_exact_lhs(w01, x):
    hi, lo = _split_hi_lo(x)
    if 2 * x.shape[0] <= MXU_K:
        return jnp.dot(jnp.concatenate([w01, w01], axis=1), jnp.concatenate([hi, lo], axis=0),
                       preferred_element_type=F32)
    return (jnp.dot(w01, hi, preferred_element_type=F32)
            + jnp.dot(w01, lo, preferred_element_type=F32))


def _layer_norm(x, eps):
    mu = jnp.mean(x, axis=-1, keepdims=True)
    xc = x - mu
    var = jnp.mean(xc * xc, axis=-1, keepdims=True)
    return xc * lax.rsqrt(var + eps)


def _mod_kernel(c_ref, w_ref, b_ref, o_ref):
    c = c_ref[...]
    sc = c * _sigmoid(c)
    o_ref[...] = _dot(sc, w_ref[...]) + b_ref[...]


def _mod_call(c, w_ada, b_ada):
    bsz, d = c.shape
    n = w_ada.shape[1]
    tn = 1536
    return pl.pallas_call(
        _mod_kernel,
        grid=(n // tn,),
        in_specs=[pl.BlockSpec((bsz, d), lambda j: (0, 0)),
                  pl.BlockSpec((d, tn), lambda j: (0, j)),
                  pl.BlockSpec((1, tn), lambda j: (0, j))],
        out_specs=pl.BlockSpec((bsz, tn), lambda j: (0, j)),
        out_shape=jax.ShapeDtypeStruct((bsz, n), F32),
        compiler_params=pltpu.CompilerParams(dimension_semantics=("arbitrary",),
                                             vmem_limit_bytes=VMEM_LIMIT),
        name="mod",
    )(c, w_ada, b_ada.reshape(1, n))


def _wprep_kernel(w_ref, o_ref, *, d_model):
    main = RW_COLS + GLA_MAIN
    o_ref[:main, :] = w_ref[:main, :].astype(BF16)
    o_ref[main:main + 2 * d_model, :] = w_ref[RW_COLS + GLA_COLS:, :].astype(BF16)
    o_ref[main + 2 * d_model:main + 2 * d_model + GLA_GATE_LORA, :] = (
        w_ref[main:RW_COLS + GLA_COLS, :].astype(BF16))
    o_ref[main + 2 * d_model + GLA_GATE_LORA:, :] = jnp.zeros(
        (LANES - GLA_GATE_LORA, o_ref.shape[1]), BF16)


def _wprep_call(w_in_t, d_model):
    n_in = w_in_t.shape[0]
    n_out = RW_COLS + GLA_MAIN + 2 * d_model + LANES
    cols = 256
    return pl.pallas_call(
        functools.partial(_wprep_kernel, d_model=d_model),
        grid=(d_model // cols,),
        in_specs=[pl.BlockSpec((n_in, cols), lambda i: (0, i))],
        out_specs=pl.BlockSpec((n_out, cols), lambda i: (0, i)),
        out_shape=jax.ShapeDtypeStruct((n_out, d_model), BF16),
        compiler_params=pltpu.CompilerParams(dimension_semantics=("arbitrary",),
                                             vmem_limit_bytes=VMEM_LIMIT),
        name="wprep",
    )(w_in_t)


def _inproj_kernel(x_ref, shift_ref, scale_ref, wp_ref, mu_ref, w0_ref, w2_ref, a0_ref, a2_ref,
                   g2_ref, kk_ref, ka_ref, rk_ref, ga2_ref, gab_ref, bd_ref,
                   r_out, k_out, v_out, lw_out, nkk_out, b_out, g_out, bonus_out,
                   gq_out, gk_out, gv_out, gsg_out, gla_out, sgr_out, sgg_out,
                   carry_ref, *, tm, d_model):
    t = pl.program_id(1)

    @pl.when(t == 0)
    def _():
        carry_ref[...] = jnp.zeros_like(carry_ref)

    u = x_ref[...] * (1.0 + scale_ref[...]) + shift_ref[...]
    proj = _dot_nt(u, wp_ref[...])

    p = proj[:, COL_RW:COL_RW + RW_COLS]
    prev = pltpu.roll(p, 1, 0)
    row = lax.broadcasted_iota(jnp.int32, p.shape, 0)
    prev = jnp.where(row == 0, carry_ref[...], prev)
    carry_ref[...] = p[tm - 1:tm, :]
    ps = p + mu_ref[...] * (prev - p)

    r = ps[:, 0:RW_WIDTH]
    k = ps[:, RW_WIDTH:2 * RW_WIDTH]
    v = ps[:, 2 * RW_WIDTH:3 * RW_WIDTH]
    z = ps[:, 3 * RW_WIDTH:3 * RW_WIDTH + LANES]
    gd = ps[:, 3 * RW_WIDTH + LANES:RW_COLS]

    wlin = w0_ref[...] + _dot(jnp.tanh(z), w2_ref[...])
    w = -_softplus(-wlin) - 0.5
    lw_out[...] = -jnp.exp(w)
    a = _sigmoid(a0_ref[...] + _dot(z, a2_ref[...]))
    g_out[...] = _dot(_sigmoid(gd), g2_ref[...])
    kkv = k * kk_ref[...]
    kp = k * (1.0 + (a - 1.0) * ka_ref[...])
    rkr = r * kp * rk_ref[...]
    bd = bd_ref[...]
    for j in range(RW_WIDTH // LANES):
        sl = slice(j * LANES, (j + 1) * LANES)
        kkj = kkv[:, sl]
        ssq = _dot_exact_rhs(kkj * kkj, bd)
        kkn = kkj / jnp.maximum(jnp.sqrt(ssq), 1e-12)
        nkk_out[:, sl] = -kkn
        b_out[:, sl] = kkn * a[:, sl]
        bonus_out[:, sl] = _dot_exact_rhs(rkr[:, sl], bd) * v[:, sl]
    r_out[...] = r
    k_out[...] = kp
    v_out[...] = v.astype(BF16)

    pg = proj[:, COL_GLA:COL_GLA + GLA_MAIN]
    gq_out[...] = pg[:, 0:GLA_KW] * (GLA_DK ** -0.5)
    gk_out[...] = pg[:, GLA_KW:2 * GLA_KW]
    gv_out[...] = pg[:, 2 * GLA_KW:2 * GLA_KW + GLA_VW].astype(BF16)
    gg = pg[:, 2 * GLA_KW + GLA_VW:GLA_MAIN]
    gsg_out[...] = gg * _sigmoid(gg)
    adg = proj[:, COL_GATE + 2 * d_model:COL_GATE + 2 * d_model + LANES]
    la = _dot(adg, ga2_ref[...]) + gab_ref[...]
    gla_out[...] = -_softplus(-la) / GLA_TAU

    sgr_out[...] = proj[:, COL_GATE:COL_GATE + d_model]
    sgg_out[...] = proj[:, COL_GATE + d_model:COL_GATE + 2 * d_model]


def _inproj_call(x2, mod3, wp, small, bsz, seq, d_model):
    tm = 256
    nt = seq // tm
    n_tok = bsz * seq
    (mu, w0, w2p, a0, a2p, g2, k_k, k_a, r_k, ga2p, gab, bd) = small

    def tok(n):
        return pl.BlockSpec((tm, n), lambda b, t: (b * nt + t, 0))

    def full(a):
        return pl.BlockSpec(a.shape, lambda b, t: (0,) * a.ndim)

    def modspec(idx):
        return pl.BlockSpec((None, 1, d_model), lambda b, t: (b, 0, idx))

    outs = [
        (RW_WIDTH, F32), (RW_WIDTH, F32), (RW_WIDTH, BF16), (RW_WIDTH, F32), (RW_WIDTH, F32),
        (RW_WIDTH, F32), (RW_WIDTH, F32), (RW_WIDTH, F32),
        (GLA_KW, F32), (GLA_KW, F32), (GLA_VW, BF16), (GLA_VW, F32), (GLA_KW, F32),
        (d_model, F32), (d_model, F32),
    ]
    return pl.pallas_call(
        functools.partial(_inproj_kernel, tm=tm, d_model=d_model),
        grid=(bsz, nt),
        in_specs=[tok(d_model), modspec(0), modspec(1), full(wp), full(mu), full(w0), full(w2p),
                  full(a0), full(a2p), full(g2), full(k_k), full(k_a), full(r_k), full(ga2p),
                  full(gab), full(bd)],
        out_specs=[tok(n) for n, _ in outs],
        out_shape=[jax.ShapeDtypeStruct((n_tok, n), dt) for n, dt in outs],
        scratch_shapes=[pltpu.VMEM((1, RW_COLS), F32)],
        compiler_params=pltpu.CompilerParams(dimension_semantics=("arbitrary", "arbitrary"),
                                             vmem_limit_bytes=VMEM_LIMIT),
        name="inproj",
    )(x2, mod3, mod3, wp, mu, w0, w2p, a0, a2p, g2, k_k, k_a, r_k, ga2p, gab, bd)


def _rwkv_kernel(r_ref, k_ref, v_ref, lw_ref, nkk_ref, b_ref, g_ref, bonus_ref, gng_ref, gnb_ref,
                 bd_ref, o_ref, h_ref, rm_s, yl_s, hl_s, *, tb, unroll):
    c = CHUNK
    gw = RW_GROUP * RW_HEAD
    nchunk = tb // c
    ngroup = RW_WIDTH // gw
    t = pl.program_id(1)

    @pl.when(t == 0)
    def _():
        h_ref[...] = jnp.zeros_like(h_ref)

    lane = lax.broadcasted_iota(jnp.int32, (c, gw), 1)
    rowi = lax.broadcasted_iota(jnp.int32, (c, gw), 0)
    head_of_lane = lane // RW_HEAD
    lmask = [head_of_lane == h for h in range(RW_GROUP)]
    strict = (lane % RW_HEAD) < rowi
    incl = (lane % RW_HEAD) <= rowi
    eye = (lane % RW_HEAD) == rowi
    eye_f = eye.astype(F32)
    eye_b = eye.astype(BF16)
    ri = lax.broadcasted_iota(jnp.int32, (c, c), 0)
    ci_ = lax.broadcasted_iota(jnp.int32, (c, c), 1)
    ltri = (ci_ <= ri).astype(BF16)

    def stack(x):
        xb = x.astype(BF16)
        zero = jnp.zeros_like(xb)
        return jnp.concatenate([jnp.where(lmask[h], xb, zero) for h in range(RW_GROUP)], axis=0)

    def local_group(gi, carry):
        units = [(u, g) for u in range(unroll) for g in range(ngroup)]
        n = len(units)
        idxs = [gi * unroll + u for u in range(unroll)]
        rows = [pl.ds(pl.multiple_of(i * c, c), c) for i in idxs]

        def ld(ref, u, g):
            return ref[rows[u], g * gw:(g + 1) * gw]

        lw = [ld(lw_ref, u, g) for u, g in units]
        gcum = [_dot_exact_lhs(ltri, x) for x in lw]
        gend, rt, at, kt, bt, bh, kh, vst = [], [], [], [], [], [], [], []
        for s, (u, g) in enumerate(units):
            g_ = gcum[s]
            ge = g_[c - 1:c, :]
            r = ld(r_ref, u, g)
            k = ld(k_ref, u, g)
            nkk = ld(nkk_ref, u, g)
            b = ld(b_ref, u, g)
            e_neg = jnp.exp(-g_)
            dk = jnp.exp(ge - g_)
            gend.append(ge)
            rt.append(r * jnp.exp(g_))
            at.append(nkk * jnp.exp(g_ - lw[s]))
            kt.append(k * e_neg)
            bt.append(b * e_neg)
            bh.append(b * dk)
            kh.append(k * dk)
            vst.append(stack(ld(v_ref, u, g)))

        x = [_dot_nt(jnp.concatenate([at[s], rt[s]], axis=0),
                     jnp.concatenate([stack(bt[s]), stack(kt[s])], axis=0)) for s in range(n)]
        a_ab = [jnp.where(strict, y[:c, :gw], 0.0) for y in x]
        a_ak = [jnp.where(strict, y[:c, gw:], 0.0) for y in x]
        a_rb = [jnp.where(incl, y[c:, :gw], 0.0) for y in x]
        a_rk = [jnp.where(incl, y[c:, gw:], 0.0) for y in x]
        tr = [_dot_nt(eye_b, jnp.concatenate([stack(bh[s]), stack(kh[s])], axis=0))
              for s in range(n)]
        bht = [y[:, :gw] for y in tr]
        kht = [y[:, gw:] for y in tr]

        tp = [eye_f + a for a in a_ab]
        pw = [_dot(a, stack(a)) for a in a_ab]
        for _ in range(4):
            res = [_dot(jnp.concatenate([pw[s], tp[s]], axis=0), stack(pw[s])) for s in range(n)]
            tp = [tp[s] + res[s][c:] for s in range(n)]
            pw = [res[s][:c] for s in range(n)]
        tinv = [tp[s] + _dot(tp[s], stack(pw[s])) for s in range(n)]

        vres = [_dot(jnp.concatenate([a_ak[s], a_rk[s], kht[s]], axis=0), vst[s]) for s in range(n)]
        akv = [y[:c] for y in vres]
        rk_v = [y[c:2 * c] for y in vres]
        h2 = [y[2 * c:] for y in vres]
        tres = [_dot(tinv[s], jnp.concatenate([stack(at[s]), stack(akv[s])], axis=1))
                for s in range(n)]
        ap = [y[:, :gw] for y in tres]
        uloc = [y[:, gw:] for y in tres]
        fres = [_dot(jnp.concatenate([a_rb[s], bht[s]], axis=0),
                     jnp.concatenate([stack(ap[s]), stack(uloc[s])], axis=1)) for s in range(n)]
        for s, (u, g) in enumerate(units):
            y = fres[s]
            rm_s[idxs[u], g, :c, :] = rt[s] + y[:c, :gw]
            rm_s[idxs[u], g, c:, :] = jnp.where(eye, jnp.exp(gend[s]), 0.0) + y[c:, :gw]
            yl_s[rows[u], g * gw:(g + 1) * gw] = y[:c, gw:] + rk_v[s]
            hl_s[idxs[u], g] = y[c:, gw:] + h2[s]
        return carry

    lax.fori_loop(0, nchunk // unroll, local_group, 0)

    hstate = [h_ref[g] for g in range(ngroup)]
    for ci in range(nchunk):
        rows = slice(ci * c, (ci + 1) * c)
        res = [_dot(rm_s[ci, g], stack(hstate[g])) for g in range(ngroup)]
        for g in range(ngroup):
            lanes = slice(g * gw, (g + 1) * gw)
            yl_s[rows, lanes] = res[g][:c] + yl_s[rows, lanes]
            hstate[g] = res[g][c:] + hl_s[ci, g]
    for g in range(ngroup):
        h_ref[g] = hstate[g]

    bd = bd_ref[...]
    for p in range(RW_WIDTH // LANES):
        lanes = slice(p * LANES, (p + 1) * LANES)
        y = yl_s[:, lanes]
        mean = _dot_exact_rhs(y, bd) * (1.0 / RW_HEAD)
        yc = y - mean
        var = _dot_exact_rhs(yc * yc, bd) * (1.0 / RW_HEAD)
        yn = yc * lax.rsqrt(var + RW_GN_EPS)
        out = (yn * gng_ref[:, lanes] + gnb_ref[:, lanes] + bonus_ref[:, lanes]) * g_ref[:, lanes]
        o_ref[:, lanes] = out.astype(o_ref.dtype)


def _rwkv_call(r, k, v, lw, nkk, b, g, bonus, gng, gnb, bd, bsz, seq):
    tb = REC_TILE
    nt = seq // tb
    n_tok = bsz * seq
    gw = RW_GROUP * RW_HEAD
    ngroup = RW_WIDTH // gw
    nchunk = tb // CHUNK

    def tok():
        return pl.BlockSpec((tb, RW_WIDTH), lambda bb, t: (bb * nt + t, 0))

    def full(a):
        return pl.BlockSpec(a.shape, lambda bb, t: (0,) * a.ndim)

    return pl.pallas_call(
        functools.partial(_rwkv_kernel, tb=tb, unroll=RWKV_UNROLL),
        grid=(bsz, nt),
        in_specs=[tok() for _ in range(8)] + [full(gng), full(gnb), full(bd)],
        out_specs=tok(),
        out_shape=jax.ShapeDtypeStruct((n_tok, RW_WIDTH), BF16),
        scratch_shapes=[pltpu.VMEM((ngroup, RW_HEAD, gw), F32),
                        pltpu.VMEM((nchunk, ngroup, CHUNK + RW_HEAD, gw), F32),
                        pltpu.VMEM((tb, RW_WIDTH), F32),
                        pltpu.VMEM((nchunk, ngroup, RW_HEAD, gw), F32)],
        compiler_params=pltpu.CompilerParams(
            dimension_semantics=("arbitrary", "arbitrary"),
            vmem_limit_bytes=VMEM_LIMIT),
        name="rwkv",
    )(r, k, v, lw, nkk, b, g, bonus, gng, gnb, bd)


def _gla_kernel(q_ref, k_ref, v_ref, sg_ref, la_ref, ng_ref, o_ref, st_ref, qs_s, oi_s, kv_s, eb_s,
                *, tb, unroll):
    c = CHUNK
    nchunk = tb // c
    t = pl.program_id(1)

    @pl.when(t == 0)
    def _():
        st_ref[...] = jnp.zeros_like(st_ref)

    lane = lax.broadcasted_iota(jnp.int32, (1, LANES), 1)
    hmask = ((lane < GLA_DK).astype(F32), (lane >= GLA_DK).astype(F32))
    ri = lax.broadcasted_iota(jnp.int32, (c, c), 0)
    ci_ = lax.broadcasted_iota(jnp.int32, (c, c), 1)
    causal = ci_ <= ri
    ltri = causal.astype(BF16)

    def local_group(gi, carry):
        units = [(u, h) for u in range(unroll) for h in range(GLA_HEADS)]
        idxs = [gi * unroll + u for u in range(unroll)]
        rows = [pl.ds(pl.multiple_of(i * c, c), c) for i in idxs]

        def qk_lanes(h):
            return slice((h // 2) * LANES, (h // 2 + 1) * LANES)

        la = [la_ref[rows[u], qk_lanes(h)] for u, h in units]
        bcum = [_dot_exact_lhs(ltri, x) for x in la]
        q_s, k_s, kdec, v = [], [], [], []
        for s, (u, h) in enumerate(units):
            bc = bcum[s]
            blast = bc[c - 1:c, :]
            q = q_ref[rows[u], qk_lanes(h)]
            k = k_ref[rows[u], qk_lanes(h)]
            hm = hmask[h % 2]
            q_s.append((q * jnp.exp(bc) * hm).astype(BF16))
            k_s.append(k * jnp.exp(-bc))
            kdec.append(k * jnp.exp(blast - bc) * hm)
            v.append(v_ref[rows[u], h * LANES:(h + 1) * LANES])
            eb_s[idxs[u], h] = jnp.exp(blast)
        att = [jnp.where(causal, _dot_nt(q_s[s], k_s[s]), 0.0) for s in range(len(units))]
        oi = [_dot(att[s], v[s]) for s in range(len(units))]
        v_t = [x.astype(F32).T.astype(BF16) for x in v]
        kv = [_dot(v_t[s], kdec[s]) for s in range(len(units))]
        for s, (u, h) in enumerate(units):
            qs_s[idxs[u], h] = q_s[s]
            oi_s[rows[u], h * LANES:(h + 1) * LANES] = oi[s]
            kv_s[idxs[u], h] = kv[s]
        return carry

    lax.fori_loop(0, nchunk // unroll, local_group, 0)

    st = [st_ref[h] for h in range(GLA_HEADS)]
    for ci in range(nchunk):
        rows = slice(ci * c, (ci + 1) * c)
        inter = [_dot_nt(qs_s[ci, h], st[h]) for h in range(GLA_HEADS)]
        for h in range(GLA_HEADS):
            lanes = slice(h * LANES, (h + 1) * LANES)
            oi_s[rows, lanes] = oi_s[rows, lanes] + inter[h]
            st[h] = st[h] * eb_s[ci, h] + kv_s[ci, h]
    for h in range(GLA_HEADS):
        st_ref[h] = st[h]

    ng = ng_ref[...]
    for h in range(GLA_HEADS):
        lanes = slice(h * LANES, (h + 1) * LANES)
        o = oi_s[:, lanes]
        o = o * lax.rsqrt(jnp.mean(o * o, axis=-1, keepdims=True) + GLA_NORM_EPS) * ng
        o_ref[:, lanes] = (o * sg_ref[:, lanes]).astype(o_ref.dtype)


def _gla_call(q, k, v, sg, la, ng, bsz, seq):
    tb = REC_TILE
    nt = seq // tb
    n_tok = bsz * seq
    nchunk = tb // CHUNK

    def tok(n):
        return pl.BlockSpec((tb, n), lambda bb, t: (bb * nt + t, 0))

    return pl.pallas_call(
        functools.partial(_gla_kernel, tb=tb, unroll=GLA_UNROLL),
        grid=(bsz, nt),
        in_specs=[tok(GLA_KW), tok(GLA_KW), tok(GLA_VW), tok(GLA_VW), tok(GLA_KW),
                  pl.BlockSpec((1, LANES), lambda bb, t: (0, 0))],
        out_specs=tok(GLA_VW),
        out_shape=jax.ShapeDtypeStruct((n_tok, GLA_VW), BF16),
        scratch_shapes=[pltpu.VMEM((GLA_HEADS, GLA_DV, LANES), F32),
                        pltpu.VMEM((nchunk, GLA_HEADS, CHUNK, LANES), BF16),
                        pltpu.VMEM((tb, GLA_VW), F32),
                        pltpu.VMEM((nchunk, GLA_HEADS, GLA_DV, LANES), F32),
                        pltpu.VMEM((nchunk, GLA_HEADS, 1, LANES), F32)],
        compiler_params=pltpu.CompilerParams(
            dimension_semantics=("arbitrary", "arbitrary"),
            vmem_limit_bytes=VMEM_LIMIT),
        name="gla",
    )(q, k, v, sg, la, ng)


def _tail_kernel(x_ref, orw_ref, ogla_ref, sgr_ref, sgg_ref, gate1_ref, shift2_ref, scale2_ref,
                 gate2_ref, wrb_ref, wgb_ref, wmix_ref, win_ref, wout_ref, ln1g_ref, ln1b_ref,
                 ln2g_ref, ln2b_ref, o_ref, *, d_ff):
    hrows = x_ref.shape[0] // TAIL_SPLIT
    rs = [slice(i * hrows, (i + 1) * hrows) for i in range(TAIL_SPLIT)]
    nsp = range(TAIL_SPLIT)
    b_rw = [jnp.dot(orw_ref[s, :], wrb_ref[...], preferred_element_type=F32) for s in rs]
    b_gla = [jnp.dot(ogla_ref[s, :], wgb_ref[...], preferred_element_type=F32) for s in rs]
    mix = [_dot(_sigmoid(sgr_ref[rs[i], :]) * b_rw[i] + _sigmoid(sgg_ref[rs[i], :]) * b_gla[i],
                wmix_ref[...]) for i in nsp]
    x1 = [_layer_norm(ALPHA * x_ref[rs[i], :] + gate1_ref[...] * mix[i], LN_EPS) * ln1g_ref[...]
          + ln1b_ref[...] for i in nsp]
    u = [(x1[i] * (1.0 + scale2_ref[...]) + shift2_ref[...]).astype(BF16) for i in nsp]
    ffn = [jnp.zeros_like(x1[i]) for i in nsp]
    ntile = d_ff // MXU_K
    bounds = [MXU_K * ((ntile * j) // FFN_CHUNKS) for j in range(FFN_CHUNKS)] + [d_ff]
    for lo, hi in zip(bounds[:-1], bounds[1:]):
        hg, hu = [], []
        for i in nsp:
            hg.append(jnp.dot(u[i], win_ref[:, lo:hi], preferred_element_type=F32))
            hu.append(jnp.dot(u[i], win_ref[:, d_ff + lo:d_ff + hi], preferred_element_type=F32))
        act = [hg[i] * _sigmoid(hg[i]) * hu[i] for i in nsp]
        ffn = [ffn[i] + _dot(act[i], wout_ref[lo:hi, :]) for i in nsp]
    for i in nsp:
        y = _layer_norm(ALPHA * x1[i] + gate2_ref[...] * ffn[i], LN_EPS)
        o_ref[rs[i], :] = y * ln2g_ref[...] + ln2b_ref[...]


def _tail_call(x2, o_rw, o_gla, sgr, sgg, mod3, wrb, wgb, wmix, win, wout, ln1g, ln1b, ln2g, ln2b,
               bsz, seq, d_model):
    tm = TAIL_TILE
    nt = seq // tm
    n_tok = bsz * seq
    d_ff = wout.shape[0]
    assert d_ff % MXU_K == 0

    def tok(n):
        return pl.BlockSpec((tm, n), lambda b, t: (b * nt + t, 0))

    def const(a):
        return pl.BlockSpec(a.shape, lambda b, t: (0,) * a.ndim, pipeline_mode=pl.Buffered(1))

    def modspec(idx):
        return pl.BlockSpec((None, 1, d_model), lambda b, t: (b, 0, idx))

    return pl.pallas_call(
        functools.partial(_tail_kernel, d_ff=d_ff),
        grid=(bsz, nt),
        in_specs=[tok(d_model), tok(RW_WIDTH), tok(GLA_VW), tok(d_model), tok(d_model),
                  modspec(2), modspec(3), modspec(4), modspec(5),
                  const(wrb), const(wgb), const(wmix), const(win), const(wout),
                  const(ln1g), const(ln1b), const(ln2g), const(ln2b)],
        out_specs=tok(d_model),
        out_shape=jax.ShapeDtypeStruct((n_tok, d_model), F32),
        compiler_params=pltpu.CompilerParams(dimension_semantics=("arbitrary", "arbitrary"),
                                             vmem_limit_bytes=VMEM_LIMIT),
        name="tail",
    )(x2, o_rw, o_gla, sgr, sgg, mod3, mod3, mod3, mod3, wrb, wgb, wmix, win, wout,
      ln1g, ln1b, ln2g, ln2b)


def kernel(x, c, w_ada, b_ada, w_in, mu_rw, rw_w0, rw_w2, rw_a0, rw_a2, rw_g2, rw_k_k, rw_k_a,
           rw_r_k, rw_gn_g, rw_gn_b, gla_a2, gla_a_b, gla_norm_g, w_rw_branch, w_gla_branch,
           w_mix_out, ln1_g, ln1_b, w_ffn_in, w_ffn_out, ln2_g, ln2_b):
    bsz, seq, d_model = x.shape
    assert w_ada.shape[0] == DEPTH and seq % REC_TILE == 0
    n_tok = bsz * seq
    l = 0

    wp = _wprep_call(jnp.swapaxes(w_in[l], 0, 1), d_model)
    zeros_lora = jnp.zeros((DECAY_LORA, RW_WIDTH), F32)
    w2p = jnp.concatenate([rw_w2[l], zeros_lora], axis=0).astype(BF16)
    a2p = jnp.concatenate([zeros_lora, rw_a2[l]], axis=0).astype(BF16)
    ga2p = jnp.concatenate(
        [gla_a2[l], jnp.zeros((LANES - GLA_GATE_LORA, GLA_KW), F32)], axis=0).astype(BF16)
    row = lambda a: a.reshape(1, -1)
    hid = jnp.arange(LANES) // RW_HEAD
    bd = (hid[:, None] == hid[None, :]).astype(BF16)
    small = (row(mu_rw[l]), row(rw_w0[l]), w2p, row(rw_a0[l]), a2p, rw_g2[l].astype(BF16),
             row(rw_k_k[l]), row(rw_k_a[l]), row(rw_r_k[l]), ga2p, row(gla_a_b[l]), bd)

    x2 = x.reshape(n_tok, d_model)
    mod = _mod_call(c, w_ada[l], b_ada[l])
    mod3 = mod.reshape(bsz, 1, 6 * d_model)

    (r, k, v, lw, nkk, b, g, bonus, gq, gk, gv, gsg, gla, sgr, sgg) = _inproj_call(
        x2, mod3, wp, small, bsz, seq, d_model)

    o_rw = _rwkv_call(r, k, v, lw, nkk, b, g, bonus, row(rw_gn_g[l]), row(rw_gn_b[l]), bd, bsz, seq)
    o_gla = _gla_call(gq, gk, gv, gsg, gla, row(gla_norm_g[l]), bsz, seq)

    out = _tail_call(x2, o_rw, o_gla, sgr, sgg, mod3, w_rw_branch[l].astype(BF16),
                     w_gla_branch[l].astype(BF16), w_mix_out[l].astype(BF16),
                     w_ffn_in[l].astype(BF16), w_ffn_out[l].astype(BF16),
                     row(ln1_g[l]), row(ln1_b[l]), row(ln2_g[l]), row(ln2_b[l]), bsz, seq, d_model)
    return out.reshape(bsz, seq, d_model)
```

```python
import functools

import jax
import jax.numpy as jnp
from jax import lax
from jax.experimental import pallas as pl
from jax.experimental.pallas import tpu as pltpu

F32 = jnp.float32
BF16 = jnp.bfloat16

RW_HEAD = 64
RW_HEADS = 8
RW_WIDTH = RW_HEADS * RW_HEAD
DECAY_LORA = 64
AAA_LORA = 64
GATE_LORA = 128
RW_GN_EPS = 64e-5
RW_COLS = 3 * RW_WIDTH + DECAY_LORA + AAA_LORA + GATE_LORA
GLA_HEADS = 4
GLA_DK = 64
GLA_DV = 128
GLA_KW = GLA_HEADS * GLA_DK
GLA_VW = GLA_HEADS * GLA_DV
GLA_GATE_LORA = 16
GLA_TAU = 16.0
GLA_NORM_EPS = 1e-5
GLA_MAIN = 2 * GLA_KW + 2 * GLA_VW
GLA_COLS = GLA_MAIN + GLA_GATE_LORA
LN_EPS = 1e-5
DEPTH = 1
ALPHA = (2.0 * DEPTH) ** 0.25

LANES = 128
MXU_K = 256
CHUNK = 64
RWKV_UNROLL = 4
RW_GROUP = 2
GLA_UNROLL = 8
REC_TILE = 512
TAIL_TILE = 512
TAIL_SPLIT = 2
FFN_CHUNKS = 2
VMEM_LIMIT = 58 * 1024 * 1024

COL_RW = 0
COL_GLA = RW_COLS
COL_GATE = COL_GLA + GLA_MAIN


def _sigmoid(x):
    return 1.0 / (1.0 + jnp.exp(-x))


def _softplus(x):
    return jnp.maximum(x, 0.0) + jnp.log1p(jnp.exp(-jnp.abs(x)))


def _dot(a, b):
    return jnp.dot(a.astype(BF16), b.astype(BF16), preferred_element_type=F32)


def _dot_nt(a, b):
    return lax.dot_general(a.astype(BF16), b.astype(BF16), (((1,), (1,)), ((), ())),
                           preferred_element_type=F32)


def _split_hi_lo(x):
    hi = x.astype(BF16)
    lo = (x - hi.astype(F32)).astype(BF16)
    return hi, lo


def _dot_exact_rhs(x, w01):
    hi, lo = _split_hi_lo(x)
    if 2 * x.shape[1] <= MXU_K:
        return jnp.dot(jnp.concatenate([hi, lo], axis=1), jnp.concatenate([w01, w01], axis=0),
                       preferred_element_type=F32)
    return (jnp.dot(hi, w01, preferred_element_type=F32)
            + jnp.dot(lo, w01, preferred_element_type=F32))


def _dot_exact_lhs(w01, x):
    hi, lo = _split_hi_lo(x)
    if 2 * x.shape[0] <= MXU_K:
        return jnp.dot(jnp.concatenate([w01, w01], axis=1), jnp.concatenate([hi, lo], axis=0),
                       preferred_element_type=F32)
    return (jnp.dot(w01, hi, preferred_element_type=F32)
            + jnp.dot(w01, lo, preferred_element_type=F32))


def _cumsum_rows(x):
    row = lax.broadcasted_iota(jnp.int32, x.shape, 0)
    shift = 1
    while shift < x.shape[0]:
        x = x + jnp.where(row >= shift, pltpu.roll(x, shift, 0), 0.0)
        shift *= 2
    return x


def _layer_norm(x, eps):
    mu = jnp.mean(x, axis=-1, keepdims=True)
    xc = x - mu
    var = jnp.mean(xc * xc, axis=-1, keepdims=True)
    return xc * lax.rsqrt(var + eps)


def _mod_kernel(c_ref, w_ref, b_ref, o_ref):
    c = c_ref[...]
    sc = c * _sigmoid(c)
    o_ref[...] = _dot(sc, w_ref[...]) + b_ref[...]


def _mod_call(c, w_ada, b_ada):
    bsz, d = c.shape
    n = w_ada.shape[1]
    tn = 1536
    return pl.pallas_call(
        _mod_kernel,
        grid=(n // tn,),
        in_specs=[pl.BlockSpec((bsz, d), lambda j: (0, 0)),
                  pl.BlockSpec((d, tn), lambda j: (0, j)),
                  pl.BlockSpec((1, tn), lambda j: (0, j))],
        out_specs=pl.BlockSpec((bsz, tn), lambda j: (0, j)),
        out_shape=jax.ShapeDtypeStruct((bsz, n), F32),
        compiler_params=pltpu.CompilerParams(dimension_semantics=("arbitrary",),
                                             vmem_limit_bytes=VMEM_LIMIT),
        name="mod",
    )(c, w_ada, b_ada.reshape(1, n))


def _wprep_kernel(w_ref, o_ref, *, d_model):
    main = RW_COLS + GLA_MAIN
    o_ref[:main, :] = w_ref[:main, :].astype(BF16)
    o_ref[main:main + 2 * d_model, :] = w_ref[RW_COLS + GLA_COLS:, :].astype(BF16)
    o_ref[main + 2 * d_model:main + 2 * d_model + GLA_GATE_LORA, :] = (
        w_ref[main:RW_COLS + GLA_COLS, :].astype(BF16))
    o_ref[main + 2 * d_model + GLA_GATE_LORA:, :] = jnp.zeros(
        (LANES - GLA_GATE_LORA, o_ref.shape[1]), BF16)


def _wprep_call(w_in_t, d_model):
    n_in = w_in_t.shape[0]
    n_out = RW_COLS + GLA_MAIN + 2 * d_model + LANES
    cols = 256
    return pl.pallas_call(
        functools.partial(_wprep_kernel, d_model=d_model),
        grid=(d_model // cols,),
        in_specs=[pl.BlockSpec((n_in, cols), lambda i: (0, i))],
        out_specs=pl.BlockSpec((n_out, cols), lambda i: (0, i)),
        out_shape=jax.ShapeDtypeStruct((n_out, d_model), BF16),
        compiler_params=pltpu.CompilerParams(dimension_semantics=("arbitrary",),
                                             vmem_limit_bytes=VMEM_LIMIT),
        name="wprep",
    )(w_in_t)


def _inproj_kernel(x_ref, shift_ref, scale_ref, wp_ref, mu_ref, w0_ref, w2_ref, a0_ref, a2_ref,
                   g2_ref, kk_ref, ka_ref, rk_ref, ga2_ref, gab_ref, bd_ref,
                   r_out, k_out, v_out, lw_out, nkk_out, b_out, g_out, bonus_out,
                   gq_out, gk_out, gv_out, gsg_out, gla_out, sgr_out, sgg_out,
                   carry_ref, *, tm, d_model):
    t = pl.program_id(1)

    @pl.when(t == 0)
    def _():
        carry_ref[...] = jnp.zeros_like(carry_ref)

    u = x_ref[...] * (1.0 + scale_ref[...]) + shift_ref[...]
    proj = _dot_nt(u, wp_ref[...])

    p = proj[:, COL_RW:COL_RW + RW_COLS]
    prev = pltpu.roll(p, 1, 0)
    row = lax.broadcasted_iota(jnp.int32, p.shape, 0)
    prev = jnp.where(row == 0, carry_ref[...], prev)
    carry_ref[...] = p[tm - 1:tm, :]
    ps = p + mu_ref[...] * (prev - p)

    r = ps[:, 0:RW_WIDTH]
    k = ps[:, RW_WIDTH:2 * RW_WIDTH]
    v = ps[:, 2 * RW_WIDTH:3 * RW_WIDTH]
    z = ps[:, 3 * RW_WIDTH:3 * RW_WIDTH + LANES]
    gd = ps[:, 3 * RW_WIDTH + LANES:RW_COLS]

    wlin = w0_ref[...] + _dot(jnp.tanh(z), w2_ref[...])
    w = -_softplus(-wlin) - 0.5
    lw_out[...] = -jnp.exp(w)
    a = _sigmoid(a0_ref[...] + _dot(z, a2_ref[...]))
    g_out[...] = _dot(_sigmoid(gd), g2_ref[...])
    kkv = k * kk_ref[...]
    kp = k * (1.0 + (a - 1.0) * ka_ref[...])
    rkr = r * kp * rk_ref[...]
    bd = bd_ref[...]
    for j in range(RW_WIDTH // LANES):
        sl = slice(j * LANES, (j + 1) * LANES)
        kkj = kkv[:, sl]
        ssq = _dot_exact_rhs(kkj * kkj, bd)
        kkn = kkj / jnp.maximum(jnp.sqrt(ssq), 1e-12)
        nkk_out[:, sl] = -kkn
        b_out[:, sl] = kkn * a[:, sl]
        bonus_out[:, sl] = _dot_exact_rhs(rkr[:, sl], bd) * v[:, sl]
    r_out[...] = r
    k_out[...] = kp
    v_out[...] = v.astype(BF16)

    pg = proj[:, COL_GLA:COL_GLA + GLA_MAIN]
    gq_out[...] = pg[:, 0:GLA_KW] * (GLA_DK ** -0.5)
    gk_out[...] = pg[:, GLA_KW:2 * GLA_KW]
    gv_out[...] = pg[:, 2 * GLA_KW:2 * GLA_KW + GLA_VW].astype(BF16)
    gg = pg[:, 2 * GLA_KW + GLA_VW:GLA_MAIN]
    gsg_out[...] = gg * _sigmoid(gg)
    adg = proj[:, COL_GATE + 2 * d_model:COL_GATE + 2 * d_model + LANES]
    la = _dot(adg, ga2_ref[...]) + gab_ref[...]
    gla_out[...] = -_softplus(-la) / GLA_TAU

    sgr_out[...] = proj[:, COL_GATE:COL_GATE + d_model]
    sgg_out[...] = proj[:, COL_GATE + d_model:COL_GATE + 2 * d_model]


def _inproj_call(x2, mod3, wp, small, bsz, seq, d_model):
    tm = 256
    nt = seq // tm
    n_tok = bsz * seq
    (mu, w0, w2p, a0, a2p, g2, k_k, k_a, r_k, ga2p, gab, bd) = small

    def tok(n):
        return pl.BlockSpec((tm, n), lambda b, t: (b * nt + t, 0))

    def full(a):
        return pl.BlockSpec(a.shape, lambda b, t: (0,) * a.ndim)

    def modspec(idx):
        return pl.BlockSpec((None, 1, d_model), lambda b, t: (b, 0, idx))

    outs = [
        (RW_WIDTH, F32), (RW_WIDTH, F32), (RW_WIDTH, BF16), (RW_WIDTH, F32), (RW_WIDTH, F32),
        (RW_WIDTH, F32), (RW_WIDTH, F32), (RW_WIDTH, F32),
        (GLA_KW, F32), (GLA_KW, F32), (GLA_VW, BF16), (GLA_VW, F32), (GLA_KW, F32),
        (d_model, F32), (d_model, F32),
    ]
    return pl.pallas_call(
        functools.partial(_inproj_kernel, tm=tm, d_model=d_model),
        grid=(bsz, nt),
        in_specs=[tok(d_model), modspec(0), modspec(1), full(wp), full(mu), full(w0), full(w2p),
                  full(a0), full(a2p), full(g2), full(k_k), full(k_a), full(r_k), full(ga2p),
                  full(gab), full(bd)],
        out_specs=[tok(n) for n, _ in outs],
        out_shape=[jax.ShapeDtypeStruct((n_tok, n), dt) for n, dt in outs],
        scratch_shapes=[pltpu.VMEM((1, RW_COLS), F32)],
        compiler_params=pltpu.CompilerParams(dimension_semantics=("arbitrary", "arbitrary"),
                                             vmem_limit_bytes=VMEM_LIMIT),
        name="inproj",
    )(x2, mod3, mod3, wp, mu, w0, w2p, a0, a2p, g2, k_k, k_a, r_k, ga2p, gab, bd)


def _rwkv_kernel(r_ref, k_ref, v_ref, lw_ref, nkk_ref, b_ref, g_ref, bonus_ref, gng_ref, gnb_ref,
                 bd_ref, o_ref, h_ref, rm_s, yl_s, hl_s, *, tb, unroll):
    c = CHUNK
    gw = RW_GROUP * RW_HEAD
    nchunk = tb // c
    ngroup = RW_WIDTH // gw
    t = pl.program_id(1)

    @pl.when(t == 0)
    def _():
        h_ref[...] = jnp.zeros_like(h_ref)

    lane = lax.broadcasted_iota(jnp.int32, (c, gw), 1)
    rowi = lax.broadcasted_iota(jnp.int32, (c, gw), 0)
    head_of_lane = lane // RW_HEAD
    lmask = [head_of_lane == h for h in range(RW_GROUP)]
    strict = (lane % RW_HEAD) < rowi
    incl = (lane % RW_HEAD) <= rowi
    eye = (lane % RW_HEAD) == rowi
    eye_f = eye.astype(F32)
    assert c == RW_HEAD

    def stack(x):
        xb = x.astype(BF16)
        zero = jnp.zeros_like(xb)
        return jnp.concatenate([jnp.where(lmask[h], xb, zero) for h in range(RW_GROUP)], axis=0)

    def block_t(x):
        xt = x.T
        return jnp.concatenate([xt[h * RW_HEAD:(h + 1) * RW_HEAD, :] for h in range(RW_GROUP)],
                               axis=1)

    def local_group(gi, carry):
        units = [(u, g) for u in range(unroll) for g in range(ngroup)]
        n = len(units)
        idxs = [gi * unroll + u for u in range(unroll)]
        rows = [pl.ds(pl.multiple_of(i * c, c), c) for i in idxs]

        def ld(ref, u, g):
            return ref[rows[u], g * gw:(g + 1) * gw]

        lw = [ld(lw_ref, u, g) for u, g in units]
        gcum = [_cumsum_rows(x) for x in lw]
        gend, rt, at, kt, bt, bh, kh, vst = [], [], [], [], [], [], [], []
        for s, (u, g) in enumerate(units):
            g_ = gcum[s]
            ge = g_[c - 1:c, :]
            r = ld(r_ref, u, g)
            k = ld(k_ref, u, g)
            nkk = ld(nkk_ref, u, g)
            b = ld(b_ref, u, g)
            e_neg = jnp.exp(-g_)
            dk = jnp.exp(ge - g_)
            gend.append(ge)
            rt.append(r * jnp.exp(g_))
            at.append(nkk * jnp.exp(g_ - lw[s]))
            kt.append(k * e_neg)
            bt.append(b * e_neg)
            bh.append(b * dk)
            kh.append(k * dk)
            vst.append(stack(ld(v_ref, u, g)))

        x = [_dot_nt(jnp.concatenate([at[s], rt[s]], axis=0),
                     jnp.concatenate([stack(bt[s]), stack(kt[s])], axis=0)) for s in range(n)]
        a_ab = [jnp.where(strict, y[:c, :gw], 0.0) for y in x]
        a_ak = [jnp.where(strict, y[:c, gw:], 0.0) for y in x]
        a_rb = [jnp.where(incl, y[c:, :gw], 0.0) for y in x]
        a_rk = [jnp.where(incl, y[c:, gw:], 0.0) for y in x]
        bht = [block_t(y) for y in bh]
        kht = [block_t(y) for y in kh]

        tp = [eye_f + a for a in a_ab]
        pw = [_dot(a, stack(a)) for a in a_ab]
        for _ in range(4):
            res = [_dot(jnp.concatenate([pw[s], tp[s]], axis=0), stack(pw[s])) for s in range(n)]
            tp = [tp[s] + res[s][c:] for s in range(n)]
            pw = [res[s][:c] for s in range(n)]
        tinv = [tp[s] + _dot(tp[s], stack(pw[s])) for s in range(n)]

        vres = [_dot(jnp.concatenate([a_ak[s], a_rk[s], kht[s]], axis=0), vst[s]) for s in range(n)]
        akv = [y[:c] for y in vres]
        rk_v = [y[c:2 * c] for y in vres]
        h2 = [y[2 * c:] for y in vres]
        tres = [_dot(tinv[s], jnp.concatenate([stack(at[s]), stack(akv[s])], axis=1))
                for s in range(n)]
        ap = [y[:, :gw] for y in tres]
        uloc = [y[:, gw:] for y in tres]
        fres = [_dot(jnp.concatenate([a_rb[s], bht[s]], axis=0),
                     jnp.concatenate([stack(ap[s]), stack(uloc[s])], axis=1)) for s in range(n)]
        for s, (u, g) in enumerate(units):
            y = fres[s]
            rm_s[idxs[u], g, :c, :] = rt[s] + y[:c, :gw]
            rm_s[idxs[u], g, c:, :] = jnp.where(eye, jnp.exp(gend[s]), 0.0) + y[c:, :gw]
            yl_s[rows[u], g * gw:(g + 1) * gw] = y[:c, gw:] + rk_v[s]
            hl_s[idxs[u], g] = y[c:, gw:] + h2[s]
        return carry

    lax.fori_loop(0, nchunk // unroll, local_group, 0)

    hstate = [h_ref[g] for g in range(ngroup)]
    for ci in range(nchunk):
        rows = slice(ci * c, (ci + 1) * c)
        res = [_dot(rm_s[ci, g], stack(hstate[g])) for g in range(ngroup)]
        for g in range(ngroup):
            lanes = slice(g * gw, (g + 1) * gw)
            yl_s[rows, lanes] = res[g][:c] + yl_s[rows, lanes]
            hstate[g] = res[g][c:] + hl_s[ci, g]
    for g in range(ngroup):
        h_ref[g] = hstate[g]

    bd = bd_ref[...]
    for p in range(RW_WIDTH // LANES):
        lanes = slice(p * LANES, (p + 1) * LANES)
        y = yl_s[:, lanes]
        mean = _dot_exact_rhs(y, bd) * (1.0 / RW_HEAD)
        yc = y - mean
        var = _dot_exact_rhs(yc * yc, bd) * (1.0 / RW_HEAD)
        yn = yc * lax.rsqrt(var + RW_GN_EPS)
        out = (yn * gng_ref[:, lanes] + gnb_ref[:, lanes] + bonus_ref[:, lanes]) * g_ref[:, lanes]
        o_ref[:, lanes] = out.astype(o_ref.dtype)


def _rwkv_call(r, k, v, lw, nkk, b, g, bonus, gng, gnb, bd, bsz, seq):
    tb = REC_TILE
    nt = seq // tb
    n_tok = bsz * seq
    gw = RW_GROUP * RW_HEAD
    ngroup = RW_WIDTH // gw
    nchunk = tb // CHUNK

    def tok():
        return pl.BlockSpec((tb, RW_WIDTH), lambda bb, t: (bb * nt + t, 0))

    def full(a):
        return pl.BlockSpec(a.shape, lambda bb, t: (0,) * a.ndim)

    return pl.pallas_call(
        functools.partial(_rwkv_kernel, tb=tb, unroll=RWKV_UNROLL),
        grid=(bsz, nt),
        in_specs=[tok() for _ in range(8)] + [full(gng), full(gnb), full(bd)],
        out_specs=tok(),
        out_shape=jax.ShapeDtypeStruct((n_tok, RW_WIDTH), BF16),
        scratch_shapes=[pltpu.VMEM((ngroup, RW_HEAD, gw), F32),
                        pltpu.VMEM((nchunk, ngroup, CHUNK + RW_HEAD, gw), F32),
                        pltpu.VMEM((tb, RW_WIDTH), F32),
                        pltpu.VMEM((nchunk, ngroup, RW_HEAD, gw), F32)],
        compiler_params=pltpu.CompilerParams(
            dimension_semantics=("arbitrary", "arbitrary"),
            vmem_limit_bytes=VMEM_LIMIT),
        name="rwkv",
    )(r, k, v, lw, nkk, b, g, bonus, gng, gnb, bd)


def _gla_kernel(q_ref, k_ref, v_ref, sg_ref, la_ref, ng_ref, o_ref, st_ref, qs_s, oi_s, kv_s, eb_s,
                *, tb, unroll):
    c = CHUNK
    nchunk = tb // c
    t = pl.program_id(1)

    @pl.when(t == 0)
    def _():
        st_ref[...] = jnp.zeros_like(st_ref)

    lane = lax.broadcasted_iota(jnp.int32, (1, LANES), 1)
    hmask = ((lane < GLA_DK).astype(F32), (lane >= GLA_DK).astype(F32))
    ri = lax.broadcasted_iota(jnp.int32, (c, c), 0)
    ci_ = lax.broadcasted_iota(jnp.int32, (c, c), 1)
    causal = ci_ <= ri
    ltri = causal.astype(BF16)

    def local_group(gi, carry):
        units = [(u, h) for u in range(unroll) for h in range(GLA_HEADS)]
        idxs = [gi * unroll + u for u in range(unroll)]
        rows = [pl.ds(pl.multiple_of(i * c, c), c) for i in idxs]

        def qk_lanes(h):
            return slice((h // 2) * LANES, (h // 2 + 1) * LANES)

        la = [la_ref[rows[u], qk_lanes(h)] for u, h in units]
        bcum = [_dot_exact_lhs(ltri, x) for x in la]
        q_s, k_s, kdec, v = [], [], [], []
        for s, (u, h) in enumerate(units):
            bc = bcum[s]
            blast = bc[c - 1:c, :]
            q = q_ref[rows[u], qk_lanes(h)]
            k = k_ref[rows[u], qk_lanes(h)]
            hm = hmask[h % 2]
            q_s.append((q * jnp.exp(bc) * hm).astype(BF16))
            k_s.append(k * jnp.exp(-bc))
            kdec.append(k * jnp.exp(blast - bc) * hm)
            v.append(v_ref[rows[u], h * LANES:(h + 1) * LANES])
            eb_s[idxs[u], h] = jnp.exp(blast)
        att = [jnp.where(causal, _dot_nt(q_s[s], k_s[s]), 0.0) for s in range(len(units))]
        oi = [_dot(att[s], v[s]) for s in range(len(units))]
        v_t = [x.astype(F32).T.astype(BF16) for x in v]
        kv = [_dot(v_t[s], kdec[s]) for s in range(len(units))]
        for s, (u, h) in enumerate(units):
            qs_s[idxs[u], h] = q_s[s]
            oi_s[rows[u], h * LANES:(h + 1) * LANES] = oi[s]
            kv_s[idxs[u], h] = kv[s]
        return carry

    lax.fori_loop(0, nchunk // unroll, local_group, 0)

    st = [st_ref[h] for h in range(GLA_HEADS)]
    for ci in range(nchunk):
        rows = slice(ci * c, (ci + 1) * c)
        inter = [_dot_nt(qs_s[ci, h], st[h]) for h in range(GLA_HEADS)]
        for h in range(GLA_HEADS):
            lanes = slice(h * LANES, (h + 1) * LANES)
            oi_s[rows, lanes] = oi_s[rows, lanes] + inter[h]
            st[h] = st[h] * eb_s[ci, h] + kv_s[ci, h]
    for h in range(GLA_HEADS):
        st_ref[h] = st[h]

    ng = ng_ref[...]
    for h in range(GLA_HEADS):
        lanes = slice(h * LANES, (h + 1) * LANES)
        o = oi_s[:, lanes]
        o = o * lax.rsqrt(jnp.mean(o * o, axis=-1, keepdims=True) + GLA_NORM_EPS) * ng
        o_ref[:, lanes] = (o * sg_ref[:, lanes]).astype(o_ref.dtype)


def _gla_call(q, k, v, sg, la, ng, bsz, seq):
    tb = REC_TILE
    nt = seq // tb
    n_tok = bsz * seq
    nchunk = tb // CHUNK

    def tok(n):
        return pl.BlockSpec((tb, n), lambda bb, t: (bb * nt + t, 0))

    return pl.pallas_call(
        functools.partial(_gla_kernel, tb=tb, unroll=GLA_UNROLL),
        grid=(bsz, nt),
        in_specs=[tok(GLA_KW), tok(GLA_KW), tok(GLA_VW), tok(GLA_VW), tok(GLA_KW),
                  pl.BlockSpec((1, LANES), lambda bb, t: (0, 0))],
        out_specs=tok(GLA_VW),
        out_shape=jax.ShapeDtypeStruct((n_tok, GLA_VW), BF16),
        scratch_shapes=[pltpu.VMEM((GLA_HEADS, GLA_DV, LANES), F32),
                        pltpu.VMEM((nchunk, GLA_HEADS, CHUNK, LANES), BF16),
                        pltpu.VMEM((tb, GLA_VW), F32),
                        pltpu.VMEM((nchunk, GLA_HEADS, GLA_DV, LANES), F32),
                        pltpu.VMEM((nchunk, GLA_HEADS, 1, LANES), F32)],
        compiler_params=pltpu.CompilerParams(
            dimension_semantics=("arbitrary", "arbitrary"),
            vmem_limit_bytes=VMEM_LIMIT),
        name="gla",
    )(q, k, v, sg, la, ng)


def _tail_kernel(x_ref, orw_ref, ogla_ref, sgr_ref, sgg_ref, gate1_ref, shift2_ref, scale2_ref,
                 gate2_ref, wrb_ref, wgb_ref, wmix_ref, win_ref, wout_ref, ln1g_ref, ln1b_ref,
                 ln2g_ref, ln2b_ref, o_ref, *, d_ff):
    hrows = x_ref.shape[0] // TAIL_SPLIT
    rs = [slice(i * hrows, (i + 1) * hrows) for i in range(TAIL_SPLIT)]
    nsp = range(TAIL_SPLIT)
    b_rw = [jnp.dot(orw_ref[s, :], wrb_ref[...], preferred_element_type=F32) for s in rs]
    b_gla = [jnp.dot(ogla_ref[s, :], wgb_ref[...], preferred_element_type=F32) for s in rs]
    mix = [_dot(_sigmoid(sgr_ref[rs[i], :]) * b_rw[i] + _sigmoid(sgg_ref[rs[i], :]) * b_gla[i],
                wmix_ref[...]) for i in nsp]
    x1 = [_layer_norm(ALPHA * x_ref[rs[i], :] + gate1_ref[...] * mix[i], LN_EPS) * ln1g_ref[...]
          + ln1b_ref[...] for i in nsp]
    u = [(x1[i] * (1.0 + scale2_ref[...]) + shift2_ref[...]).astype(BF16) for i in nsp]
    ffn = [jnp.zeros_like(x1[i]) for i in nsp]
    ntile = d_ff // MXU_K
    bounds = [MXU_K * ((ntile * j) // FFN_CHUNKS) for j in range(FFN_CHUNKS)] + [d_ff]
    for lo, hi in zip(bounds[:-1], bounds[1:]):
        hg, hu = [], []
        for i in nsp:
            hg.append(jnp.dot(u[i], win_ref[:, lo:hi], preferred_element_type=F32))
            hu.append(jnp.dot(u[i], win_ref[:, d_ff + lo:d_ff + hi], preferred_element_type=F32))
        act = [hg[i] * _sigmoid(hg[i]) * hu[i] for i in nsp]
        ffn = [ffn[i] + _dot(act[i], wout_ref[lo:hi, :]) for i in nsp]
    for i in nsp:
        y = _layer_norm(ALPHA * x1[i] + gate2_ref[...] * ffn[i], LN_EPS)
        o_ref[rs[i], :] = y * ln2g_ref[...] + ln2b_ref[...]


def _tail_call(x2, o_rw, o_gla, sgr, sgg, mod3, wrb, wgb, wmix, win, wout, ln1g, ln1b, ln2g, ln2b,
               bsz, seq, d_model):
    tm = TAIL_TILE
    nt = seq // tm
    n_tok = bsz * seq
    d_ff = wout.shape[0]
    assert d_ff % MXU_K == 0

    def tok(n):
        return pl.BlockSpec((tm, n), lambda b, t: (b * nt + t, 0))

    def const(a):
        return pl.BlockSpec(a.shape, lambda b, t: (0,) * a.ndim, pipeline_mode=pl.Buffered(1))

    def modspec(idx):
        return pl.BlockSpec((None, 1, d_model), lambda b, t: (b, 0, idx))

    return pl.pallas_call(
        functools.partial(_tail_kernel, d_ff=d_ff),
        grid=(bsz, nt),
        in_specs=[tok(d_model), tok(RW_WIDTH), tok(GLA_VW), tok(d_model), tok(d_model),
                  modspec(2), modspec(3), modspec(4), modspec(5),
                  const(wrb), const(wgb), const(wmix), const(win), const(wout),
                  const(ln1g), const(ln1b), const(ln2g), const(ln2b)],
        out_specs=tok(d_model),
        out_shape=jax.ShapeDtypeStruct((n_tok, d_model), F32),
        compiler_params=pltpu.CompilerParams(dimension_semantics=("arbitrary", "arbitrary"),
                                             vmem_limit_bytes=VMEM_LIMIT),
        name="tail",
    )(x2, o_rw, o_gla, sgr, sgg, mod3, mod3, mod3, mod3, wrb, wgb, wmix, win, wout,
      ln1g, ln1b, ln2g, ln2b)


def kernel(x, c, w_ada, b_ada, w_in, mu_rw, rw_w0, rw_w2, rw_a0, rw_a2, rw_g2, rw_k_k, rw_k_a,
           rw_r_k, rw_gn_g, rw_gn_b, gla_a2, gla_a_b, gla_norm_g, w_rw_branch, w_gla_branch,
           w_mix_out, ln1_g, ln1_b, w_ffn_in, w_ffn_out, ln2_g, ln2_b):
    bsz, seq, d_model = x.shape
    assert w_ada.shape[0] == DEPTH and seq % REC_TILE == 0
    n_tok = bsz * seq
    l = 0

    wp = _wprep_call(jnp.swapaxes(w_in[l], 0, 1), d_model)
    zeros_lora = jnp.zeros((DECAY_LORA, RW_WIDTH), F32)
    w2p = jnp.concatenate([rw_w2[l], zeros_lora], axis=0).astype(BF16)
    a2p = jnp.concatenate([zeros_lora, rw_a2[l]], axis=0).astype(BF16)
    ga2p = jnp.concatenate(
        [gla_a2[l], jnp.zeros((LANES - GLA_GATE_LORA, GLA_KW), F32)], axis=0).astype(BF16)
    row = lambda a: a.reshape(1, -1)
    hid = jnp.arange(LANES) // RW_HEAD
    bd = (hid[:, None] == hid[None, :]).astype(BF16)
    small = (row(mu_rw[l]), row(rw_w0[l]), w2p, row(rw_a0[l]), a2p, rw_g2[l].astype(BF16),
             row(rw_k_k[l]), row(rw_k_a[l]), row(rw_r_k[l]), ga2p, row(gla_a_b[l]), bd)

    x2 = x.reshape(n_tok, d_model)
    mod = _mod_call(c, w_ada[l], b_ada[l])
    mod3 = mod.reshape(bsz, 1, 6 * d_model)

    (r, k, v, lw, nkk, b, g, bonus, gq, gk, gv, gsg, gla, sgr, sgg) = _inproj_call(
        x2, mod3, wp, small, bsz, seq, d_model)

    o_rw = _rwkv_call(r, k, v, lw, nkk, b, g, bonus, row(rw_gn_g[l]), row(rw_gn_b[l]), bd, bsz, seq)
    o_gla = _gla_call(gq, gk, gv, gsg, gla, row(gla_norm_g[l]), bsz, seq)

    out = _tail_call(x2, o_rw, o_gla, sgr, sgg, mod3, w_rw_branch[l].astype(BF16),
                     w_gla_branch[l].astype(BF16), w_mix_out[l].astype(BF16),
                     w_ffn_in[l].astype(BF16), w_ffn_out[l].astype(BF16),
                     row(ln1_g[l]), row(ln1_b[l]), row(ln2_g[l]), row(ln2_b[l]), bsz, seq, d_model)
    return out.reshape(bsz, seq, d_model)
```

```python
import functools

import jax
import jax.numpy as jnp
from jax import lax
from jax.experimental import pallas as pl
from jax.experimental.pallas import tpu as pltpu

F32 = jnp.float32
BF16 = jnp.bfloat16

RW_HEAD = 64
RW_HEADS = 8
RW_WIDTH = RW_HEADS * RW_HEAD
DECAY_LORA = 64
AAA_LORA = 64
GATE_LORA = 128
RW_GN_EPS = 64e-5
RW_COLS = 3 * RW_WIDTH + DECAY_LORA + AAA_LORA + GATE_LORA
GLA_HEADS = 4
GLA_DK = 64
GLA_DV = 128
GLA_KW = GLA_HEADS * GLA_DK
GLA_VW = GLA_HEADS * GLA_DV
GLA_GATE_LORA = 16
GLA_TAU = 16.0
GLA_NORM_EPS = 1e-5
GLA_MAIN = 2 * GLA_KW + 2 * GLA_VW
GLA_COLS = GLA_MAIN + GLA_GATE_LORA
LN_EPS = 1e-5
DEPTH = 1
ALPHA = (2.0 * DEPTH) ** 0.25

LANES = 128
MXU_K = 256
CHUNK = 64
RWKV_UNROLL = 4
RW_GROUP = 2
GLA_UNROLL = 8
RWKV_TILE = 1024
GLA_TILE = 2048
TAIL_TILE = 512
TAIL_SPLIT = 2
FFN_CHUNKS = 2
VMEM_LIMIT = 58 * 1024 * 1024

COL_RW = 0
COL_GLA = RW_COLS
COL_GATE = COL_GLA + GLA_MAIN


def _sigmoid(x):
    return 1.0 / (1.0 + jnp.exp(-x))


def _softplus(x):
    return jnp.maximum(x, 0.0) + jnp.log1p(jnp.exp(-jnp.abs(x)))


def _dot(a, b):
    return jnp.dot(a.astype(BF16), b.astype(BF16), preferred_element_type=F32)


def _dot_nt(a, b):
    return lax.dot_general(a.astype(BF16), b.astype(BF16), (((1,), (1,)), ((), ())),
                           preferred_element_type=F32)


def _split_hi_lo(x):
    hi = x.astype(BF16)
    lo = (x - hi.astype(F32)).astype(BF16)
    return hi, lo


def _dot_exact_rhs(x, w01):
    hi, lo = _split_hi_lo(x)
    if 2 * x.shape[1] <= MXU_K:
        return jnp.dot(jnp.concatenate([hi, lo], axis=1), jnp.concatenate([w01, w01], axis=0),
                       preferred_element_type=F32)
    return (jnp.dot(hi, w01, preferred_element_type=F32)
            + jnp.dot(lo, w01, preferred_element_type=F32))


def _dot_exact_lhs(w01, x):
    hi, lo = _split_hi_lo(x)
    if 2 * x.shape[0] <= MXU_K:
        return jnp.dot(jnp.concatenate([w01, w01], axis=1), jnp.concatenate([hi, lo], axis=0),
                       preferred_element_type=F32)
    return (jnp.dot(w01, hi, preferred_element_type=F32)
            + jnp.dot(w01, lo, preferred_element_type=F32))


def _cumsum_rows(x):
    row = lax.broadcasted_iota(jnp.int32, x.shape, 0)
    shift = 1
    while shift < x.shape[0]:
        x = x + jnp.where(row >= shift, pltpu.roll(x, shift, 0), 0.0)
        shift *= 2
    return x


def _layer_norm(x, eps):
    mu = jnp.mean(x, axis=-1, keepdims=True)
    xc = x - mu
    var = jnp.mean(xc * xc, axis=-1, keepdims=True)
    return xc * lax.rsqrt(var + eps)


def _mod_kernel(c_ref, w_ref, b_ref, o_ref):
    c = c_ref[...]
    sc = c * _sigmoid(c)
    o_ref[...] = _dot(sc, w_ref[...]) + b_ref[...]


def _mod_call(c, w_ada, b_ada):
    bsz, d = c.shape
    n = w_ada.shape[1]
    tn = 1536
    return pl.pallas_call(
        _mod_kernel,
        grid=(n // tn,),
        in_specs=[pl.BlockSpec((bsz, d), lambda j: (0, 0)),
                  pl.BlockSpec((d, tn), lambda j: (0, j)),
                  pl.BlockSpec((1, tn), lambda j: (0, j))],
        out_specs=pl.BlockSpec((bsz, tn), lambda j: (0, j)),
        out_shape=jax.ShapeDtypeStruct((bsz, n), F32),
        compiler_params=pltpu.CompilerParams(dimension_semantics=("arbitrary",),
                                             vmem_limit_bytes=VMEM_LIMIT),
        name="mod",
    )(c, w_ada, b_ada.reshape(1, n))


def _wprep_kernel(w_ref, o_ref, *, d_model):
    main = RW_COLS + GLA_MAIN
    o_ref[:main, :] = w_ref[:main, :].astype(BF16)
    o_ref[main:main + 2 * d_model, :] = w_ref[RW_COLS + GLA_COLS:, :].astype(BF16)
    o_ref[main + 2 * d_model:main + 2 * d_model + GLA_GATE_LORA, :] = (
        w_ref[main:RW_COLS + GLA_COLS, :].astype(BF16))
    o_ref[main + 2 * d_model + GLA_GATE_LORA:, :] = jnp.zeros(
        (LANES - GLA_GATE_LORA, o_ref.shape[1]), BF16)


def _wprep_call(w_in_t, d_model):
    n_in = w_in_t.shape[0]
    n_out = RW_COLS + GLA_MAIN + 2 * d_model + LANES
    cols = 256
    return pl.pallas_call(
        functools.partial(_wprep_kernel, d_model=d_model),
        grid=(d_model // cols,),
        in_specs=[pl.BlockSpec((n_in, cols), lambda i: (0, i))],
        out_specs=pl.BlockSpec((n_out, cols), lambda i: (0, i)),
        out_shape=jax.ShapeDtypeStruct((n_out, d_model), BF16),
        compiler_params=pltpu.CompilerParams(dimension_semantics=("arbitrary",),
                                             vmem_limit_bytes=VMEM_LIMIT),
        name="wprep",
    )(w_in_t)


def _inproj_kernel(x_ref, shift_ref, scale_ref, wp_ref, mu_ref, w0_ref, w2_ref, a0_ref, a2_ref,
                   g2_ref, kk_ref, ka_ref, rk_ref, ga2_ref, gab_ref, bd_ref,
                   r_out, k_out, v_out, lw_out, nkk_out, b_out, g_out, bonus_out,
                   gq_out, gk_out, gv_out, gsg_out, gla_out, sgr_out, sgg_out,
                   carry_ref, *, tm, d_model):
    t = pl.program_id(1)

    @pl.when(t == 0)
    def _():
        carry_ref[...] = jnp.zeros_like(carry_ref)

    u = x_ref[...] * (1.0 + scale_ref[...]) + shift_ref[...]
    proj = _dot_nt(u, wp_ref[...])

    p = proj[:, COL_RW:COL_RW + RW_COLS]
    prev = pltpu.roll(p, 1, 0)
    row = lax.broadcasted_iota(jnp.int32, p.shape, 0)
    prev = jnp.where(row == 0, carry_ref[...], prev)
    carry_ref[...] = p[tm - 1:tm, :]
    ps = p + mu_ref[...] * (prev - p)

    r = ps[:, 0:RW_WIDTH]
    k = ps[:, RW_WIDTH:2 * RW_WIDTH]
    v = ps[:, 2 * RW_WIDTH:3 * RW_WIDTH]
    z = ps[:, 3 * RW_WIDTH:3 * RW_WIDTH + LANES]
    gd = ps[:, 3 * RW_WIDTH + LANES:RW_COLS]

    wlin = w0_ref[...] + _dot(jnp.tanh(z), w2_ref[...])
    w = -_softplus(-wlin) - 0.5
    lw_out[...] = -jnp.exp(w)
    a = _sigmoid(a0_ref[...] + _dot(z, a2_ref[...]))
    g_out[...] = _dot(_sigmoid(gd), g2_ref[...])
    kkv = k * kk_ref[...]
    kp = k * (1.0 + (a - 1.0) * ka_ref[...])
    rkr = r * kp * rk_ref[...]
    bd = bd_ref[...]
    for j in range(RW_WIDTH // LANES):
        sl = slice(j * LANES, (j + 1) * LANES)
        kkj = kkv[:, sl]
        ssq = _dot_exact_rhs(kkj * kkj, bd)
        kkn = kkj / jnp.maximum(jnp.sqrt(ssq), 1e-12)
        nkk_out[:, sl] = -kkn
        b_out[:, sl] = kkn * a[:, sl]
        bonus_out[:, sl] = _dot_exact_rhs(rkr[:, sl], bd) * v[:, sl]
    r_out[...] = r
    k_out[...] = kp
    v_out[...] = v.astype(BF16)

    pg = proj[:, COL_GLA:COL_GLA + GLA_MAIN]
    gq_out[...] = pg[:, 0:GLA_KW] * (GLA_DK ** -0.5)
    gk_out[...] = pg[:, GLA_KW:2 * GLA_KW]
    gv_out[...] = pg[:, 2 * GLA_KW:2 * GLA_KW + GLA_VW].astype(BF16)
    gg = pg[:, 2 * GLA_KW + GLA_VW:GLA_MAIN]
    gsg_out[...] = gg * _sigmoid(gg)
    adg = proj[:, COL_GATE + 2 * d_model:COL_GATE + 2 * d_model + LANES]
    la = _dot(adg, ga2_ref[...]) + gab_ref[...]
    gla_out[...] = -_softplus(-la) / GLA_TAU

    sgr_out[...] = proj[:, COL_GATE:COL_GATE + d_model]
    sgg_out[...] = proj[:, COL_GATE + d_model:COL_GATE + 2 * d_model]


def _inproj_call(x2, mod3, wp, small, bsz, seq, d_model):
    tm = 256
    nt = seq // tm
    n_tok = bsz * seq
    (mu, w0, w2p, a0, a2p, g2, k_k, k_a, r_k, ga2p, gab, bd) = small

    def tok(n):
        return pl.BlockSpec((tm, n), lambda b, t: (b * nt + t, 0))

    def full(a):
        return pl.BlockSpec(a.shape, lambda b, t: (0,) * a.ndim)

    def modspec(idx):
        return pl.BlockSpec((None, 1, d_model), lambda b, t: (b, 0, idx))

    outs = [
        (RW_WIDTH, F32), (RW_WIDTH, F32), (RW_WIDTH, BF16), (RW_WIDTH, F32), (RW_WIDTH, F32),
        (RW_WIDTH, F32), (RW_WIDTH, F32), (RW_WIDTH, F32),
        (GLA_KW, F32), (GLA_KW, F32), (GLA_VW, BF16), (GLA_VW, F32), (GLA_KW, F32),
        (d_model, F32), (d_model, F32),
    ]
    return pl.pallas_call(
        functools.partial(_inproj_kernel, tm=tm, d_model=d_model),
        grid=(bsz, nt),
        in_specs=[tok(d_model), modspec(0), modspec(1), full(wp), full(mu), full(w0), full(w2p),
                  full(a0), full(a2p), full(g2), full(k_k), full(k_a), full(r_k), full(ga2p),
                  full(gab), full(bd)],
        out_specs=[tok(n) for n, _ in outs],
        out_shape=[jax.ShapeDtypeStruct((n_tok, n), dt) for n, dt in outs],
        scratch_shapes=[pltpu.VMEM((1, RW_COLS), F32)],
        compiler_params=pltpu.CompilerParams(dimension_semantics=("arbitrary", "arbitrary"),
                                             vmem_limit_bytes=VMEM_LIMIT),
        name="inproj",
    )(x2, mod3, mod3, wp, mu, w0, w2p, a0, a2p, g2, k_k, k_a, r_k, ga2p, gab, bd)


def _rwkv_kernel(r_ref, k_ref, v_ref, lw_ref, nkk_ref, b_ref, g_ref, bonus_ref, gng_ref, gnb_ref,
                 bd_ref, o_ref, h_ref, rm_s, yl_s, hl_s, *, tb, unroll):
    c = CHUNK
    gw = RW_GROUP * RW_HEAD
    nchunk = tb // c
    ngroup = RW_WIDTH // gw
    t = pl.program_id(1)

    @pl.when(t == 0)
    def _():
        h_ref[...] = jnp.zeros_like(h_ref)

    lane = lax.broadcasted_iota(jnp.int32, (c, gw), 1)
    rowi = lax.broadcasted_iota(jnp.int32, (c, gw), 0)
    head_of_lane = lane // RW_HEAD
    lmask = [head_of_lane == h for h in range(RW_GROUP)]
    strict = (lane % RW_HEAD) < rowi
    incl = (lane % RW_HEAD) <= rowi
    eye = (lane % RW_HEAD) == rowi
    eye_f = eye.astype(F32)
    assert c == RW_HEAD

    def stack(x):
        xb = x.astype(BF16)
        zero = jnp.zeros_like(xb)
        return jnp.concatenate([jnp.where(lmask[h], xb, zero) for h in range(RW_GROUP)], axis=0)

    def block_t(x):
        xt = x.T
        return jnp.concatenate([xt[h * RW_HEAD:(h + 1) * RW_HEAD, :] for h in range(RW_GROUP)],
                               axis=1)

    def local_group(gi, carry):
        units = [(u, g) for u in range(unroll) for g in range(ngroup)]
        n = len(units)
        idxs = [gi * unroll + u for u in range(unroll)]
        rows = [pl.ds(pl.multiple_of(i * c, c), c) for i in idxs]

        def ld(ref, u, g):
            return ref[rows[u], g * gw:(g + 1) * gw]

        lw = [ld(lw_ref, u, g) for u, g in units]
        gcum = [_cumsum_rows(x) for x in lw]
        gend, rt, at, kt, bt, bh, kh, vst = [], [], [], [], [], [], [], []
        for s, (u, g) in enumerate(units):
            g_ = gcum[s]
            ge = g_[c - 1:c, :]
            r = ld(r_ref, u, g)
            k = ld(k_ref, u, g)
            nkk = ld(nkk_ref, u, g)
            b = ld(b_ref, u, g)
            e_neg = jnp.exp(-g_)
            dk = jnp.exp(ge - g_)
            gend.append(ge)
            rt.append(r * jnp.exp(g_))
            at.append(nkk * jnp.exp(g_ - lw[s]))
            kt.append(k * e_neg)
            bt.append(b * e_neg)
            bh.append(b * dk)
            kh.append(k * dk)
            vst.append(stack(ld(v_ref, u, g)))

        x = [_dot_nt(jnp.concatenate([at[s], rt[s]], axis=0),
                     jnp.concatenate([stack(bt[s]), stack(kt[s])], axis=0)) for s in range(n)]
        a_ab = [jnp.where(strict, y[:c, :gw], 0.0) for y in x]
        a_ak = [jnp.where(strict, y[:c, gw:], 0.0) for y in x]
        a_rb = [jnp.where(incl, y[c:, :gw], 0.0) for y in x]
        a_rk = [jnp.where(incl, y[c:, gw:], 0.0) for y in x]
        bht = [block_t(y) for y in bh]
        kht = [block_t(y) for y in kh]

        tp = [eye_f + a for a in a_ab]
        pw = [_dot(a, stack(a)) for a in a_ab]
        for _ in range(4):
            res = [_dot(jnp.concatenate([pw[s], tp[s]], axis=0), stack(pw[s])) for s in range(n)]
            tp = [tp[s] + res[s][c:] for s in range(n)]
            pw = [res[s][:c] for s in range(n)]
        tinv = [tp[s] + _dot(tp[s], stack(pw[s])) for s in range(n)]

        vres = [_dot(jnp.concatenate([a_ak[s], a_rk[s], kht[s]], axis=0), vst[s]) for s in range(n)]
        akv = [y[:c] for y in vres]
        rk_v = [y[c:2 * c] for y in vres]
        h2 = [y[2 * c:] for y in vres]
        tres = [_dot(tinv[s], jnp.concatenate([stack(at[s]), stack(akv[s])], axis=1))
                for s in range(n)]
        ap = [y[:, :gw] for y in tres]
        uloc = [y[:, gw:] for y in tres]
        fres = [_dot(jnp.concatenate([a_rb[s], bht[s]], axis=0),
                     jnp.concatenate([stack(ap[s]), stack(uloc[s])], axis=1)) for s in range(n)]
        for s, (u, g) in enumerate(units):
            y = fres[s]
            rm_s[idxs[u], g, :c, :] = rt[s] + y[:c, :gw]
            rm_s[idxs[u], g, c:, :] = jnp.where(eye, jnp.exp(gend[s]), 0.0) + y[c:, :gw]
            yl_s[rows[u], g * gw:(g + 1) * gw] = y[:c, gw:] + rk_v[s]
            hl_s[idxs[u], g] = y[c:, gw:] + h2[s]
        return carry

    lax.fori_loop(0, nchunk // unroll, local_group, 0)

    hstate = [h_ref[g] for g in range(ngroup)]
    for ci in range(nchunk):
        rows = slice(ci * c, (ci + 1) * c)
        res = [_dot(rm_s[ci, g], stack(hstate[g])) for g in range(ngroup)]
        for g in range(ngroup):
            lanes = slice(g * gw, (g + 1) * gw)
            yl_s[rows, lanes] = res[g][:c] + yl_s[rows, lanes]
            hstate[g] = res[g][c:] + hl_s[ci, g]
    for g in range(ngroup):
        h_ref[g] = hstate[g]

    bd = bd_ref[...]
    for p in range(RW_WIDTH // LANES):
        lanes = slice(p * LANES, (p + 1) * LANES)
        y = yl_s[:, lanes]
        mean = _dot_exact_rhs(y, bd) * (1.0 / RW_HEAD)
        yc = y - mean
        var = _dot_exact_rhs(yc * yc, bd) * (1.0 / RW_HEAD)
        yn = yc * lax.rsqrt(var + RW_GN_EPS)
        out = (yn * gng_ref[:, lanes] + gnb_ref[:, lanes] + bonus_ref[:, lanes]) * g_ref[:, lanes]
        o_ref[:, lanes] = out.astype(o_ref.dtype)


def _rwkv_call(r, k, v, lw, nkk, b, g, bonus, gng, gnb, bd, bsz, seq):
    tb = RWKV_TILE
    nt = seq // tb
    n_tok = bsz * seq
    gw = RW_GROUP * RW_HEAD
    ngroup = RW_WIDTH // gw
    nchunk = tb // CHUNK

    def tok():
        return pl.BlockSpec((tb, RW_WIDTH), lambda bb, t: (bb * nt + t, 0))

    def full(a):
        return pl.BlockSpec(a.shape, lambda bb, t: (0,) * a.ndim)

    return pl.pallas_call(
        functools.partial(_rwkv_kernel, tb=tb, unroll=RWKV_UNROLL),
        grid=(bsz, nt),
        in_specs=[tok() for _ in range(8)] + [full(gng), full(gnb), full(bd)],
        out_specs=tok(),
        out_shape=jax.ShapeDtypeStruct((n_tok, RW_WIDTH), BF16),
        scratch_shapes=[pltpu.VMEM((ngroup, RW_HEAD, gw), F32),
                        pltpu.VMEM((nchunk, ngroup, CHUNK + RW_HEAD, gw), F32),
                        pltpu.VMEM((tb, RW_WIDTH), F32),
                        pltpu.VMEM((nchunk, ngroup, RW_HEAD, gw), F32)],
        compiler_params=pltpu.CompilerParams(
            dimension_semantics=("arbitrary", "arbitrary"),
            vmem_limit_bytes=VMEM_LIMIT),
        name="rwkv",
    )(r, k, v, lw, nkk, b, g, bonus, gng, gnb, bd)


def _gla_kernel(q_ref, k_ref, v_ref, sg_ref, la_ref, ng_ref, o_ref, st_ref, qs_s, oi_s, kv_s, eb_s,
                *, tb, unroll):
    c = CHUNK
    nchunk = tb // c
    t = pl.program_id(1)

    @pl.when(t == 0)
    def _():
        st_ref[...] = jnp.zeros_like(st_ref)

    lane = lax.broadcasted_iota(jnp.int32, (1, LANES), 1)
    hmask = ((lane < GLA_DK).astype(F32), (lane >= GLA_DK).astype(F32))
    ri = lax.broadcasted_iota(jnp.int32, (c, c), 0)
    ci_ = lax.broadcasted_iota(jnp.int32, (c, c), 1)
    causal = ci_ <= ri
    ltri = causal.astype(BF16)

    def local_group(gi, carry):
        units = [(u, h) for u in range(unroll) for h in range(GLA_HEADS)]
        idxs = [gi * unroll + u for u in range(unroll)]
        rows = [pl.ds(pl.multiple_of(i * c, c), c) for i in idxs]

        def qk_lanes(h):
            return slice((h // 2) * LANES, (h // 2 + 1) * LANES)

        la = [la_ref[rows[u], qk_lanes(h)] for u, h in units]
        bcum = [_dot_exact_lhs(ltri, x) for x in la]
        q_s, k_s, kdec, v = [], [], [], []
        for s, (u, h) in enumerate(units):
            bc = bcum[s]
            blast = bc[c - 1:c, :]
            q = q_ref[rows[u], qk_lanes(h)]
            k = k_ref[rows[u], qk_lanes(h)]
            hm = hmask[h % 2]
            q_s.append((q * jnp.exp(bc) * hm).astype(BF16))
            k_s.append(k * jnp.exp(-bc))
            kdec.append(k * jnp.exp(blast - bc) * hm)
            v.append(v_ref[rows[u], h * LANES:(h + 1) * LANES])
            eb_s[idxs[u], h] = jnp.exp(blast)
        att = [jnp.where(causal, _dot_nt(q_s[s], k_s[s]), 0.0) for s in range(len(units))]
        oi = [_dot(att[s], v[s]) for s in range(len(units))]
        v_t = [x.astype(F32).T.astype(BF16) for x in v]
        kv = [_dot(v_t[s], kdec[s]) for s in range(len(units))]
        for s, (u, h) in enumerate(units):
            qs_s[idxs[u], h] = q_s[s]
            oi_s[rows[u], h * LANES:(h + 1) * LANES] = oi[s]
            kv_s[idxs[u], h] = kv[s]
        return carry

    lax.fori_loop(0, nchunk // unroll, local_group, 0)

    st = [st_ref[h] for h in range(GLA_HEADS)]
    for ci in range(nchunk):
        rows = slice(ci * c, (ci + 1) * c)
        inter = [_dot_nt(qs_s[ci, h], st[h]) for h in range(GLA_HEADS)]
        for h in range(GLA_HEADS):
            lanes = slice(h * LANES, (h + 1) * LANES)
            oi_s[rows, lanes] = oi_s[rows, lanes] + inter[h]
            st[h] = st[h] * eb_s[ci, h] + kv_s[ci, h]
    for h in range(GLA_HEADS):
        st_ref[h] = st[h]

    ng = ng_ref[...]
    for h in range(GLA_HEADS):
        lanes = slice(h * LANES, (h + 1) * LANES)
        o = oi_s[:, lanes]
        o = o * lax.rsqrt(jnp.mean(o * o, axis=-1, keepdims=True) + GLA_NORM_EPS) * ng
        o_ref[:, lanes] = (o * sg_ref[:, lanes]).astype(o_ref.dtype)


def _gla_call(q, k, v, sg, la, ng, bsz, seq):
    tb = GLA_TILE
    nt = seq // tb
    n_tok = bsz * seq
    nchunk = tb // CHUNK

    def tok(n):
        return pl.BlockSpec((tb, n), lambda bb, t: (bb * nt + t, 0))

    return pl.pallas_call(
        functools.partial(_gla_kernel, tb=tb, unroll=GLA_UNROLL),
        grid=(bsz, nt),
        in_specs=[tok(GLA_KW), tok(GLA_KW), tok(GLA_VW), tok(GLA_VW), tok(GLA_KW),
                  pl.BlockSpec((1, LANES), lambda bb, t: (0, 0))],
        out_specs=tok(GLA_VW),
        out_shape=jax.ShapeDtypeStruct((n_tok, GLA_VW), BF16),
        scratch_shapes=[pltpu.VMEM((GLA_HEADS, GLA_DV, LANES), F32),
                        pltpu.VMEM((nchunk, GLA_HEADS, CHUNK, LANES), BF16),
                        pltpu.VMEM((tb, GLA_VW), F32),
                        pltpu.VMEM((nchunk, GLA_HEADS, GLA_DV, LANES), F32),
                        pltpu.VMEM((nchunk, GLA_HEADS, 1, LANES), F32)],
        compiler_params=pltpu.CompilerParams(
            dimension_semantics=("arbitrary", "arbitrary"),
            vmem_limit_bytes=VMEM_LIMIT),
        name="gla",
    )(q, k, v, sg, la, ng)


def _tail_kernel(x_ref, orw_ref, ogla_ref, sgr_ref, sgg_ref, gate1_ref, shift2_ref, scale2_ref,
                 gate2_ref, wrb_ref, wgb_ref, wmix_ref, win_ref, wout_ref, ln1g_ref, ln1b_ref,
                 ln2g_ref, ln2b_ref, o_ref, *, d_ff):
    hrows = x_ref.shape[0] // TAIL_SPLIT
    rs = [slice(i * hrows, (i + 1) * hrows) for i in range(TAIL_SPLIT)]
    nsp = range(TAIL_SPLIT)
    b_rw = [jnp.dot(orw_ref[s, :], wrb_ref[...], preferred_element_type=F32) for s in rs]
    b_gla = [jnp.dot(ogla_ref[s, :], wgb_ref[...], preferred_element_type=F32) for s in rs]
    mix = [_dot(_sigmoid(sgr_ref[rs[i], :]) * b_rw[i] + _sigmoid(sgg_ref[rs[i], :]) * b_gla[i],
                wmix_ref[...]) for i in nsp]
    x1 = [_layer_norm(ALPHA * x_ref[rs[i], :] + gate1_ref[...] * mix[i], LN_EPS) * ln1g_ref[...]
          + ln1b_ref[...] for i in nsp]
    u = [(x1[i] * (1.0 + scale2_ref[...]) + shift2_ref[...]).astype(BF16) for i in nsp]
    ffn = [jnp.zeros_like(x1[i]) for i in nsp]
    ntile = d_ff // MXU_K
    bounds = [MXU_K * ((ntile * j) // FFN_CHUNKS) for j in range(FFN_CHUNKS)] + [d_ff]
    for lo, hi in zip(bounds[:-1], bounds[1:]):
        hg, hu = [], []
        for i in nsp:
            hg.append(jnp.dot(u[i], win_ref[:, lo:hi], preferred_element_type=F32))
            hu.append(jnp.dot(u[i], win_ref[:, d_ff + lo:d_ff + hi], preferred_element_type=F32))
        act = [hg[i] * _sigmoid(hg[i]) * hu[i] for i in nsp]
        ffn = [ffn[i] + _dot(act[i], wout_ref[lo:hi, :]) for i in nsp]
    for i in nsp:
        y = _layer_norm(ALPHA * x1[i] + gate2_ref[...] * ffn[i], LN_EPS)
        o_ref[rs[i], :] = y * ln2g_ref[...] + ln2b_ref[...]


def _tail_call(x2, o_rw, o_gla, sgr, sgg, mod3, wrb, wgb, wmix, win, wout, ln1g, ln1b, ln2g, ln2b,
               bsz, seq, d_model):
    tm = TAIL_TILE
    nt = seq // tm
    n_tok = bsz * seq
    d_ff = wout.shape[0]
    assert d_ff % MXU_K == 0

    def tok(n):
        return pl.BlockSpec((tm, n), lambda b, t: (b * nt + t, 0))

    def const(a):
        return pl.BlockSpec(a.shape, lambda b, t: (0,) * a.ndim, pipeline_mode=pl.Buffered(1))

    def modspec(idx):
        return pl.BlockSpec((None, 1, d_model), lambda b, t: (b, 0, idx))

    return pl.pallas_call(
        functools.partial(_tail_kernel, d_ff=d_ff),
        grid=(bsz, nt),
        in_specs=[tok(d_model), tok(RW_WIDTH), tok(GLA_VW), tok(d_model), tok(d_model),
                  modspec(2), modspec(3), modspec(4), modspec(5),
                  const(wrb), const(wgb), const(wmix), const(win), const(wout),
                  const(ln1g), const(ln1b), const(ln2g), const(ln2b)],
        out_specs=tok(d_model),
        out_shape=jax.ShapeDtypeStruct((n_tok, d_model), F32),
        compiler_params=pltpu.CompilerParams(dimension_semantics=("arbitrary", "arbitrary"),
                                             vmem_limit_bytes=VMEM_LIMIT),
        name="tail",
    )(x2, o_rw, o_gla, sgr, sgg, mod3, mod3, mod3, mod3, wrb, wgb, wmix, win, wout,
      ln1g, ln1b, ln2g, ln2b)


def kernel(x, c, w_ada, b_ada, w_in, mu_rw, rw_w0, rw_w2, rw_a0, rw_a2, rw_g2, rw_k_k, rw_k_a,
           rw_r_k, rw_gn_g, rw_gn_b, gla_a2, gla_a_b, gla_norm_g, w_rw_branch, w_gla_branch,
           w_mix_out, ln1_g, ln1_b, w_ffn_in, w_ffn_out, ln2_g, ln2_b):
    bsz, seq, d_model = x.shape
    assert w_ada.shape[0] == DEPTH and seq % RWKV_TILE == 0 and seq % GLA_TILE == 0
    n_tok = bsz * seq
    l = 0

    wp = _wprep_call(jnp.swapaxes(w_in[l], 0, 1), d_model)
    zeros_lora = jnp.zeros((DECAY_LORA, RW_WIDTH), F32)
    w2p = jnp.concatenate([rw_w2[l], zeros_lora], axis=0).astype(BF16)
    a2p = jnp.concatenate([zeros_lora, rw_a2[l]], axis=0).astype(BF16)
    ga2p = jnp.concatenate(
        [gla_a2[l], jnp.zeros((LANES - GLA_GATE_LORA, GLA_KW), F32)], axis=0).astype(BF16)
    row = lambda a: a.reshape(1, -1)
    hid = jnp.arange(LANES) // RW_HEAD
    bd = (hid[:, None] == hid[None, :]).astype(BF16)
    small = (row(mu_rw[l]), row(rw_w0[l]), w2p, row(rw_a0[l]), a2p, rw_g2[l].astype(BF16),
             row(rw_k_k[l]), row(rw_k_a[l]), row(rw_r_k[l]), ga2p, row(gla_a_b[l]), bd)

    x2 = x.reshape(n_tok, d_model)
    mod = _mod_call(c, w_ada[l], b_ada[l])
    mod3 = mod.reshape(bsz, 1, 6 * d_model)

    (r, k, v, lw, nkk, b, g, bonus, gq, gk, gv, gsg, gla, sgr, sgg) = _inproj_call(
        x2, mod3, wp, small, bsz, seq, d_model)

    o_rw = _rwkv_call(r, k, v, lw, nkk, b, g, bonus, row(rw_gn_g[l]), row(rw_gn_b[l]), bd, bsz, seq)
    o_gla = _gla_call(gq, gk, gv, gsg, gla, row(gla_norm_g[l]), bsz, seq)

    out = _tail_call(x2, o_rw, o_gla, sgr, sgg, mod3, w_rw_branch[l].astype(BF16),
                     w_gla_branch[l].astype(BF16), w_mix_out[l].astype(BF16),
                     w_ffn_in[l].astype(BF16), w_ffn_out[l].astype(BF16),
                     row(ln1_g[l]), row(ln1_b[l]), row(ln2_g[l]), row(ln2_b[l]), bsz, seq, d_model)
    return out.reshape(bsz, seq, d_model)
```

```python
import functools

import jax
import jax.numpy as jnp
from jax import lax
from jax.experimental import pallas as pl
from jax.experimental.pallas import tpu as pltpu

F32 = jnp.float32
BF16 = jnp.bfloat16

RW_HEAD = 64
RW_HEADS = 8
RW_WIDTH = RW_HEADS * RW_HEAD
DECAY_LORA = 64
AAA_LORA = 64
GATE_LORA = 128
RW_GN_EPS = 64e-5
RW_COLS = 3 * RW_WIDTH + DECAY_LORA + AAA_LORA + GATE_LORA
GLA_HEADS = 4
GLA_DK = 64
GLA_DV = 128
GLA_KW = GLA_HEADS * GLA_DK
GLA_VW = GLA_HEADS * GLA_DV
GLA_GATE_LORA = 16
GLA_TAU = 16.0
GLA_NORM_EPS = 1e-5
GLA_MAIN = 2 * GLA_KW + 2 * GLA_VW
GLA_COLS = GLA_MAIN + GLA_GATE_LORA
LN_EPS = 1e-5
DEPTH = 1
ALPHA = (2.0 * DEPTH) ** 0.25

LANES = 128
MXU_K = 256
CHUNK = 64
RWKV_UNROLL = 1
RW_GROUP = 2
GLA_UNROLL = 8
RWKV_TILE = 256
RWKV_BATCH = 4
GLA_TILE = 2048
TAIL_TILE = 512
TAIL_SPLIT = 2
FFN_CHUNKS = 2
VMEM_LIMIT = 58 * 1024 * 1024

COL_RW = 0
COL_GLA = RW_COLS
COL_GATE = COL_GLA + GLA_MAIN


def _sigmoid(x):
    return 1.0 / (1.0 + jnp.exp(-x))


def _softplus(x):
    return jnp.maximum(x, 0.0) + jnp.log1p(jnp.exp(-jnp.abs(x)))


def _dot(a, b):
    return jnp.dot(a.astype(BF16), b.astype(BF16), preferred_element_type=F32)


def _dot_nt(a, b):
    return lax.dot_general(a.astype(BF16), b.astype(BF16), (((1,), (1,)), ((), ())),
                           preferred_element_type=F32)


def _split_hi_lo(x):
    hi = x.astype(BF16)
    lo = (x - hi.astype(F32)).astype(BF16)
    return hi, lo


def _dot_exact_rhs(x, w01):
    hi, lo = _split_hi_lo(x)
    if 2 * x.shape[1] <= MXU_K:
        return jnp.dot(jnp.concatenate([hi, lo], axis=1), jnp.concatenate([w01, w01], axis=0),
                       preferred_element_type=F32)
    return (jnp.dot(hi, w01, preferred_element_type=F32)
            + jnp.dot(lo, w01, preferred_element_type=F32))


def _dot_exact_lhs(w01, x):
    hi, lo = _split_hi_lo(x)
    if 2 * x.shape[0] <= MXU_K:
        return jnp.dot(jnp.concatenate([w01, w01], axis=1), jnp.concatenate([hi, lo], axis=0),
                       preferred_element_type=F32)
    return (jnp.dot(w01, hi, preferred_element_type=F32)
            + jnp.dot(w01, lo, preferred_element_type=F32))


def _cumsum_rows(x):
    row = lax.broadcasted_iota(jnp.int32, x.shape, 0)
    shift = 1
    while shift < x.shape[0]:
        x = x + jnp.where(row >= shift, pltpu.roll(x, shift, 0), 0.0)
        shift *= 2
    return x


def _layer_norm(x, eps):
    mu = jnp.mean(x, axis=-1, keepdims=True)
    xc = x - mu
    var = jnp.mean(xc * xc, axis=-1, keepdims=True)
    return xc * lax.rsqrt(var + eps)


def _mod_kernel(c_ref, w_ref, b_ref, o_ref):
    c = c_ref[...]
    sc = c * _sigmoid(c)
    o_ref[...] = _dot(sc, w_ref[...]) + b_ref[...]


def _mod_call(c, w_ada, b_ada):
    bsz, d = c.shape
    n = w_ada.shape[1]
    tn = 1536
    return pl.pallas_call(
        _mod_kernel,
        grid=(n // tn,),
        in_specs=[pl.BlockSpec((bsz, d), lambda j: (0, 0)),
                  pl.BlockSpec((d, tn), lambda j: (0, j)),
                  pl.BlockSpec((1, tn), lambda j: (0, j))],
        out_specs=pl.BlockSpec((bsz, tn), lambda j: (0, j)),
        out_shape=jax.ShapeDtypeStruct((bsz, n), F32),
        compiler_params=pltpu.CompilerParams(dimension_semantics=("arbitrary",),
                                             vmem_limit_bytes=VMEM_LIMIT),
        name="mod",
    )(c, w_ada, b_ada.reshape(1, n))


def _wprep_kernel(w_ref, o_ref, *, d_model):
    main = RW_COLS + GLA_MAIN
    o_ref[:main, :] = w_ref[:main, :].astype(BF16)
    o_ref[main:main + 2 * d_model, :] = w_ref[RW_COLS + GLA_COLS:, :].astype(BF16)
    o_ref[main + 2 * d_model:main + 2 * d_model + GLA_GATE_LORA, :] = (
        w_ref[main:RW_COLS + GLA_COLS, :].astype(BF16))
    o_ref[main + 2 * d_model + GLA_GATE_LORA:, :] = jnp.zeros(
        (LANES - GLA_GATE_LORA, o_ref.shape[1]), BF16)


def _wprep_call(w_in_t, d_model):
    n_in = w_in_t.shape[0]
    n_out = RW_COLS + GLA_MAIN + 2 * d_model + LANES
    cols = 256
    return pl.pallas_call(
        functools.partial(_wprep_kernel, d_model=d_model),
        grid=(d_model // cols,),
        in_specs=[pl.BlockSpec((n_in, cols), lambda i: (0, i))],
        out_specs=pl.BlockSpec((n_out, cols), lambda i: (0, i)),
        out_shape=jax.ShapeDtypeStruct((n_out, d_model), BF16),
        compiler_params=pltpu.CompilerParams(dimension_semantics=("arbitrary",),
                                             vmem_limit_bytes=VMEM_LIMIT),
        name="wprep",
    )(w_in_t)


def _inproj_kernel(x_ref, shift_ref, scale_ref, wp_ref, mu_ref, w0_ref, w2_ref, a0_ref, a2_ref,
                   g2_ref, kk_ref, ka_ref, rk_ref, ga2_ref, gab_ref, bd_ref,
                   r_out, k_out, v_out, lw_out, nkk_out, b_out, g_out, bonus_out,
                   gq_out, gk_out, gv_out, gsg_out, gla_out, sgr_out, sgg_out,
                   carry_ref, *, tm, d_model):
    t = pl.program_id(1)

    @pl.when(t == 0)
    def _():
        carry_ref[...] = jnp.zeros_like(carry_ref)

    u = x_ref[...] * (1.0 + scale_ref[...]) + shift_ref[...]
    proj = _dot_nt(u, wp_ref[...])

    p = proj[:, COL_RW:COL_RW + RW_COLS]
    prev = pltpu.roll(p, 1, 0)
    row = lax.broadcasted_iota(jnp.int32, p.shape, 0)
    prev = jnp.where(row == 0, carry_ref[...], prev)
    carry_ref[...] = p[tm - 1:tm, :]
    ps = p + mu_ref[...] * (prev - p)

    r = ps[:, 0:RW_WIDTH]
    k = ps[:, RW_WIDTH:2 * RW_WIDTH]
    v = ps[:, 2 * RW_WIDTH:3 * RW_WIDTH]
    z = ps[:, 3 * RW_WIDTH:3 * RW_WIDTH + LANES]
    gd = ps[:, 3 * RW_WIDTH + LANES:RW_COLS]

    wlin = w0_ref[...] + _dot(jnp.tanh(z), w2_ref[...])
    w = -_softplus(-wlin) - 0.5
    lw_out[...] = -jnp.exp(w)
    a = _sigmoid(a0_ref[...] + _dot(z, a2_ref[...]))
    g_out[...] = _dot(_sigmoid(gd), g2_ref[...])
    kkv = k * kk_ref[...]
    kp = k * (1.0 + (a - 1.0) * ka_ref[...])
    rkr = r * kp * rk_ref[...]
    bd = bd_ref[...]
    for j in range(RW_WIDTH // LANES):
        sl = slice(j * LANES, (j + 1) * LANES)
        kkj = kkv[:, sl]
        ssq = _dot_exact_rhs(kkj * kkj, bd)
        kkn = kkj / jnp.maximum(jnp.sqrt(ssq), 1e-12)
        nkk_out[:, sl] = -kkn
        b_out[:, sl] = kkn * a[:, sl]
        bonus_out[:, sl] = _dot_exact_rhs(rkr[:, sl], bd) * v[:, sl]
    r_out[...] = r
    k_out[...] = kp
    v_out[...] = v.astype(BF16)

    pg = proj[:, COL_GLA:COL_GLA + GLA_MAIN]
    gq_out[...] = pg[:, 0:GLA_KW] * (GLA_DK ** -0.5)
    gk_out[...] = pg[:, GLA_KW:2 * GLA_KW]
    gv_out[...] = pg[:, 2 * GLA_KW:2 * GLA_KW + GLA_VW].astype(BF16)
    gg = pg[:, 2 * GLA_KW + GLA_VW:GLA_MAIN]
    gsg_out[...] = gg * _sigmoid(gg)
    adg = proj[:, COL_GATE + 2 * d_model:COL_GATE + 2 * d_model + LANES]
    la = _dot(adg, ga2_ref[...]) + gab_ref[...]
    gla_out[...] = -_softplus(-la) / GLA_TAU

    sgr_out[...] = proj[:, COL_GATE:COL_GATE + d_model]
    sgg_out[...] = proj[:, COL_GATE + d_model:COL_GATE + 2 * d_model]


def _inproj_call(x2, mod3, wp, small, bsz, seq, d_model):
    tm = 256
    nt = seq // tm
    n_tok = bsz * seq
    (mu, w0, w2p, a0, a2p, g2, k_k, k_a, r_k, ga2p, gab, bd) = small

    def tok(n):
        return pl.BlockSpec((tm, n), lambda b, t: (b * nt + t, 0))

    def full(a):
        return pl.BlockSpec(a.shape, lambda b, t: (0,) * a.ndim)

    def modspec(idx):
        return pl.BlockSpec((None, 1, d_model), lambda b, t: (b, 0, idx))

    outs = [
        (RW_WIDTH, F32), (RW_WIDTH, F32), (RW_WIDTH, BF16), (RW_WIDTH, F32), (RW_WIDTH, F32),
        (RW_WIDTH, F32), (RW_WIDTH, F32), (RW_WIDTH, F32),
        (GLA_KW, F32), (GLA_KW, F32), (GLA_VW, BF16), (GLA_VW, F32), (GLA_KW, F32),
        (d_model, F32), (d_model, F32),
    ]
    return pl.pallas_call(
        functools.partial(_inproj_kernel, tm=tm, d_model=d_model),
        grid=(bsz, nt),
        in_specs=[tok(d_model), modspec(0), modspec(1), full(wp), full(mu), full(w0), full(w2p),
                  full(a0), full(a2p), full(g2), full(k_k), full(k_a), full(r_k), full(ga2p),
                  full(gab), full(bd)],
        out_specs=[tok(n) for n, _ in outs],
        out_shape=[jax.ShapeDtypeStruct((n_tok, n), dt) for n, dt in outs],
        scratch_shapes=[pltpu.VMEM((1, RW_COLS), F32)],
        compiler_params=pltpu.CompilerParams(dimension_semantics=("arbitrary", "arbitrary"),
                                             vmem_limit_bytes=VMEM_LIMIT),
        name="inproj",
    )(x2, mod3, mod3, wp, mu, w0, w2p, a0, a2p, g2, k_k, k_a, r_k, ga2p, gab, bd)


def _rwkv_kernel(r_ref, k_ref, v_ref, lw_ref, nkk_ref, b_ref, g_ref, bonus_ref, gng_ref, gnb_ref,
                 bd_ref, o_ref, h_ref, rm_s, yl_s, hl_s, *, nb, tb, unroll):
    c = CHUNK
    gw = RW_GROUP * RW_HEAD
    nchunk = tb // c
    ngroup = RW_WIDTH // gw
    t = pl.program_id(1)

    @pl.when(t == 0)
    def _():
        h_ref[...] = jnp.zeros_like(h_ref)

    lane = lax.broadcasted_iota(jnp.int32, (c, gw), 1)
    rowi = lax.broadcasted_iota(jnp.int32, (c, gw), 0)
    head_of_lane = lane // RW_HEAD
    lmask = [head_of_lane == h for h in range(RW_GROUP)]
    strict = (lane % RW_HEAD) < rowi
    incl = (lane % RW_HEAD) <= rowi
    eye = (lane % RW_HEAD) == rowi
    eye_f = eye.astype(F32)
    assert c == RW_HEAD

    def stack(x):
        xb = x.astype(BF16)
        zero = jnp.zeros_like(xb)
        return jnp.concatenate([jnp.where(lmask[h], xb, zero) for h in range(RW_GROUP)], axis=0)

    def block_t(x):
        xt = x.T
        return jnp.concatenate([xt[h * RW_HEAD:(h + 1) * RW_HEAD, :] for h in range(RW_GROUP)],
                               axis=1)

    def local_group(gi, carry):
        units = [(e, u, g) for e in range(nb) for u in range(unroll) for g in range(ngroup)]
        n = len(units)
        idxs = [gi * unroll + u for u in range(unroll)]
        rows = [pl.ds(pl.multiple_of(i * c, c), c) for i in idxs]

        def ld(ref, e, u, g):
            return ref[e, rows[u], g * gw:(g + 1) * gw]

        lw = [ld(lw_ref, e, u, g) for e, u, g in units]
        gcum = [_cumsum_rows(x) for x in lw]
        gend, rt, at, kt, bt, bh, kh, vst = [], [], [], [], [], [], [], []
        for s, (e, u, g) in enumerate(units):
            g_ = gcum[s]
            ge = g_[c - 1:c, :]
            r = ld(r_ref, e, u, g)
            k = ld(k_ref, e, u, g)
            nkk = ld(nkk_ref, e, u, g)
            b = ld(b_ref, e, u, g)
            e_neg = jnp.exp(-g_)
            dk = jnp.exp(ge - g_)
            gend.append(ge)
            rt.append(r * jnp.exp(g_))
            at.append(nkk * jnp.exp(g_ - lw[s]))
            kt.append(k * e_neg)
            bt.append(b * e_neg)
            bh.append(b * dk)
            kh.append(k * dk)
            vst.append(stack(ld(v_ref, e, u, g)))

        x = [_dot_nt(jnp.concatenate([at[s], rt[s]], axis=0),
                     jnp.concatenate([stack(bt[s]), stack(kt[s])], axis=0)) for s in range(n)]
        a_ab = [jnp.where(strict, y[:c, :gw], 0.0) for y in x]
        a_ak = [jnp.where(strict, y[:c, gw:], 0.0) for y in x]
        a_rb = [jnp.where(incl, y[c:, :gw], 0.0) for y in x]
        a_rk = [jnp.where(incl, y[c:, gw:], 0.0) for y in x]
        bht = [block_t(y) for y in bh]
        kht = [block_t(y) for y in kh]

        tp = [eye_f + a for a in a_ab]
        pw = [_dot(a, stack(a)) for a in a_ab]
        for _ in range(4):
            res = [_dot(jnp.concatenate([pw[s], tp[s]], axis=0), stack(pw[s])) for s in range(n)]
            tp = [tp[s] + res[s][c:] for s in range(n)]
            pw = [res[s][:c] for s in range(n)]
        tinv = [tp[s] + _dot(tp[s], stack(pw[s])) for s in range(n)]

        vres = [_dot(jnp.concatenate([a_ak[s], a_rk[s], kht[s]], axis=0), vst[s]) for s in range(n)]
        akv = [y[:c] for y in vres]
        rk_v = [y[c:2 * c] for y in vres]
        h2 = [y[2 * c:] for y in vres]
        tres = [_dot(tinv[s], jnp.concatenate([stack(at[s]), stack(akv[s])], axis=1))
                for s in range(n)]
        ap = [y[:, :gw] for y in tres]
        uloc = [y[:, gw:] for y in tres]
        fres = [_dot(jnp.concatenate([a_rb[s], bht[s]], axis=0),
                     jnp.concatenate([stack(ap[s]), stack(uloc[s])], axis=1)) for s in range(n)]
        for s, (e, u, g) in enumerate(units):
            y = fres[s]
            rm_s[e, idxs[u], g, :c, :] = rt[s] + y[:c, :gw]
            rm_s[e, idxs[u], g, c:, :] = jnp.where(eye, jnp.exp(gend[s]), 0.0) + y[c:, :gw]
            yl_s[e, rows[u], g * gw:(g + 1) * gw] = y[:c, gw:] + rk_v[s]
            hl_s[e, idxs[u], g] = y[c:, gw:] + h2[s]
        return carry

    lax.fori_loop(0, nchunk // unroll, local_group, 0)

    chains = [(e, g) for e in range(nb) for g in range(ngroup)]
    hstate = [h_ref[e, g] for e, g in chains]
    for ci in range(nchunk):
        rows = slice(ci * c, (ci + 1) * c)
        res = [_dot(rm_s[e, ci, g], stack(hstate[j])) for j, (e, g) in enumerate(chains)]
        for j, (e, g) in enumerate(chains):
            lanes = slice(g * gw, (g + 1) * gw)
            yl_s[e, rows, lanes] = res[j][:c] + yl_s[e, rows, lanes]
            hstate[j] = res[j][c:] + hl_s[e, ci, g]
    for j, (e, g) in enumerate(chains):
        h_ref[e, g] = hstate[j]

    bd = bd_ref[...]
    for p in range(RW_WIDTH // LANES):
        lanes = slice(p * LANES, (p + 1) * LANES)
        y = yl_s[:, :, lanes].reshape(nb * tb, LANES)
        mean = _dot_exact_rhs(y, bd) * (1.0 / RW_HEAD)
        yc = y - mean
        var = _dot_exact_rhs(yc * yc, bd) * (1.0 / RW_HEAD)
        yn = yc * lax.rsqrt(var + RW_GN_EPS)
        bonus = bonus_ref[:, :, lanes].reshape(nb * tb, LANES)
        gate = g_ref[:, :, lanes].reshape(nb * tb, LANES)
        out = (yn * gng_ref[:, lanes] + gnb_ref[:, lanes] + bonus) * gate
        o_ref[:, :, lanes] = out.reshape(nb, tb, LANES).astype(o_ref.dtype)


def _rwkv_call(r, k, v, lw, nkk, b, g, bonus, gng, gnb, bd, bsz, seq):
    tb = RWKV_TILE
    nb = RWKV_BATCH
    nt = seq // tb
    gw = RW_GROUP * RW_HEAD
    ngroup = RW_WIDTH // gw
    nchunk = tb // CHUNK
    assert bsz % nb == 0

    def tok():
        return pl.BlockSpec((nb, tb, RW_WIDTH), lambda bb, t: (bb, t, 0))

    def full(a):
        return pl.BlockSpec(a.shape, lambda bb, t: (0,) * a.ndim)

    per_seq = lambda a: a.reshape(bsz, seq, RW_WIDTH)
    out = pl.pallas_call(
        functools.partial(_rwkv_kernel, nb=nb, tb=tb, unroll=RWKV_UNROLL),
        grid=(bsz // nb, nt),
        in_specs=[tok() for _ in range(8)] + [full(gng), full(gnb), full(bd)],
        out_specs=tok(),
        out_shape=jax.ShapeDtypeStruct((bsz, seq, RW_WIDTH), BF16),
        scratch_shapes=[pltpu.VMEM((nb, ngroup, RW_HEAD, gw), F32),
                        pltpu.VMEM((nb, nchunk, ngroup, CHUNK + RW_HEAD, gw), F32),
                        pltpu.VMEM((nb, tb, RW_WIDTH), F32),
                        pltpu.VMEM((nb, nchunk, ngroup, RW_HEAD, gw), F32)],
        compiler_params=pltpu.CompilerParams(
            dimension_semantics=("arbitrary", "arbitrary"),
            vmem_limit_bytes=VMEM_LIMIT),
        name="rwkv",
    )(*[per_seq(a) for a in (r, k, v, lw, nkk, b, g, bonus)], gng, gnb, bd)
    return out.reshape(bsz * seq, RW_WIDTH)


def _gla_kernel(q_ref, k_ref, v_ref, sg_ref, la_ref, ng_ref, o_ref, st_ref, qs_s, oi_s, kv_s, eb_s,
                *, tb, unroll):
    c = CHUNK
    nchunk = tb // c
    t = pl.program_id(1)

    @pl.when(t == 0)
    def _():
        st_ref[...] = jnp.zeros_like(st_ref)

    lane = lax.broadcasted_iota(jnp.int32, (1, LANES), 1)
    hmask = ((lane < GLA_DK).astype(F32), (lane >= GLA_DK).astype(F32))
    ri = lax.broadcasted_iota(jnp.int32, (c, c), 0)
    ci_ = lax.broadcasted_iota(jnp.int32, (c, c), 1)
    causal = ci_ <= ri
    ltri = causal.astype(BF16)

    def local_group(gi, carry):
        units = [(u, h) for u in range(unroll) for h in range(GLA_HEADS)]
        idxs = [gi * unroll + u for u in range(unroll)]
        rows = [pl.ds(pl.multiple_of(i * c, c), c) for i in idxs]

        def qk_lanes(h):
            return slice((h // 2) * LANES, (h // 2 + 1) * LANES)

        la = [la_ref[rows[u], qk_lanes(h)] for u, h in units]
        bcum = [_dot_exact_lhs(ltri, x) for x in la]
        q_s, k_s, kdec, v = [], [], [], []
        for s, (u, h) in enumerate(units):
            bc = bcum[s]
            blast = bc[c - 1:c, :]
            q = q_ref[rows[u], qk_lanes(h)]
            k = k_ref[rows[u], qk_lanes(h)]
            hm = hmask[h % 2]
            q_s.append((q * jnp.exp(bc) * hm).astype(BF16))
            k_s.append(k * jnp.exp(-bc))
            kdec.append(k * jnp.exp(blast - bc) * hm)
            v.append(v_ref[rows[u], h * LANES:(h + 1) * LANES])
            eb_s[idxs[u], h] = jnp.exp(blast)
        att = [jnp.where(causal, _dot_nt(q_s[s], k_s[s]), 0.0) for s in range(len(units))]
        oi = [_dot(att[s], v[s]) for s in range(len(units))]
        v_t = [x.astype(F32).T.astype(BF16) for x in v]
        kv = [_dot(v_t[s], kdec[s]) for s in range(len(units))]
        for s, (u, h) in enumerate(units):
            qs_s[idxs[u], h] = q_s[s]
            oi_s[rows[u], h * LANES:(h + 1) * LANES] = oi[s]
            kv_s[idxs[u], h] = kv[s]
        return carry

    lax.fori_loop(0, nchunk // unroll, local_group, 0)

    st = [st_ref[h] for h in range(GLA_HEADS)]
    for ci in range(nchunk):
        rows = slice(ci * c, (ci + 1) * c)
        inter = [_dot_nt(qs_s[ci, h], st[h]) for h in range(GLA_HEADS)]
        for h in range(GLA_HEADS):
            lanes = slice(h * LANES, (h + 1) * LANES)
            oi_s[rows, lanes] = oi_s[rows, lanes] + inter[h]
            st[h] = st[h] * eb_s[ci, h] + kv_s[ci, h]
    for h in range(GLA_HEADS):
        st_ref[h] = st[h]

    ng = ng_ref[...]
    for h in range(GLA_HEADS):
        lanes = slice(h * LANES, (h + 1) * LANES)
        o = oi_s[:, lanes]
        o = o * lax.rsqrt(jnp.mean(o * o, axis=-1, keepdims=True) + GLA_NORM_EPS) * ng
        o_ref[:, lanes] = (o * sg_ref[:, lanes]).astype(o_ref.dtype)


def _gla_call(q, k, v, sg, la, ng, bsz, seq):
    tb = GLA_TILE
    nt = seq // tb
    n_tok = bsz * seq
    nchunk = tb // CHUNK

    def tok(n):
        return pl.BlockSpec((tb, n), lambda bb, t: (bb * nt + t, 0))

    return pl.pallas_call(
        functools.partial(_gla_kernel, tb=tb, unroll=GLA_UNROLL),
        grid=(bsz, nt),
        in_specs=[tok(GLA_KW), tok(GLA_KW), tok(GLA_VW), tok(GLA_VW), tok(GLA_KW),
                  pl.BlockSpec((1, LANES), lambda bb, t: (0, 0))],
        out_specs=tok(GLA_VW),
        out_shape=jax.ShapeDtypeStruct((n_tok, GLA_VW), BF16),
        scratch_shapes=[pltpu.VMEM((GLA_HEADS, GLA_DV, LANES), F32),
                        pltpu.VMEM((nchunk, GLA_HEADS, CHUNK, LANES), BF16),
                        pltpu.VMEM((tb, GLA_VW), F32),
                        pltpu.VMEM((nchunk, GLA_HEADS, GLA_DV, LANES), F32),
                        pltpu.VMEM((nchunk, GLA_HEADS, 1, LANES), F32)],
        compiler_params=pltpu.CompilerParams(
            dimension_semantics=("arbitrary", "arbitrary"),
            vmem_limit_bytes=VMEM_LIMIT),
        name="gla",
    )(q, k, v, sg, la, ng)


def _tail_kernel(x_ref, orw_ref, ogla_ref, sgr_ref, sgg_ref, gate1_ref, shift2_ref, scale2_ref,
                 gate2_ref, wrb_ref, wgb_ref, wmix_ref, win_ref, wout_ref, ln1g_ref, ln1b_ref,
                 ln2g_ref, ln2b_ref, o_ref, *, d_ff):
    hrows = x_ref.shape[0] // TAIL_SPLIT
    rs = [slice(i * hrows, (i + 1) * hrows) for i in range(TAIL_SPLIT)]
    nsp = range(TAIL_SPLIT)
    b_rw = [jnp.dot(orw_ref[s, :], wrb_ref[...], preferred_element_type=F32) for s in rs]
    b_gla = [jnp.dot(ogla_ref[s, :], wgb_ref[...], preferred_element_type=F32) for s in rs]
    mix = [_dot(_sigmoid(sgr_ref[rs[i], :]) * b_rw[i] + _sigmoid(sgg_ref[rs[i], :]) * b_gla[i],
                wmix_ref[...]) for i in nsp]
    x1 = [_layer_norm(ALPHA * x_ref[rs[i], :] + gate1_ref[...] * mix[i], LN_EPS) * ln1g_ref[...]
          + ln1b_ref[...] for i in nsp]
    u = [(x1[i] * (1.0 + scale2_ref[...]) + shift2_ref[...]).astype(BF16) for i in nsp]
    ffn = [jnp.zeros_like(x1[i]) for i in nsp]
    ntile = d_ff // MXU_K
    bounds = [MXU_K * ((ntile * j) // FFN_CHUNKS) for j in range(FFN_CHUNKS)] + [d_ff]
    for lo, hi in zip(bounds[:-1], bounds[1:]):
        hg, hu = [], []
        for i in nsp:
            hg.append(jnp.dot(u[i], win_ref[:, lo:hi], preferred_element_type=F32))
            hu.append(jnp.dot(u[i], win_ref[:, d_ff + lo:d_ff + hi], preferred_element_type=F32))
        act = [hg[i] * _sigmoid(hg[i]) * hu[i] for i in nsp]
        ffn = [ffn[i] + _dot(act[i], wout_ref[lo:hi, :]) for i in nsp]
    for i in nsp:
        y = _layer_norm(ALPHA * x1[i] + gate2_ref[...] * ffn[i], LN_EPS)
        o_ref[rs[i], :] = y * ln2g_ref[...] + ln2b_ref[...]


def _tail_call(x2, o_rw, o_gla, sgr, sgg, mod3, wrb, wgb, wmix, win, wout, ln1g, ln1b, ln2g, ln2b,
               bsz, seq, d_model):
    tm = TAIL_TILE
    nt = seq // tm
    n_tok = bsz * seq
    d_ff = wout.shape[0]
    assert d_ff % MXU_K == 0

    def tok(n):
        return pl.BlockSpec((tm, n), lambda b, t: (b * nt + t, 0))

    def const(a):
        return pl.BlockSpec(a.shape, lambda b, t: (0,) * a.ndim, pipeline_mode=pl.Buffered(1))

    def modspec(idx):
        return pl.BlockSpec((None, 1, d_model), lambda b, t: (b, 0, idx))

    return pl.pallas_call(
        functools.partial(_tail_kernel, d_ff=d_ff),
        grid=(bsz, nt),
        in_specs=[tok(d_model), tok(RW_WIDTH), tok(GLA_VW), tok(d_model), tok(d_model),
                  modspec(2), modspec(3), modspec(4), modspec(5),
                  const(wrb), const(wgb), const(wmix), const(win), const(wout),
                  const(ln1g), const(ln1b), const(ln2g), const(ln2b)],
        out_specs=tok(d_model),
        out_shape=jax.ShapeDtypeStruct((n_tok, d_model), F32),
        compiler_params=pltpu.CompilerParams(dimension_semantics=("arbitrary", "arbitrary"),
                                             vmem_limit_bytes=VMEM_LIMIT),
        name="tail",
    )(x2, o_rw, o_gla, sgr, sgg, mod3, mod3, mod3, mod3, wrb, wgb, wmix, win, wout,
      ln1g, ln1b, ln2g, ln2b)


def kernel(x, c, w_ada, b_ada, w_in, mu_rw, rw_w0, rw_w2, rw_a0, rw_a2, rw_g2, rw_k_k, rw_k_a,
           rw_r_k, rw_gn_g, rw_gn_b, gla_a2, gla_a_b, gla_norm_g, w_rw_branch, w_gla_branch,
           w_mix_out, ln1_g, ln1_b, w_ffn_in, w_ffn_out, ln2_g, ln2_b):
    bsz, seq, d_model = x.shape
    assert w_ada.shape[0] == DEPTH and seq % RWKV_TILE == 0 and seq % GLA_TILE == 0
    n_tok = bsz * seq
    l = 0

    wp = _wprep_call(jnp.swapaxes(w_in[l], 0, 1), d_model)
    zeros_lora = jnp.zeros((DECAY_LORA, RW_WIDTH), F32)
    w2p = jnp.concatenate([rw_w2[l], zeros_lora], axis=0).astype(BF16)
    a2p = jnp.concatenate([zeros_lora, rw_a2[l]], axis=0).astype(BF16)
    ga2p = jnp.concatenate(
        [gla_a2[l], jnp.zeros((LANES - GLA_GATE_LORA, GLA_KW), F32)], axis=0).astype(BF16)
    row = lambda a: a.reshape(1, -1)
    hid = jnp.arange(LANES) // RW_HEAD
    bd = (hid[:, None] == hid[None, :]).astype(BF16)
    small = (row(mu_rw[l]), row(rw_w0[l]), w2p, row(rw_a0[l]), a2p, rw_g2[l].astype(BF16),
             row(rw_k_k[l]), row(rw_k_a[l]), row(rw_r_k[l]), ga2p, row(gla_a_b[l]), bd)

    x2 = x.reshape(n_tok, d_model)
    mod = _mod_call(c, w_ada[l], b_ada[l])
    mod3 = mod.reshape(bsz, 1, 6 * d_model)

    (r, k, v, lw, nkk, b, g, bonus, gq, gk, gv, gsg, gla, sgr, sgg) = _inproj_call(
        x2, mod3, wp, small, bsz, seq, d_model)

    o_rw = _rwkv_call(r, k, v, lw, nkk, b, g, bonus, row(rw_gn_g[l]), row(rw_gn_b[l]), bd, bsz, seq)
    o_gla = _gla_call(gq, gk, gv, gsg, gla, row(gla_norm_g[l]), bsz, seq)

    out = _tail_call(x2, o_rw, o_gla, sgr, sgg, mod3, w_rw_branch[l].astype(BF16),
                     w_gla_branch[l].astype(BF16), w_mix_out[l].astype(BF16),
                     w_ffn_in[l].astype(BF16), w_ffn_out[l].astype(BF16),
                     row(ln1_g[l]), row(ln1_b[l]), row(ln2_g[l]), row(ln2_b[l]), bsz, seq, d_model)
    return out.reshape(bsz, seq, d_model)
```

```python
import functools

import jax
import jax.numpy as jnp
from jax import lax
from jax.experimental import pallas as pl
from jax.experimental.pallas import tpu as pltpu

F32 = jnp.float32
BF16 = jnp.bfloat16

RW_HEAD = 64
RW_HEADS = 8
RW_WIDTH = RW_HEADS * RW_HEAD
DECAY_LORA = 64
AAA_LORA = 64
GATE_LORA = 128
RW_GN_EPS = 64e-5
RW_COLS = 3 * RW_WIDTH + DECAY_LORA + AAA_LORA + GATE_LORA
GLA_HEADS = 4
GLA_DK = 64
GLA_DV = 128
GLA_KW = GLA_HEADS * GLA_DK
GLA_VW = GLA_HEADS * GLA_DV
GLA_GATE_LORA = 16
GLA_TAU = 16.0
GLA_NORM_EPS = 1e-5
GLA_MAIN = 2 * GLA_KW + 2 * GLA_VW
GLA_COLS = GLA_MAIN + GLA_GATE_LORA
LN_EPS = 1e-5
DEPTH = 1
ALPHA = (2.0 * DEPTH) ** 0.25

LANES = 128
MXU_K = 256
CHUNK = 64
MOD_TILE = 1536
WPREP_COLS = 256
INPROJ_TILE = 256
RWKV_UNROLL = 1
RW_GROUP = 2
GLA_UNROLL = 8
RWKV_TILE = 256
RWKV_BATCH = 4
GLA_TILE = 2048
TAIL_TILE = 512
TAIL_SPLIT = 2
FFN_CHUNKS = 2
VMEM_LIMIT = 58 * 1024 * 1024

COL_RW = 0
COL_GLA = RW_COLS
COL_GATE = COL_GLA + GLA_MAIN


def _sigmoid(x):
    return 1.0 / (1.0 + jnp.exp(-x))


def _softplus(x):
    return jnp.maximum(x, 0.0) + jnp.log1p(jnp.exp(-jnp.abs(x)))


def _dot(a, b):
    return jnp.dot(a.astype(BF16), b.astype(BF16), preferred_element_type=F32)


def _dot_nt(a, b):
    return lax.dot_general(a.astype(BF16), b.astype(BF16), (((1,), (1,)), ((), ())),
                           preferred_element_type=F32)


def _split_hi_lo(x):
    hi = x.astype(BF16)
    lo = (x - hi.astype(F32)).astype(BF16)
    return hi, lo


def _dot_exact_rhs(x, w01):
    hi, lo = _split_hi_lo(x)
    if 2 * x.shape[1] <= MXU_K:
        return jnp.dot(jnp.concatenate([hi, lo], axis=1), jnp.concatenate([w01, w01], axis=0),
                       preferred_element_type=F32)
    return (jnp.dot(hi, w01, preferred_element_type=F32)
            + jnp.dot(lo, w01, preferred_element_type=F32))


def _dot_exact_lhs(w01, x):
    hi, lo = _split_hi_lo(x)
    if 2 * x.shape[0] <= MXU_K:
        return jnp.dot(jnp.concatenate([w01, w01], axis=1), jnp.concatenate([hi, lo], axis=0),
                       preferred_element_type=F32)
    return (jnp.dot(w01, hi, preferred_element_type=F32)
            + jnp.dot(w01, lo, preferred_element_type=F32))


def _cumsum_rows(x):
    row = lax.broadcasted_iota(jnp.int32, x.shape, 0)
    shift = 1
    while shift < x.shape[0]:
        x = x + jnp.where(row >= shift, pltpu.roll(x, shift, 0), 0.0)
        shift *= 2
    return x


def _layer_norm(x, eps):
    mu = jnp.mean(x, axis=-1, keepdims=True)
    xc = x - mu
    var = jnp.mean(xc * xc, axis=-1, keepdims=True)
    return xc * lax.rsqrt(var + eps)


def _mod_kernel(c_ref, w_ref, b_ref, o_ref):
    c = c_ref[...]
    sc = c * _sigmoid(c)
    o_ref[...] = _dot(sc, w_ref[...]) + b_ref[...]


def _mod_call(c, w_ada, b_ada):
    bsz, d = c.shape
    n = w_ada.shape[1]
    tn = MOD_TILE
    assert n % tn == 0
    return pl.pallas_call(
        _mod_kernel,
        grid=(n // tn,),
        in_specs=[pl.BlockSpec((bsz, d), lambda j: (0, 0)),
                  pl.BlockSpec((d, tn), lambda j: (0, j)),
                  pl.BlockSpec((1, tn), lambda j: (0, j))],
        out_specs=pl.BlockSpec((bsz, tn), lambda j: (0, j)),
        out_shape=jax.ShapeDtypeStruct((bsz, n), F32),
        compiler_params=pltpu.CompilerParams(dimension_semantics=("arbitrary",),
                                             vmem_limit_bytes=VMEM_LIMIT),
        name="mod",
    )(c, w_ada, b_ada.reshape(1, n))


def _wprep_kernel(w_ref, o_ref, *, d_model):
    main = RW_COLS + GLA_MAIN
    o_ref[:main, :] = w_ref[:main, :].astype(BF16)
    o_ref[main:main + 2 * d_model, :] = w_ref[RW_COLS + GLA_COLS:, :].astype(BF16)
    o_ref[main + 2 * d_model:main + 2 * d_model + GLA_GATE_LORA, :] = (
        w_ref[main:RW_COLS + GLA_COLS, :].astype(BF16))
    o_ref[main + 2 * d_model + GLA_GATE_LORA:, :] = jnp.zeros(
        (LANES - GLA_GATE_LORA, o_ref.shape[1]), BF16)


def _wprep_call(w_in_t, d_model):
    n_in = w_in_t.shape[0]
    n_out = RW_COLS + GLA_MAIN + 2 * d_model + LANES
    cols = WPREP_COLS
    assert d_model % cols == 0
    return pl.pallas_call(
        functools.partial(_wprep_kernel, d_model=d_model),
        grid=(d_model // cols,),
        in_specs=[pl.BlockSpec((n_in, cols), lambda i: (0, i))],
        out_specs=pl.BlockSpec((n_out, cols), lambda i: (0, i)),
        out_shape=jax.ShapeDtypeStruct((n_out, d_model), BF16),
        compiler_params=pltpu.CompilerParams(dimension_semantics=("arbitrary",),
                                             vmem_limit_bytes=VMEM_LIMIT),
        name="wprep",
    )(w_in_t)


def _inproj_kernel(x_ref, shift_ref, scale_ref, wp_ref, mu_ref, w0_ref, w2_ref, a0_ref, a2_ref,
                   g2_ref, kk_ref, ka_ref, rk_ref, ga2_ref, gab_ref, bd_ref,
                   r_out, k_out, v_out, lw_out, nkk_out, b_out, g_out, bonus_out,
                   gq_out, gk_out, gv_out, gsg_out, gla_out, sgr_out, sgg_out,
                   carry_ref, *, tm, d_model):
    t = pl.program_id(1)

    @pl.when(t == 0)
    def _():
        carry_ref[...] = jnp.zeros_like(carry_ref)

    u = x_ref[...] * (1.0 + scale_ref[...]) + shift_ref[...]
    proj = _dot_nt(u, wp_ref[...])

    p = proj[:, COL_RW:COL_RW + RW_COLS]
    prev = pltpu.roll(p, 1, 0)
    row = lax.broadcasted_iota(jnp.int32, p.shape, 0)
    prev = jnp.where(row == 0, carry_ref[...], prev)
    carry_ref[...] = p[tm - 1:tm, :]
    ps = p + mu_ref[...] * (prev - p)

    r = ps[:, 0:RW_WIDTH]
    k = ps[:, RW_WIDTH:2 * RW_WIDTH]
    v = ps[:, 2 * RW_WIDTH:3 * RW_WIDTH]
    z = ps[:, 3 * RW_WIDTH:3 * RW_WIDTH + LANES]
    gd = ps[:, 3 * RW_WIDTH + LANES:RW_COLS]

    wlin = w0_ref[...] + _dot(jnp.tanh(z), w2_ref[...])
    w = -_softplus(-wlin) - 0.5
    lw_out[...] = -jnp.exp(w)
    a = _sigmoid(a0_ref[...] + _dot(z, a2_ref[...]))
    g_out[...] = _dot(_sigmoid(gd), g2_ref[...])
    kkv = k * kk_ref[...]
    kp = k * (1.0 + (a - 1.0) * ka_ref[...])
    rkr = r * kp * rk_ref[...]
    bd = bd_ref[...]
    for j in range(RW_WIDTH // LANES):
        sl = slice(j * LANES, (j + 1) * LANES)
        kkj = kkv[:, sl]
        ssq = _dot_exact_rhs(kkj * kkj, bd)
        kkn = kkj / jnp.maximum(jnp.sqrt(ssq), 1e-12)
        nkk_out[:, sl] = -kkn
        b_out[:, sl] = kkn * a[:, sl]
        bonus_out[:, sl] = _dot_exact_rhs(rkr[:, sl], bd) * v[:, sl]
    r_out[...] = r
    k_out[...] = kp
    v_out[...] = v.astype(BF16)

    pg = proj[:, COL_GLA:COL_GLA + GLA_MAIN]
    gq_out[...] = pg[:, 0:GLA_KW] * (GLA_DK ** -0.5)
    gk_out[...] = pg[:, GLA_KW:2 * GLA_KW]
    gv_out[...] = pg[:, 2 * GLA_KW:2 * GLA_KW + GLA_VW].astype(BF16)
    gg = pg[:, 2 * GLA_KW + GLA_VW:GLA_MAIN]
    gsg_out[...] = gg * _sigmoid(gg)
    adg = proj[:, COL_GATE + 2 * d_model:COL_GATE + 2 * d_model + LANES]
    la = _dot(adg, ga2_ref[...]) + gab_ref[...]
    gla_out[...] = -_softplus(-la) / GLA_TAU

    sgr_out[...] = proj[:, COL_GATE:COL_GATE + d_model]
    sgg_out[...] = proj[:, COL_GATE + d_model:COL_GATE + 2 * d_model]


def _inproj_call(x2, mod3, wp, small, bsz, seq, d_model):
    tm = INPROJ_TILE
    nt = seq // tm
    n_tok = bsz * seq
    (mu, w0, w2p, a0, a2p, g2, k_k, k_a, r_k, ga2p, gab, bd) = small

    def tok(n):
        return pl.BlockSpec((tm, n), lambda b, t: (b * nt + t, 0))

    def full(a):
        return pl.BlockSpec(a.shape, lambda b, t: (0,) * a.ndim)

    def modspec(idx):
        return pl.BlockSpec((None, 1, d_model), lambda b, t: (b, 0, idx))

    outs = [
        (RW_WIDTH, F32), (RW_WIDTH, F32), (RW_WIDTH, BF16), (RW_WIDTH, F32), (RW_WIDTH, F32),
        (RW_WIDTH, F32), (RW_WIDTH, F32), (RW_WIDTH, F32),
        (GLA_KW, F32), (GLA_KW, F32), (GLA_VW, BF16), (GLA_VW, F32), (GLA_KW, F32),
        (d_model, F32), (d_model, F32),
    ]
    return pl.pallas_call(
        functools.partial(_inproj_kernel, tm=tm, d_model=d_model),
        grid=(bsz, nt),
        in_specs=[tok(d_model), modspec(0), modspec(1), full(wp), full(mu), full(w0), full(w2p),
                  full(a0), full(a2p), full(g2), full(k_k), full(k_a), full(r_k), full(ga2p),
                  full(gab), full(bd)],
        out_specs=[tok(n) for n, _ in outs],
        out_shape=[jax.ShapeDtypeStruct((n_tok, n), dt) for n, dt in outs],
        scratch_shapes=[pltpu.VMEM((1, RW_COLS), F32)],
        compiler_params=pltpu.CompilerParams(dimension_semantics=("arbitrary", "arbitrary"),
                                             vmem_limit_bytes=VMEM_LIMIT),
        name="inproj",
    )(x2, mod3, mod3, wp, mu, w0, w2p, a0, a2p, g2, k_k, k_a, r_k, ga2p, gab, bd)


def _rwkv_kernel(*refs, nb, tb, unroll, ncast):
    (r_ref, k_ref, v_ref, lw_ref, nkk_ref, b_ref, g_ref, bonus_ref, gng_ref, gnb_ref,
     bd_ref) = refs[:11]
    cast_in = refs[11:11 + ncast]
    o_ref = refs[11 + ncast]
    cast_out = refs[12 + ncast:12 + 2 * ncast]
    h_ref, rm_s, yl_s, hl_s = refs[12 + 2 * ncast:]
    for w_in_ref, w_out_ref in zip(cast_in, cast_out):
        w_out_ref[...] = w_in_ref[...].astype(BF16)

    c = CHUNK
    gw = RW_GROUP * RW_HEAD
    nchunk = tb // c
    ngroup = RW_WIDTH // gw
    t = pl.program_id(1)

    @pl.when(t == 0)
    def _():
        h_ref[...] = jnp.zeros_like(h_ref)

    lane = lax.broadcasted_iota(jnp.int32, (c, gw), 1)
    rowi = lax.broadcasted_iota(jnp.int32, (c, gw), 0)
    head_of_lane = lane // RW_HEAD
    lmask = [head_of_lane == h for h in range(RW_GROUP)]
    strict = (lane % RW_HEAD) < rowi
    incl = (lane % RW_HEAD) <= rowi
    eye = (lane % RW_HEAD) == rowi
    eye_f = eye.astype(F32)
    assert c == RW_HEAD

    def stack(x):
        xb = x.astype(BF16)
        zero = jnp.zeros_like(xb)
        return jnp.concatenate([jnp.where(lmask[h], xb, zero) for h in range(RW_GROUP)], axis=0)

    def block_t(x):
        xt = x.T
        return jnp.concatenate([xt[h * RW_HEAD:(h + 1) * RW_HEAD, :] for h in range(RW_GROUP)],
                               axis=1)

    def local_group(gi, carry):
        units = [(e, u, g) for e in range(nb) for u in range(unroll) for g in range(ngroup)]
        n = len(units)
        idxs = [gi * unroll + u for u in range(unroll)]
        rows = [pl.ds(pl.multiple_of(i * c, c), c) for i in idxs]

        def ld(ref, e, u, g):
            return ref[e, rows[u], g * gw:(g + 1) * gw]

        lw = [ld(lw_ref, e, u, g) for e, u, g in units]
        gcum = [_cumsum_rows(x) for x in lw]
        gend, rt, at, kt, bt, bh, kh, vst = [], [], [], [], [], [], [], []
        for s, (e, u, g) in enumerate(units):
            g_ = gcum[s]
            ge = g_[c - 1:c, :]
            r = ld(r_ref, e, u, g)
            k = ld(k_ref, e, u, g)
            nkk = ld(nkk_ref, e, u, g)
            b = ld(b_ref, e, u, g)
            e_neg = jnp.exp(-g_)
            dk = jnp.exp(ge - g_)
            gend.append(ge)
            rt.append(r * jnp.exp(g_))
            at.append(nkk * jnp.exp(g_ - lw[s]))
            kt.append(k * e_neg)
            bt.append(b * e_neg)
            bh.append(b * dk)
            kh.append(k * dk)
            vst.append(stack(ld(v_ref, e, u, g)))

        x = [_dot_nt(jnp.concatenate([at[s], rt[s]], axis=0),
                     jnp.concatenate([stack(bt[s]), stack(kt[s])], axis=0)) for s in range(n)]
        a_ab = [jnp.where(strict, y[:c, :gw], 0.0) for y in x]
        a_ak = [jnp.where(strict, y[:c, gw:], 0.0) for y in x]
        a_rb = [jnp.where(incl, y[c:, :gw], 0.0) for y in x]
        a_rk = [jnp.where(incl, y[c:, gw:], 0.0) for y in x]
        bht = [block_t(y) for y in bh]
        kht = [block_t(y) for y in kh]

        tp = [eye_f + a for a in a_ab]
        pw = [_dot(a, stack(a)) for a in a_ab]
        for _ in range(4):
            res = [_dot(jnp.concatenate([pw[s], tp[s]], axis=0), stack(pw[s])) for s in range(n)]
            tp = [tp[s] + res[s][c:] for s in range(n)]
            pw = [res[s][:c] for s in range(n)]
        tinv = [tp[s] + _dot(tp[s], stack(pw[s])) for s in range(n)]

        vres = [_dot(jnp.concatenate([a_ak[s], a_rk[s], kht[s]], axis=0), vst[s]) for s in range(n)]
        akv = [y[:c] for y in vres]
        rk_v = [y[c:2 * c] for y in vres]
        h2 = [y[2 * c:] for y in vres]
        tres = [_dot(tinv[s], jnp.concatenate([stack(at[s]), stack(akv[s])], axis=1))
                for s in range(n)]
        ap = [y[:, :gw] for y in tres]
        uloc = [y[:, gw:] for y in tres]
        fres = [_dot(jnp.concatenate([a_rb[s], bht[s]], axis=0),
                     jnp.concatenate([stack(ap[s]), stack(uloc[s])], axis=1)) for s in range(n)]
        for s, (e, u, g) in enumerate(units):
            y = fres[s]
            rm_s[e, idxs[u], g, :c, :] = rt[s] + y[:c, :gw]
            rm_s[e, idxs[u], g, c:, :] = jnp.where(eye, jnp.exp(gend[s]), 0.0) + y[c:, :gw]
            yl_s[e, rows[u], g * gw:(g + 1) * gw] = y[:c, gw:] + rk_v[s]
            hl_s[e, idxs[u], g] = y[c:, gw:] + h2[s]
        return carry

    lax.fori_loop(0, nchunk // unroll, local_group, 0)

    chains = [(e, g) for e in range(nb) for g in range(ngroup)]
    hstate = [h_ref[e, g] for e, g in chains]
    for ci in range(nchunk):
        rows = slice(ci * c, (ci + 1) * c)
        res = [_dot(rm_s[e, ci, g], stack(hstate[j])) for j, (e, g) in enumerate(chains)]
        for j, (e, g) in enumerate(chains):
            lanes = slice(g * gw, (g + 1) * gw)
            yl_s[e, rows, lanes] = res[j][:c] + yl_s[e, rows, lanes]
            hstate[j] = res[j][c:] + hl_s[e, ci, g]
    for j, (e, g) in enumerate(chains):
        h_ref[e, g] = hstate[j]

    bd = bd_ref[...]
    for p in range(RW_WIDTH // LANES):
        lanes = slice(p * LANES, (p + 1) * LANES)
        y = yl_s[:, :, lanes].reshape(nb * tb, LANES)
        mean = _dot_exact_rhs(y, bd) * (1.0 / RW_HEAD)
        yc = y - mean
        var = _dot_exact_rhs(yc * yc, bd) * (1.0 / RW_HEAD)
        yn = yc * lax.rsqrt(var + RW_GN_EPS)
        bonus = bonus_ref[:, :, lanes].reshape(nb * tb, LANES)
        gate = g_ref[:, :, lanes].reshape(nb * tb, LANES)
        out = (yn * gng_ref[:, lanes] + gnb_ref[:, lanes] + bonus) * gate
        o_ref[:, :, lanes] = out.reshape(nb, tb, LANES).astype(o_ref.dtype)


def _rwkv_call(r, k, v, lw, nkk, b, g, bonus, gng, gnb, bd, weights, bsz, seq):
    tb = RWKV_TILE
    nb = RWKV_BATCH
    nt = seq // tb
    gw = RW_GROUP * RW_HEAD
    ngroup = RW_WIDTH // gw
    nchunk = tb // CHUNK
    nsteps = (bsz // nb) * nt
    assert bsz % nb == 0
    assert all(w.shape[0] % (16 * nsteps) == 0 for w in weights)

    def tok():
        return pl.BlockSpec((nb, tb, RW_WIDTH), lambda bb, t: (bb, t, 0))

    def full(a):
        return pl.BlockSpec(a.shape, lambda bb, t: (0,) * a.ndim)

    def wblock(w):
        return pl.BlockSpec((w.shape[0] // nsteps, w.shape[1]), lambda bb, t: (bb * nt + t, 0))

    per_seq = lambda a: a.reshape(bsz, seq, RW_WIDTH)
    out, *cast = pl.pallas_call(
        functools.partial(_rwkv_kernel, nb=nb, tb=tb, unroll=RWKV_UNROLL, ncast=len(weights)),
        grid=(bsz // nb, nt),
        in_specs=([tok() for _ in range(8)] + [full(gng), full(gnb), full(bd)]
                  + [wblock(w) for w in weights]),
        out_specs=[tok()] + [wblock(w) for w in weights],
        out_shape=([jax.ShapeDtypeStruct((bsz, seq, RW_WIDTH), BF16)]
                   + [jax.ShapeDtypeStruct(w.shape, BF16) for w in weights]),
        scratch_shapes=[pltpu.VMEM((nb, ngroup, RW_HEAD, gw), F32),
                        pltpu.VMEM((nb, nchunk, ngroup, CHUNK + RW_HEAD, gw), F32),
                        pltpu.VMEM((nb, tb, RW_WIDTH), F32),
                        pltpu.VMEM((nb, nchunk, ngroup, RW_HEAD, gw), F32)],
        compiler_params=pltpu.CompilerParams(
            dimension_semantics=("arbitrary", "arbitrary"),
            vmem_limit_bytes=VMEM_LIMIT),
        name="rwkv",
    )(*[per_seq(a) for a in (r, k, v, lw, nkk, b, g, bonus)], gng, gnb, bd, *weights)
    return out.reshape(bsz * seq, RW_WIDTH), cast


def _gla_kernel(q_ref, k_ref, v_ref, sg_ref, la_ref, ng_ref, o_ref, st_ref, qs_s, oi_s, kv_s, eb_s,
                *, tb, unroll):
    c = CHUNK
    nchunk = tb // c
    t = pl.program_id(1)

    @pl.when(t == 0)
    def _():
        st_ref[...] = jnp.zeros_like(st_ref)

    lane = lax.broadcasted_iota(jnp.int32, (1, LANES), 1)
    hmask = ((lane < GLA_DK).astype(F32), (lane >= GLA_DK).astype(F32))
    ri = lax.broadcasted_iota(jnp.int32, (c, c), 0)
    ci_ = lax.broadcasted_iota(jnp.int32, (c, c), 1)
    causal = ci_ <= ri
    ltri = causal.astype(BF16)

    def local_group(gi, carry):
        units = [(u, h) for u in range(unroll) for h in range(GLA_HEADS)]
        idxs = [gi * unroll + u for u in range(unroll)]
        rows = [pl.ds(pl.multiple_of(i * c, c), c) for i in idxs]

        def qk_lanes(h):
            return slice((h // 2) * LANES, (h // 2 + 1) * LANES)

        la = [la_ref[rows[u], qk_lanes(h)] for u, h in units]
        bcum = [_dot_exact_lhs(ltri, x) for x in la]
        q_s, k_s, kdec, v = [], [], [], []
        for s, (u, h) in enumerate(units):
            bc = bcum[s]
            blast = bc[c - 1:c, :]
            q = q_ref[rows[u], qk_lanes(h)]
            k = k_ref[rows[u], qk_lanes(h)]
            hm = hmask[h % 2]
            q_s.append((q * jnp.exp(bc) * hm).astype(BF16))
            k_s.append(k * jnp.exp(-bc))
            kdec.append(k * jnp.exp(blast - bc) * hm)
            v.append(v_ref[rows[u], h * LANES:(h + 1) * LANES])
            eb_s[idxs[u], h] = jnp.exp(blast)
        att = [jnp.where(causal, _dot_nt(q_s[s], k_s[s]), 0.0) for s in range(len(units))]
        oi = [_dot(att[s], v[s]) for s in range(len(units))]
        v_t = [x.astype(F32).T.astype(BF16) for x in v]
        kv = [_dot(v_t[s], kdec[s]) for s in range(len(units))]
        for s, (u, h) in enumerate(units):
            qs_s[idxs[u], h] = q_s[s]
            oi_s[rows[u], h * LANES:(h + 1) * LANES] = oi[s]
            kv_s[idxs[u], h] = kv[s]
        return carry

    lax.fori_loop(0, nchunk // unroll, local_group, 0)

    st = [st_ref[h] for h in range(GLA_HEADS)]
    for ci in range(nchunk):
        rows = slice(ci * c, (ci + 1) * c)
        inter = [_dot_nt(qs_s[ci, h], st[h]) for h in range(GLA_HEADS)]
        for h in range(GLA_HEADS):
            lanes = slice(h * LANES, (h + 1) * LANES)
            oi_s[rows, lanes] = oi_s[rows, lanes] + inter[h]
            st[h] = st[h] * eb_s[ci, h] + kv_s[ci, h]
    for h in range(GLA_HEADS):
        st_ref[h] = st[h]

    ng = ng_ref[...]
    for h in range(GLA_HEADS):
        lanes = slice(h * LANES, (h + 1) * LANES)
        o = oi_s[:, lanes]
        o = o * lax.rsqrt(jnp.mean(o * o, axis=-1, keepdims=True) + GLA_NORM_EPS) * ng
        o_ref[:, lanes] = (o * sg_ref[:, lanes]).astype(o_ref.dtype)


def _gla_call(q, k, v, sg, la, ng, bsz, seq):
    tb = GLA_TILE
    nt = seq // tb
    n_tok = bsz * seq
    nchunk = tb // CHUNK

    def tok(n):
        return pl.BlockSpec((tb, n), lambda bb, t: (bb * nt + t, 0))

    return pl.pallas_call(
        functools.partial(_gla_kernel, tb=tb, unroll=GLA_UNROLL),
        grid=(bsz, nt),
        in_specs=[tok(GLA_KW), tok(GLA_KW), tok(GLA_VW), tok(GLA_VW), tok(GLA_KW),
                  pl.BlockSpec((1, LANES), lambda bb, t: (0, 0))],
        out_specs=tok(GLA_VW),
        out_shape=jax.ShapeDtypeStruct((n_tok, GLA_VW), BF16),
        scratch_shapes=[pltpu.VMEM((GLA_HEADS, GLA_DV, LANES), F32),
                        pltpu.VMEM((nchunk, GLA_HEADS, CHUNK, LANES), BF16),
                        pltpu.VMEM((tb, GLA_VW), F32),
                        pltpu.VMEM((nchunk, GLA_HEADS, GLA_DV, LANES), F32),
                        pltpu.VMEM((nchunk, GLA_HEADS, 1, LANES), F32)],
        compiler_params=pltpu.CompilerParams(
            dimension_semantics=("arbitrary", "arbitrary"),
            vmem_limit_bytes=VMEM_LIMIT),
        name="gla",
    )(q, k, v, sg, la, ng)


def _tail_kernel(x_ref, orw_ref, ogla_ref, sgr_ref, sgg_ref, gate1_ref, shift2_ref, scale2_ref,
                 gate2_ref, wrb_ref, wgb_ref, wmix_ref, win_ref, wout_ref, ln1g_ref, ln1b_ref,
                 ln2g_ref, ln2b_ref, o_ref, *, d_ff):
    hrows = x_ref.shape[0] // TAIL_SPLIT
    rs = [slice(i * hrows, (i + 1) * hrows) for i in range(TAIL_SPLIT)]
    nsp = range(TAIL_SPLIT)
    b_rw = [jnp.dot(orw_ref[s, :], wrb_ref[...], preferred_element_type=F32) for s in rs]
    b_gla = [jnp.dot(ogla_ref[s, :], wgb_ref[...], preferred_element_type=F32) for s in rs]
    mix = [_dot(_sigmoid(sgr_ref[rs[i], :]) * b_rw[i] + _sigmoid(sgg_ref[rs[i], :]) * b_gla[i],
                wmix_ref[...]) for i in nsp]
    x1 = [_layer_norm(ALPHA * x_ref[rs[i], :] + gate1_ref[...] * mix[i], LN_EPS) * ln1g_ref[...]
          + ln1b_ref[...] for i in nsp]
    u = [(x1[i] * (1.0 + scale2_ref[...]) + shift2_ref[...]).astype(BF16) for i in nsp]
    ffn = [jnp.zeros_like(x1[i]) for i in nsp]
    ntile = d_ff // MXU_K
    bounds = [MXU_K * ((ntile * j) // FFN_CHUNKS) for j in range(FFN_CHUNKS)] + [d_ff]
    for lo, hi in zip(bounds[:-1], bounds[1:]):
        hg, hu = [], []
        for i in nsp:
            hg.append(jnp.dot(u[i], win_ref[:, lo:hi], preferred_element_type=F32))
            hu.append(jnp.dot(u[i], win_ref[:, d_ff + lo:d_ff + hi], preferred_element_type=F32))
        act = [hg[i] * _sigmoid(hg[i]) * hu[i] for i in nsp]
        ffn = [ffn[i] + _dot(act[i], wout_ref[lo:hi, :]) for i in nsp]
    for i in nsp:
        y = _layer_norm(ALPHA * x1[i] + gate2_ref[...] * ffn[i], LN_EPS)
        o_ref[rs[i], :] = y * ln2g_ref[...] + ln2b_ref[...]


def _tail_call(x2, o_rw, o_gla, sgr, sgg, mod3, wrb, wgb, wmix, win, wout, ln1g, ln1b, ln2g, ln2b,
               bsz, seq, d_model):
    tm = TAIL_TILE
    nt = seq // tm
    n_tok = bsz * seq
    d_ff = wout.shape[0]
    assert d_ff % MXU_K == 0

    def tok(n):
        return pl.BlockSpec((tm, n), lambda b, t: (b * nt + t, 0))

    def const(a):
        return pl.BlockSpec(a.shape, lambda b, t: (0,) * a.ndim, pipeline_mode=pl.Buffered(1))

    def modspec(idx):
        return pl.BlockSpec((None, 1, d_model), lambda b, t: (b, 0, idx))

    return pl.pallas_call(
        functools.partial(_tail_kernel, d_ff=d_ff),
        grid=(bsz, nt),
        in_specs=[tok(d_model), tok(RW_WIDTH), tok(GLA_VW), tok(d_model), tok(d_model),
                  modspec(2), modspec(3), modspec(4), modspec(5),
                  const(wrb), const(wgb), const(wmix), const(win), const(wout),
                  const(ln1g), const(ln1b), const(ln2g), const(ln2b)],
        out_specs=tok(d_model),
        out_shape=jax.ShapeDtypeStruct((n_tok, d_model), F32),
        compiler_params=pltpu.CompilerParams(dimension_semantics=("arbitrary", "arbitrary"),
                                             vmem_limit_bytes=VMEM_LIMIT),
        name="tail",
    )(x2, o_rw, o_gla, sgr, sgg, mod3, mod3, mod3, mod3, wrb, wgb, wmix, win, wout,
      ln1g, ln1b, ln2g, ln2b)


def kernel(x, c, w_ada, b_ada, w_in, mu_rw, rw_w0, rw_w2, rw_a0, rw_a2, rw_g2, rw_k_k, rw_k_a,
           rw_r_k, rw_gn_g, rw_gn_b, gla_a2, gla_a_b, gla_norm_g, w_rw_branch, w_gla_branch,
           w_mix_out, ln1_g, ln1_b, w_ffn_in, w_ffn_out, ln2_g, ln2_b):
    bsz, seq, d_model = x.shape
    assert w_ada.shape[0] == DEPTH
    assert all(seq % tile == 0 for tile in (INPROJ_TILE, RWKV_TILE, GLA_TILE, TAIL_TILE))
    n_tok = bsz * seq
    l = 0

    wp = _wprep_call(jnp.swapaxes(w_in[l], 0, 1), d_model)
    zeros_lora = jnp.zeros((DECAY_LORA, RW_WIDTH), F32)
    w2p = jnp.concatenate([rw_w2[l], zeros_lora], axis=0).astype(BF16)
    a2p = jnp.concatenate([zeros_lora, rw_a2[l]], axis=0).astype(BF16)
    ga2p = jnp.concatenate(
        [gla_a2[l], jnp.zeros((LANES - GLA_GATE_LORA, GLA_KW), F32)], axis=0).astype(BF16)
    row = lambda a: a.reshape(1, -1)
    hid = jnp.arange(LANES) // RW_HEAD
    bd = (hid[:, None] == hid[None, :]).astype(BF16)
    small = (row(mu_rw[l]), row(rw_w0[l]), w2p, row(rw_a0[l]), a2p, rw_g2[l].astype(BF16),
             row(rw_k_k[l]), row(rw_k_a[l]), row(rw_r_k[l]), ga2p, row(gla_a_b[l]), bd)

    x2 = x.reshape(n_tok, d_model)
    mod = _mod_call(c, w_ada[l], b_ada[l])
    mod3 = mod.reshape(bsz, 1, 6 * d_model)

    (r, k, v, lw, nkk, b, g, bonus, gq, gk, gv, gsg, gla, sgr, sgg) = _inproj_call(
        x2, mod3, wp, small, bsz, seq, d_model)

    o_rw, (wrb, wgb, wmix, win, wout) = _rwkv_call(
        r, k, v, lw, nkk, b, g, bonus, row(rw_gn_g[l]), row(rw_gn_b[l]), bd,
        (w_rw_branch[l], w_gla_branch[l], w_mix_out[l], w_ffn_in[l], w_ffn_out[l]), bsz, seq)
    o_gla = _gla_call(gq, gk, gv, gsg, gla, row(gla_norm_g[l]), bsz, seq)

    out = _tail_call(x2, o_rw, o_gla, sgr, sgg, mod3, wrb, wgb, wmix, win, wout,
                     row(ln1_g[l]), row(ln1_b[l]), row(ln2_g[l]), row(ln2_b[l]), bsz, seq, d_model)
    return out.reshape(bsz, seq, d_model)
```

```python
import functools

import jax
import jax.numpy as jnp
from jax import lax
from jax.experimental import pallas as pl
from jax.experimental.pallas import tpu as pltpu

F32 = jnp.float32
BF16 = jnp.bfloat16

RW_HEAD = 64
RW_HEADS = 8
RW_WIDTH = RW_HEADS * RW_HEAD
DECAY_LORA = 64
AAA_LORA = 64
GATE_LORA = 128
RW_GN_EPS = 64e-5
RW_COLS = 3 * RW_WIDTH + DECAY_LORA + AAA_LORA + GATE_LORA
GLA_HEADS = 4
GLA_DK = 64
GLA_DV = 128
GLA_KW = GLA_HEADS * GLA_DK
GLA_VW = GLA_HEADS * GLA_DV
GLA_GATE_LORA = 16
GLA_TAU = 16.0
GLA_NORM_EPS = 1e-5
GLA_MAIN = 2 * GLA_KW + 2 * GLA_VW
GLA_COLS = GLA_MAIN + GLA_GATE_LORA
LN_EPS = 1e-5
DEPTH = 1
ALPHA = (2.0 * DEPTH) ** 0.25

LANES = 128
MXU_K = 256
CHUNK = 64
MOD_TILE = 768
WPREP_COLS = 128
INPROJ_TILE = 256
RWKV_UNROLL = 1
RW_GROUP = 2
GLA_UNROLL = 8
RWKV_TILE = 256
RWKV_BATCH = 4
GLA_TILE = 2048
TAIL_TILE = 512
TAIL_SPLIT = 2
FFN_CHUNKS = 2
VMEM_LIMIT = 58 * 1024 * 1024

COL_RW = 0
COL_GLA = RW_COLS
COL_GATE = COL_GLA + GLA_MAIN


def _sigmoid(x):
    return 1.0 / (1.0 + jnp.exp(-x))


def _softplus(x):
    return jnp.maximum(x, 0.0) + jnp.log1p(jnp.exp(-jnp.abs(x)))


def _dot(a, b):
    return jnp.dot(a.astype(BF16), b.astype(BF16), preferred_element_type=F32)


def _dot_nt(a, b):
    return lax.dot_general(a.astype(BF16), b.astype(BF16), (((1,), (1,)), ((), ())),
                           preferred_element_type=F32)


def _split_hi_lo(x):
    hi = x.astype(BF16)
    lo = (x - hi.astype(F32)).astype(BF16)
    return hi, lo


def _dot_exact_rhs(x, w01):
    hi, lo = _split_hi_lo(x)
    if 2 * x.shape[1] <= MXU_K:
        return jnp.dot(jnp.concatenate([hi, lo], axis=1), jnp.concatenate([w01, w01], axis=0),
                       preferred_element_type=F32)
    return (jnp.dot(hi, w01, preferred_element_type=F32)
            + jnp.dot(lo, w01, preferred_element_type=F32))


def _dot_exact_lhs(w01, x):
    hi, lo = _split_hi_lo(x)
    if 2 * x.shape[0] <= MXU_K:
        return jnp.dot(jnp.concatenate([w01, w01], axis=1), jnp.concatenate([hi, lo], axis=0),
                       preferred_element_type=F32)
    return (jnp.dot(w01, hi, preferred_element_type=F32)
            + jnp.dot(w01, lo, preferred_element_type=F32))


def _cumsum_rows(x):
    row = lax.broadcasted_iota(jnp.int32, x.shape, 0)
    shift = 1
    while shift < x.shape[0]:
        x = x + jnp.where(row >= shift, pltpu.roll(x, shift, 0), 0.0)
        shift *= 2
    return x


def _layer_norm(x, eps):
    mu = jnp.mean(x, axis=-1, keepdims=True)
    xc = x - mu
    var = jnp.mean(xc * xc, axis=-1, keepdims=True)
    return xc * lax.rsqrt(var + eps)


def _mod_kernel(c_ref, w_ref, b_ref, o_ref):
    c = c_ref[...]
    sc = c * _sigmoid(c)
    o_ref[...] = _dot(sc, w_ref[...]) + b_ref[...]


def _mod_call(c, w_ada, b_ada):
    bsz, d = c.shape
    n = w_ada.shape[1]
    tn = MOD_TILE
    assert n % tn == 0
    return pl.pallas_call(
        _mod_kernel,
        grid=(n // tn,),
        in_specs=[pl.BlockSpec((bsz, d), lambda j: (0, 0)),
                  pl.BlockSpec((d, tn), lambda j: (0, j)),
                  pl.BlockSpec((1, tn), lambda j: (0, j))],
        out_specs=pl.BlockSpec((bsz, tn), lambda j: (0, j)),
        out_shape=jax.ShapeDtypeStruct((bsz, n), F32),
        compiler_params=pltpu.CompilerParams(dimension_semantics=("arbitrary",),
                                             vmem_limit_bytes=VMEM_LIMIT),
        name="mod",
    )(c, w_ada, b_ada.reshape(1, n))


def _wprep_kernel(w_ref, o_ref, *, d_model):
    main = RW_COLS + GLA_MAIN
    o_ref[:main, :] = w_ref[:main, :].astype(BF16)
    o_ref[main:main + 2 * d_model, :] = w_ref[RW_COLS + GLA_COLS:, :].astype(BF16)
    o_ref[main + 2 * d_model:main + 2 * d_model + GLA_GATE_LORA, :] = (
        w_ref[main:RW_COLS + GLA_COLS, :].astype(BF16))
    o_ref[main + 2 * d_model + GLA_GATE_LORA:, :] = jnp.zeros(
        (LANES - GLA_GATE_LORA, o_ref.shape[1]), BF16)


def _wprep_call(w_in_t, d_model):
    n_in = w_in_t.shape[0]
    n_out = RW_COLS + GLA_MAIN + 2 * d_model + LANES
    cols = WPREP_COLS
    assert d_model % cols == 0
    return pl.pallas_call(
        functools.partial(_wprep_kernel, d_model=d_model),
        grid=(d_model // cols,),
        in_specs=[pl.BlockSpec((n_in, cols), lambda i: (0, i))],
        out_specs=pl.BlockSpec((n_out, cols), lambda i: (0, i)),
        out_shape=jax.ShapeDtypeStruct((n_out, d_model), BF16),
        compiler_params=pltpu.CompilerParams(dimension_semantics=("arbitrary",),
                                             vmem_limit_bytes=VMEM_LIMIT),
        name="wprep",
    )(w_in_t)


def _inproj_kernel(x_ref, shift_ref, scale_ref, wp_ref, mu_ref, w0_ref, w2_ref, a0_ref, a2_ref,
                   g2_ref, kk_ref, ka_ref, rk_ref, ga2_ref, gab_ref, bd_ref,
                   r_out, k_out, v_out, lw_out, nkk_out, b_out, g_out, bonus_out,
                   gq_out, gk_out, gv_out, gsg_out, gla_out, sgr_out, sgg_out,
                   carry_ref, *, tm, d_model):
    t = pl.program_id(1)

    @pl.when(t == 0)
    def _():
        carry_ref[...] = jnp.zeros_like(carry_ref)

    u = x_ref[...] * (1.0 + scale_ref[...]) + shift_ref[...]
    proj = _dot_nt(u, wp_ref[...])

    p = proj[:, COL_RW:COL_RW + RW_COLS]
    prev = pltpu.roll(p, 1, 0)
    row = lax.broadcasted_iota(jnp.int32, p.shape, 0)
    prev = jnp.where(row == 0, carry_ref[...], prev)
    carry_ref[...] = p[tm - 1:tm, :]
    ps = p + mu_ref[...] * (prev - p)

    r = ps[:, 0:RW_WIDTH]
    k = ps[:, RW_WIDTH:2 * RW_WIDTH]
    v = ps[:, 2 * RW_WIDTH:3 * RW_WIDTH]
    z = ps[:, 3 * RW_WIDTH:3 * RW_WIDTH + LANES]
    gd = ps[:, 3 * RW_WIDTH + LANES:RW_COLS]

    wlin = w0_ref[...] + _dot(jnp.tanh(z), w2_ref[...])
    w = -_softplus(-wlin) - 0.5
    lw_out[...] = -jnp.exp(w)
    a = _sigmoid(a0_ref[...] + _dot(z, a2_ref[...]))
    g_out[...] = _dot(_sigmoid(gd), g2_ref[...])
    kkv = k * kk_ref[...]
    kp = k * (1.0 + (a - 1.0) * ka_ref[...])
    rkr = r * kp * rk_ref[...]
    bd = bd_ref[...]
    for j in range(RW_WIDTH // LANES):
        sl = slice(j * LANES, (j + 1) * LANES)
        kkj = kkv[:, sl]
        ssq = _dot_exact_rhs(kkj * kkj, bd)
        kkn = kkj / jnp.maximum(jnp.sqrt(ssq), 1e-12)
        nkk_out[:, sl] = -kkn
        b_out[:, sl] = kkn * a[:, sl]
        bonus_out[:, sl] = _dot_exact_rhs(rkr[:, sl], bd) * v[:, sl]
    r_out[...] = r
    k_out[...] = kp
    v_out[...] = v.astype(BF16)

    pg = proj[:, COL_GLA:COL_GLA + GLA_MAIN]
    gq_out[...] = pg[:, 0:GLA_KW] * (GLA_DK ** -0.5)
    gk_out[...] = pg[:, GLA_KW:2 * GLA_KW]
    gv_out[...] = pg[:, 2 * GLA_KW:2 * GLA_KW + GLA_VW].astype(BF16)
    gg = pg[:, 2 * GLA_KW + GLA_VW:GLA_MAIN]
    gsg_out[...] = gg * _sigmoid(gg)
    adg = proj[:, COL_GATE + 2 * d_model:COL_GATE + 2 * d_model + LANES]
    la = _dot(adg, ga2_ref[...]) + gab_ref[...]
    gla_out[...] = -_softplus(-la) / GLA_TAU

    sgr_out[...] = proj[:, COL_GATE:COL_GATE + d_model]
    sgg_out[...] = proj[:, COL_GATE + d_model:COL_GATE + 2 * d_model]


def _inproj_call(x2, mod3, wp, small, bsz, seq, d_model):
    tm = INPROJ_TILE
    nt = seq // tm
    n_tok = bsz * seq
    (mu, w0, w2p, a0, a2p, g2, k_k, k_a, r_k, ga2p, gab, bd) = small

    def tok(n):
        return pl.BlockSpec((tm, n), lambda b, t: (b * nt + t, 0))

    def full(a):
        return pl.BlockSpec(a.shape, lambda b, t: (0,) * a.ndim)

    def modspec(idx):
        return pl.BlockSpec((None, 1, d_model), lambda b, t: (b, 0, idx))

    outs = [
        (RW_WIDTH, F32), (RW_WIDTH, F32), (RW_WIDTH, BF16), (RW_WIDTH, F32), (RW_WIDTH, F32),
        (RW_WIDTH, F32), (RW_WIDTH, F32), (RW_WIDTH, F32),
        (GLA_KW, F32), (GLA_KW, F32), (GLA_VW, BF16), (GLA_VW, F32), (GLA_KW, F32),
        (d_model, F32), (d_model, F32),
    ]
    return pl.pallas_call(
        functools.partial(_inproj_kernel, tm=tm, d_model=d_model),
        grid=(bsz, nt),
        in_specs=[tok(d_model), modspec(0), modspec(1), full(wp), full(mu), full(w0), full(w2p),
                  full(a0), full(a2p), full(g2), full(k_k), full(k_a), full(r_k), full(ga2p),
                  full(gab), full(bd)],
        out_specs=[tok(n) for n, _ in outs],
        out_shape=[jax.ShapeDtypeStruct((n_tok, n), dt) for n, dt in outs],
        scratch_shapes=[pltpu.VMEM((1, RW_COLS), F32)],
        compiler_params=pltpu.CompilerParams(dimension_semantics=("arbitrary", "arbitrary"),
                                             vmem_limit_bytes=VMEM_LIMIT),
        name="inproj",
    )(x2, mod3, mod3, wp, mu, w0, w2p, a0, a2p, g2, k_k, k_a, r_k, ga2p, gab, bd)


def _rwkv_kernel(*refs, nb, tb, unroll, ncast):
    (r_ref, k_ref, v_ref, lw_ref, nkk_ref, b_ref, g_ref, bonus_ref, gng_ref, gnb_ref,
     bd_ref) = refs[:11]
    cast_in = refs[11:11 + ncast]
    o_ref = refs[11 + ncast]
    cast_out = refs[12 + ncast:12 + 2 * ncast]
    h_ref, rm_s, yl_s, hl_s = refs[12 + 2 * ncast:]
    for w_in_ref, w_out_ref in zip(cast_in, cast_out):
        w_out_ref[...] = w_in_ref[...].astype(BF16)

    c = CHUNK
    gw = RW_GROUP * RW_HEAD
    nchunk = tb // c
    ngroup = RW_WIDTH // gw
    t = pl.program_id(1)

    @pl.when(t == 0)
    def _():
        h_ref[...] = jnp.zeros_like(h_ref)

    lane = lax.broadcasted_iota(jnp.int32, (c, gw), 1)
    rowi = lax.broadcasted_iota(jnp.int32, (c, gw), 0)
    head_of_lane = lane // RW_HEAD
    lmask = [head_of_lane == h for h in range(RW_GROUP)]
    strict = (lane % RW_HEAD) < rowi
    incl = (lane % RW_HEAD) <= rowi
    eye = (lane % RW_HEAD) == rowi
    eye_f = eye.astype(F32)
    assert c == RW_HEAD

    def stack(x):
        xb = x.astype(BF16)
        zero = jnp.zeros_like(xb)
        return jnp.concatenate([jnp.where(lmask[h], xb, zero) for h in range(RW_GROUP)], axis=0)

    def block_t(x):
        xt = x.T
        return jnp.concatenate([xt[h * RW_HEAD:(h + 1) * RW_HEAD, :] for h in range(RW_GROUP)],
                               axis=1)

    def local_group(gi, carry):
        units = [(e, u, g) for e in range(nb) for u in range(unroll) for g in range(ngroup)]
        n = len(units)
        idxs = [gi * unroll + u for u in range(unroll)]
        rows = [pl.ds(pl.multiple_of(i * c, c), c) for i in idxs]

        def ld(ref, e, u, g):
            return ref[e, rows[u], g * gw:(g + 1) * gw]

        lw = [ld(lw_ref, e, u, g) for e, u, g in units]
        gcum = [_cumsum_rows(x) for x in lw]
        gend, rt, at, kt, bt, bh, kh, vst = [], [], [], [], [], [], [], []
        for s, (e, u, g) in enumerate(units):
            g_ = gcum[s]
            ge = g_[c - 1:c, :]
            r = ld(r_ref, e, u, g)
            k = ld(k_ref, e, u, g)
            nkk = ld(nkk_ref, e, u, g)
            b = ld(b_ref, e, u, g)
            e_neg = jnp.exp(-g_)
            dk = jnp.exp(ge - g_)
            gend.append(ge)
            rt.append(r * jnp.exp(g_))
            at.append(nkk * jnp.exp(g_ - lw[s]))
            kt.append(k * e_neg)
            bt.append(b * e_neg)
            bh.append(b * dk)
            kh.append(k * dk)
            vst.append(stack(ld(v_ref, e, u, g)))

        x = [_dot_nt(jnp.concatenate([at[s], rt[s]], axis=0),
                     jnp.concatenate([stack(bt[s]), stack(kt[s])], axis=0)) for s in range(n)]
        a_ab = [jnp.where(strict, y[:c, :gw], 0.0) for y in x]
        a_ak = [jnp.where(strict, y[:c, gw:], 0.0) for y in x]
        a_rb = [jnp.where(incl, y[c:, :gw], 0.0) for y in x]
        a_rk = [jnp.where(incl, y[c:, gw:], 0.0) for y in x]
        bht = [block_t(y) for y in bh]
        kht = [block_t(y) for y in kh]

        tp = [eye_f + a for a in a_ab]
        pw = [_dot(a, stack(a)) for a in a_ab]
        for _ in range(4):
            res = [_dot(jnp.concatenate([pw[s], tp[s]], axis=0), stack(pw[s])) for s in range(n)]
            tp = [tp[s] + res[s][c:] for s in range(n)]
            pw = [res[s][:c] for s in range(n)]
        tinv = [tp[s] + _dot(tp[s], stack(pw[s])) for s in range(n)]

        vres = [_dot(jnp.concatenate([a_ak[s], a_rk[s], kht[s]], axis=0), vst[s]) for s in range(n)]
        akv = [y[:c] for y in vres]
        rk_v = [y[c:2 * c] for y in vres]
        h2 = [y[2 * c:] for y in vres]
        tres = [_dot(tinv[s], jnp.concatenate([stack(at[s]), stack(akv[s])], axis=1))
                for s in range(n)]
        ap = [y[:, :gw] for y in tres]
        uloc = [y[:, gw:] for y in tres]
        fres = [_dot(jnp.concatenate([a_rb[s], bht[s]], axis=0),
                     jnp.concatenate([stack(ap[s]), stack(uloc[s])], axis=1)) for s in range(n)]
        for s, (e, u, g) in enumerate(units):
            y = fres[s]
            rm_s[e, idxs[u], g, :c, :] = rt[s] + y[:c, :gw]
            rm_s[e, idxs[u], g, c:, :] = jnp.where(eye, jnp.exp(gend[s]), 0.0) + y[c:, :gw]
            yl_s[e, rows[u], g * gw:(g + 1) * gw] = y[:c, gw:] + rk_v[s]
            hl_s[e, idxs[u], g] = y[c:, gw:] + h2[s]
        return carry

    lax.fori_loop(0, nchunk // unroll, local_group, 0)

    chains = [(e, g) for e in range(nb) for g in range(ngroup)]
    hstate = [h_ref[e, g] for e, g in chains]
    for ci in range(nchunk):
        rows = slice(ci * c, (ci + 1) * c)
        res = [_dot(rm_s[e, ci, g], stack(hstate[j])) for j, (e, g) in enumerate(chains)]
        for j, (e, g) in enumerate(chains):
            lanes = slice(g * gw, (g + 1) * gw)
            yl_s[e, rows, lanes] = res[j][:c] + yl_s[e, rows, lanes]
            hstate[j] = res[j][c:] + hl_s[e, ci, g]
    for j, (e, g) in enumerate(chains):
        h_ref[e, g] = hstate[j]

    bd = bd_ref[...]
    for p in range(RW_WIDTH // LANES):
        lanes = slice(p * LANES, (p + 1) * LANES)
        y = yl_s[:, :, lanes].reshape(nb * tb, LANES)
        mean = _dot_exact_rhs(y, bd) * (1.0 / RW_HEAD)
        yc = y - mean
        var = _dot_exact_rhs(yc * yc, bd) * (1.0 / RW_HEAD)
        yn = yc * lax.rsqrt(var + RW_GN_EPS)
        bonus = bonus_ref[:, :, lanes].reshape(nb * tb, LANES)
        gate = g_ref[:, :, lanes].reshape(nb * tb, LANES)
        out = (yn * gng_ref[:, lanes] + gnb_ref[:, lanes] + bonus) * gate
        o_ref[:, :, lanes] = out.reshape(nb, tb, LANES).astype(o_ref.dtype)


def _rwkv_call(r, k, v, lw, nkk, b, g, bonus, gng, gnb, bd, weights, bsz, seq):
    tb = RWKV_TILE
    nb = RWKV_BATCH
    nt = seq // tb
    gw = RW_GROUP * RW_HEAD
    ngroup = RW_WIDTH // gw
    nchunk = tb // CHUNK
    nsteps = (bsz // nb) * nt
    assert bsz % nb == 0
    assert all(w.shape[0] % (16 * nsteps) == 0 for w in weights)

    def tok():
        return pl.BlockSpec((nb, tb, RW_WIDTH), lambda bb, t: (bb, t, 0))

    def full(a):
        return pl.BlockSpec(a.shape, lambda bb, t: (0,) * a.ndim)

    def wblock(w):
        return pl.BlockSpec((w.shape[0] // nsteps, w.shape[1]), lambda bb, t: (bb * nt + t, 0))

    per_seq = lambda a: a.reshape(bsz, seq, RW_WIDTH)
    out, *cast = pl.pallas_call(
        functools.partial(_rwkv_kernel, nb=nb, tb=tb, unroll=RWKV_UNROLL, ncast=len(weights)),
        grid=(bsz // nb, nt),
        in_specs=([tok() for _ in range(8)] + [full(gng), full(gnb), full(bd)]
                  + [wblock(w) for w in weights]),
        out_specs=[tok()] + [wblock(w) for w in weights],
        out_shape=([jax.ShapeDtypeStruct((bsz, seq, RW_WIDTH), BF16)]
                   + [jax.ShapeDtypeStruct(w.shape, BF16) for w in weights]),
        scratch_shapes=[pltpu.VMEM((nb, ngroup, RW_HEAD, gw), F32),
                        pltpu.VMEM((nb, nchunk, ngroup, CHUNK + RW_HEAD, gw), F32),
                        pltpu.VMEM((nb, tb, RW_WIDTH), F32),
                        pltpu.VMEM((nb, nchunk, ngroup, RW_HEAD, gw), F32)],
        compiler_params=pltpu.CompilerParams(
            dimension_semantics=("arbitrary", "arbitrary"),
            vmem_limit_bytes=VMEM_LIMIT),
        name="rwkv",
    )(*[per_seq(a) for a in (r, k, v, lw, nkk, b, g, bonus)], gng, gnb, bd, *weights)
    return out.reshape(bsz * seq, RW_WIDTH), cast


def _gla_kernel(q_ref, k_ref, v_ref, sg_ref, la_ref, ng_ref, o_ref, st_ref, qs_s, oi_s, kv_s, eb_s,
                *, tb, unroll):
    c = CHUNK
    nchunk = tb // c
    t = pl.program_id(1)

    @pl.when(t == 0)
    def _():
        st_ref[...] = jnp.zeros_like(st_ref)

    lane = lax.broadcasted_iota(jnp.int32, (1, LANES), 1)
    hmask = ((lane < GLA_DK).astype(F32), (lane >= GLA_DK).astype(F32))
    ri = lax.broadcasted_iota(jnp.int32, (c, c), 0)
    ci_ = lax.broadcasted_iota(jnp.int32, (c, c), 1)
    causal = ci_ <= ri
    ltri = causal.astype(BF16)

    def local_group(gi, carry):
        units = [(u, h) for u in range(unroll) for h in range(GLA_HEADS)]
        idxs = [gi * unroll + u for u in range(unroll)]
        rows = [pl.ds(pl.multiple_of(i * c, c), c) for i in idxs]

        def qk_lanes(h):
            return slice((h // 2) * LANES, (h // 2 + 1) * LANES)

        la = [la_ref[rows[u], qk_lanes(h)] for u, h in units]
        bcum = [_dot_exact_lhs(ltri, x) for x in la]
        q_s, k_s, kdec, v = [], [], [], []
        for s, (u, h) in enumerate(units):
            bc = bcum[s]
            blast = bc[c - 1:c, :]
            q = q_ref[rows[u], qk_lanes(h)]
            k = k_ref[rows[u], qk_lanes(h)]
            hm = hmask[h % 2]
            q_s.append((q * jnp.exp(bc) * hm).astype(BF16))
            k_s.append(k * jnp.exp(-bc))
            kdec.append(k * jnp.exp(blast - bc) * hm)
            v.append(v_ref[rows[u], h * LANES:(h + 1) * LANES])
            eb_s[idxs[u], h] = jnp.exp(blast)
        att = [jnp.where(causal, _dot_nt(q_s[s], k_s[s]), 0.0) for s in range(len(units))]
        oi = [_dot(att[s], v[s]) for s in range(len(units))]
        v_t = [x.astype(F32).T.astype(BF16) for x in v]
        kv = [_dot(v_t[s], kdec[s]) for s in range(len(units))]
        for s, (u, h) in enumerate(units):
            qs_s[idxs[u], h] = q_s[s]
            oi_s[rows[u], h * LANES:(h + 1) * LANES] = oi[s]
            kv_s[idxs[u], h] = kv[s]
        return carry

    lax.fori_loop(0, nchunk // unroll, local_group, 0)

    st = [st_ref[h] for h in range(GLA_HEADS)]
    for ci in range(nchunk):
        rows = slice(ci * c, (ci + 1) * c)
        inter = [_dot_nt(qs_s[ci, h], st[h]) for h in range(GLA_HEADS)]
        for h in range(GLA_HEADS):
            lanes = slice(h * LANES, (h + 1) * LANES)
            oi_s[rows, lanes] = oi_s[rows, lanes] + inter[h]
            st[h] = st[h] * eb_s[ci, h] + kv_s[ci, h]
    for h in range(GLA_HEADS):
        st_ref[h] = st[h]

    ng = ng_ref[...]
    for h in range(GLA_HEADS):
        lanes = slice(h * LANES, (h + 1) * LANES)
        o = oi_s[:, lanes]
        o = o * lax.rsqrt(jnp.mean(o * o, axis=-1, keepdims=True) + GLA_NORM_EPS) * ng
        o_ref[:, lanes] = (o * sg_ref[:, lanes]).astype(o_ref.dtype)


def _gla_call(q, k, v, sg, la, ng, bsz, seq):
    tb = GLA_TILE
    nt = seq // tb
    n_tok = bsz * seq
    nchunk = tb // CHUNK

    def tok(n):
        return pl.BlockSpec((tb, n), lambda bb, t: (bb * nt + t, 0))

    return pl.pallas_call(
        functools.partial(_gla_kernel, tb=tb, unroll=GLA_UNROLL),
        grid=(bsz, nt),
        in_specs=[tok(GLA_KW), tok(GLA_KW), tok(GLA_VW), tok(GLA_VW), tok(GLA_KW),
                  pl.BlockSpec((1, LANES), lambda bb, t: (0, 0))],
        out_specs=tok(GLA_VW),
        out_shape=jax.ShapeDtypeStruct((n_tok, GLA_VW), BF16),
        scratch_shapes=[pltpu.VMEM((GLA_HEADS, GLA_DV, LANES), F32),
                        pltpu.VMEM((nchunk, GLA_HEADS, CHUNK, LANES), BF16),
                        pltpu.VMEM((tb, GLA_VW), F32),
                        pltpu.VMEM((nchunk, GLA_HEADS, GLA_DV, LANES), F32),
                        pltpu.VMEM((nchunk, GLA_HEADS, 1, LANES), F32)],
        compiler_params=pltpu.CompilerParams(
            dimension_semantics=("arbitrary", "arbitrary"),
            vmem_limit_bytes=VMEM_LIMIT),
        name="gla",
    )(q, k, v, sg, la, ng)


def _tail_kernel(x_ref, orw_ref, ogla_ref, sgr_ref, sgg_ref, gate1_ref, shift2_ref, scale2_ref,
                 gate2_ref, wrb_ref, wgb_ref, wmix_ref, win_ref, wout_ref, ln1g_ref, ln1b_ref,
                 ln2g_ref, ln2b_ref, o_ref, *, d_ff):
    hrows = x_ref.shape[0] // TAIL_SPLIT
    rs = [slice(i * hrows, (i + 1) * hrows) for i in range(TAIL_SPLIT)]
    nsp = range(TAIL_SPLIT)
    b_rw = [jnp.dot(orw_ref[s, :], wrb_ref[...], preferred_element_type=F32) for s in rs]
    b_gla = [jnp.dot(ogla_ref[s, :], wgb_ref[...], preferred_element_type=F32) for s in rs]
    mix = [_dot(_sigmoid(sgr_ref[rs[i], :]) * b_rw[i] + _sigmoid(sgg_ref[rs[i], :]) * b_gla[i],
                wmix_ref[...]) for i in nsp]
    x1 = [_layer_norm(ALPHA * x_ref[rs[i], :] + gate1_ref[...] * mix[i], LN_EPS) * ln1g_ref[...]
          + ln1b_ref[...] for i in nsp]
    u = [(x1[i] * (1.0 + scale2_ref[...]) + shift2_ref[...]).astype(BF16) for i in nsp]
    ffn = [jnp.zeros_like(x1[i]) for i in nsp]
    ntile = d_ff // MXU_K
    bounds = [MXU_K * ((ntile * j) // FFN_CHUNKS) for j in range(FFN_CHUNKS)] + [d_ff]
    for lo, hi in zip(bounds[:-1], bounds[1:]):
        hg, hu = [], []
        for i in nsp:
            hg.append(jnp.dot(u[i], win_ref[:, lo:hi], preferred_element_type=F32))
            hu.append(jnp.dot(u[i], win_ref[:, d_ff + lo:d_ff + hi], preferred_element_type=F32))
        act = [hg[i] * _sigmoid(hg[i]) * hu[i] for i in nsp]
        ffn = [ffn[i] + _dot(act[i], wout_ref[lo:hi, :]) for i in nsp]
    for i in nsp:
        y = _layer_norm(ALPHA * x1[i] + gate2_ref[...] * ffn[i], LN_EPS)
        o_ref[rs[i], :] = y * ln2g_ref[...] + ln2b_ref[...]


def _tail_call(x2, o_rw, o_gla, sgr, sgg, mod3, wrb, wgb, wmix, win, wout, ln1g, ln1b, ln2g, ln2b,
               bsz, seq, d_model):
    tm = TAIL_TILE
    nt = seq // tm
    n_tok = bsz * seq
    d_ff = wout.shape[0]
    assert d_ff % MXU_K == 0

    def tok(n):
        return pl.BlockSpec((tm, n), lambda b, t: (b * nt + t, 0))

    def const(a):
        return pl.BlockSpec(a.shape, lambda b, t: (0,) * a.ndim, pipeline_mode=pl.Buffered(1))

    def modspec(idx):
        return pl.BlockSpec((None, 1, d_model), lambda b, t: (b, 0, idx))

    return pl.pallas_call(
        functools.partial(_tail_kernel, d_ff=d_ff),
        grid=(bsz, nt),
        in_specs=[tok(d_model), tok(RW_WIDTH), tok(GLA_VW), tok(d_model), tok(d_model),
                  modspec(2), modspec(3), modspec(4), modspec(5),
                  const(wrb), const(wgb), const(wmix), const(win), const(wout),
                  const(ln1g), const(ln1b), const(ln2g), const(ln2b)],
        out_specs=tok(d_model),
        out_shape=jax.ShapeDtypeStruct((n_tok, d_model), F32),
        compiler_params=pltpu.CompilerParams(dimension_semantics=("arbitrary", "arbitrary"),
                                             vmem_limit_bytes=VMEM_LIMIT),
        name="tail",
    )(x2, o_rw, o_gla, sgr, sgg, mod3, mod3, mod3, mod3, wrb, wgb, wmix, win, wout,
      ln1g, ln1b, ln2g, ln2b)


def kernel(x, c, w_ada, b_ada, w_in, mu_rw, rw_w0, rw_w2, rw_a0, rw_a2, rw_g2, rw_k_k, rw_k_a,
           rw_r_k, rw_gn_g, rw_gn_b, gla_a2, gla_a_b, gla_norm_g, w_rw_branch, w_gla_branch,
           w_mix_out, ln1_g, ln1_b, w_ffn_in, w_ffn_out, ln2_g, ln2_b):
    bsz, seq, d_model = x.shape
    assert w_ada.shape[0] == DEPTH
    assert all(seq % tile == 0 for tile in (INPROJ_TILE, RWKV_TILE, GLA_TILE, TAIL_TILE))
    n_tok = bsz * seq
    l = 0

    wp = _wprep_call(jnp.swapaxes(w_in[l], 0, 1), d_model)
    zeros_lora = jnp.zeros((DECAY_LORA, RW_WIDTH), F32)
    w2p = jnp.concatenate([rw_w2[l], zeros_lora], axis=0).astype(BF16)
    a2p = jnp.concatenate([zeros_lora, rw_a2[l]], axis=0).astype(BF16)
    ga2p = jnp.concatenate(
        [gla_a2[l], jnp.zeros((LANES - GLA_GATE_LORA, GLA_KW), F32)], axis=0).astype(BF16)
    row = lambda a: a.reshape(1, -1)
    hid = jnp.arange(LANES) // RW_HEAD
    bd = (hid[:, None] == hid[None, :]).astype(BF16)
    small = (row(mu_rw[l]), row(rw_w0[l]), w2p, row(rw_a0[l]), a2p, rw_g2[l].astype(BF16),
             row(rw_k_k[l]), row(rw_k_a[l]), row(rw_r_k[l]), ga2p, row(gla_a_b[l]), bd)

    x2 = x.reshape(n_tok, d_model)
    mod = _mod_call(c, w_ada[l], b_ada[l])
    mod3 = mod.reshape(bsz, 1, 6 * d_model)

    (r, k, v, lw, nkk, b, g, bonus, gq, gk, gv, gsg, gla, sgr, sgg) = _inproj_call(
        x2, mod3, wp, small, bsz, seq, d_model)

    o_rw, (wrb, wgb, wmix, win, wout) = _rwkv_call(
        r, k, v, lw, nkk, b, g, bonus, row(rw_gn_g[l]), row(rw_gn_b[l]), bd,
        (w_rw_branch[l], w_gla_branch[l], w_mix_out[l], w_ffn_in[l], w_ffn_out[l]), bsz, seq)
    o_gla = _gla_call(gq, gk, gv, gsg, gla, row(gla_norm_g[l]), bsz, seq)

    out = _tail_call(x2, o_rw, o_gla, sgr, sgg, mod3, wrb, wgb, wmix, win, wout,
                     row(ln1_g[l]), row(ln1_b[l]), row(ln2_g[l]), row(ln2_b[l]), bsz, seq, d_model)
    return out.reshape(bsz, seq, d_model)
```

```python
import functools

import jax
import jax.numpy as jnp
from jax import lax
from jax.experimental import pallas as pl
from jax.experimental.pallas import tpu as pltpu

F32 = jnp.float32
BF16 = jnp.bfloat16

RW_HEAD = 64
RW_HEADS = 8
RW_WIDTH = RW_HEADS * RW_HEAD
DECAY_LORA = 64
AAA_LORA = 64
GATE_LORA = 128
RW_GN_EPS = 64e-5
RW_COLS = 3 * RW_WIDTH + DECAY_LORA + AAA_LORA + GATE_LORA
GLA_HEADS = 4
GLA_DK = 64
GLA_DV = 128
GLA_KW = GLA_HEADS * GLA_DK
GLA_VW = GLA_HEADS * GLA_DV
GLA_GATE_LORA = 16
GLA_TAU = 16.0
GLA_NORM_EPS = 1e-5
GLA_MAIN = 2 * GLA_KW + 2 * GLA_VW
GLA_COLS = GLA_MAIN + GLA_GATE_LORA
LN_EPS = 1e-5
DEPTH = 1
ALPHA = (2.0 * DEPTH) ** 0.25

LANES = 128
MXU_K = 256
CHUNK = 64
MOD_TILE = 1536
WPREP_COLS = 256
INPROJ_TILE = 256
RWKV_UNROLL = 1
RW_GROUP = 2
GLA_UNROLL = 8
RWKV_TILE = 256
RWKV_BATCH = 4
GLA_TILE = 2048
TAIL_TILE = 512
TAIL_SPLIT = 2
FFN_CHUNKS = 2
VMEM_LIMIT = 58 * 1024 * 1024

COL_RW = 0
COL_GLA = RW_COLS
COL_GATE = COL_GLA + GLA_MAIN


def _sigmoid(x):
    return 1.0 / (1.0 + jnp.exp(-x))


def _softplus(x):
    return jnp.maximum(x, 0.0) + jnp.log1p(jnp.exp(-jnp.abs(x)))


def _dot(a, b):
    return jnp.dot(a.astype(BF16), b.astype(BF16), preferred_element_type=F32)


def _dot_nt(a, b):
    return lax.dot_general(a.astype(BF16), b.astype(BF16), (((1,), (1,)), ((), ())),
                           preferred_element_type=F32)


def _split_hi_lo(x):
    hi = x.astype(BF16)
    lo = (x - hi.astype(F32)).astype(BF16)
    return hi, lo


def _dot_exact_rhs(x, w01):
    hi, lo = _split_hi_lo(x)
    if 2 * x.shape[1] <= MXU_K:
        return jnp.dot(jnp.concatenate([hi, lo], axis=1), jnp.concatenate([w01, w01], axis=0),
                       preferred_element_type=F32)
    return (jnp.dot(hi, w01, preferred_element_type=F32)
            + jnp.dot(lo, w01, preferred_element_type=F32))


def _dot_exact_lhs(w01, x):
    hi, lo = _split_hi_lo(x)
    if 2 * x.shape[0] <= MXU_K:
        return jnp.dot(jnp.concatenate([w01, w01], axis=1), jnp.concatenate([hi, lo], axis=0),
                       preferred_element_type=F32)
    return (jnp.dot(w01, hi, preferred_element_type=F32)
            + jnp.dot(w01, lo, preferred_element_type=F32))


def _cumsum_rows(x):
    row = lax.broadcasted_iota(jnp.int32, x.shape, 0)
    shift = 1
    while shift < x.shape[0]:
        x = x + jnp.where(row >= shift, pltpu.roll(x, shift, 0), 0.0)
        shift *= 2
    return x


def _layer_norm(x, eps):
    mu = jnp.mean(x, axis=-1, keepdims=True)
    xc = x - mu
    var = jnp.mean(xc * xc, axis=-1, keepdims=True)
    return xc * lax.rsqrt(var + eps)


def _mod_kernel(c_ref, w_ref, b_ref, o_ref):
    c = c_ref[...]
    sc = c * _sigmoid(c)
    o_ref[...] = _dot(sc, w_ref[...]) + b_ref[...]


def _mod_call(c, w_ada, b_ada):
    bsz, d = c.shape
    n = w_ada.shape[1]
    tn = MOD_TILE
    assert n % tn == 0
    return pl.pallas_call(
        _mod_kernel,
        grid=(n // tn,),
        in_specs=[pl.BlockSpec((bsz, d), lambda j: (0, 0)),
                  pl.BlockSpec((d, tn), lambda j: (0, j)),
                  pl.BlockSpec((1, tn), lambda j: (0, j))],
        out_specs=pl.BlockSpec((bsz, tn), lambda j: (0, j)),
        out_shape=jax.ShapeDtypeStruct((bsz, n), F32),
        compiler_params=pltpu.CompilerParams(dimension_semantics=("arbitrary",),
                                             vmem_limit_bytes=VMEM_LIMIT),
        name="mod",
    )(c, w_ada, b_ada.reshape(1, n))


def _wprep_kernel(w_ref, o_ref, *, d_model):
    main = RW_COLS + GLA_MAIN
    o_ref[:main, :] = w_ref[:main, :].astype(BF16)
    o_ref[main:main + 2 * d_model, :] = w_ref[RW_COLS + GLA_COLS:, :].astype(BF16)
    o_ref[main + 2 * d_model:main + 2 * d_model + GLA_GATE_LORA, :] = (
        w_ref[main:RW_COLS + GLA_COLS, :].astype(BF16))
    o_ref[main + 2 * d_model + GLA_GATE_LORA:, :] = jnp.zeros(
        (LANES - GLA_GATE_LORA, o_ref.shape[1]), BF16)


def _wprep_call(w_in_t, d_model):
    n_in = w_in_t.shape[0]
    n_out = RW_COLS + GLA_MAIN + 2 * d_model + LANES
    cols = WPREP_COLS
    assert d_model % cols == 0
    return pl.pallas_call(
        functools.partial(_wprep_kernel, d_model=d_model),
        grid=(d_model // cols,),
        in_specs=[pl.BlockSpec((n_in, cols), lambda i: (0, i))],
        out_specs=pl.BlockSpec((n_out, cols), lambda i: (0, i)),
        out_shape=jax.ShapeDtypeStruct((n_out, d_model), BF16),
        compiler_params=pltpu.CompilerParams(dimension_semantics=("arbitrary",),
                                             vmem_limit_bytes=VMEM_LIMIT),
        name="wprep",
    )(w_in_t)


def _inproj_kernel(x_ref, shift_ref, scale_ref, wp_ref, mu_ref, w0_ref, w2_ref, a0_ref, a2_ref,
                   g2_ref, kk_ref, ka_ref, rk_ref, ga2_ref, gab_ref, bd_ref,
                   r_out, k_out, v_out, lw_out, nkk_out, b_out, g_out, bonus_out,
                   gq_out, gk_out, gv_out, gsg_out, gla_out, sgr_out, sgg_out,
                   carry_ref, *, tm, d_model):
    t = pl.program_id(1)

    @pl.when(t == 0)
    def _():
        carry_ref[...] = jnp.zeros_like(carry_ref)

    u = x_ref[...] * (1.0 + scale_ref[...]) + shift_ref[...]
    proj = _dot_nt(u, wp_ref[...])

    p = proj[:, COL_RW:COL_RW + RW_COLS]
    prev = pltpu.roll(p, 1, 0)
    row = lax.broadcasted_iota(jnp.int32, p.shape, 0)
    prev = jnp.where(row == 0, carry_ref[...], prev)
    carry_ref[...] = p[tm - 1:tm, :]
    ps = p + mu_ref[...] * (prev - p)

    r = ps[:, 0:RW_WIDTH]
    k = ps[:, RW_WIDTH:2 * RW_WIDTH]
    v = ps[:, 2 * RW_WIDTH:3 * RW_WIDTH]
    z = ps[:, 3 * RW_WIDTH:3 * RW_WIDTH + LANES]
    gd = ps[:, 3 * RW_WIDTH + LANES:RW_COLS]

    wlin = w0_ref[...] + _dot(jnp.tanh(z), w2_ref[...])
    w = -_softplus(-wlin) - 0.5
    lw_out[...] = -jnp.exp(w)
    a = _sigmoid(a0_ref[...] + _dot(z, a2_ref[...]))
    g_out[...] = _dot(_sigmoid(gd), g2_ref[...])
    kkv = k * kk_ref[...]
    kp = k * (1.0 + (a - 1.0) * ka_ref[...])
    rkr = r * kp * rk_ref[...]
    bd = bd_ref[...]
    for j in range(RW_WIDTH // LANES):
        sl = slice(j * LANES, (j + 1) * LANES)
        kkj = kkv[:, sl]
        ssq = _dot_exact_rhs(kkj * kkj, bd)
        kkn = kkj / jnp.maximum(jnp.sqrt(ssq), 1e-12)
        nkk_out[:, sl] = -kkn
        b_out[:, sl] = kkn * a[:, sl]
        bonus_out[:, sl] = _dot_exact_rhs(rkr[:, sl], bd) * v[:, sl]
    r_out[...] = r
    k_out[...] = kp
    v_out[...] = v.astype(BF16)

    pg = proj[:, COL_GLA:COL_GLA + GLA_MAIN]
    gq_out[...] = pg[:, 0:GLA_KW] * (GLA_DK ** -0.5)
    gk_out[...] = pg[:, GLA_KW:2 * GLA_KW]
    gv_out[...] = pg[:, 2 * GLA_KW:2 * GLA_KW + GLA_VW].astype(BF16)
    gg = pg[:, 2 * GLA_KW + GLA_VW:GLA_MAIN]
    gsg_out[...] = gg * _sigmoid(gg)
    adg = proj[:, COL_GATE + 2 * d_model:COL_GATE + 2 * d_model + LANES]
    la = _dot(adg, ga2_ref[...]) + gab_ref[...]
    gla_out[...] = -_softplus(-la) / GLA_TAU

    sgr_out[...] = proj[:, COL_GATE:COL_GATE + d_model]
    sgg_out[...] = proj[:, COL_GATE + d_model:COL_GATE + 2 * d_model]


def _inproj_call(x2, mod3, wp, small, bsz, seq, d_model):
    tm = INPROJ_TILE
    nt = seq // tm
    n_tok = bsz * seq
    (mu, w0, w2p, a0, a2p, g2, k_k, k_a, r_k, ga2p, gab, bd) = small

    def tok(n):
        return pl.BlockSpec((tm, n), lambda b, t: (b * nt + t, 0))

    def full(a):
        return pl.BlockSpec(a.shape, lambda b, t: (0,) * a.ndim)

    def modspec(idx):
        return pl.BlockSpec((None, 1, d_model), lambda b, t: (b, 0, idx))

    outs = [
        (RW_WIDTH, F32), (RW_WIDTH, F32), (RW_WIDTH, BF16), (RW_WIDTH, F32), (RW_WIDTH, F32),
        (RW_WIDTH, F32), (RW_WIDTH, F32), (RW_WIDTH, F32),
        (GLA_KW, F32), (GLA_KW, F32), (GLA_VW, BF16), (GLA_VW, F32), (GLA_KW, F32),
        (d_model, F32), (d_model, F32),
    ]
    return pl.pallas_call(
        functools.partial(_inproj_kernel, tm=tm, d_model=d_model),
        grid=(bsz, nt),
        in_specs=[tok(d_model), modspec(0), modspec(1), full(wp), full(mu), full(w0), full(w2p),
                  full(a0), full(a2p), full(g2), full(k_k), full(k_a), full(r_k), full(ga2p),
                  full(gab), full(bd)],
        out_specs=[tok(n) for n, _ in outs],
        out_shape=[jax.ShapeDtypeStruct((n_tok, n), dt) for n, dt in outs],
        scratch_shapes=[pltpu.VMEM((1, RW_COLS), F32)],
        compiler_params=pltpu.CompilerParams(dimension_semantics=("arbitrary", "arbitrary"),
                                             vmem_limit_bytes=VMEM_LIMIT),
        name="inproj",
    )(x2, mod3, mod3, wp, mu, w0, w2p, a0, a2p, g2, k_k, k_a, r_k, ga2p, gab, bd)


def _rwkv_kernel(*refs, nb, tb, unroll, ncast):
    (r_ref, k_ref, v_ref, lw_ref, nkk_ref, b_ref, g_ref, bonus_ref, gng_ref, gnb_ref,
     bd_ref) = refs[:11]
    cast_in = refs[11:11 + ncast]
    o_ref = refs[11 + ncast]
    cast_out = refs[12 + ncast:12 + 2 * ncast]
    h_ref, rm_s, yl_s, hl_s = refs[12 + 2 * ncast:]
    for w_in_ref, w_out_ref in zip(cast_in, cast_out):
        w_out_ref[...] = w_in_ref[...].astype(BF16)

    c = CHUNK
    gw = RW_GROUP * RW_HEAD
    nchunk = tb // c
    ngroup = RW_WIDTH // gw
    t = pl.program_id(1)

    @pl.when(t == 0)
    def _():
        h_ref[...] = jnp.zeros_like(h_ref)

    lane = lax.broadcasted_iota(jnp.int32, (c, gw), 1)
    rowi = lax.broadcasted_iota(jnp.int32, (c, gw), 0)
    head_of_lane = lane // RW_HEAD
    lmask = [head_of_lane == h for h in range(RW_GROUP)]
    strict = (lane % RW_HEAD) < rowi
    incl = (lane % RW_HEAD) <= rowi
    eye = (lane % RW_HEAD) == rowi
    eye_f = eye.astype(F32)
    assert c == RW_HEAD

    def stack(x):
        xb = x.astype(BF16)
        zero = jnp.zeros_like(xb)
        return jnp.concatenate([jnp.where(lmask[h], xb, zero) for h in range(RW_GROUP)], axis=0)

    def block_t(x):
        xt = x.T
        return jnp.concatenate([xt[h * RW_HEAD:(h + 1) * RW_HEAD, :] for h in range(RW_GROUP)],
                               axis=1)

    def local_group(gi, carry):
        units = [(e, u, g) for e in range(nb) for u in range(unroll) for g in range(ngroup)]
        n = len(units)
        idxs = [gi * unroll + u for u in range(unroll)]
        rows = [pl.ds(pl.multiple_of(i * c, c), c) for i in idxs]

        def ld(ref, e, u, g):
            return ref[e, rows[u], g * gw:(g + 1) * gw]

        lw = [ld(lw_ref, e, u, g) for e, u, g in units]
        gcum = [_cumsum_rows(x) for x in lw]
        gend, rt, at, kt, bt, bh, kh, vst = [], [], [], [], [], [], [], []
        for s, (e, u, g) in enumerate(units):
            g_ = gcum[s]
            ge = g_[c - 1:c, :]
            r = ld(r_ref, e, u, g)
            k = ld(k_ref, e, u, g)
            nkk = ld(nkk_ref, e, u, g)
            b = ld(b_ref, e, u, g)
            e_neg = jnp.exp(-g_)
            dk = jnp.exp(ge - g_)
            gend.append(ge)
            rt.append(r * jnp.exp(g_))
            at.append(nkk * jnp.exp(g_ - lw[s]))
            kt.append(k * e_neg)
            bt.append(b * e_neg)
            bh.append(b * dk)
            kh.append(k * dk)
            vst.append(stack(ld(v_ref, e, u, g)))

        x = [_dot_nt(jnp.concatenate([at[s], rt[s]], axis=0),
                     jnp.concatenate([stack(bt[s]), stack(kt[s])], axis=0)) for s in range(n)]
        a_ab = [jnp.where(strict, y[:c, :gw], 0.0) for y in x]
        a_ak = [jnp.where(strict, y[:c, gw:], 0.0) for y in x]
        a_rb = [jnp.where(incl, y[c:, :gw], 0.0) for y in x]
        a_rk = [jnp.where(incl, y[c:, gw:], 0.0) for y in x]
        bht = [block_t(y) for y in bh]
        kht = [block_t(y) for y in kh]

        tp = [eye_f + a for a in a_ab]
        pw = [_dot(a, stack(a)) for a in a_ab]
        for _ in range(4):
            res = [_dot(jnp.concatenate([pw[s], tp[s]], axis=0), stack(pw[s])) for s in range(n)]
            tp = [tp[s] + res[s][c:] for s in range(n)]
            pw = [res[s][:c] for s in range(n)]
        tinv = [tp[s] + _dot(tp[s], stack(pw[s])) for s in range(n)]

        vres = [_dot(jnp.concatenate([a_ak[s], a_rk[s], kht[s]], axis=0), vst[s]) for s in range(n)]
        akv = [y[:c] for y in vres]
        rk_v = [y[c:2 * c] for y in vres]
        h2 = [y[2 * c:] for y in vres]
        tres = [_dot(tinv[s], jnp.concatenate([stack(at[s]), stack(akv[s])], axis=1))
                for s in range(n)]
        ap = [y[:, :gw] for y in tres]
        uloc = [y[:, gw:] for y in tres]
        fres = [_dot(jnp.concatenate([a_rb[s], bht[s]], axis=0),
                     jnp.concatenate([stack(ap[s]), stack(uloc[s])], axis=1)) for s in range(n)]
        for s, (e, u, g) in enumerate(units):
            y = fres[s]
            rm_s[e, idxs[u], g, :c, :] = rt[s] + y[:c, :gw]
            rm_s[e, idxs[u], g, c:, :] = jnp.where(eye, jnp.exp(gend[s]), 0.0) + y[c:, :gw]
            yl_s[e, rows[u], g * gw:(g + 1) * gw] = y[:c, gw:] + rk_v[s]
            hl_s[e, idxs[u], g] = y[c:, gw:] + h2[s]
        return carry

    lax.fori_loop(0, nchunk // unroll, local_group, 0)

    chains = [(e, g) for e in range(nb) for g in range(ngroup)]
    hstate = [h_ref[e, g] for e, g in chains]
    for ci in range(nchunk):
        rows = slice(ci * c, (ci + 1) * c)
        res = [_dot(rm_s[e, ci, g], stack(hstate[j])) for j, (e, g) in enumerate(chains)]
        for j, (e, g) in enumerate(chains):
            lanes = slice(g * gw, (g + 1) * gw)
            yl_s[e, rows, lanes] = res[j][:c] + yl_s[e, rows, lanes]
            hstate[j] = res[j][c:] + hl_s[e, ci, g]
    for j, (e, g) in enumerate(chains):
        h_ref[e, g] = hstate[j]

    bd = bd_ref[...]
    for p in range(RW_WIDTH // LANES):
        lanes = slice(p * LANES, (p + 1) * LANES)
        y = yl_s[:, :, lanes].reshape(nb * tb, LANES)
        mean = _dot_exact_rhs(y, bd) * (1.0 / RW_HEAD)
        yc = y - mean
        var = _dot_exact_rhs(yc * yc, bd) * (1.0 / RW_HEAD)
        yn = yc * lax.rsqrt(var + RW_GN_EPS)
        bonus = bonus_ref[:, :, lanes].reshape(nb * tb, LANES)
        gate = g_ref[:, :, lanes].reshape(nb * tb, LANES)
        out = (yn * gng_ref[:, lanes] + gnb_ref[:, lanes] + bonus) * gate
        o_ref[:, :, lanes] = out.reshape(nb, tb, LANES).astype(o_ref.dtype)


def _rwkv_call(r, k, v, lw, nkk, b, g, bonus, gng, gnb, bd, weights, bsz, seq):
    tb = RWKV_TILE
    nb = RWKV_BATCH
    nt = seq // tb
    gw = RW_GROUP * RW_HEAD
    ngroup = RW_WIDTH // gw
    nchunk = tb // CHUNK
    nsteps = (bsz // nb) * nt
    assert bsz % nb == 0
    assert all(w.shape[0] % (16 * nsteps) == 0 for w in weights)

    def tok():
        return pl.BlockSpec((nb, tb, RW_WIDTH), lambda bb, t: (bb, t, 0))

    def full(a):
        return pl.BlockSpec(a.shape, lambda bb, t: (0,) * a.ndim)

    def wblock(w):
        return pl.BlockSpec((w.shape[0] // nsteps, w.shape[1]), lambda bb, t: (bb * nt + t, 0))

    per_seq = lambda a: a.reshape(bsz, seq, RW_WIDTH)
    out, *cast = pl.pallas_call(
        functools.partial(_rwkv_kernel, nb=nb, tb=tb, unroll=RWKV_UNROLL, ncast=len(weights)),
        grid=(bsz // nb, nt),
        in_specs=([tok() for _ in range(8)] + [full(gng), full(gnb), full(bd)]
                  + [wblock(w) for w in weights]),
        out_specs=[tok()] + [wblock(w) for w in weights],
        out_shape=([jax.ShapeDtypeStruct((bsz, seq, RW_WIDTH), BF16)]
                   + [jax.ShapeDtypeStruct(w.shape, BF16) for w in weights]),
        scratch_shapes=[pltpu.VMEM((nb, ngroup, RW_HEAD, gw), F32),
                        pltpu.VMEM((nb, nchunk, ngroup, CHUNK + RW_HEAD, gw), F32),
                        pltpu.VMEM((nb, tb, RW_WIDTH), F32),
                        pltpu.VMEM((nb, nchunk, ngroup, RW_HEAD, gw), F32)],
        compiler_params=pltpu.CompilerParams(
            dimension_semantics=("arbitrary", "arbitrary"),
            vmem_limit_bytes=VMEM_LIMIT),
        name="rwkv",
    )(*[per_seq(a) for a in (r, k, v, lw, nkk, b, g, bonus)], gng, gnb, bd, *weights)
    return out.reshape(bsz * seq, RW_WIDTH), cast


def _gla_kernel(q_ref, k_ref, v_ref, sg_ref, la_ref, ng_ref, o_ref, st_ref, qs_s, oi_s, kv_s, eb_s,
                *, tb, unroll):
    c = CHUNK
    nchunk = tb // c
    t = pl.program_id(1)

    @pl.when(t == 0)
    def _():
        st_ref[...] = jnp.zeros_like(st_ref)

    lane = lax.broadcasted_iota(jnp.int32, (1, LANES), 1)
    hmask = ((lane < GLA_DK).astype(F32), (lane >= GLA_DK).astype(F32))
    ri = lax.broadcasted_iota(jnp.int32, (c, c), 0)
    ci_ = lax.broadcasted_iota(jnp.int32, (c, c), 1)
    causal = ci_ <= ri
    ltri = causal.astype(BF16)

    def local_group(gi, carry):
        units = [(u, h) for u in range(unroll) for h in range(GLA_HEADS)]
        idxs = [gi * unroll + u for u in range(unroll)]
        rows = [pl.ds(pl.multiple_of(i * c, c), c) for i in idxs]

        def qk_lanes(h):
            return slice((h // 2) * LANES, (h // 2 + 1) * LANES)

        la = [la_ref[rows[u], qk_lanes(h)] for u, h in units]
        bcum = [_dot_exact_lhs(ltri, x) for x in la]
        q_s, k_s, kdec, v = [], [], [], []
        for s, (u, h) in enumerate(units):
            bc = bcum[s]
            blast = bc[c - 1:c, :]
            q = q_ref[rows[u], qk_lanes(h)]
            k = k_ref[rows[u], qk_lanes(h)]
            hm = hmask[h % 2]
            q_s.append((q * jnp.exp(bc) * hm).astype(BF16))
            k_s.append(k * jnp.exp(-bc))
            kdec.append(k * jnp.exp(blast - bc) * hm)
            v.append(v_ref[rows[u], h * LANES:(h + 1) * LANES])
            eb_s[idxs[u], h] = jnp.exp(blast)
        att = [jnp.where(causal, _dot_nt(q_s[s], k_s[s]), 0.0) for s in range(len(units))]
        oi = [_dot(att[s], v[s]) for s in range(len(units))]
        v_t = [x.astype(F32).T.astype(BF16) for x in v]
        kv = [_dot(v_t[s], kdec[s]) for s in range(len(units))]
        for s, (u, h) in enumerate(units):
            qs_s[idxs[u], h] = q_s[s]
            oi_s[rows[u], h * LANES:(h + 1) * LANES] = oi[s]
            kv_s[idxs[u], h] = kv[s]
        return carry

    lax.fori_loop(0, nchunk // unroll, local_group, 0)

    st = [st_ref[h] for h in range(GLA_HEADS)]
    for ci in range(nchunk):
        rows = slice(ci * c, (ci + 1) * c)
        inter = [_dot_nt(qs_s[ci, h], st[h]) for h in range(GLA_HEADS)]
        for h in range(GLA_HEADS):
            lanes = slice(h * LANES, (h + 1) * LANES)
            oi_s[rows, lanes] = oi_s[rows, lanes] + inter[h]
            st[h] = st[h] * eb_s[ci, h] + kv_s[ci, h]
    for h in range(GLA_HEADS):
        st_ref[h] = st[h]

    ng = ng_ref[...]
    for h in range(GLA_HEADS):
        lanes = slice(h * LANES, (h + 1) * LANES)
        o = oi_s[:, lanes]
        o = o * lax.rsqrt(jnp.mean(o * o, axis=-1, keepdims=True) + GLA_NORM_EPS) * ng
        o_ref[:, lanes] = (o * sg_ref[:, lanes]).astype(o_ref.dtype)


def _gla_call(q, k, v, sg, la, ng, bsz, seq):
    tb = GLA_TILE
    nt = seq // tb
    n_tok = bsz * seq
    nchunk = tb // CHUNK

    def tok(n):
        return pl.BlockSpec((tb, n), lambda bb, t: (bb * nt + t, 0))

    return pl.pallas_call(
        functools.partial(_gla_kernel, tb=tb, unroll=GLA_UNROLL),
        grid=(bsz, nt),
        in_specs=[tok(GLA_KW), tok(GLA_KW), tok(GLA_VW), tok(GLA_VW), tok(GLA_KW),
                  pl.BlockSpec((1, LANES), lambda bb, t: (0, 0))],
        out_specs=tok(GLA_VW),
        out_shape=jax.ShapeDtypeStruct((n_tok, GLA_VW), BF16),
        scratch_shapes=[pltpu.VMEM((GLA_HEADS, GLA_DV, LANES), F32),
                        pltpu.VMEM((nchunk, GLA_HEADS, CHUNK, LANES), BF16),
                        pltpu.VMEM((tb, GLA_VW), F32),
                        pltpu.VMEM((nchunk, GLA_HEADS, GLA_DV, LANES), F32),
                        pltpu.VMEM((nchunk, GLA_HEADS, 1, LANES), F32)],
        compiler_params=pltpu.CompilerParams(
            dimension_semantics=("arbitrary", "arbitrary"),
            vmem_limit_bytes=VMEM_LIMIT),
        name="gla",
    )(q, k, v, sg, la, ng)


def _tail_kernel(x_ref, orw_ref, ogla_ref, sgr_ref, sgg_ref, gate1_ref, shift2_ref, scale2_ref,
                 gate2_ref, wrb_ref, wgb_ref, wmix_ref, win_ref, wout_ref, ln1g_ref, ln1b_ref,
                 ln2g_ref, ln2b_ref, o_ref, *, d_ff):
    hrows = x_ref.shape[0] // TAIL_SPLIT
    rs = [slice(i * hrows, (i + 1) * hrows) for i in range(TAIL_SPLIT)]
    nsp = range(TAIL_SPLIT)
    b_rw = [jnp.dot(orw_ref[s, :], wrb_ref[...], preferred_element_type=F32) for s in rs]
    b_gla = [jnp.dot(ogla_ref[s, :], wgb_ref[...], preferred_element_type=F32) for s in rs]
    mix = [_dot(_sigmoid(sgr_ref[rs[i], :]) * b_rw[i] + _sigmoid(sgg_ref[rs[i], :]) * b_gla[i],
                wmix_ref[...]) for i in nsp]
    x1 = [_layer_norm(ALPHA * x_ref[rs[i], :] + gate1_ref[...] * mix[i], LN_EPS) * ln1g_ref[...]
          + ln1b_ref[...] for i in nsp]
    u = [(x1[i] * (1.0 + scale2_ref[...]) + shift2_ref[...]).astype(BF16) for i in nsp]
    ffn = [jnp.zeros_like(x1[i]) for i in nsp]
    ntile = d_ff // MXU_K
    bounds = [MXU_K * ((ntile * j) // FFN_CHUNKS) for j in range(FFN_CHUNKS)] + [d_ff]
    for lo, hi in zip(bounds[:-1], bounds[1:]):
        hg, hu = [], []
        for i in nsp:
            hg.append(jnp.dot(u[i], win_ref[:, lo:hi], preferred_element_type=F32))
            hu.append(jnp.dot(u[i], win_ref[:, d_ff + lo:d_ff + hi], preferred_element_type=F32))
        act = [hg[i] * _sigmoid(hg[i]) * hu[i] for i in nsp]
        ffn = [ffn[i] + _dot(act[i], wout_ref[lo:hi, :]) for i in nsp]
    for i in nsp:
        y = _layer_norm(ALPHA * x1[i] + gate2_ref[...] * ffn[i], LN_EPS)
        o_ref[rs[i], :] = y * ln2g_ref[...] + ln2b_ref[...]


def _tail_call(x2, o_rw, o_gla, sgr, sgg, mod3, wrb, wgb, wmix, win, wout, ln1g, ln1b, ln2g, ln2b,
               bsz, seq, d_model):
    tm = TAIL_TILE
    nt = seq // tm
    n_tok = bsz * seq
    d_ff = wout.shape[0]
    assert d_ff % MXU_K == 0

    def tok(n):
        return pl.BlockSpec((tm, n), lambda b, t: (b * nt + t, 0))

    def const(a):
        return pl.BlockSpec(a.shape, lambda b, t: (0,) * a.ndim, pipeline_mode=pl.Buffered(1))

    def modspec(idx):
        return pl.BlockSpec((None, 1, d_model), lambda b, t: (b, 0, idx))

    return pl.pallas_call(
        functools.partial(_tail_kernel, d_ff=d_ff),
        grid=(bsz, nt),
        in_specs=[tok(d_model), tok(RW_WIDTH), tok(GLA_VW), tok(d_model), tok(d_model),
                  modspec(2), modspec(3), modspec(4), modspec(5),
                  const(wrb), const(wgb), const(wmix), const(win), const(wout),
                  const(ln1g), const(ln1b), const(ln2g), const(ln2b)],
        out_specs=tok(d_model),
        out_shape=jax.ShapeDtypeStruct((n_tok, d_model), F32),
        compiler_params=pltpu.CompilerParams(dimension_semantics=("arbitrary", "arbitrary"),
                                             vmem_limit_bytes=VMEM_LIMIT),
        name="tail",
    )(x2, o_rw, o_gla, sgr, sgg, mod3, mod3, mod3, mod3, wrb, wgb, wmix, win, wout,
      ln1g, ln1b, ln2g, ln2b)


def kernel(x, c, w_ada, b_ada, w_in, mu_rw, rw_w0, rw_w2, rw_a0, rw_a2, rw_g2, rw_k_k, rw_k_a,
           rw_r_k, rw_gn_g, rw_gn_b, gla_a2, gla_a_b, gla_norm_g, w_rw_branch, w_gla_branch,
           w_mix_out, ln1_g, ln1_b, w_ffn_in, w_ffn_out, ln2_g, ln2_b):
    bsz, seq, d_model = x.shape
    assert w_ada.shape[0] == DEPTH
    assert all(seq % tile == 0 for tile in (INPROJ_TILE, RWKV_TILE, GLA_TILE, TAIL_TILE))
    n_tok = bsz * seq
    l = 0

    wp = _wprep_call(jnp.swapaxes(w_in[l], 0, 1), d_model)
    zeros_lora = jnp.zeros((DECAY_LORA, RW_WIDTH), F32)
    w2p = jnp.concatenate([rw_w2[l], zeros_lora], axis=0).astype(BF16)
    a2p = jnp.concatenate([zeros_lora, rw_a2[l]], axis=0).astype(BF16)
    ga2p = jnp.concatenate(
        [gla_a2[l], jnp.zeros((LANES - GLA_GATE_LORA, GLA_KW), F32)], axis=0).astype(BF16)
    row = lambda a: a.reshape(1, -1)
    hid = jnp.arange(LANES) // RW_HEAD
    bd = (hid[:, None] == hid[None, :]).astype(BF16)
    small = (row(mu_rw[l]), row(rw_w0[l]), w2p, row(rw_a0[l]), a2p, rw_g2[l].astype(BF16),
             row(rw_k_k[l]), row(rw_k_a[l]), row(rw_r_k[l]), ga2p, row(gla_a_b[l]), bd)

    x2 = x.reshape(n_tok, d_model)
    mod = _mod_call(c, w_ada[l], b_ada[l])
    mod3 = mod.reshape(bsz, 1, 6 * d_model)

    (r, k, v, lw, nkk, b, g, bonus, gq, gk, gv, gsg, gla, sgr, sgg) = _inproj_call(
        x2, mod3, wp, small, bsz, seq, d_model)

    o_rw, (wrb, wgb, wmix, win, wout) = _rwkv_call(
        r, k, v, lw, nkk, b, g, bonus, row(rw_gn_g[l]), row(rw_gn_b[l]), bd,
        (w_rw_branch[l], w_gla_branch[l], w_mix_out[l], w_ffn_in[l], w_ffn_out[l]), bsz, seq)
    o_gla = _gla_call(gq, gk, gv, gsg, gla, row(gla_norm_g[l]), bsz, seq)

    out = _tail_call(x2, o_rw, o_gla, sgr, sgg, mod3, wrb, wgb, wmix, win, wout,
                     row(ln1_g[l]), row(ln1_b[l]), row(ln2_g[l]), row(ln2_b[l]), bsz, seq, d_model)
    return out.reshape(bsz, seq, d_model)
```

```python
import functools

import jax
import jax.numpy as jnp
from jax import lax
from jax.experimental import pallas as pl
from jax.experimental.pallas import tpu as pltpu

F32 = jnp.float32
BF16 = jnp.bfloat16

RW_HEAD = 64
RW_HEADS = 8
RW_WIDTH = RW_HEADS * RW_HEAD
DECAY_LORA = 64
AAA_LORA = 64
GATE_LORA = 128
RW_GN_EPS = 64e-5
RW_COLS = 3 * RW_WIDTH + DECAY_LORA + AAA_LORA + GATE_LORA
GLA_HEADS = 4
GLA_DK = 64
GLA_DV = 128
GLA_KW = GLA_HEADS * GLA_DK
GLA_VW = GLA_HEADS * GLA_DV
GLA_GATE_LORA = 16
GLA_TAU = 16.0
GLA_NORM_EPS = 1e-5
GLA_MAIN = 2 * GLA_KW + 2 * GLA_VW
GLA_COLS = GLA_MAIN + GLA_GATE_LORA
LN_EPS = 1e-5
DEPTH = 1
ALPHA = (2.0 * DEPTH) ** 0.25

LANES = 128
MXU_K = 256
CHUNK = 64
MOD_TILE = 1536
WPREP_COLS = 256
INPROJ_TILE = 512
RWKV_UNROLL = 1
RW_GROUP = 2
GLA_UNROLL = 8
RWKV_TILE = 256
RWKV_BATCH = 4
GLA_TILE = 2048
TAIL_TILE = 512
TAIL_SPLIT = 2
FFN_CHUNKS = 2
VMEM_LIMIT = 58 * 1024 * 1024

COL_RW = 0
COL_GLA = RW_COLS
COL_GATE = COL_GLA + GLA_MAIN


def _sigmoid(x):
    return 1.0 / (1.0 + jnp.exp(-x))


def _softplus(x):
    return jnp.maximum(x, 0.0) + jnp.log1p(jnp.exp(-jnp.abs(x)))


def _dot(a, b):
    return jnp.dot(a.astype(BF16), b.astype(BF16), preferred_element_type=F32)


def _dot_nt(a, b):
    return lax.dot_general(a.astype(BF16), b.astype(BF16), (((1,), (1,)), ((), ())),
                           preferred_element_type=F32)


def _split_hi_lo(x):
    hi = x.astype(BF16)
    lo = (x - hi.astype(F32)).astype(BF16)
    return hi, lo


def _dot_exact_rhs(x, w01):
    hi, lo = _split_hi_lo(x)
    if 2 * x.shape[1] <= MXU_K:
        return jnp.dot(jnp.concatenate([hi, lo], axis=1), jnp.concatenate([w01, w01], axis=0),
                       preferred_element_type=F32)
    return (jnp.dot(hi, w01, preferred_element_type=F32)
            + jnp.dot(lo, w01, preferred_element_type=F32))


def _dot_exact_lhs(w01, x):
    hi, lo = _split_hi_lo(x)
    if 2 * x.shape[0] <= MXU_K:
        return jnp.dot(jnp.concatenate([w01, w01], axis=1), jnp.concatenate([hi, lo], axis=0),
                       preferred_element_type=F32)
    return (jnp.dot(w01, hi, preferred_element_type=F32)
            + jnp.dot(w01, lo, preferred_element_type=F32))


def _cumsum_rows(x):
    row = lax.broadcasted_iota(jnp.int32, x.shape, 0)
    shift = 1
    while shift < x.shape[0]:
        x = x + jnp.where(row >= shift, pltpu.roll(x, shift, 0), 0.0)
        shift *= 2
    return x


def _layer_norm(x, eps):
    mu = jnp.mean(x, axis=-1, keepdims=True)
    xc = x - mu
    var = jnp.mean(xc * xc, axis=-1, keepdims=True)
    return xc * lax.rsqrt(var + eps)


def _mod_kernel(c_ref, w_ref, b_ref, o_ref):
    c = c_ref[...]
    sc = c * _sigmoid(c)
    o_ref[...] = _dot(sc, w_ref[...]) + b_ref[...]


def _mod_call(c, w_ada, b_ada):
    bsz, d = c.shape
    n = w_ada.shape[1]
    tn = MOD_TILE
    assert n % tn == 0
    return pl.pallas_call(
        _mod_kernel,
        grid=(n // tn,),
        in_specs=[pl.BlockSpec((bsz, d), lambda j: (0, 0)),
                  pl.BlockSpec((d, tn), lambda j: (0, j)),
                  pl.BlockSpec((1, tn), lambda j: (0, j))],
        out_specs=pl.BlockSpec((bsz, tn), lambda j: (0, j)),
        out_shape=jax.ShapeDtypeStruct((bsz, n), F32),
        compiler_params=pltpu.CompilerParams(dimension_semantics=("arbitrary",),
                                             vmem_limit_bytes=VMEM_LIMIT),
        name="mod",
    )(c, w_ada, b_ada.reshape(1, n))


def _wprep_kernel(w_ref, o_ref, *, d_model):
    main = RW_COLS + GLA_MAIN
    o_ref[:main, :] = w_ref[:main, :].astype(BF16)
    o_ref[main:main + 2 * d_model, :] = w_ref[RW_COLS + GLA_COLS:, :].astype(BF16)
    o_ref[main + 2 * d_model:main + 2 * d_model + GLA_GATE_LORA, :] = (
        w_ref[main:RW_COLS + GLA_COLS, :].astype(BF16))
    o_ref[main + 2 * d_model + GLA_GATE_LORA:, :] = jnp.zeros(
        (LANES - GLA_GATE_LORA, o_ref.shape[1]), BF16)


def _wprep_call(w_in_t, d_model):
    n_in = w_in_t.shape[0]
    n_out = RW_COLS + GLA_MAIN + 2 * d_model + LANES
    cols = WPREP_COLS
    assert d_model % cols == 0
    return pl.pallas_call(
        functools.partial(_wprep_kernel, d_model=d_model),
        grid=(d_model // cols,),
        in_specs=[pl.BlockSpec((n_in, cols), lambda i: (0, i))],
        out_specs=pl.BlockSpec((n_out, cols), lambda i: (0, i)),
        out_shape=jax.ShapeDtypeStruct((n_out, d_model), BF16),
        compiler_params=pltpu.CompilerParams(dimension_semantics=("arbitrary",),
                                             vmem_limit_bytes=VMEM_LIMIT),
        name="wprep",
    )(w_in_t)


def _inproj_kernel(x_ref, shift_ref, scale_ref, wp_ref, mu_ref, w0_ref, w2_ref, a0_ref, a2_ref,
                   g2_ref, kk_ref, ka_ref, rk_ref, ga2_ref, gab_ref, bd_ref,
                   r_out, k_out, v_out, lw_out, nkk_out, b_out, g_out, bonus_out,
                   gq_out, gk_out, gv_out, gsg_out, gla_out, sgr_out, sgg_out,
                   carry_ref, *, tm, d_model):
    t = pl.program_id(1)

    @pl.when(t == 0)
    def _():
        carry_ref[...] = jnp.zeros_like(carry_ref)

    u = (x_ref[...] * (1.0 + scale_ref[...]) + shift_ref[...]).astype(BF16)
    p = _dot_nt(u, wp_ref[COL_RW:COL_RW + RW_COLS, :])

    prev = pltpu.roll(p, 1, 0)
    row = lax.broadcasted_iota(jnp.int32, p.shape, 0)
    prev = jnp.where(row == 0, carry_ref[...], prev)
    carry_ref[...] = p[tm - 1:tm, :]
    ps = p + mu_ref[...] * (prev - p)

    r = ps[:, 0:RW_WIDTH]
    k = ps[:, RW_WIDTH:2 * RW_WIDTH]
    v = ps[:, 2 * RW_WIDTH:3 * RW_WIDTH]
    z = ps[:, 3 * RW_WIDTH:3 * RW_WIDTH + LANES]
    gd = ps[:, 3 * RW_WIDTH + LANES:RW_COLS]

    wlin = w0_ref[...] + _dot(jnp.tanh(z), w2_ref[...])
    w = -_softplus(-wlin) - 0.5
    lw_out[...] = -jnp.exp(w)
    a = _sigmoid(a0_ref[...] + _dot(z, a2_ref[...]))
    g_out[...] = _dot(_sigmoid(gd), g2_ref[...])
    kkv = k * kk_ref[...]
    kp = k * (1.0 + (a - 1.0) * ka_ref[...])
    rkr = r * kp * rk_ref[...]
    bd = bd_ref[...]
    for j in range(RW_WIDTH // LANES):
        sl = slice(j * LANES, (j + 1) * LANES)
        kkj = kkv[:, sl]
        ssq = _dot_exact_rhs(kkj * kkj, bd)
        kkn = kkj / jnp.maximum(jnp.sqrt(ssq), 1e-12)
        nkk_out[:, sl] = -kkn
        b_out[:, sl] = kkn * a[:, sl]
        bonus_out[:, sl] = _dot_exact_rhs(rkr[:, sl], bd) * v[:, sl]
    r_out[...] = r
    k_out[...] = kp
    v_out[...] = v.astype(BF16)

    pg = _dot_nt(u, wp_ref[COL_GLA:COL_GLA + GLA_MAIN, :])
    gq_out[...] = pg[:, 0:GLA_KW] * (GLA_DK ** -0.5)
    gk_out[...] = pg[:, GLA_KW:2 * GLA_KW]
    gv_out[...] = pg[:, 2 * GLA_KW:2 * GLA_KW + GLA_VW].astype(BF16)
    gg = pg[:, 2 * GLA_KW + GLA_VW:GLA_MAIN]
    gsg_out[...] = gg * _sigmoid(gg)
    gates = _dot_nt(u, wp_ref[COL_GATE:, :])
    adg = gates[:, 2 * d_model:2 * d_model + LANES]
    la = _dot(adg, ga2_ref[...]) + gab_ref[...]
    gla_out[...] = -_softplus(-la) / GLA_TAU

    sgr_out[...] = gates[:, :d_model]
    sgg_out[...] = gates[:, d_model:2 * d_model]


def _inproj_call(x2, mod3, wp, small, bsz, seq, d_model):
    tm = INPROJ_TILE
    nt = seq // tm
    n_tok = bsz * seq
    (mu, w0, w2p, a0, a2p, g2, k_k, k_a, r_k, ga2p, gab, bd) = small

    def tok(n):
        return pl.BlockSpec((tm, n), lambda b, t: (b * nt + t, 0))

    def full(a):
        return pl.BlockSpec(a.shape, lambda b, t: (0,) * a.ndim, pipeline_mode=pl.Buffered(1))

    def modspec(idx):
        return pl.BlockSpec((None, 1, d_model), lambda b, t: (b, 0, idx))

    outs = [
        (RW_WIDTH, F32), (RW_WIDTH, F32), (RW_WIDTH, BF16), (RW_WIDTH, F32), (RW_WIDTH, F32),
        (RW_WIDTH, F32), (RW_WIDTH, F32), (RW_WIDTH, F32),
        (GLA_KW, F32), (GLA_KW, F32), (GLA_VW, BF16), (GLA_VW, F32), (GLA_KW, F32),
        (d_model, F32), (d_model, F32),
    ]
    return pl.pallas_call(
        functools.partial(_inproj_kernel, tm=tm, d_model=d_model),
        grid=(bsz, nt),
        in_specs=[tok(d_model), modspec(0), modspec(1), full(wp), full(mu), full(w0), full(w2p),
                  full(a0), full(a2p), full(g2), full(k_k), full(k_a), full(r_k), full(ga2p),
                  full(gab), full(bd)],
        out_specs=[tok(n) for n, _ in outs],
        out_shape=[jax.ShapeDtypeStruct((n_tok, n), dt) for n, dt in outs],
        scratch_shapes=[pltpu.VMEM((1, RW_COLS), F32)],
        compiler_params=pltpu.CompilerParams(dimension_semantics=("arbitrary", "arbitrary"),
                                             vmem_limit_bytes=VMEM_LIMIT),
        name="inproj",
    )(x2, mod3, mod3, wp, mu, w0, w2p, a0, a2p, g2, k_k, k_a, r_k, ga2p, gab, bd)


def _rwkv_kernel(*refs, nb, tb, unroll, ncast):
    (r_ref, k_ref, v_ref, lw_ref, nkk_ref, b_ref, g_ref, bonus_ref, gng_ref, gnb_ref,
     bd_ref) = refs[:11]
    cast_in = refs[11:11 + ncast]
    o_ref = refs[11 + ncast]
    cast_out = refs[12 + ncast:12 + 2 * ncast]
    h_ref, rm_s, yl_s, hl_s = refs[12 + 2 * ncast:]
    for w_in_ref, w_out_ref in zip(cast_in, cast_out):
        w_out_ref[...] = w_in_ref[...].astype(BF16)

    c = CHUNK
    gw = RW_GROUP * RW_HEAD
    nchunk = tb // c
    ngroup = RW_WIDTH // gw
    t = pl.program_id(1)

    @pl.when(t == 0)
    def _():
        h_ref[...] = jnp.zeros_like(h_ref)

    lane = lax.broadcasted_iota(jnp.int32, (c, gw), 1)
    rowi = lax.broadcasted_iota(jnp.int32, (c, gw), 0)
    head_of_lane = lane // RW_HEAD
    lmask = [head_of_lane == h for h in range(RW_GROUP)]
    strict = (lane % RW_HEAD) < rowi
    incl = (lane % RW_HEAD) <= rowi
    eye = (lane % RW_HEAD) == rowi
    eye_f = eye.astype(F32)
    assert c == RW_HEAD

    def stack(x):
        xb = x.astype(BF16)
        zero = jnp.zeros_like(xb)
        return jnp.concatenate([jnp.where(lmask[h], xb, zero) for h in range(RW_GROUP)], axis=0)

    def block_t(x):
        xt = x.T
        return jnp.concatenate([xt[h * RW_HEAD:(h + 1) * RW_HEAD, :] for h in range(RW_GROUP)],
                               axis=1)

    def local_group(gi, carry):
        units = [(e, u, g) for e in range(nb) for u in range(unroll) for g in range(ngroup)]
        n = len(units)
        idxs = [gi * unroll + u for u in range(unroll)]
        rows = [pl.ds(pl.multiple_of(i * c, c), c) for i in idxs]

        def ld(ref, e, u, g):
            return ref[e, rows[u], g * gw:(g + 1) * gw]

        lw = [ld(lw_ref, e, u, g) for e, u, g in units]
        gcum = [_cumsum_rows(x) for x in lw]
        gend, rt, at, kt, bt, bh, kh, vst = [], [], [], [], [], [], [], []
        for s, (e, u, g) in enumerate(units):
            g_ = gcum[s]
            ge = g_[c - 1:c, :]
            r = ld(r_ref, e, u, g)
            k = ld(k_ref, e, u, g)
            nkk = ld(nkk_ref, e, u, g)
            b = ld(b_ref, e, u, g)
            e_neg = jnp.exp(-g_)
            dk = jnp.exp(ge - g_)
            gend.append(ge)
            rt.append(r * jnp.exp(g_))
            at.append(nkk * jnp.exp(g_ - lw[s]))
            kt.append(k * e_neg)
            bt.append(b * e_neg)
            bh.append(b * dk)
            kh.append(k * dk)
            vst.append(stack(ld(v_ref, e, u, g)))

        x = [_dot_nt(jnp.concatenate([at[s], rt[s]], axis=0),
                     jnp.concatenate([stack(bt[s]), stack(kt[s])], axis=0)) for s in range(n)]
        a_ab = [jnp.where(strict, y[:c, :gw], 0.0) for y in x]
        a_ak = [jnp.where(strict, y[:c, gw:], 0.0) for y in x]
        a_rb = [jnp.where(incl, y[c:, :gw], 0.0) for y in x]
        a_rk = [jnp.where(incl, y[c:, gw:], 0.0) for y in x]
        bht = [block_t(y) for y in bh]
        kht = [block_t(y) for y in kh]

        tp = [eye_f + a for a in a_ab]
        pw = [_dot(a, stack(a)) for a in a_ab]
        for _ in range(4):
            res = [_dot(jnp.concatenate([pw[s], tp[s]], axis=0), stack(pw[s])) for s in range(n)]
            tp = [tp[s] + res[s][c:] for s in range(n)]
            pw = [res[s][:c] for s in range(n)]
        tinv = [tp[s] + _dot(tp[s], stack(pw[s])) for s in range(n)]

        vres = [_dot(jnp.concatenate([a_ak[s], a_rk[s], kht[s]], axis=0), vst[s]) for s in range(n)]
        akv = [y[:c] for y in vres]
        rk_v = [y[c:2 * c] for y in vres]
        h2 = [y[2 * c:] for y in vres]
        tres = [_dot(tinv[s], jnp.concatenate([stack(at[s]), stack(akv[s])], axis=1))
                for s in range(n)]
        ap = [y[:, :gw] for y in tres]
        uloc = [y[:, gw:] for y in tres]
        fres = [_dot(jnp.concatenate([a_rb[s], bht[s]], axis=0),
                     jnp.concatenate([stack(ap[s]), stack(uloc[s])], axis=1)) for s in range(n)]
        for s, (e, u, g) in enumerate(units):
            y = fres[s]
            rm_s[e, idxs[u], g, :c, :] = rt[s] + y[:c, :gw]
            rm_s[e, idxs[u], g, c:, :] = jnp.where(eye, jnp.exp(gend[s]), 0.0) + y[c:, :gw]
            yl_s[e, rows[u], g * gw:(g + 1) * gw] = y[:c, gw:] + rk_v[s]
            hl_s[e, idxs[u], g] = y[c:, gw:] + h2[s]
        return carry

    lax.fori_loop(0, nchunk // unroll, local_group, 0)

    chains = [(e, g) for e in range(nb) for g in range(ngroup)]
    hstate = [h_ref[e, g] for e, g in chains]
    for ci in range(nchunk):
        rows = slice(ci * c, (ci + 1) * c)
        res = [_dot(rm_s[e, ci, g], stack(hstate[j])) for j, (e, g) in enumerate(chains)]
        for j, (e, g) in enumerate(chains):
            lanes = slice(g * gw, (g + 1) * gw)
            yl_s[e, rows, lanes] = res[j][:c] + yl_s[e, rows, lanes]
            hstate[j] = res[j][c:] + hl_s[e, ci, g]
    for j, (e, g) in enumerate(chains):
        h_ref[e, g] = hstate[j]

    bd = bd_ref[...]
    for p in range(RW_WIDTH // LANES):
        lanes = slice(p * LANES, (p + 1) * LANES)
        y = yl_s[:, :, lanes].reshape(nb * tb, LANES)
        mean = _dot_exact_rhs(y, bd) * (1.0 / RW_HEAD)
        yc = y - mean
        var = _dot_exact_rhs(yc * yc, bd) * (1.0 / RW_HEAD)
        yn = yc * lax.rsqrt(var + RW_GN_EPS)
        bonus = bonus_ref[:, :, lanes].reshape(nb * tb, LANES)
        gate = g_ref[:, :, lanes].reshape(nb * tb, LANES)
        out = (yn * gng_ref[:, lanes] + gnb_ref[:, lanes] + bonus) * gate
        o_ref[:, :, lanes] = out.reshape(nb, tb, LANES).astype(o_ref.dtype)


def _rwkv_call(r, k, v, lw, nkk, b, g, bonus, gng, gnb, bd, weights, bsz, seq):
    tb = RWKV_TILE
    nb = RWKV_BATCH
    nt = seq // tb
    gw = RW_GROUP * RW_HEAD
    ngroup = RW_WIDTH // gw
    nchunk = tb // CHUNK
    nsteps = (bsz // nb) * nt
    assert bsz % nb == 0
    assert all(w.shape[0] % (16 * nsteps) == 0 for w in weights)

    def tok():
        return pl.BlockSpec((nb, tb, RW_WIDTH), lambda bb, t: (bb, t, 0))

    def full(a):
        return pl.BlockSpec(a.shape, lambda bb, t: (0,) * a.ndim)

    def wblock(w):
        return pl.BlockSpec((w.shape[0] // nsteps, w.shape[1]), lambda bb, t: (bb * nt + t, 0))

    per_seq = lambda a: a.reshape(bsz, seq, RW_WIDTH)
    out, *cast = pl.pallas_call(
        functools.partial(_rwkv_kernel, nb=nb, tb=tb, unroll=RWKV_UNROLL, ncast=len(weights)),
        grid=(bsz // nb, nt),
        in_specs=([tok() for _ in range(8)] + [full(gng), full(gnb), full(bd)]
                  + [wblock(w) for w in weights]),
        out_specs=[tok()] + [wblock(w) for w in weights],
        out_shape=([jax.ShapeDtypeStruct((bsz, seq, RW_WIDTH), BF16)]
                   + [jax.ShapeDtypeStruct(w.shape, BF16) for w in weights]),
        scratch_shapes=[pltpu.VMEM((nb, ngroup, RW_HEAD, gw), F32),
                        pltpu.VMEM((nb, nchunk, ngroup, CHUNK + RW_HEAD, gw), F32),
                        pltpu.VMEM((nb, tb, RW_WIDTH), F32),
                        pltpu.VMEM((nb, nchunk, ngroup, RW_HEAD, gw), F32)],
        compiler_params=pltpu.CompilerParams(
            dimension_semantics=("arbitrary", "arbitrary"),
            vmem_limit_bytes=VMEM_LIMIT),
        name="rwkv",
    )(*[per_seq(a) for a in (r, k, v, lw, nkk, b, g, bonus)], gng, gnb, bd, *weights)
    return out.reshape(bsz * seq, RW_WIDTH), cast


def _gla_kernel(q_ref, k_ref, v_ref, sg_ref, la_ref, ng_ref, o_ref, st_ref, qs_s, oi_s, kv_s, eb_s,
                *, tb, unroll):
    c = CHUNK
    nchunk = tb // c
    t = pl.program_id(1)

    @pl.when(t == 0)
    def _():
        st_ref[...] = jnp.zeros_like(st_ref)

    lane = lax.broadcasted_iota(jnp.int32, (1, LANES), 1)
    hmask = ((lane < GLA_DK).astype(F32), (lane >= GLA_DK).astype(F32))
    ri = lax.broadcasted_iota(jnp.int32, (c, c), 0)
    ci_ = lax.broadcasted_iota(jnp.int32, (c, c), 1)
    causal = ci_ <= ri
    ltri = causal.astype(BF16)

    def local_group(gi, carry):
        units = [(u, h) for u in range(unroll) for h in range(GLA_HEADS)]
        idxs = [gi * unroll + u for u in range(unroll)]
        rows = [pl.ds(pl.multiple_of(i * c, c), c) for i in idxs]

        def qk_lanes(h):
            return slice((h // 2) * LANES, (h // 2 + 1) * LANES)

        la = [la_ref[rows[u], qk_lanes(h)] for u, h in units]
        bcum = [_dot_exact_lhs(ltri, x) for x in la]
        q_s, k_s, kdec, v = [], [], [], []
        for s, (u, h) in enumerate(units):
            bc = bcum[s]
            blast = bc[c - 1:c, :]
            q = q_ref[rows[u], qk_lanes(h)]
            k = k_ref[rows[u], qk_lanes(h)]
            hm = hmask[h % 2]
            q_s.append((q * jnp.exp(bc) * hm).astype(BF16))
            k_s.append(k * jnp.exp(-bc))
            kdec.append(k * jnp.exp(blast - bc) * hm)
            v.append(v_ref[rows[u], h * LANES:(h + 1) * LANES])
            eb_s[idxs[u], h] = jnp.exp(blast)
        att = [jnp.where(causal, _dot_nt(q_s[s], k_s[s]), 0.0) for s in range(len(units))]
        oi = [_dot(att[s], v[s]) for s in range(len(units))]
        v_t = [x.astype(F32).T.astype(BF16) for x in v]
        kv = [_dot(v_t[s], kdec[s]) for s in range(len(units))]
        for s, (u, h) in enumerate(units):
            qs_s[idxs[u], h] = q_s[s]
            oi_s[rows[u], h * LANES:(h + 1) * LANES] = oi[s]
            kv_s[idxs[u], h] = kv[s]
        return carry

    lax.fori_loop(0, nchunk // unroll, local_group, 0)

    st = [st_ref[h] for h in range(GLA_HEADS)]
    for ci in range(nchunk):
        rows = slice(ci * c, (ci + 1) * c)
        inter = [_dot_nt(qs_s[ci, h], st[h]) for h in range(GLA_HEADS)]
        for h in range(GLA_HEADS):
            lanes = slice(h * LANES, (h + 1) * LANES)
            oi_s[rows, lanes] = oi_s[rows, lanes] + inter[h]
            st[h] = st[h] * eb_s[ci, h] + kv_s[ci, h]
    for h in range(GLA_HEADS):
        st_ref[h] = st[h]

    ng = ng_ref[...]
    for h in range(GLA_HEADS):
        lanes = slice(h * LANES, (h + 1) * LANES)
        o = oi_s[:, lanes]
        o = o * lax.rsqrt(jnp.mean(o * o, axis=-1, keepdims=True) + GLA_NORM_EPS) * ng
        o_ref[:, lanes] = (o * sg_ref[:, lanes]).astype(o_ref.dtype)


def _gla_call(q, k, v, sg, la, ng, bsz, seq):
    tb = GLA_TILE
    nt = seq // tb
    n_tok = bsz * seq
    nchunk = tb // CHUNK

    def tok(n):
        return pl.BlockSpec((tb, n), lambda bb, t: (bb * nt + t, 0))

    return pl.pallas_call(
        functools.partial(_gla_kernel, tb=tb, unroll=GLA_UNROLL),
        grid=(bsz, nt),
        in_specs=[tok(GLA_KW), tok(GLA_KW), tok(GLA_VW), tok(GLA_VW), tok(GLA_KW),
                  pl.BlockSpec((1, LANES), lambda bb, t: (0, 0))],
        out_specs=tok(GLA_VW),
        out_shape=jax.ShapeDtypeStruct((n_tok, GLA_VW), BF16),
        scratch_shapes=[pltpu.VMEM((GLA_HEADS, GLA_DV, LANES), F32),
                        pltpu.VMEM((nchunk, GLA_HEADS, CHUNK, LANES), BF16),
                        pltpu.VMEM((tb, GLA_VW), F32),
                        pltpu.VMEM((nchunk, GLA_HEADS, GLA_DV, LANES), F32),
                        pltpu.VMEM((nchunk, GLA_HEADS, 1, LANES), F32)],
        compiler_params=pltpu.CompilerParams(
            dimension_semantics=("arbitrary", "arbitrary"),
            vmem_limit_bytes=VMEM_LIMIT),
        name="gla",
    )(q, k, v, sg, la, ng)


def _tail_kernel(x_ref, orw_ref, ogla_ref, sgr_ref, sgg_ref, gate1_ref, shift2_ref, scale2_ref,
                 gate2_ref, wrb_ref, wgb_ref, wmix_ref, win_ref, wout_ref, ln1g_ref, ln1b_ref,
                 ln2g_ref, ln2b_ref, o_ref, *, d_ff):
    hrows = x_ref.shape[0] // TAIL_SPLIT
    rs = [slice(i * hrows, (i + 1) * hrows) for i in range(TAIL_SPLIT)]
    nsp = range(TAIL_SPLIT)
    b_rw = [jnp.dot(orw_ref[s, :], wrb_ref[...], preferred_element_type=F32) for s in rs]
    b_gla = [jnp.dot(ogla_ref[s, :], wgb_ref[...], preferred_element_type=F32) for s in rs]
    mix = [_dot(_sigmoid(sgr_ref[rs[i], :]) * b_rw[i] + _sigmoid(sgg_ref[rs[i], :]) * b_gla[i],
                wmix_ref[...]) for i in nsp]
    x1 = [_layer_norm(ALPHA * x_ref[rs[i], :] + gate1_ref[...] * mix[i], LN_EPS) * ln1g_ref[...]
          + ln1b_ref[...] for i in nsp]
    u = [(x1[i] * (1.0 + scale2_ref[...]) + shift2_ref[...]).astype(BF16) for i in nsp]
    ffn = [jnp.zeros_like(x1[i]) for i in nsp]
    ntile = d_ff // MXU_K
    bounds = [MXU_K * ((ntile * j) // FFN_CHUNKS) for j in range(FFN_CHUNKS)] + [d_ff]
    for lo, hi in zip(bounds[:-1], bounds[1:]):
        hg, hu = [], []
        for i in nsp:
            hg.append(jnp.dot(u[i], win_ref[:, lo:hi], preferred_element_type=F32))
            hu.append(jnp.dot(u[i], win_ref[:, d_ff + lo:d_ff + hi], preferred_element_type=F32))
        act = [hg[i] * _sigmoid(hg[i]) * hu[i] for i in nsp]
        ffn = [ffn[i] + _dot(act[i], wout_ref[lo:hi, :]) for i in nsp]
    for i in nsp:
        y = _layer_norm(ALPHA * x1[i] + gate2_ref[...] * ffn[i], LN_EPS)
        o_ref[rs[i], :] = y * ln2g_ref[...] + ln2b_ref[...]


def _tail_call(x2, o_rw, o_gla, sgr, sgg, mod3, wrb, wgb, wmix, win, wout, ln1g, ln1b, ln2g, ln2b,
               bsz, seq, d_model):
    tm = TAIL_TILE
    nt = seq // tm
    n_tok = bsz * seq
    d_ff = wout.shape[0]
    assert d_ff % MXU_K == 0

    def tok(n):
        return pl.BlockSpec((tm, n), lambda b, t: (b * nt + t, 0))

    def const(a):
        return pl.BlockSpec(a.shape, lambda b, t: (0,) * a.ndim, pipeline_mode=pl.Buffered(1))

    def modspec(idx):
        return pl.BlockSpec((None, 1, d_model), lambda b, t: (b, 0, idx))

    return pl.pallas_call(
        functools.partial(_tail_kernel, d_ff=d_ff),
        grid=(bsz, nt),
        in_specs=[tok(d_model), tok(RW_WIDTH), tok(GLA_VW), tok(d_model), tok(d_model),
                  modspec(2), modspec(3), modspec(4), modspec(5),
                  const(wrb), const(wgb), const(wmix), const(win), const(wout),
                  const(ln1g), const(ln1b), const(ln2g), const(ln2b)],
        out_specs=tok(d_model),
        out_shape=jax.ShapeDtypeStruct((n_tok, d_model), F32),
        compiler_params=pltpu.CompilerParams(dimension_semantics=("arbitrary", "arbitrary"),
                                             vmem_limit_bytes=VMEM_LIMIT),
        name="tail",
    )(x2, o_rw, o_gla, sgr, sgg, mod3, mod3, mod3, mod3, wrb, wgb, wmix, win, wout,
      ln1g, ln1b, ln2g, ln2b)


def kernel(x, c, w_ada, b_ada, w_in, mu_rw, rw_w0, rw_w2, rw_a0, rw_a2, rw_g2, rw_k_k, rw_k_a,
           rw_r_k, rw_gn_g, rw_gn_b, gla_a2, gla_a_b, gla_norm_g, w_rw_branch, w_gla_branch,
           w_mix_out, ln1_g, ln1_b, w_ffn_in, w_ffn_out, ln2_g, ln2_b):
    bsz, seq, d_model = x.shape
    assert w_ada.shape[0] == DEPTH
    assert all(seq % tile == 0 for tile in (INPROJ_TILE, RWKV_TILE, GLA_TILE, TAIL_TILE))
    n_tok = bsz * seq
    l = 0

    wp = _wprep_call(jnp.swapaxes(w_in[l], 0, 1), d_model)
    zeros_lora = jnp.zeros((DECAY_LORA, RW_WIDTH), F32)
    w2p = jnp.concatenate([rw_w2[l], zeros_lora], axis=0).astype(BF16)
    a2p = jnp.concatenate([zeros_lora, rw_a2[l]], axis=0).astype(BF16)
    ga2p = jnp.concatenate(
        [gla_a2[l], jnp.zeros((LANES - GLA_GATE_LORA, GLA_KW), F32)], axis=0).astype(BF16)
    row = lambda a: a.reshape(1, -1)
    hid = jnp.arange(LANES) // RW_HEAD
    bd = (hid[:, None] == hid[None, :]).astype(BF16)
    small = (row(mu_rw[l]), row(rw_w0[l]), w2p, row(rw_a0[l]), a2p, rw_g2[l].astype(BF16),
             row(rw_k_k[l]), row(rw_k_a[l]), row(rw_r_k[l]), ga2p, row(gla_a_b[l]), bd)

    x2 = x.reshape(n_tok, d_model)
    mod = _mod_call(c, w_ada[l], b_ada[l])
    mod3 = mod.reshape(bsz, 1, 6 * d_model)

    (r, k, v, lw, nkk, b, g, bonus, gq, gk, gv, gsg, gla, sgr, sgg) = _inproj_call(
        x2, mod3, wp, small, bsz, seq, d_model)

    o_rw, (wrb, wgb, wmix, win, wout) = _rwkv_call(
        r, k, v, lw, nkk, b, g, bonus, row(rw_gn_g[l]), row(rw_gn_b[l]), bd,
        (w_rw_branch[l], w_gla_branch[l], w_mix_out[l], w_ffn_in[l], w_ffn_out[l]), bsz, seq)
    o_gla = _gla_call(gq, gk, gv, gsg, gla, row(gla_norm_g[l]), bsz, seq)

    out = _tail_call(x2, o_rw, o_gla, sgr, sgg, mod3, wrb, wgb, wmix, win, wout,
                     row(ln1_g[l]), row(ln1_b[l]), row(ln2_g[l]), row(ln2_b[l]), bsz, seq, d_model)
    return out.reshape(bsz, seq, d_model)
```

```python
import functools

import jax
import jax.numpy as jnp
from jax import lax
from jax.experimental import pallas as pl
from jax.experimental.pallas import tpu as pltpu

F32 = jnp.float32
BF16 = jnp.bfloat16

RW_HEAD = 64
RW_HEADS = 8
RW_WIDTH = RW_HEADS * RW_HEAD
DECAY_LORA = 64
AAA_LORA = 64
GATE_LORA = 128
RW_GN_EPS = 64e-5
RW_COLS = 3 * RW_WIDTH + DECAY_LORA + AAA_LORA + GATE_LORA
GLA_HEADS = 4
GLA_DK = 64
GLA_DV = 128
GLA_KW = GLA_HEADS * GLA_DK
GLA_VW = GLA_HEADS * GLA_DV
GLA_GATE_LORA = 16
GLA_TAU = 16.0
GLA_NORM_EPS = 1e-5
GLA_MAIN = 2 * GLA_KW + 2 * GLA_VW
GLA_COLS = GLA_MAIN + GLA_GATE_LORA
LN_EPS = 1e-5
DEPTH = 1
ALPHA = (2.0 * DEPTH) ** 0.25

LANES = 128
MXU_K = 256
CHUNK = 64
MOD_TILE = 1536
WPREP_COLS = 256
INPROJ_TILE = 512
RWKV_UNROLL = 1
RW_GROUP = 2
GLA_UNROLL = 8
RWKV_TILE = 256
RWKV_BATCH = 4
GLA_TILE = 2048
TAIL_TILE = 512
TAIL_SPLIT = 2
FFN_CHUNKS = 2
VMEM_LIMIT = 58 * 1024 * 1024

COL_RW = 0
COL_GLA = RW_COLS
COL_GATE = COL_GLA + GLA_MAIN


def _sigmoid(x):
    return 1.0 / (1.0 + jnp.exp(-x))


def _softplus(x):
    return jnp.maximum(x, 0.0) + jnp.log1p(jnp.exp(-jnp.abs(x)))


def _dot(a, b):
    return jnp.dot(a.astype(BF16), b.astype(BF16), preferred_element_type=F32)


def _dot_nt(a, b):
    return lax.dot_general(a.astype(BF16), b.astype(BF16), (((1,), (1,)), ((), ())),
                           preferred_element_type=F32)


def _split_hi_lo(x):
    hi = x.astype(BF16)
    lo = (x - hi.astype(F32)).astype(BF16)
    return hi, lo


def _dot_exact_rhs(x, w01):
    hi, lo = _split_hi_lo(x)
    if 2 * x.shape[1] <= MXU_K:
        return jnp.dot(jnp.concatenate([hi, lo], axis=1), jnp.concatenate([w01, w01], axis=0),
                       preferred_element_type=F32)
    return (jnp.dot(hi, w01, preferred_element_type=F32)
            + jnp.dot(lo, w01, preferred_element_type=F32))


def _dot_exact_lhs(w01, x):
    hi, lo = _split_hi_lo(x)
    if 2 * x.shape[0] <= MXU_K:
        return jnp.dot(jnp.concatenate([w01, w01], axis=1), jnp.concatenate([hi, lo], axis=0),
                       preferred_element_type=F32)
    return (jnp.dot(w01, hi, preferred_element_type=F32)
            + jnp.dot(w01, lo, preferred_element_type=F32))


def _cumsum_rows(x):
    row = lax.broadcasted_iota(jnp.int32, x.shape, 0)
    shift = 1
    while shift < x.shape[0]:
        x = x + jnp.where(row >= shift, pltpu.roll(x, shift, 0), 0.0)
        shift *= 2
    return x


def _layer_norm(x, eps):
    mu = jnp.mean(x, axis=-1, keepdims=True)
    xc = x - mu
    var = jnp.mean(xc * xc, axis=-1, keepdims=True)
    return xc * lax.rsqrt(var + eps)


def _mod_kernel(c_ref, w_ref, b_ref, o_ref):
    c = c_ref[...]
    sc = c * _sigmoid(c)
    o_ref[...] = _dot(sc, w_ref[...]) + b_ref[...]


def _mod_call(c, w_ada, b_ada):
    bsz, d = c.shape
    n = w_ada.shape[1]
    tn = MOD_TILE
    assert n % tn == 0
    return pl.pallas_call(
        _mod_kernel,
        grid=(n // tn,),
        in_specs=[pl.BlockSpec((bsz, d), lambda j: (0, 0)),
                  pl.BlockSpec((d, tn), lambda j: (0, j)),
                  pl.BlockSpec((1, tn), lambda j: (0, j))],
        out_specs=pl.BlockSpec((bsz, tn), lambda j: (0, j)),
        out_shape=jax.ShapeDtypeStruct((bsz, n), F32),
        compiler_params=pltpu.CompilerParams(dimension_semantics=("arbitrary",),
                                             vmem_limit_bytes=VMEM_LIMIT),
        name="mod",
    )(c, w_ada, b_ada.reshape(1, n))


def _wprep_kernel(w_ref, o_ref, *, d_model):
    main = RW_COLS + GLA_MAIN
    o_ref[:main, :] = w_ref[:main, :].astype(BF16)
    o_ref[main:main + 2 * d_model, :] = w_ref[RW_COLS + GLA_COLS:, :].astype(BF16)
    o_ref[main + 2 * d_model:main + 2 * d_model + GLA_GATE_LORA, :] = (
        w_ref[main:RW_COLS + GLA_COLS, :].astype(BF16))
    o_ref[main + 2 * d_model + GLA_GATE_LORA:, :] = jnp.zeros(
        (LANES - GLA_GATE_LORA, o_ref.shape[1]), BF16)


def _wprep_call(w_in_t, d_model):
    n_in = w_in_t.shape[0]
    n_out = RW_COLS + GLA_MAIN + 2 * d_model + LANES
    cols = WPREP_COLS
    assert d_model % cols == 0
    return pl.pallas_call(
        functools.partial(_wprep_kernel, d_model=d_model),
        grid=(d_model // cols,),
        in_specs=[pl.BlockSpec((n_in, cols), lambda i: (0, i))],
        out_specs=pl.BlockSpec((n_out, cols), lambda i: (0, i)),
        out_shape=jax.ShapeDtypeStruct((n_out, d_model), BF16),
        compiler_params=pltpu.CompilerParams(dimension_semantics=("arbitrary",),
                                             vmem_limit_bytes=VMEM_LIMIT),
        name="wprep",
    )(w_in_t)


def _inproj_kernel(x_ref, shift_ref, scale_ref, wp_ref, mu_ref, w0_ref, w2_ref, a0_ref, a2_ref,
                   g2_ref, kk_ref, ka_ref, rk_ref, ga2_ref, gab_ref, bd_ref,
                   r_out, k_out, v_out, lw_out, nkk_out, b_out, g_out, bonus_out,
                   gq_out, gk_out, gv_out, gsg_out, gla_out, sgr_out, sgg_out,
                   carry_ref, *, tm, d_model):
    t = pl.program_id(1)

    @pl.when(t == 0)
    def _():
        carry_ref[...] = jnp.zeros_like(carry_ref)

    u = (x_ref[...] * (1.0 + scale_ref[...]) + shift_ref[...]).astype(BF16)
    p = _dot_nt(u, wp_ref[COL_RW:COL_RW + RW_COLS, :])
    pg = _dot_nt(u, wp_ref[COL_GLA:COL_GLA + GLA_MAIN, :])

    prev = pltpu.roll(p, 1, 0)
    row = lax.broadcasted_iota(jnp.int32, p.shape, 0)
    prev = jnp.where(row == 0, carry_ref[...], prev)
    carry_ref[...] = p[tm - 1:tm, :]
    ps = p + mu_ref[...] * (prev - p)

    r = ps[:, 0:RW_WIDTH]
    k = ps[:, RW_WIDTH:2 * RW_WIDTH]
    v = ps[:, 2 * RW_WIDTH:3 * RW_WIDTH]
    z = ps[:, 3 * RW_WIDTH:3 * RW_WIDTH + LANES]
    gd = ps[:, 3 * RW_WIDTH + LANES:RW_COLS]

    w_lora = _dot(jnp.tanh(z), w2_ref[...])
    a_lora = _dot(z, a2_ref[...])
    g_out[...] = _dot(_sigmoid(gd), g2_ref[...])
    gates = _dot_nt(u, wp_ref[COL_GATE:, :])

    w = -_softplus(-(w0_ref[...] + w_lora)) - 0.5
    lw_out[...] = -jnp.exp(w)
    a = _sigmoid(a0_ref[...] + a_lora)
    kkv = k * kk_ref[...]
    kp = k * (1.0 + (a - 1.0) * ka_ref[...])
    rkr = r * kp * rk_ref[...]
    bd = bd_ref[...]
    for j in range(RW_WIDTH // LANES):
        sl = slice(j * LANES, (j + 1) * LANES)
        kkj = kkv[:, sl]
        ssq = _dot_exact_rhs(kkj * kkj, bd)
        kkn = kkj / jnp.maximum(jnp.sqrt(ssq), 1e-12)
        nkk_out[:, sl] = -kkn
        b_out[:, sl] = kkn * a[:, sl]
        bonus_out[:, sl] = _dot_exact_rhs(rkr[:, sl], bd) * v[:, sl]
    r_out[...] = r
    k_out[...] = kp
    v_out[...] = v.astype(BF16)

    gq_out[...] = pg[:, 0:GLA_KW] * (GLA_DK ** -0.5)
    gk_out[...] = pg[:, GLA_KW:2 * GLA_KW]
    gv_out[...] = pg[:, 2 * GLA_KW:2 * GLA_KW + GLA_VW].astype(BF16)
    gg = pg[:, 2 * GLA_KW + GLA_VW:GLA_MAIN]
    gsg_out[...] = gg * _sigmoid(gg)
    adg = gates[:, 2 * d_model:2 * d_model + LANES]
    la = _dot(adg, ga2_ref[...]) + gab_ref[...]
    gla_out[...] = -_softplus(-la) / GLA_TAU

    sgr_out[...] = gates[:, :d_model]
    sgg_out[...] = gates[:, d_model:2 * d_model]


def _inproj_call(x2, mod3, wp, small, bsz, seq, d_model):
    tm = INPROJ_TILE
    nt = seq // tm
    n_tok = bsz * seq
    (mu, w0, w2p, a0, a2p, g2, k_k, k_a, r_k, ga2p, gab, bd) = small

    def tok(n):
        return pl.BlockSpec((tm, n), lambda b, t: (b * nt + t, 0))

    def full(a):
        return pl.BlockSpec(a.shape, lambda b, t: (0,) * a.ndim, pipeline_mode=pl.Buffered(1))

    def modspec(idx):
        return pl.BlockSpec((None, 1, d_model), lambda b, t: (b, 0, idx))

    outs = [
        (RW_WIDTH, F32), (RW_WIDTH, F32), (RW_WIDTH, BF16), (RW_WIDTH, F32), (RW_WIDTH, F32),
        (RW_WIDTH, F32), (RW_WIDTH, F32), (RW_WIDTH, F32),
        (GLA_KW, F32), (GLA_KW, F32), (GLA_VW, BF16), (GLA_VW, F32), (GLA_KW, F32),
        (d_model, F32), (d_model, F32),
    ]
    return pl.pallas_call(
        functools.partial(_inproj_kernel, tm=tm, d_model=d_model),
        grid=(bsz, nt),
        in_specs=[tok(d_model), modspec(0), modspec(1), full(wp), full(mu), full(w0), full(w2p),
                  full(a0), full(a2p), full(g2), full(k_k), full(k_a), full(r_k), full(ga2p),
                  full(gab), full(bd)],
        out_specs=[tok(n) for n, _ in outs],
        out_shape=[jax.ShapeDtypeStruct((n_tok, n), dt) for n, dt in outs],
        scratch_shapes=[pltpu.VMEM((1, RW_COLS), F32)],
        compiler_params=pltpu.CompilerParams(dimension_semantics=("arbitrary", "arbitrary"),
                                             vmem_limit_bytes=VMEM_LIMIT),
        name="inproj",
    )(x2, mod3, mod3, wp, mu, w0, w2p, a0, a2p, g2, k_k, k_a, r_k, ga2p, gab, bd)


def _rwkv_kernel(*refs, nb, tb, unroll, ncast):
    (r_ref, k_ref, v_ref, lw_ref, nkk_ref, b_ref, g_ref, bonus_ref, gng_ref, gnb_ref,
     bd_ref) = refs[:11]
    cast_in = refs[11:11 + ncast]
    o_ref = refs[11 + ncast]
    cast_out = refs[12 + ncast:12 + 2 * ncast]
    h_ref, rm_s, yl_s, hl_s = refs[12 + 2 * ncast:]
    for w_in_ref, w_out_ref in zip(cast_in, cast_out):
        w_out_ref[...] = w_in_ref[...].astype(BF16)

    c = CHUNK
    gw = RW_GROUP * RW_HEAD
    nchunk = tb // c
    ngroup = RW_WIDTH // gw
    t = pl.program_id(1)

    @pl.when(t == 0)
    def _():
        h_ref[...] = jnp.zeros_like(h_ref)

    lane = lax.broadcasted_iota(jnp.int32, (c, gw), 1)
    rowi = lax.broadcasted_iota(jnp.int32, (c, gw), 0)
    head_of_lane = lane // RW_HEAD
    lmask = [head_of_lane == h for h in range(RW_GROUP)]
    strict = (lane % RW_HEAD) < rowi
    incl = (lane % RW_HEAD) <= rowi
    eye = (lane % RW_HEAD) == rowi
    eye_f = eye.astype(F32)
    assert c == RW_HEAD

    def stack(x):
        xb = x.astype(BF16)
        zero = jnp.zeros_like(xb)
        return jnp.concatenate([jnp.where(lmask[h], xb, zero) for h in range(RW_GROUP)], axis=0)

    def block_t(x):
        xt = x.T
        return jnp.concatenate([xt[h * RW_HEAD:(h + 1) * RW_HEAD, :] for h in range(RW_GROUP)],
                               axis=1)

    def local_group(gi, carry):
        units = [(e, u, g) for e in range(nb) for u in range(unroll) for g in range(ngroup)]
        n = len(units)
        idxs = [gi * unroll + u for u in range(unroll)]
        rows = [pl.ds(pl.multiple_of(i * c, c), c) for i in idxs]

        def ld(ref, e, u, g):
            return ref[e, rows[u], g * gw:(g + 1) * gw]

        lw = [ld(lw_ref, e, u, g) for e, u, g in units]
        gcum = [_cumsum_rows(x) for x in lw]
        gend, rt, at, kt, bt, bh, kh, vst = [], [], [], [], [], [], [], []
        for s, (e, u, g) in enumerate(units):
            g_ = gcum[s]
            ge = g_[c - 1:c, :]
            r = ld(r_ref, e, u, g)
            k = ld(k_ref, e, u, g)
            nkk = ld(nkk_ref, e, u, g)
            b = ld(b_ref, e, u, g)
            e_neg = jnp.exp(-g_)
            dk = jnp.exp(ge - g_)
            gend.append(ge)
            rt.append(r * jnp.exp(g_))
            at.append(nkk * jnp.exp(g_ - lw[s]))
            kt.append(k * e_neg)
            bt.append(b * e_neg)
            bh.append(b * dk)
            kh.append(k * dk)
            vst.append(stack(ld(v_ref, e, u, g)))

        x = [_dot_nt(jnp.concatenate([at[s], rt[s]], axis=0),
                     jnp.concatenate([stack(bt[s]), stack(kt[s])], axis=0)) for s in range(n)]
        a_ab = [jnp.where(strict, y[:c, :gw], 0.0) for y in x]
        a_ak = [jnp.where(strict, y[:c, gw:], 0.0) for y in x]
        a_rb = [jnp.where(incl, y[c:, :gw], 0.0) for y in x]
        a_rk = [jnp.where(incl, y[c:, gw:], 0.0) for y in x]
        bht = [block_t(y) for y in bh]
        kht = [block_t(y) for y in kh]

        tp = [eye_f + a for a in a_ab]
        pw = [_dot(a, stack(a)) for a in a_ab]
        for _ in range(4):
            res = [_dot(jnp.concatenate([pw[s], tp[s]], axis=0), stack(pw[s])) for s in range(n)]
            tp = [tp[s] + res[s][c:] for s in range(n)]
            pw = [res[s][:c] for s in range(n)]
        tinv = [tp[s] + _dot(tp[s], stack(pw[s])) for s in range(n)]

        vres = [_dot(jnp.concatenate([a_ak[s], a_rk[s], kht[s]], axis=0), vst[s]) for s in range(n)]
        akv = [y[:c] for y in vres]
        rk_v = [y[c:2 * c] for y in vres]
        h2 = [y[2 * c:] for y in vres]
        tres = [_dot(tinv[s], jnp.concatenate([stack(at[s]), stack(akv[s])], axis=1))
                for s in range(n)]
        ap = [y[:, :gw] for y in tres]
        uloc = [y[:, gw:] for y in tres]
        fres = [_dot(jnp.concatenate([a_rb[s], bht[s]], axis=0),
                     jnp.concatenate([stack(ap[s]), stack(uloc[s])], axis=1)) for s in range(n)]
        for s, (e, u, g) in enumerate(units):
            y = fres[s]
            rm_s[e, idxs[u], g, :c, :] = rt[s] + y[:c, :gw]
            rm_s[e, idxs[u], g, c:, :] = jnp.where(eye, jnp.exp(gend[s]), 0.0) + y[c:, :gw]
            yl_s[e, rows[u], g * gw:(g + 1) * gw] = y[:c, gw:] + rk_v[s]
            hl_s[e, idxs[u], g] = y[c:, gw:] + h2[s]
        return carry

    lax.fori_loop(0, nchunk // unroll, local_group, 0)

    chains = [(e, g) for e in range(nb) for g in range(ngroup)]
    hstate = [h_ref[e, g] for e, g in chains]
    for ci in range(nchunk):
        rows = slice(ci * c, (ci + 1) * c)
        res = [_dot(rm_s[e, ci, g], stack(hstate[j])) for j, (e, g) in enumerate(chains)]
        for j, (e, g) in enumerate(chains):
            lanes = slice(g * gw, (g + 1) * gw)
            yl_s[e, rows, lanes] = res[j][:c] + yl_s[e, rows, lanes]
            hstate[j] = res[j][c:] + hl_s[e, ci, g]
    for j, (e, g) in enumerate(chains):
        h_ref[e, g] = hstate[j]

    bd = bd_ref[...]
    for p in range(RW_WIDTH // LANES):
        lanes = slice(p * LANES, (p + 1) * LANES)
        y = yl_s[:, :, lanes].reshape(nb * tb, LANES)
        mean = _dot_exact_rhs(y, bd) * (1.0 / RW_HEAD)
        yc = y - mean
        var = _dot_exact_rhs(yc * yc, bd) * (1.0 / RW_HEAD)
        yn = yc * lax.rsqrt(var + RW_GN_EPS)
        bonus = bonus_ref[:, :, lanes].reshape(nb * tb, LANES)
        gate = g_ref[:, :, lanes].reshape(nb * tb, LANES)
        out = (yn * gng_ref[:, lanes] + gnb_ref[:, lanes] + bonus) * gate
        o_ref[:, :, lanes] = out.reshape(nb, tb, LANES).astype(o_ref.dtype)


def _rwkv_call(r, k, v, lw, nkk, b, g, bonus, gng, gnb, bd, weights, bsz, seq):
    tb = RWKV_TILE
    nb = RWKV_BATCH
    nt = seq // tb
    gw = RW_GROUP * RW_HEAD
    ngroup = RW_WIDTH // gw
    nchunk = tb // CHUNK
    nsteps = (bsz // nb) * nt
    assert bsz % nb == 0
    assert all(w.shape[0] % (16 * nsteps) == 0 for w in weights)

    def tok():
        return pl.BlockSpec((nb, tb, RW_WIDTH), lambda bb, t: (bb, t, 0))

    def full(a):
        return pl.BlockSpec(a.shape, lambda bb, t: (0,) * a.ndim)

    def wblock(w):
        return pl.BlockSpec((w.shape[0] // nsteps, w.shape[1]), lambda bb, t: (bb * nt + t, 0))

    per_seq = lambda a: a.reshape(bsz, seq, RW_WIDTH)
    out, *cast = pl.pallas_call(
        functools.partial(_rwkv_kernel, nb=nb, tb=tb, unroll=RWKV_UNROLL, ncast=len(weights)),
        grid=(bsz // nb, nt),
        in_specs=([tok() for _ in range(8)] + [full(gng), full(gnb), full(bd)]
                  + [wblock(w) for w in weights]),
        out_specs=[tok()] + [wblock(w) for w in weights],
        out_shape=([jax.ShapeDtypeStruct((bsz, seq, RW_WIDTH), BF16)]
                   + [jax.ShapeDtypeStruct(w.shape, BF16) for w in weights]),
        scratch_shapes=[pltpu.VMEM((nb, ngroup, RW_HEAD, gw), F32),
                        pltpu.VMEM((nb, nchunk, ngroup, CHUNK + RW_HEAD, gw), F32),
                        pltpu.VMEM((nb, tb, RW_WIDTH), F32),
                        pltpu.VMEM((nb, nchunk, ngroup, RW_HEAD, gw), F32)],
        compiler_params=pltpu.CompilerParams(
            dimension_semantics=("arbitrary", "arbitrary"),
            vmem_limit_bytes=VMEM_LIMIT),
        name="rwkv",
    )(*[per_seq(a) for a in (r, k, v, lw, nkk, b, g, bonus)], gng, gnb, bd, *weights)
    return out.reshape(bsz * seq, RW_WIDTH), cast


def _gla_kernel(q_ref, k_ref, v_ref, sg_ref, la_ref, ng_ref, o_ref, st_ref, qs_s, oi_s, kv_s, eb_s,
                *, tb, unroll):
    c = CHUNK
    nchunk = tb // c
    t = pl.program_id(1)

    @pl.when(t == 0)
    def _():
        st_ref[...] = jnp.zeros_like(st_ref)

    lane = lax.broadcasted_iota(jnp.int32, (1, LANES), 1)
    hmask = ((lane < GLA_DK).astype(F32), (lane >= GLA_DK).astype(F32))
    ri = lax.broadcasted_iota(jnp.int32, (c, c), 0)
    ci_ = lax.broadcasted_iota(jnp.int32, (c, c), 1)
    causal = ci_ <= ri
    ltri = causal.astype(BF16)

    def local_group(gi, carry):
        units = [(u, h) for u in range(unroll) for h in range(GLA_HEADS)]
        idxs = [gi * unroll + u for u in range(unroll)]
        rows = [pl.ds(pl.multiple_of(i * c, c), c) for i in idxs]

        def qk_lanes(h):
            return slice((h // 2) * LANES, (h // 2 + 1) * LANES)

        la = [la_ref[rows[u], qk_lanes(h)] for u, h in units]
        bcum = [_dot_exact_lhs(ltri, x) for x in la]
        q_s, k_s, kdec, v = [], [], [], []
        for s, (u, h) in enumerate(units):
            bc = bcum[s]
            blast = bc[c - 1:c, :]
            q = q_ref[rows[u], qk_lanes(h)]
            k = k_ref[rows[u], qk_lanes(h)]
            hm = hmask[h % 2]
            q_s.append((q * jnp.exp(bc) * hm).astype(BF16))
            k_s.append(k * jnp.exp(-bc))
            kdec.append(k * jnp.exp(blast - bc) * hm)
            v.append(v_ref[rows[u], h * LANES:(h + 1) * LANES])
            eb_s[idxs[u], h] = jnp.exp(blast)
        att = [jnp.where(causal, _dot_nt(q_s[s], k_s[s]), 0.0) for s in range(len(units))]
        oi = [_dot(att[s], v[s]) for s in range(len(units))]
        v_t = [x.astype(F32).T.astype(BF16) for x in v]
        kv = [_dot(v_t[s], kdec[s]) for s in range(len(units))]
        for s, (u, h) in enumerate(units):
            qs_s[idxs[u], h] = q_s[s]
            oi_s[rows[u], h * LANES:(h + 1) * LANES] = oi[s]
            kv_s[idxs[u], h] = kv[s]
        return carry

    lax.fori_loop(0, nchunk // unroll, local_group, 0)

    st = [st_ref[h] for h in range(GLA_HEADS)]
    for ci in range(nchunk):
        rows = slice(ci * c, (ci + 1) * c)
        inter = [_dot_nt(qs_s[ci, h], st[h]) for h in range(GLA_HEADS)]
        for h in range(GLA_HEADS):
            lanes = slice(h * LANES, (h + 1) * LANES)
            oi_s[rows, lanes] = oi_s[rows, lanes] + inter[h]
            st[h] = st[h] * eb_s[ci, h] + kv_s[ci, h]
    for h in range(GLA_HEADS):
        st_ref[h] = st[h]

    ng = ng_ref[...]
    for h in range(GLA_HEADS):
        lanes = slice(h * LANES, (h + 1) * LANES)
        o = oi_s[:, lanes]
        o = o * lax.rsqrt(jnp.mean(o * o, axis=-1, keepdims=True) + GLA_NORM_EPS) * ng
        o_ref[:, lanes] = (o * sg_ref[:, lanes]).astype(o_ref.dtype)


def _gla_call(q, k, v, sg, la, ng, bsz, seq):
    tb = GLA_TILE
    nt = seq // tb
    n_tok = bsz * seq
    nchunk = tb // CHUNK

    def tok(n):
        return pl.BlockSpec((tb, n), lambda bb, t: (bb * nt + t, 0))

    return pl.pallas_call(
        functools.partial(_gla_kernel, tb=tb, unroll=GLA_UNROLL),
        grid=(bsz, nt),
        in_specs=[tok(GLA_KW), tok(GLA_KW), tok(GLA_VW), tok(GLA_VW), tok(GLA_KW),
                  pl.BlockSpec((1, LANES), lambda bb, t: (0, 0))],
        out_specs=tok(GLA_VW),
        out_shape=jax.ShapeDtypeStruct((n_tok, GLA_VW), BF16),
        scratch_shapes=[pltpu.VMEM((GLA_HEADS, GLA_DV, LANES), F32),
                        pltpu.VMEM((nchunk, GLA_HEADS, CHUNK, LANES), BF16),
                        pltpu.VMEM((tb, GLA_VW), F32),
                        pltpu.VMEM((nchunk, GLA_HEADS, GLA_DV, LANES), F32),
                        pltpu.VMEM((nchunk, GLA_HEADS, 1, LANES), F32)],
        compiler_params=pltpu.CompilerParams(
            dimension_semantics=("arbitrary", "arbitrary"),
            vmem_limit_bytes=VMEM_LIMIT),
        name="gla",
    )(q, k, v, sg, la, ng)


def _tail_kernel(x_ref, orw_ref, ogla_ref, sgr_ref, sgg_ref, gate1_ref, shift2_ref, scale2_ref,
                 gate2_ref, wrb_ref, wgb_ref, wmix_ref, win_ref, wout_ref, ln1g_ref, ln1b_ref,
                 ln2g_ref, ln2b_ref, o_ref, *, d_ff):
    hrows = x_ref.shape[0] // TAIL_SPLIT
    rs = [slice(i * hrows, (i + 1) * hrows) for i in range(TAIL_SPLIT)]
    nsp = range(TAIL_SPLIT)
    b_rw = [jnp.dot(orw_ref[s, :], wrb_ref[...], preferred_element_type=F32) for s in rs]
    b_gla = [jnp.dot(ogla_ref[s, :], wgb_ref[...], preferred_element_type=F32) for s in rs]
    mix = [_dot(_sigmoid(sgr_ref[rs[i], :]) * b_rw[i] + _sigmoid(sgg_ref[rs[i], :]) * b_gla[i],
                wmix_ref[...]) for i in nsp]
    x1 = [_layer_norm(ALPHA * x_ref[rs[i], :] + gate1_ref[...] * mix[i], LN_EPS) * ln1g_ref[...]
          + ln1b_ref[...] for i in nsp]
    u = [(x1[i] * (1.0 + scale2_ref[...]) + shift2_ref[...]).astype(BF16) for i in nsp]
    ffn = [jnp.zeros_like(x1[i]) for i in nsp]
    ntile = d_ff // MXU_K
    bounds = [MXU_K * ((ntile * j) // FFN_CHUNKS) for j in range(FFN_CHUNKS)] + [d_ff]
    for lo, hi in zip(bounds[:-1], bounds[1:]):
        hg, hu = [], []
        for i in nsp:
            hg.append(jnp.dot(u[i], win_ref[:, lo:hi], preferred_element_type=F32))
            hu.append(jnp.dot(u[i], win_ref[:, d_ff + lo:d_ff + hi], preferred_element_type=F32))
        act = [hg[i] * _sigmoid(hg[i]) * hu[i] for i in nsp]
        ffn = [ffn[i] + _dot(act[i], wout_ref[lo:hi, :]) for i in nsp]
    for i in nsp:
        y = _layer_norm(ALPHA * x1[i] + gate2_ref[...] * ffn[i], LN_EPS)
        o_ref[rs[i], :] = y * ln2g_ref[...] + ln2b_ref[...]


def _tail_call(x2, o_rw, o_gla, sgr, sgg, mod3, wrb, wgb, wmix, win, wout, ln1g, ln1b, ln2g, ln2b,
               bsz, seq, d_model):
    tm = TAIL_TILE
    nt = seq // tm
    n_tok = bsz * seq
    d_ff = wout.shape[0]
    assert d_ff % MXU_K == 0

    def tok(n):
        return pl.BlockSpec((tm, n), lambda b, t: (b * nt + t, 0))

    def const(a):
        return pl.BlockSpec(a.shape, lambda b, t: (0,) * a.ndim, pipeline_mode=pl.Buffered(1))

    def modspec(idx):
        return pl.BlockSpec((None, 1, d_model), lambda b, t: (b, 0, idx))

    return pl.pallas_call(
        functools.partial(_tail_kernel, d_ff=d_ff),
        grid=(bsz, nt),
        in_specs=[tok(d_model), tok(RW_WIDTH), tok(GLA_VW), tok(d_model), tok(d_model),
                  modspec(2), modspec(3), modspec(4), modspec(5),
                  const(wrb), const(wgb), const(wmix), const(win), const(wout),
                  const(ln1g), const(ln1b), const(ln2g), const(ln2b)],
        out_specs=tok(d_model),
        out_shape=jax.ShapeDtypeStruct((n_tok, d_model), F32),
        compiler_params=pltpu.CompilerParams(dimension_semantics=("arbitrary", "arbitrary"),
                                             vmem_limit_bytes=VMEM_LIMIT),
        name="tail",
    )(x2, o_rw, o_gla, sgr, sgg, mod3, mod3, mod3, mod3, wrb, wgb, wmix, win, wout,
      ln1g, ln1b, ln2g, ln2b)


def kernel(x, c, w_ada, b_ada, w_in, mu_rw, rw_w0, rw_w2, rw_a0, rw_a2, rw_g2, rw_k_k, rw_k_a,
           rw_r_k, rw_gn_g, rw_gn_b, gla_a2, gla_a_b, gla_norm_g, w_rw_branch, w_gla_branch,
           w_mix_out, ln1_g, ln1_b, w_ffn_in, w_ffn_out, ln2_g, ln2_b):
    bsz, seq, d_model = x.shape
    assert w_ada.shape[0] == DEPTH
    assert all(seq % tile == 0 for tile in (INPROJ_TILE, RWKV_TILE, GLA_TILE, TAIL_TILE))
    n_tok = bsz * seq
    l = 0

    wp = _wprep_call(jnp.swapaxes(w_in[l], 0, 1), d_model)
    zeros_lora = jnp.zeros((DECAY_LORA, RW_WIDTH), F32)
    w2p = jnp.concatenate([rw_w2[l], zeros_lora], axis=0).astype(BF16)
    a2p = jnp.concatenate([zeros_lora, rw_a2[l]], axis=0).astype(BF16)
    ga2p = jnp.concatenate(
        [gla_a2[l], jnp.zeros((LANES - GLA_GATE_LORA, GLA_KW), F32)], axis=0).astype(BF16)
    row = lambda a: a.reshape(1, -1)
    hid = jnp.arange(LANES) // RW_HEAD
    bd = (hid[:, None] == hid[None, :]).astype(BF16)
    small = (row(mu_rw[l]), row(rw_w0[l]), w2p, row(rw_a0[l]), a2p, rw_g2[l].astype(BF16),
             row(rw_k_k[l]), row(rw_k_a[l]), row(rw_r_k[l]), ga2p, row(gla_a_b[l]), bd)

    x2 = x.reshape(n_tok, d_model)
    mod = _mod_call(c, w_ada[l], b_ada[l])
    mod3 = mod.reshape(bsz, 1, 6 * d_model)

    (r, k, v, lw, nkk, b, g, bonus, gq, gk, gv, gsg, gla, sgr, sgg) = _inproj_call(
        x2, mod3, wp, small, bsz, seq, d_model)

    o_rw, (wrb, wgb, wmix, win, wout) = _rwkv_call(
        r, k, v, lw, nkk, b, g, bonus, row(rw_gn_g[l]), row(rw_gn_b[l]), bd,
        (w_rw_branch[l], w_gla_branch[l], w_mix_out[l], w_ffn_in[l], w_ffn_out[l]), bsz, seq)
    o_gla = _gla_call(gq, gk, gv, gsg, gla, row(gla_norm_g[l]), bsz, seq)

    out = _tail_call(x2, o_rw, o_gla, sgr, sgg, mod3, wrb, wgb, wmix, win, wout,
                     row(ln1_g[l]), row(ln1_b[l]), row(ln2_g[l]), row(ln2_b[l]), bsz, seq, d_model)
    return out.reshape(bsz, seq, d_model)
```

```python
import functools

import jax
import jax.numpy as jnp
from jax import lax
from jax.experimental import pallas as pl
from jax.experimental.pallas import tpu as pltpu

F32 = jnp.float32
BF16 = jnp.bfloat16

RW_HEAD = 64
RW_HEADS = 8
RW_WIDTH = RW_HEADS * RW_HEAD
DECAY_LORA = 64
AAA_LORA = 64
GATE_LORA = 128
RW_GN_EPS = 64e-5
RW_COLS = 3 * RW_WIDTH + DECAY_LORA + AAA_LORA + GATE_LORA
GLA_HEADS = 4
GLA_DK = 64
GLA_DV = 128
GLA_KW = GLA_HEADS * GLA_DK
GLA_VW = GLA_HEADS * GLA_DV
GLA_GATE_LORA = 16
GLA_TAU = 16.0
GLA_NORM_EPS = 1e-5
GLA_MAIN = 2 * GLA_KW + 2 * GLA_VW
GLA_COLS = GLA_MAIN + GLA_GATE_LORA
LN_EPS = 1e-5
DEPTH = 1
ALPHA = (2.0 * DEPTH) ** 0.25

LANES = 128
MXU_K = 256
CHUNK = 64
MOD_TILE = 1536
WPREP_COLS = 256
INPROJ_TILE = 512
RWKV_UNROLL = 1
RW_GROUP = 2
GLA_UNROLL = 8
RWKV_TILE = 256
RWKV_BATCH = 4
GLA_TILE = 2048
TAIL_TILE = 512
TAIL_SPLIT = 2
FFN_CHUNKS = 2
VMEM_LIMIT = 58 * 1024 * 1024

COL_RW = 0
COL_GLA = RW_COLS
COL_GATE = COL_GLA + GLA_MAIN


def _sigmoid(x):
    return 1.0 / (1.0 + jnp.exp(-x))


def _softplus(x):
    return jnp.maximum(x, 0.0) + jnp.log1p(jnp.exp(-jnp.abs(x)))


def _dot(a, b):
    return jnp.dot(a.astype(BF16), b.astype(BF16), preferred_element_type=F32)


def _dot_nt(a, b):
    return lax.dot_general(a.astype(BF16), b.astype(BF16), (((1,), (1,)), ((), ())),
                           preferred_element_type=F32)


def _split_hi_lo(x):
    hi = x.astype(BF16)
    lo = (x - hi.astype(F32)).astype(BF16)
    return hi, lo


def _dot_exact_rhs(x, w01):
    hi, lo = _split_hi_lo(x)
    if 2 * x.shape[1] <= MXU_K:
        return jnp.dot(jnp.concatenate([hi, lo], axis=1), jnp.concatenate([w01, w01], axis=0),
                       preferred_element_type=F32)
    return (jnp.dot(hi, w01, preferred_element_type=F32)
            + jnp.dot(lo, w01, preferred_element_type=F32))


def _dot_exact_lhs(w01, x):
    hi, lo = _split_hi_lo(x)
    if 2 * x.shape[0] <= MXU_K:
        return jnp.dot(jnp.concatenate([w01, w01], axis=1), jnp.concatenate([hi, lo], axis=0),
                       preferred_element_type=F32)
    return (jnp.dot(w01, hi, preferred_element_type=F32)
            + jnp.dot(w01, lo, preferred_element_type=F32))


def _cumsum_rows(x):
    row = lax.broadcasted_iota(jnp.int32, x.shape, 0)
    shift = 1
    while shift < x.shape[0]:
        x = x + jnp.where(row >= shift, pltpu.roll(x, shift, 0), 0.0)
        shift *= 2
    return x


def _layer_norm(x, eps):
    mu = jnp.mean(x, axis=-1, keepdims=True)
    xc = x - mu
    var = jnp.mean(xc * xc, axis=-1, keepdims=True)
    return xc * lax.rsqrt(var + eps)


def _mod_kernel(c_ref, w_ref, b_ref, o_ref):
    c = c_ref[...]
    sc = c * _sigmoid(c)
    o_ref[...] = _dot(sc, w_ref[...]) + b_ref[...]


def _mod_call(c, w_ada, b_ada):
    bsz, d = c.shape
    n = w_ada.shape[1]
    tn = MOD_TILE
    assert n % tn == 0
    return pl.pallas_call(
        _mod_kernel,
        grid=(n // tn,),
        in_specs=[pl.BlockSpec((bsz, d), lambda j: (0, 0)),
                  pl.BlockSpec((d, tn), lambda j: (0, j)),
                  pl.BlockSpec((1, tn), lambda j: (0, j))],
        out_specs=pl.BlockSpec((bsz, tn), lambda j: (0, j)),
        out_shape=jax.ShapeDtypeStruct((bsz, n), F32),
        compiler_params=pltpu.CompilerParams(dimension_semantics=("arbitrary",),
                                             vmem_limit_bytes=VMEM_LIMIT),
        name="mod",
    )(c, w_ada, b_ada.reshape(1, n))


def _wprep_kernel(w_ref, o_ref, *, d_model):
    main = RW_COLS + GLA_MAIN
    o_ref[:main, :] = w_ref[:main, :].astype(BF16)
    o_ref[main:main + 2 * d_model, :] = w_ref[RW_COLS + GLA_COLS:, :].astype(BF16)
    o_ref[main + 2 * d_model:main + 2 * d_model + GLA_GATE_LORA, :] = (
        w_ref[main:RW_COLS + GLA_COLS, :].astype(BF16))
    o_ref[main + 2 * d_model + GLA_GATE_LORA:, :] = jnp.zeros(
        (LANES - GLA_GATE_LORA, o_ref.shape[1]), BF16)


def _wprep_call(w_in_t, d_model):
    n_in = w_in_t.shape[0]
    n_out = RW_COLS + GLA_MAIN + 2 * d_model + LANES
    cols = WPREP_COLS
    assert d_model % cols == 0
    return pl.pallas_call(
        functools.partial(_wprep_kernel, d_model=d_model),
        grid=(d_model // cols,),
        in_specs=[pl.BlockSpec((n_in, cols), lambda i: (0, i))],
        out_specs=pl.BlockSpec((n_out, cols), lambda i: (0, i)),
        out_shape=jax.ShapeDtypeStruct((n_out, d_model), BF16),
        compiler_params=pltpu.CompilerParams(dimension_semantics=("arbitrary",),
                                             vmem_limit_bytes=VMEM_LIMIT),
        name="wprep",
    )(w_in_t)


def _inproj_kernel(x_ref, shift_ref, scale_ref, wp_ref, mu_ref, w0_ref, w2_ref, a0_ref, a2_ref,
                   g2_ref, kk_ref, ka_ref, rk_ref, ga2_ref, gab_ref, bd_ref,
                   r_out, k_out, v_out, lw_out, nkk_out, b_out, g_out, bonus_out,
                   gq_out, gk_out, gv_out, gsg_out, gla_out, sgr_out, sgg_out,
                   carry_ref, *, tm, d_model):
    t = pl.program_id(1)

    @pl.when(t == 0)
    def _():
        carry_ref[...] = jnp.zeros_like(carry_ref)

    u = (x_ref[...] * (1.0 + scale_ref[...]) + shift_ref[...]).astype(BF16)
    p = _dot_nt(u, wp_ref[COL_RW:COL_RW + RW_COLS, :])
    pg = _dot_nt(u, wp_ref[COL_GLA:COL_GLA + GLA_MAIN, :])

    prev = pltpu.roll(p, 1, 0)
    row = lax.broadcasted_iota(jnp.int32, p.shape, 0)
    prev = jnp.where(row == 0, carry_ref[...], prev)
    carry_ref[...] = p[tm - 1:tm, :]
    ps = p + mu_ref[...] * (prev - p)

    r = ps[:, 0:RW_WIDTH]
    k = ps[:, RW_WIDTH:2 * RW_WIDTH]
    v = ps[:, 2 * RW_WIDTH:3 * RW_WIDTH]
    z = ps[:, 3 * RW_WIDTH:3 * RW_WIDTH + LANES]
    gd = ps[:, 3 * RW_WIDTH + LANES:RW_COLS]

    w_lora = _dot(jnp.tanh(z), w2_ref[...])
    a_lora = _dot(z, a2_ref[...])
    g_out[...] = _dot(_sigmoid(gd), g2_ref[...])
    gates = _dot_nt(u, wp_ref[COL_GATE:, :])

    w = -_softplus(-(w0_ref[...] + w_lora)) - 0.5
    lw_out[...] = -jnp.exp(w)
    a = _sigmoid(a0_ref[...] + a_lora)
    kkv = k * kk_ref[...]
    kp = k * (1.0 + (a - 1.0) * ka_ref[...])
    rkr = r * kp * rk_ref[...]
    bd = bd_ref[...]
    for j in range(RW_WIDTH // LANES):
        sl = slice(j * LANES, (j + 1) * LANES)
        kkj = kkv[:, sl]
        ssq = _dot_exact_rhs(kkj * kkj, bd)
        kkn = kkj / jnp.maximum(jnp.sqrt(ssq), 1e-12)
        nkk_out[:, sl] = -kkn
        b_out[:, sl] = kkn * a[:, sl]
        bonus_out[:, sl] = _dot_exact_rhs(rkr[:, sl], bd) * v[:, sl]
    r_out[...] = r
    k_out[...] = kp
    v_out[...] = v.astype(BF16)

    gq_out[...] = pg[:, 0:GLA_KW] * (GLA_DK ** -0.5)
    gk_out[...] = pg[:, GLA_KW:2 * GLA_KW]
    gv_out[...] = pg[:, 2 * GLA_KW:2 * GLA_KW + GLA_VW].astype(BF16)
    gg = pg[:, 2 * GLA_KW + GLA_VW:GLA_MAIN]
    gsg_out[...] = gg * _sigmoid(gg)
    adg = gates[:, 2 * d_model:2 * d_model + LANES]
    la = _dot(adg, ga2_ref[...]) + gab_ref[...]
    gla_out[...] = -_softplus(-la) / GLA_TAU

    sgr_out[...] = gates[:, :d_model]
    sgg_out[...] = gates[:, d_model:2 * d_model]


def _inproj_call(x2, mod3, wp, small, bsz, seq, d_model):
    tm = INPROJ_TILE
    nt = seq // tm
    n_tok = bsz * seq
    (mu, w0, w2p, a0, a2p, g2, k_k, k_a, r_k, ga2p, gab, bd) = small

    def tok(n):
        return pl.BlockSpec((tm, n), lambda b, t: (b * nt + t, 0))

    def full(a):
        return pl.BlockSpec(a.shape, lambda b, t: (0,) * a.ndim, pipeline_mode=pl.Buffered(1))

    def modspec(idx):
        return pl.BlockSpec((None, 1, d_model), lambda b, t: (b, 0, idx))

    outs = [
        (RW_WIDTH, F32), (RW_WIDTH, F32), (RW_WIDTH, BF16), (RW_WIDTH, F32), (RW_WIDTH, F32),
        (RW_WIDTH, F32), (RW_WIDTH, F32), (RW_WIDTH, F32),
        (GLA_KW, F32), (GLA_KW, F32), (GLA_VW, BF16), (GLA_VW, F32), (GLA_KW, F32),
        (d_model, F32), (d_model, F32),
    ]
    return pl.pallas_call(
        functools.partial(_inproj_kernel, tm=tm, d_model=d_model),
        grid=(bsz, nt),
        in_specs=[tok(d_model), modspec(0), modspec(1), full(wp), full(mu), full(w0), full(w2p),
                  full(a0), full(a2p), full(g2), full(k_k), full(k_a), full(r_k), full(ga2p),
                  full(gab), full(bd)],
        out_specs=[tok(n) for n, _ in outs],
        out_shape=[jax.ShapeDtypeStruct((n_tok, n), dt) for n, dt in outs],
        scratch_shapes=[pltpu.VMEM((1, RW_COLS), F32)],
        compiler_params=pltpu.CompilerParams(dimension_semantics=("arbitrary", "arbitrary"),
                                             vmem_limit_bytes=VMEM_LIMIT),
        name="inproj",
    )(x2, mod3, mod3, wp, mu, w0, w2p, a0, a2p, g2, k_k, k_a, r_k, ga2p, gab, bd)


def _rwkv_kernel(*refs, nb, tb, unroll, ncast):
    (r_ref, k_ref, v_ref, lw_ref, nkk_ref, b_ref, g_ref, bonus_ref, gng_ref, gnb_ref,
     bd_ref) = refs[:11]
    cast_in = refs[11:11 + ncast]
    o_ref = refs[11 + ncast]
    cast_out = refs[12 + ncast:12 + 2 * ncast]
    h_ref, rm_s, yl_s, hl_s = refs[12 + 2 * ncast:]
    for w_in_ref, w_out_ref in zip(cast_in, cast_out):
        w_out_ref[...] = w_in_ref[...].astype(BF16)

    c = CHUNK
    gw = RW_GROUP * RW_HEAD
    nchunk = tb // c
    ngroup = RW_WIDTH // gw
    t = pl.program_id(1)

    @pl.when(t == 0)
    def _():
        h_ref[...] = jnp.zeros_like(h_ref)

    lane = lax.broadcasted_iota(jnp.int32, (c, gw), 1)
    rowi = lax.broadcasted_iota(jnp.int32, (c, gw), 0)
    head_of_lane = lane // RW_HEAD
    lmask = [head_of_lane == h for h in range(RW_GROUP)]
    strict = (lane % RW_HEAD) < rowi
    incl = (lane % RW_HEAD) <= rowi
    eye = (lane % RW_HEAD) == rowi
    eye_f = eye.astype(F32)
    assert c == RW_HEAD

    def stack(x):
        xb = x.astype(BF16)
        zero = jnp.zeros_like(xb)
        return jnp.concatenate([jnp.where(lmask[h], xb, zero) for h in range(RW_GROUP)], axis=0)

    def block_t(x):
        xt = x.T
        return jnp.concatenate([xt[h * RW_HEAD:(h + 1) * RW_HEAD, :] for h in range(RW_GROUP)],
                               axis=1)

    def local_group(gi, carry):
        units = [(e, u, g) for e in range(nb) for u in range(unroll) for g in range(ngroup)]
        n = len(units)
        idxs = [gi * unroll + u for u in range(unroll)]
        rows = [pl.ds(pl.multiple_of(i * c, c), c) for i in idxs]

        def ld(ref, e, u, g):
            return ref[e, rows[u], g * gw:(g + 1) * gw]

        lw = [ld(lw_ref, e, u, g) for e, u, g in units]
        gcum = [_cumsum_rows(x) for x in lw]
        gend, rt, at, kt, bt, bh, kh, vst = [], [], [], [], [], [], [], []
        for s, (e, u, g) in enumerate(units):
            g_ = gcum[s]
            ge = g_[c - 1:c, :]
            r = ld(r_ref, e, u, g)
            k = ld(k_ref, e, u, g)
            nkk = ld(nkk_ref, e, u, g)
            b = ld(b_ref, e, u, g)
            e_neg = jnp.exp(-g_)
            dk = jnp.exp(ge - g_)
            gend.append(ge)
            rt.append(r * jnp.exp(g_))
            at.append(nkk * jnp.exp(g_ - lw[s]))
            kt.append(k * e_neg)
            bt.append(b * e_neg)
            bh.append(b * dk)
            kh.append(k * dk)
            vst.append(stack(ld(v_ref, e, u, g)))

        x = [_dot_nt(jnp.concatenate([at[s], rt[s]], axis=0),
                     jnp.concatenate([stack(bt[s]), stack(kt[s])], axis=0)) for s in range(n)]
        a_ab = [jnp.where(strict, y[:c, :gw], 0.0) for y in x]
        a_ak = [jnp.where(strict, y[:c, gw:], 0.0) for y in x]
        a_rb = [jnp.where(incl, y[c:, :gw], 0.0) for y in x]
        a_rk = [jnp.where(incl, y[c:, gw:], 0.0) for y in x]
        bht = [block_t(y) for y in bh]
        kht = [block_t(y) for y in kh]

        tp = [eye_f + a for a in a_ab]
        pw = [_dot(a, stack(a)) for a in a_ab]
        for _ in range(4):
            res = [_dot(jnp.concatenate([pw[s], tp[s]], axis=0), stack(pw[s])) for s in range(n)]
            tp = [tp[s] + res[s][c:] for s in range(n)]
            pw = [res[s][:c] for s in range(n)]
        tinv = [tp[s] + _dot(tp[s], stack(pw[s])) for s in range(n)]

        vres = [_dot(jnp.concatenate([a_ak[s], a_rk[s], kht[s]], axis=0), vst[s]) for s in range(n)]
        akv = [y[:c] for y in vres]
        rk_v = [y[c:2 * c] for y in vres]
        h2 = [y[2 * c:] for y in vres]
        tres = [_dot(tinv[s], jnp.concatenate([stack(at[s]), stack(akv[s])], axis=1))
                for s in range(n)]
        ap = [y[:, :gw] for y in tres]
        uloc = [y[:, gw:] for y in tres]
        fres = [_dot(jnp.concatenate([a_rb[s], bht[s]], axis=0),
                     jnp.concatenate([stack(ap[s]), stack(uloc[s])], axis=1)) for s in range(n)]
        for s, (e, u, g) in enumerate(units):
            y = fres[s]
            rm_s[e, idxs[u], g, :c, :] = rt[s] + y[:c, :gw]
            rm_s[e, idxs[u], g, c:, :] = jnp.where(eye, jnp.exp(gend[s]), 0.0) + y[c:, :gw]
            yl_s[e, rows[u], g * gw:(g + 1) * gw] = y[:c, gw:] + rk_v[s]
            hl_s[e, idxs[u], g] = y[c:, gw:] + h2[s]
        return carry

    lax.fori_loop(0, nchunk // unroll, local_group, 0)

    chains = [(e, g) for e in range(nb) for g in range(ngroup)]
    hstate = [h_ref[e, g] for e, g in chains]
    for ci in range(nchunk):
        rows = slice(ci * c, (ci + 1) * c)
        res = [_dot(rm_s[e, ci, g], stack(hstate[j])) for j, (e, g) in enumerate(chains)]
        for j, (e, g) in enumerate(chains):
            lanes = slice(g * gw, (g + 1) * gw)
            yl_s[e, rows, lanes] = res[j][:c] + yl_s[e, rows, lanes]
            hstate[j] = res[j][c:] + hl_s[e, ci, g]
    for j, (e, g) in enumerate(chains):
        h_ref[e, g] = hstate[j]

    bd = bd_ref[...]
    for p in range(RW_WIDTH // LANES):
        lanes = slice(p * LANES, (p + 1) * LANES)
        y = yl_s[:, :, lanes].reshape(nb * tb, LANES)
        mean = _dot_exact_rhs(y, bd) * (1.0 / RW_HEAD)
        yc = y - mean
        var = _dot_exact_rhs(yc * yc, bd) * (1.0 / RW_HEAD)
        yn = yc * lax.rsqrt(var + RW_GN_EPS)
        bonus = bonus_ref[:, :, lanes].reshape(nb * tb, LANES)
        gate = g_ref[:, :, lanes].reshape(nb * tb, LANES)
        out = (yn * gng_ref[:, lanes] + gnb_ref[:, lanes] + bonus) * gate
        o_ref[:, :, lanes] = out.reshape(nb, tb, LANES).astype(o_ref.dtype)


def _rwkv_call(r, k, v, lw, nkk, b, g, bonus, gng, gnb, bd, weights, bsz, seq):
    tb = RWKV_TILE
    nb = RWKV_BATCH
    nt = seq // tb
    gw = RW_GROUP * RW_HEAD
    ngroup = RW_WIDTH // gw
    nchunk = tb // CHUNK
    nsteps = (bsz // nb) * nt
    assert bsz % nb == 0
    assert all(w.shape[0] % (16 * nsteps) == 0 for w in weights)

    def tok():
        return pl.BlockSpec((nb, tb, RW_WIDTH), lambda bb, t: (bb, t, 0))

    def full(a):
        return pl.BlockSpec(a.shape, lambda bb, t: (0,) * a.ndim)

    def wblock(w):
        return pl.BlockSpec((w.shape[0] // nsteps, w.shape[1]), lambda bb, t: (bb * nt + t, 0))

    per_seq = lambda a: a.reshape(bsz, seq, RW_WIDTH)
    out, *cast = pl.pallas_call(
        functools.partial(_rwkv_kernel, nb=nb, tb=tb, unroll=RWKV_UNROLL, ncast=len(weights)),
        grid=(bsz // nb, nt),
        in_specs=([tok() for _ in range(8)] + [full(gng), full(gnb), full(bd)]
                  + [wblock(w) for w in weights]),
        out_specs=[tok()] + [wblock(w) for w in weights],
        out_shape=([jax.ShapeDtypeStruct((bsz, seq, RW_WIDTH), BF16)]
                   + [jax.ShapeDtypeStruct(w.shape, BF16) for w in weights]),
        scratch_shapes=[pltpu.VMEM((nb, ngroup, RW_HEAD, gw), F32),
                        pltpu.VMEM((nb, nchunk, ngroup, CHUNK + RW_HEAD, gw), F32),
                        pltpu.VMEM((nb, tb, RW_WIDTH), F32),
                        pltpu.VMEM((nb, nchunk, ngroup, RW_HEAD, gw), F32)],
        compiler_params=pltpu.CompilerParams(
            dimension_semantics=("arbitrary", "arbitrary"),
            vmem_limit_bytes=VMEM_LIMIT),
        name="rwkv",
    )(*[per_seq(a) for a in (r, k, v, lw, nkk, b, g, bonus)], gng, gnb, bd, *weights)
    return out.reshape(bsz * seq, RW_WIDTH), cast


def _gla_kernel(q_ref, k_ref, v_ref, sg_ref, la_ref, ng_ref, o_ref, st_ref, qs_s, oi_s, kv_s, eb_s,
                *, tb, unroll):
    c = CHUNK
    nchunk = tb // c
    t = pl.program_id(1)

    @pl.when(t == 0)
    def _():
        st_ref[...] = jnp.zeros_like(st_ref)

    lane = lax.broadcasted_iota(jnp.int32, (1, LANES), 1)
    hmask = ((lane < GLA_DK).astype(F32), (lane >= GLA_DK).astype(F32))
    ri = lax.broadcasted_iota(jnp.int32, (c, c), 0)
    ci_ = lax.broadcasted_iota(jnp.int32, (c, c), 1)
    causal = ci_ <= ri
    ltri = causal.astype(BF16)

    def local_group(gi, carry):
        units = [(u, h) for u in range(unroll) for h in range(GLA_HEADS)]
        idxs = [gi * unroll + u for u in range(unroll)]
        rows = [pl.ds(pl.multiple_of(i * c, c), c) for i in idxs]

        def qk_lanes(h):
            return slice((h // 2) * LANES, (h // 2 + 1) * LANES)

        la = [la_ref[rows[u], qk_lanes(h)] for u, h in units]
        bcum = [_dot_exact_lhs(ltri, x) for x in la]
        q_s, k_s, kdec, v = [], [], [], []
        for s, (u, h) in enumerate(units):
            bc = bcum[s]
            blast = bc[c - 1:c, :]
            q = q_ref[rows[u], qk_lanes(h)]
            k = k_ref[rows[u], qk_lanes(h)]
            hm = hmask[h % 2]
            q_s.append((q * jnp.exp(bc) * hm).astype(BF16))
            k_s.append(k * jnp.exp(-bc))
            kdec.append(k * jnp.exp(blast - bc) * hm)
            v.append(v_ref[rows[u], h * LANES:(h + 1) * LANES])
            eb_s[idxs[u], h] = jnp.exp(blast)
        att = [jnp.where(causal, _dot_nt(q_s[s], k_s[s]), 0.0) for s in range(len(units))]
        oi = [_dot(att[s], v[s]) for s in range(len(units))]
        v_t = [x.astype(F32).T.astype(BF16) for x in v]
        kv = [_dot(v_t[s], kdec[s]) for s in range(len(units))]
        for s, (u, h) in enumerate(units):
            qs_s[idxs[u], h] = q_s[s]
            oi_s[rows[u], h * LANES:(h + 1) * LANES] = oi[s]
            kv_s[idxs[u], h] = kv[s]
        return carry

    lax.fori_loop(0, nchunk // unroll, local_group, 0)

    st = [st_ref[h] for h in range(GLA_HEADS)]
    for ci in range(nchunk):
        rows = slice(ci * c, (ci + 1) * c)
        inter = [_dot_nt(qs_s[ci, h], st[h]) for h in range(GLA_HEADS)]
        for h in range(GLA_HEADS):
            lanes = slice(h * LANES, (h + 1) * LANES)
            oi_s[rows, lanes] = oi_s[rows, lanes] + inter[h]
            st[h] = st[h] * eb_s[ci, h] + kv_s[ci, h]
    for h in range(GLA_HEADS):
        st_ref[h] = st[h]

    ng = ng_ref[...]
    for h in range(GLA_HEADS):
        lanes = slice(h * LANES, (h + 1) * LANES)
        o = oi_s[:, lanes]
        o = o * lax.rsqrt(jnp.mean(o * o, axis=-1, keepdims=True) + GLA_NORM_EPS) * ng
        o_ref[:, lanes] = (o * sg_ref[:, lanes]).astype(o_ref.dtype)


def _gla_call(q, k, v, sg, la, ng, bsz, seq):
    tb = GLA_TILE
    nt = seq // tb
    n_tok = bsz * seq
    nchunk = tb // CHUNK

    def tok(n):
        return pl.BlockSpec((tb, n), lambda bb, t: (bb * nt + t, 0))

    return pl.pallas_call(
        functools.partial(_gla_kernel, tb=tb, unroll=GLA_UNROLL),
        grid=(bsz, nt),
        in_specs=[tok(GLA_KW), tok(GLA_KW), tok(GLA_VW), tok(GLA_VW), tok(GLA_KW),
                  pl.BlockSpec((1, LANES), lambda bb, t: (0, 0))],
        out_specs=tok(GLA_VW),
        out_shape=jax.ShapeDtypeStruct((n_tok, GLA_VW), BF16),
        scratch_shapes=[pltpu.VMEM((GLA_HEADS, GLA_DV, LANES), F32),
                        pltpu.VMEM((nchunk, GLA_HEADS, CHUNK, LANES), BF16),
                        pltpu.VMEM((tb, GLA_VW), F32),
                        pltpu.VMEM((nchunk, GLA_HEADS, GLA_DV, LANES), F32),
                        pltpu.VMEM((nchunk, GLA_HEADS, 1, LANES), F32)],
        compiler_params=pltpu.CompilerParams(
            dimension_semantics=("arbitrary", "arbitrary"),
            vmem_limit_bytes=VMEM_LIMIT),
        name="gla",
    )(q, k, v, sg, la, ng)


def _tail_kernel(x_ref, orw_ref, ogla_ref, sgr_ref, sgg_ref, gate1_ref, shift2_ref, scale2_ref,
                 gate2_ref, wrb_ref, wgb_ref, wmix_ref, win_ref, wout_ref, ln1g_ref, ln1b_ref,
                 ln2g_ref, ln2b_ref, o_ref, *, d_ff):
    hrows = x_ref.shape[0] // TAIL_SPLIT
    rs = [slice(i * hrows, (i + 1) * hrows) for i in range(TAIL_SPLIT)]
    nsp = range(TAIL_SPLIT)
    b_rw = [jnp.dot(orw_ref[s, :], wrb_ref[...], preferred_element_type=F32) for s in rs]
    b_gla = [jnp.dot(ogla_ref[s, :], wgb_ref[...], preferred_element_type=F32) for s in rs]
    mix = [_dot(_sigmoid(sgr_ref[rs[i], :]) * b_rw[i] + _sigmoid(sgg_ref[rs[i], :]) * b_gla[i],
                wmix_ref[...]) for i in nsp]
    x1 = [_layer_norm(ALPHA * x_ref[rs[i], :] + gate1_ref[...] * mix[i], LN_EPS) * ln1g_ref[...]
          + ln1b_ref[...] for i in nsp]
    u = [(x1[i] * (1.0 + scale2_ref[...]) + shift2_ref[...]).astype(BF16) for i in nsp]
    ffn = [jnp.zeros_like(x1[i]) for i in nsp]
    ntile = d_ff // MXU_K
    bounds = [MXU_K * ((ntile * j) // FFN_CHUNKS) for j in range(FFN_CHUNKS)] + [d_ff]
    for lo, hi in zip(bounds[:-1], bounds[1:]):
        hg, hu = [], []
        for i in nsp:
            hg.append(jnp.dot(u[i], win_ref[:, lo:hi], preferred_element_type=F32))
            hu.append(jnp.dot(u[i], win_ref[:, d_ff + lo:d_ff + hi], preferred_element_type=F32))
        act = [hg[i] * _sigmoid(hg[i]) * hu[i] for i in nsp]
        ffn = [ffn[i] + _dot(act[i], wout_ref[lo:hi, :]) for i in nsp]
    for i in nsp:
        y = _layer_norm(ALPHA * x1[i] + gate2_ref[...] * ffn[i], LN_EPS)
        o_ref[rs[i], :] = y * ln2g_ref[...] + ln2b_ref[...]


def _tail_call(x2, o_rw, o_gla, sgr, sgg, mod3, wrb, wgb, wmix, win, wout, ln1g, ln1b, ln2g, ln2b,
               bsz, seq, d_model):
    tm = TAIL_TILE
    nt = seq // tm
    n_tok = bsz * seq
    d_ff = wout.shape[0]
    assert d_ff % MXU_K == 0

    def tok(n):
        return pl.BlockSpec((tm, n), lambda b, t: (b * nt + t, 0))

    def modspec(idx):
        return pl.BlockSpec((1, 1, d_model), lambda b, t: (b, 0, idx))

    def outer(x_hbm, orw_hbm, ogla_hbm, sgr_hbm, sgg_hbm, mod_hbm, wrb_ref, wgb_ref, wmix_ref,
              win_ref, wout_ref, ln1g_ref, ln1b_ref, ln2g_ref, ln2b_ref, o_hbm):
        def body(x_ref, orw_ref, ogla_ref, sgr_ref, sgg_ref, g1, sh2, sc2, g2, o_ref):
            _tail_kernel(x_ref, orw_ref, ogla_ref, sgr_ref, sgg_ref,
                         g1.at[0], sh2.at[0], sc2.at[0], g2.at[0],
                         wrb_ref, wgb_ref, wmix_ref, win_ref, wout_ref,
                         ln1g_ref, ln1b_ref, ln2g_ref, ln2b_ref, o_ref, d_ff=d_ff)

        pltpu.emit_pipeline(
            body, grid=(bsz, nt),
            in_specs=[tok(d_model), tok(RW_WIDTH), tok(GLA_VW), tok(d_model), tok(d_model),
                      modspec(2), modspec(3), modspec(4), modspec(5)],
            out_specs=[tok(d_model)],
        )(x_hbm, orw_hbm, ogla_hbm, sgr_hbm, sgg_hbm, mod_hbm, mod_hbm, mod_hbm, mod_hbm, o_hbm)

    hbm = pl.BlockSpec(memory_space=pl.ANY)
    vmem = pl.BlockSpec(memory_space=pltpu.VMEM)
    return pl.pallas_call(
        outer,
        in_specs=[hbm] * 6 + [vmem] * 9,
        out_specs=hbm,
        out_shape=jax.ShapeDtypeStruct((n_tok, d_model), F32),
        compiler_params=pltpu.CompilerParams(vmem_limit_bytes=VMEM_LIMIT),
        name="tail",
    )(x2, o_rw, o_gla, sgr, sgg, mod3, wrb, wgb, wmix, win, wout, ln1g, ln1b, ln2g, ln2b)


def kernel(x, c, w_ada, b_ada, w_in, mu_rw, rw_w0, rw_w2, rw_a0, rw_a2, rw_g2, rw_k_k, rw_k_a,
           rw_r_k, rw_gn_g, rw_gn_b, gla_a2, gla_a_b, gla_norm_g, w_rw_branch, w_gla_branch,
           w_mix_out, ln1_g, ln1_b, w_ffn_in, w_ffn_out, ln2_g, ln2_b):
    bsz, seq, d_model = x.shape
    assert w_ada.shape[0] == DEPTH
    assert all(seq % tile == 0 for tile in (INPROJ_TILE, RWKV_TILE, GLA_TILE, TAIL_TILE))
    n_tok = bsz * seq
    l = 0

    wp = _wprep_call(jnp.swapaxes(w_in[l], 0, 1), d_model)
    zeros_lora = jnp.zeros((DECAY_LORA, RW_WIDTH), F32)
    w2p = jnp.concatenate([rw_w2[l], zeros_lora], axis=0).astype(BF16)
    a2p = jnp.concatenate([zeros_lora, rw_a2[l]], axis=0).astype(BF16)
    ga2p = jnp.concatenate(
        [gla_a2[l], jnp.zeros((LANES - GLA_GATE_LORA, GLA_KW), F32)], axis=0).astype(BF16)
    row = lambda a: a.reshape(1, -1)
    hid = jnp.arange(LANES) // RW_HEAD
    bd = (hid[:, None] == hid[None, :]).astype(BF16)
    small = (row(mu_rw[l]), row(rw_w0[l]), w2p, row(rw_a0[l]), a2p, rw_g2[l].astype(BF16),
             row(rw_k_k[l]), row(rw_k_a[l]), row(rw_r_k[l]), ga2p, row(gla_a_b[l]), bd)

    x2 = x.reshape(n_tok, d_model)
    mod = _mod_call(c, w_ada[l], b_ada[l])
    mod3 = mod.reshape(bsz, 1, 6 * d_model)

    (r, k, v, lw, nkk, b, g, bonus, gq, gk, gv, gsg, gla, sgr, sgg) = _inproj_call(
        x2, mod3, wp, small, bsz, seq, d_model)

    o_rw, (wrb, wgb, wmix, win, wout) = _rwkv_call(
        r, k, v, lw, nkk, b, g, bonus, row(rw_gn_g[l]), row(rw_gn_b[l]), bd,
        (w_rw_branch[l], w_gla_branch[l], w_mix_out[l], w_ffn_in[l], w_ffn_out[l]), bsz, seq)
    o_gla = _gla_call(gq, gk, gv, gsg, gla, row(gla_norm_g[l]), bsz, seq)

    out = _tail_call(x2, o_rw, o_gla, sgr, sgg, mod3, wrb, wgb, wmix, win, wout,
                     row(ln1_g[l]), row(ln1_b[l]), row(ln2_g[l]), row(ln2_b[l]), bsz, seq, d_model)
    return out.reshape(bsz, seq, d_model)
```

```python
import functools

import jax
import jax.numpy as jnp
from jax import lax
from jax.experimental import pallas as pl
from jax.experimental.pallas import tpu as pltpu

F32 = jnp.float32
BF16 = jnp.bfloat16

RW_HEAD = 64
RW_HEADS = 8
RW_WIDTH = RW_HEADS * RW_HEAD
DECAY_LORA = 64
AAA_LORA = 64
GATE_LORA = 128
RW_GN_EPS = 64e-5
RW_COLS = 3 * RW_WIDTH + DECAY_LORA + AAA_LORA + GATE_LORA
GLA_HEADS = 4
GLA_DK = 64
GLA_DV = 128
GLA_KW = GLA_HEADS * GLA_DK
GLA_VW = GLA_HEADS * GLA_DV
GLA_GATE_LORA = 16
GLA_TAU = 16.0
GLA_NORM_EPS = 1e-5
GLA_MAIN = 2 * GLA_KW + 2 * GLA_VW
GLA_COLS = GLA_MAIN + GLA_GATE_LORA
LN_EPS = 1e-5
DEPTH = 1
ALPHA = (2.0 * DEPTH) ** 0.25

LANES = 128
MXU_K = 256
CHUNK = 64
MOD_TILE = 1536
WPREP_COLS = 256
INPROJ_TILE = 512
RWKV_UNROLL = 1
RW_GROUP = 2
GLA_UNROLL = 8
RWKV_TILE = 256
RWKV_BATCH = 4
GLA_TILE = 2048
TAIL_TILE = 512
TAIL_SPLIT = 2
FFN_CHUNKS = 2
VMEM_LIMIT = 58 * 1024 * 1024

COL_RW = 0
COL_GLA = RW_COLS
COL_GATE = COL_GLA + GLA_MAIN


def _sigmoid(x):
    return 1.0 / (1.0 + jnp.exp(-x))


def _softplus(x):
    return jnp.maximum(x, 0.0) + jnp.log1p(jnp.exp(-jnp.abs(x)))


def _dot(a, b):
    return jnp.dot(a.astype(BF16), b.astype(BF16), preferred_element_type=F32)


def _dot_nt(a, b):
    return lax.dot_general(a.astype(BF16), b.astype(BF16), (((1,), (1,)), ((), ())),
                           preferred_element_type=F32)


def _split_hi_lo(x):
    hi = x.astype(BF16)
    lo = (x - hi.astype(F32)).astype(BF16)
    return hi, lo


def _dot_exact_rhs(x, w01):
    hi, lo = _split_hi_lo(x)
    if 2 * x.shape[1] <= MXU_K:
        return jnp.dot(jnp.concatenate([hi, lo], axis=1), jnp.concatenate([w01, w01], axis=0),
                       preferred_element_type=F32)
    return (jnp.dot(hi, w01, preferred_element_type=F32)
            + jnp.dot(lo, w01, preferred_element_type=F32))


def _dot_exact_lhs(w01, x):
    hi, lo = _split_hi_lo(x)
    if 2 * x.shape[0] <= MXU_K:
        return jnp.dot(jnp.concatenate([w01, w01], axis=1), jnp.concatenate([hi, lo], axis=0),
                       preferred_element_type=F32)
    return (jnp.dot(w01, hi, preferred_element_type=F32)
            + jnp.dot(w01, lo, preferred_element_type=F32))


def _cumsum_rows(x):
    row = lax.broadcasted_iota(jnp.int32, x.shape, 0)
    shift = 1
    while shift < x.shape[0]:
        x = x + jnp.where(row >= shift, pltpu.roll(x, shift, 0), 0.0)
        shift *= 2
    return x


def _layer_norm(x, eps):
    mu = jnp.mean(x, axis=-1, keepdims=True)
    xc = x - mu
    var = jnp.mean(xc * xc, axis=-1, keepdims=True)
    return xc * lax.rsqrt(var + eps)


def _mod_kernel(c_ref, w_ref, b_ref, o_ref):
    c = c_ref[...]
    sc = c * _sigmoid(c)
    o_ref[...] = _dot(sc, w_ref[...]) + b_ref[...]


def _mod_call(c, w_ada, b_ada):
    bsz, d = c.shape
    n = w_ada.shape[1]
    tn = MOD_TILE
    assert n % tn == 0
    return pl.pallas_call(
        _mod_kernel,
        grid=(n // tn,),
        in_specs=[pl.BlockSpec((bsz, d), lambda j: (0, 0)),
                  pl.BlockSpec((d, tn), lambda j: (0, j)),
                  pl.BlockSpec((1, tn), lambda j: (0, j))],
        out_specs=pl.BlockSpec((bsz, tn), lambda j: (0, j)),
        out_shape=jax.ShapeDtypeStruct((bsz, n), F32),
        compiler_params=pltpu.CompilerParams(dimension_semantics=("arbitrary",),
                                             vmem_limit_bytes=VMEM_LIMIT),
        name="mod",
    )(c, w_ada, b_ada.reshape(1, n))


def _wprep_kernel(w_ref, o_ref, *, d_model):
    main = RW_COLS + GLA_MAIN
    o_ref[:main, :] = w_ref[:main, :].astype(BF16)
    o_ref[main:main + 2 * d_model, :] = w_ref[RW_COLS + GLA_COLS:, :].astype(BF16)
    o_ref[main + 2 * d_model:main + 2 * d_model + GLA_GATE_LORA, :] = (
        w_ref[main:RW_COLS + GLA_COLS, :].astype(BF16))
    o_ref[main + 2 * d_model + GLA_GATE_LORA:, :] = jnp.zeros(
        (LANES - GLA_GATE_LORA, o_ref.shape[1]), BF16)


def _wprep_call(w_in_t, d_model):
    n_in = w_in_t.shape[0]
    n_out = RW_COLS + GLA_MAIN + 2 * d_model + LANES
    cols = WPREP_COLS
    assert d_model % cols == 0
    return pl.pallas_call(
        functools.partial(_wprep_kernel, d_model=d_model),
        grid=(d_model // cols,),
        in_specs=[pl.BlockSpec((n_in, cols), lambda i: (0, i))],
        out_specs=pl.BlockSpec((n_out, cols), lambda i: (0, i)),
        out_shape=jax.ShapeDtypeStruct((n_out, d_model), BF16),
        compiler_params=pltpu.CompilerParams(dimension_semantics=("arbitrary",),
                                             vmem_limit_bytes=VMEM_LIMIT),
        name="wprep",
    )(w_in_t)


def _inproj_kernel(x_ref, shift_ref, scale_ref, wp_ref, mu_ref, w0_ref, w2_ref, a0_ref, a2_ref,
                   g2_ref, kk_ref, ka_ref, rk_ref, ga2_ref, gab_ref, bd_ref,
                   r_out, k_out, v_out, lw_out, nkk_out, b_out, g_out, bonus_out,
                   gq_out, gk_out, gv_out, gsg_out, gla_out, sgr_out, sgg_out,
                   carry_ref, *, tm, d_model):
    t = pl.program_id(1)

    @pl.when(t == 0)
    def _():
        carry_ref[...] = jnp.zeros_like(carry_ref)

    u = (x_ref[...] * (1.0 + scale_ref[...]) + shift_ref[...]).astype(BF16)
    p = _dot_nt(u, wp_ref[COL_RW:COL_RW + RW_COLS, :])
    pg = _dot_nt(u, wp_ref[COL_GLA:COL_GLA + GLA_MAIN, :])

    prev = pltpu.roll(p, 1, 0)
    row = lax.broadcasted_iota(jnp.int32, p.shape, 0)
    prev = jnp.where(row == 0, carry_ref[...], prev)
    carry_ref[...] = p[tm - 1:tm, :]
    ps = p + mu_ref[...] * (prev - p)

    r = ps[:, 0:RW_WIDTH]
    k = ps[:, RW_WIDTH:2 * RW_WIDTH]
    v = ps[:, 2 * RW_WIDTH:3 * RW_WIDTH]
    z = ps[:, 3 * RW_WIDTH:3 * RW_WIDTH + LANES]
    gd = ps[:, 3 * RW_WIDTH + LANES:RW_COLS]

    w_lora = _dot(jnp.tanh(z), w2_ref[...])
    a_lora = _dot(z, a2_ref[...])
    g_out[...] = _dot(_sigmoid(gd), g2_ref[...])
    gates = _dot_nt(u, wp_ref[COL_GATE:, :])

    w = -_softplus(-(w0_ref[...] + w_lora)) - 0.5
    lw_out[...] = -jnp.exp(w)
    a = _sigmoid(a0_ref[...] + a_lora)
    kkv = k * kk_ref[...]
    kp = k * (1.0 + (a - 1.0) * ka_ref[...])
    rkr = r * kp * rk_ref[...]
    bd = bd_ref[...]
    for j in range(RW_WIDTH // LANES):
        sl = slice(j * LANES, (j + 1) * LANES)
        kkj = kkv[:, sl]
        ssq = _dot_exact_rhs(kkj * kkj, bd)
        kkn = kkj / jnp.maximum(jnp.sqrt(ssq), 1e-12)
        nkk_out[:, sl] = -kkn
        b_out[:, sl] = kkn * a[:, sl]
        bonus_out[:, sl] = _dot_exact_rhs(rkr[:, sl], bd) * v[:, sl]
    r_out[...] = r
    k_out[...] = kp
    v_out[...] = v.astype(BF16)

    gq_out[...] = pg[:, 0:GLA_KW] * (GLA_DK ** -0.5)
    gk_out[...] = pg[:, GLA_KW:2 * GLA_KW]
    gv_out[...] = pg[:, 2 * GLA_KW:2 * GLA_KW + GLA_VW].astype(BF16)
    gg = pg[:, 2 * GLA_KW + GLA_VW:GLA_MAIN]
    gsg_out[...] = gg * _sigmoid(gg)
    adg = gates[:, 2 * d_model:2 * d_model + LANES]
    la = _dot(adg, ga2_ref[...]) + gab_ref[...]
    gla_out[...] = -_softplus(-la) / GLA_TAU

    sgr_out[...] = gates[:, :d_model]
    sgg_out[...] = gates[:, d_model:2 * d_model]


def _inproj_call(x2, mod3, wp, small, bsz, seq, d_model):
    tm = INPROJ_TILE
    nt = seq // tm
    n_tok = bsz * seq
    def tok(n):
        return pl.BlockSpec((tm, n), lambda b, t: (b * nt + t, 0))

    def modspec(idx):
        return pl.BlockSpec((1, 1, d_model), lambda b, t: (b, 0, idx))

    outs = [
        (RW_WIDTH, F32), (RW_WIDTH, F32), (RW_WIDTH, BF16), (RW_WIDTH, F32), (RW_WIDTH, F32),
        (RW_WIDTH, F32), (RW_WIDTH, F32), (RW_WIDTH, F32),
        (GLA_KW, F32), (GLA_KW, F32), (GLA_VW, BF16), (GLA_VW, F32), (GLA_KW, F32),
        (d_model, F32), (d_model, F32),
    ]
    nres = 1 + len(small)

    def outer(x_hbm, mod_hbm, *refs):
        resident = refs[:nres]
        out_hbm = refs[nres:nres + len(outs)]
        carry_ref = refs[nres + len(outs)]

        def body(x_ref, sh, sc, *out_refs):
            _inproj_kernel(x_ref, sh.at[0], sc.at[0], *resident, *out_refs, carry_ref,
                           tm=tm, d_model=d_model)

        pltpu.emit_pipeline(
            body, grid=(bsz, nt),
            in_specs=[tok(d_model), modspec(0), modspec(1)],
            out_specs=[tok(n) for n, _ in outs],
        )(x_hbm, mod_hbm, mod_hbm, *out_hbm)

    hbm = pl.BlockSpec(memory_space=pl.ANY)
    vmem = pl.BlockSpec(memory_space=pltpu.VMEM)
    return pl.pallas_call(
        outer,
        in_specs=[hbm, hbm] + [vmem] * nres,
        out_specs=[hbm] * len(outs),
        out_shape=[jax.ShapeDtypeStruct((n_tok, n), dt) for n, dt in outs],
        scratch_shapes=[pltpu.VMEM((1, RW_COLS), F32)],
        compiler_params=pltpu.CompilerParams(vmem_limit_bytes=VMEM_LIMIT),
        name="inproj",
    )(x2, mod3, wp, *small)


def _rwkv_kernel(*refs, nb, tb, unroll, ncast):
    (r_ref, k_ref, v_ref, lw_ref, nkk_ref, b_ref, g_ref, bonus_ref, gng_ref, gnb_ref,
     bd_ref) = refs[:11]
    cast_in = refs[11:11 + ncast]
    o_ref = refs[11 + ncast]
    cast_out = refs[12 + ncast:12 + 2 * ncast]
    h_ref, rm_s, yl_s, hl_s = refs[12 + 2 * ncast:]
    for w_in_ref, w_out_ref in zip(cast_in, cast_out):
        w_out_ref[...] = w_in_ref[...].astype(BF16)

    c = CHUNK
    gw = RW_GROUP * RW_HEAD
    nchunk = tb // c
    ngroup = RW_WIDTH // gw
    t = pl.program_id(1)

    @pl.when(t == 0)
    def _():
        h_ref[...] = jnp.zeros_like(h_ref)

    lane = lax.broadcasted_iota(jnp.int32, (c, gw), 1)
    rowi = lax.broadcasted_iota(jnp.int32, (c, gw), 0)
    head_of_lane = lane // RW_HEAD
    lmask = [head_of_lane == h for h in range(RW_GROUP)]
    strict = (lane % RW_HEAD) < rowi
    incl = (lane % RW_HEAD) <= rowi
    eye = (lane % RW_HEAD) == rowi
    eye_f = eye.astype(F32)
    assert c == RW_HEAD

    def stack(x):
        xb = x.astype(BF16)
        zero = jnp.zeros_like(xb)
        return jnp.concatenate([jnp.where(lmask[h], xb, zero) for h in range(RW_GROUP)], axis=0)

    def block_t(x):
        xt = x.T
        return jnp.concatenate([xt[h * RW_HEAD:(h + 1) * RW_HEAD, :] for h in range(RW_GROUP)],
                               axis=1)

    def local_group(gi, carry):
        units = [(e, u, g) for e in range(nb) for u in range(unroll) for g in range(ngroup)]
        n = len(units)
        idxs = [gi * unroll + u for u in range(unroll)]
        rows = [pl.ds(pl.multiple_of(i * c, c), c) for i in idxs]

        def ld(ref, e, u, g):
            return ref[e, rows[u], g * gw:(g + 1) * gw]

        lw = [ld(lw_ref, e, u, g) for e, u, g in units]
        gcum = [_cumsum_rows(x) for x in lw]
        gend, rt, at, kt, bt, bh, kh, vst = [], [], [], [], [], [], [], []
        for s, (e, u, g) in enumerate(units):
            g_ = gcum[s]
            ge = g_[c - 1:c, :]
            r = ld(r_ref, e, u, g)
            k = ld(k_ref, e, u, g)
            nkk = ld(nkk_ref, e, u, g)
            b = ld(b_ref, e, u, g)
            e_neg = jnp.exp(-g_)
            dk = jnp.exp(ge - g_)
            gend.append(ge)
            rt.append(r * jnp.exp(g_))
            at.append(nkk * jnp.exp(g_ - lw[s]))
            kt.append(k * e_neg)
            bt.append(b * e_neg)
            bh.append(b * dk)
            kh.append(k * dk)
            vst.append(stack(ld(v_ref, e, u, g)))

        x = [_dot_nt(jnp.concatenate([at[s], rt[s]], axis=0),
                     jnp.concatenate([stack(bt[s]), stack(kt[s])], axis=0)) for s in range(n)]
        a_ab = [jnp.where(strict, y[:c, :gw], 0.0) for y in x]
        a_ak = [jnp.where(strict, y[:c, gw:], 0.0) for y in x]
        a_rb = [jnp.where(incl, y[c:, :gw], 0.0) for y in x]
        a_rk = [jnp.where(incl, y[c:, gw:], 0.0) for y in x]
        bht = [block_t(y) for y in bh]
        kht = [block_t(y) for y in kh]

        tp = [eye_f + a for a in a_ab]
        pw = [_dot(a, stack(a)) for a in a_ab]
        for _ in range(4):
            res = [_dot(jnp.concatenate([pw[s], tp[s]], axis=0), stack(pw[s])) for s in range(n)]
            tp = [tp[s] + res[s][c:] for s in range(n)]
            pw = [res[s][:c] for s in range(n)]
        tinv = [tp[s] + _dot(tp[s], stack(pw[s])) for s in range(n)]

        vres = [_dot(jnp.concatenate([a_ak[s], a_rk[s], kht[s]], axis=0), vst[s]) for s in range(n)]
        akv = [y[:c] for y in vres]
        rk_v = [y[c:2 * c] for y in vres]
        h2 = [y[2 * c:] for y in vres]
        tres = [_dot(tinv[s], jnp.concatenate([stack(at[s]), stack(akv[s])], axis=1))
                for s in range(n)]
        ap = [y[:, :gw] for y in tres]
        uloc = [y[:, gw:] for y in tres]
        fres = [_dot(jnp.concatenate([a_rb[s], bht[s]], axis=0),
                     jnp.concatenate([stack(ap[s]), stack(uloc[s])], axis=1)) for s in range(n)]
        for s, (e, u, g) in enumerate(units):
            y = fres[s]
            rm_s[e, idxs[u], g, :c, :] = rt[s] + y[:c, :gw]
            rm_s[e, idxs[u], g, c:, :] = jnp.where(eye, jnp.exp(gend[s]), 0.0) + y[c:, :gw]
            yl_s[e, rows[u], g * gw:(g + 1) * gw] = y[:c, gw:] + rk_v[s]
            hl_s[e, idxs[u], g] = y[c:, gw:] + h2[s]
        return carry

    lax.fori_loop(0, nchunk // unroll, local_group, 0)

    chains = [(e, g) for e in range(nb) for g in range(ngroup)]
    hstate = [h_ref[e, g] for e, g in chains]
    for ci in range(nchunk):
        rows = slice(ci * c, (ci + 1) * c)
        res = [_dot(rm_s[e, ci, g], stack(hstate[j])) for j, (e, g) in enumerate(chains)]
        for j, (e, g) in enumerate(chains):
            lanes = slice(g * gw, (g + 1) * gw)
            yl_s[e, rows, lanes] = res[j][:c] + yl_s[e, rows, lanes]
            hstate[j] = res[j][c:] + hl_s[e, ci, g]
    for j, (e, g) in enumerate(chains):
        h_ref[e, g] = hstate[j]

    bd = bd_ref[...]
    for p in range(RW_WIDTH // LANES):
        lanes = slice(p * LANES, (p + 1) * LANES)
        y = yl_s[:, :, lanes].reshape(nb * tb, LANES)
        mean = _dot_exact_rhs(y, bd) * (1.0 / RW_HEAD)
        yc = y - mean
        var = _dot_exact_rhs(yc * yc, bd) * (1.0 / RW_HEAD)
        yn = yc * lax.rsqrt(var + RW_GN_EPS)
        bonus = bonus_ref[:, :, lanes].reshape(nb * tb, LANES)
        gate = g_ref[:, :, lanes].reshape(nb * tb, LANES)
        out = (yn * gng_ref[:, lanes] + gnb_ref[:, lanes] + bonus) * gate
        o_ref[:, :, lanes] = out.reshape(nb, tb, LANES).astype(o_ref.dtype)


def _rwkv_call(r, k, v, lw, nkk, b, g, bonus, gng, gnb, bd, weights, bsz, seq):
    tb = RWKV_TILE
    nb = RWKV_BATCH
    nt = seq // tb
    gw = RW_GROUP * RW_HEAD
    ngroup = RW_WIDTH // gw
    nchunk = tb // CHUNK
    nsteps = (bsz // nb) * nt
    assert bsz % nb == 0
    assert all(w.shape[0] % (16 * nsteps) == 0 for w in weights)

    def tok():
        return pl.BlockSpec((nb, tb, RW_WIDTH), lambda bb, t: (bb, t, 0))

    def full(a):
        return pl.BlockSpec(a.shape, lambda bb, t: (0,) * a.ndim)

    def wblock(w):
        return pl.BlockSpec((w.shape[0] // nsteps, w.shape[1]), lambda bb, t: (bb * nt + t, 0))

    per_seq = lambda a: a.reshape(bsz, seq, RW_WIDTH)
    out, *cast = pl.pallas_call(
        functools.partial(_rwkv_kernel, nb=nb, tb=tb, unroll=RWKV_UNROLL, ncast=len(weights)),
        grid=(bsz // nb, nt),
        in_specs=([tok() for _ in range(8)] + [full(gng), full(gnb), full(bd)]
                  + [wblock(w) for w in weights]),
        out_specs=[tok()] + [wblock(w) for w in weights],
        out_shape=([jax.ShapeDtypeStruct((bsz, seq, RW_WIDTH), BF16)]
                   + [jax.ShapeDtypeStruct(w.shape, BF16) for w in weights]),
        scratch_shapes=[pltpu.VMEM((nb, ngroup, RW_HEAD, gw), F32),
                        pltpu.VMEM((nb, nchunk, ngroup, CHUNK + RW_HEAD, gw), F32),
                        pltpu.VMEM((nb, tb, RW_WIDTH), F32),
                        pltpu.VMEM((nb, nchunk, ngroup, RW_HEAD, gw), F32)],
        compiler_params=pltpu.CompilerParams(
            dimension_semantics=("arbitrary", "arbitrary"),
            vmem_limit_bytes=VMEM_LIMIT),
        name="rwkv",
    )(*[per_seq(a) for a in (r, k, v, lw, nkk, b, g, bonus)], gng, gnb, bd, *weights)
    return out.reshape(bsz * seq, RW_WIDTH), cast


def _gla_kernel(q_ref, k_ref, v_ref, sg_ref, la_ref, ng_ref, o_ref, st_ref, qs_s, oi_s, kv_s, eb_s,
                *, tb, unroll):
    c = CHUNK
    nchunk = tb // c
    t = pl.program_id(1)

    @pl.when(t == 0)
    def _():
        st_ref[...] = jnp.zeros_like(st_ref)

    lane = lax.broadcasted_iota(jnp.int32, (1, LANES), 1)
    hmask = ((lane < GLA_DK).astype(F32), (lane >= GLA_DK).astype(F32))
    ri = lax.broadcasted_iota(jnp.int32, (c, c), 0)
    ci_ = lax.broadcasted_iota(jnp.int32, (c, c), 1)
    causal = ci_ <= ri
    ltri = causal.astype(BF16)

    def local_group(gi, carry):
        units = [(u, h) for u in range(unroll) for h in range(GLA_HEADS)]
        idxs = [gi * unroll + u for u in range(unroll)]
        rows = [pl.ds(pl.multiple_of(i * c, c), c) for i in idxs]

        def qk_lanes(h):
            return slice((h // 2) * LANES, (h // 2 + 1) * LANES)

        la = [la_ref[rows[u], qk_lanes(h)] for u, h in units]
        bcum = [_dot_exact_lhs(ltri, x) for x in la]
        q_s, k_s, kdec, v = [], [], [], []
        for s, (u, h) in enumerate(units):
            bc = bcum[s]
            blast = bc[c - 1:c, :]
            q = q_ref[rows[u], qk_lanes(h)]
            k = k_ref[rows[u], qk_lanes(h)]
            hm = hmask[h % 2]
            q_s.append((q * jnp.exp(bc) * hm).astype(BF16))
            k_s.append(k * jnp.exp(-bc))
            kdec.append(k * jnp.exp(blast - bc) * hm)
            v.append(v_ref[rows[u], h * LANES:(h + 1) * LANES])
            eb_s[idxs[u], h] = jnp.exp(blast)
        att = [jnp.where(causal, _dot_nt(q_s[s], k_s[s]), 0.0) for s in range(len(units))]
        oi = [_dot(att[s], v[s]) for s in range(len(units))]
        v_t = [x.astype(F32).T.astype(BF16) for x in v]
        kv = [_dot(v_t[s], kdec[s]) for s in range(len(units))]
        for s, (u, h) in enumerate(units):
            qs_s[idxs[u], h] = q_s[s]
            oi_s[rows[u], h * LANES:(h + 1) * LANES] = oi[s]
            kv_s[idxs[u], h] = kv[s]
        return carry

    lax.fori_loop(0, nchunk // unroll, local_group, 0)

    st = [st_ref[h] for h in range(GLA_HEADS)]
    for ci in range(nchunk):
        rows = slice(ci * c, (ci + 1) * c)
        inter = [_dot_nt(qs_s[ci, h], st[h]) for h in range(GLA_HEADS)]
        for h in range(GLA_HEADS):
            lanes = slice(h * LANES, (h + 1) * LANES)
            oi_s[rows, lanes] = oi_s[rows, lanes] + inter[h]
            st[h] = st[h] * eb_s[ci, h] + kv_s[ci, h]
    for h in range(GLA_HEADS):
        st_ref[h] = st[h]

    ng = ng_ref[...]
    for h in range(GLA_HEADS):
        lanes = slice(h * LANES, (h + 1) * LANES)
        o = oi_s[:, lanes]
        o = o * lax.rsqrt(jnp.mean(o * o, axis=-1, keepdims=True) + GLA_NORM_EPS) * ng
        o_ref[:, lanes] = (o * sg_ref[:, lanes]).astype(o_ref.dtype)


def _gla_call(q, k, v, sg, la, ng, bsz, seq):
    tb = GLA_TILE
    nt = seq // tb
    n_tok = bsz * seq
    nchunk = tb // CHUNK

    def tok(n):
        return pl.BlockSpec((tb, n), lambda bb, t: (bb * nt + t, 0))

    def outer(q_hbm, k_hbm, v_hbm, sg_hbm, la_hbm, ng_ref, o_hbm, *scratch):
        def body(q_ref, k_ref, v_ref, sg_ref, la_ref, o_ref):
            _gla_kernel(q_ref, k_ref, v_ref, sg_ref, la_ref, ng_ref, o_ref, *scratch,
                        tb=tb, unroll=GLA_UNROLL)

        pltpu.emit_pipeline(
            body, grid=(bsz, nt),
            in_specs=[tok(GLA_KW), tok(GLA_KW), tok(GLA_VW), tok(GLA_VW), tok(GLA_KW)],
            out_specs=[tok(GLA_VW)],
        )(q_hbm, k_hbm, v_hbm, sg_hbm, la_hbm, o_hbm)

    hbm = pl.BlockSpec(memory_space=pl.ANY)
    return pl.pallas_call(
        outer,
        in_specs=[hbm] * 5 + [pl.BlockSpec(memory_space=pltpu.VMEM)],
        out_specs=hbm,
        out_shape=jax.ShapeDtypeStruct((n_tok, GLA_VW), BF16),
        scratch_shapes=[pltpu.VMEM((GLA_HEADS, GLA_DV, LANES), F32),
                        pltpu.VMEM((nchunk, GLA_HEADS, CHUNK, LANES), BF16),
                        pltpu.VMEM((tb, GLA_VW), F32),
                        pltpu.VMEM((nchunk, GLA_HEADS, GLA_DV, LANES), F32),
                        pltpu.VMEM((nchunk, GLA_HEADS, 1, LANES), F32)],
        compiler_params=pltpu.CompilerParams(vmem_limit_bytes=VMEM_LIMIT),
        name="gla",
    )(q, k, v, sg, la, ng)


def _tail_kernel(x_ref, orw_ref, ogla_ref, sgr_ref, sgg_ref, gate1_ref, shift2_ref, scale2_ref,
                 gate2_ref, wrb_ref, wgb_ref, wmix_ref, win_ref, wout_ref, ln1g_ref, ln1b_ref,
                 ln2g_ref, ln2b_ref, o_ref, *, d_ff):
    hrows = x_ref.shape[0] // TAIL_SPLIT
    rs = [slice(i * hrows, (i + 1) * hrows) for i in range(TAIL_SPLIT)]
    nsp = range(TAIL_SPLIT)
    b_rw = [jnp.dot(orw_ref[s, :], wrb_ref[...], preferred_element_type=F32) for s in rs]
    b_gla = [jnp.dot(ogla_ref[s, :], wgb_ref[...], preferred_element_type=F32) for s in rs]
    mix = [_dot(_sigmoid(sgr_ref[rs[i], :]) * b_rw[i] + _sigmoid(sgg_ref[rs[i], :]) * b_gla[i],
                wmix_ref[...]) for i in nsp]
    x1 = [_layer_norm(ALPHA * x_ref[rs[i], :] + gate1_ref[...] * mix[i], LN_EPS) * ln1g_ref[...]
          + ln1b_ref[...] for i in nsp]
    u = [(x1[i] * (1.0 + scale2_ref[...]) + shift2_ref[...]).astype(BF16) for i in nsp]
    ffn = [jnp.zeros_like(x1[i]) for i in nsp]
    ntile = d_ff // MXU_K
    bounds = [MXU_K * ((ntile * j) // FFN_CHUNKS) for j in range(FFN_CHUNKS)] + [d_ff]
    for lo, hi in zip(bounds[:-1], bounds[1:]):
        hg, hu = [], []
        for i in nsp:
            hg.append(jnp.dot(u[i], win_ref[:, lo:hi], preferred_element_type=F32))
            hu.append(jnp.dot(u[i], win_ref[:, d_ff + lo:d_ff + hi], preferred_element_type=F32))
        act = [hg[i] * _sigmoid(hg[i]) * hu[i] for i in nsp]
        ffn = [ffn[i] + _dot(act[i], wout_ref[lo:hi, :]) for i in nsp]
    for i in nsp:
        y = _layer_norm(ALPHA * x1[i] + gate2_ref[...] * ffn[i], LN_EPS)
        o_ref[rs[i], :] = y * ln2g_ref[...] + ln2b_ref[...]


def _tail_call(x2, o_rw, o_gla, sgr, sgg, mod3, wrb, wgb, wmix, win, wout, ln1g, ln1b, ln2g, ln2b,
               bsz, seq, d_model):
    tm = TAIL_TILE
    nt = seq // tm
    n_tok = bsz * seq
    d_ff = wout.shape[0]
    assert d_ff % MXU_K == 0

    def tok(n):
        return pl.BlockSpec((tm, n), lambda b, t: (b * nt + t, 0))

    def modspec(idx):
        return pl.BlockSpec((1, 1, d_model), lambda b, t: (b, 0, idx))

    def outer(x_hbm, orw_hbm, ogla_hbm, sgr_hbm, sgg_hbm, mod_hbm, wrb_ref, wgb_ref, wmix_ref,
              win_ref, wout_ref, ln1g_ref, ln1b_ref, ln2g_ref, ln2b_ref, o_hbm):
        def body(x_ref, orw_ref, ogla_ref, sgr_ref, sgg_ref, g1, sh2, sc2, g2, o_ref):
            _tail_kernel(x_ref, orw_ref, ogla_ref, sgr_ref, sgg_ref,
                         g1.at[0], sh2.at[0], sc2.at[0], g2.at[0],
                         wrb_ref, wgb_ref, wmix_ref, win_ref, wout_ref,
                         ln1g_ref, ln1b_ref, ln2g_ref, ln2b_ref, o_ref, d_ff=d_ff)

        pltpu.emit_pipeline(
            body, grid=(bsz, nt),
            in_specs=[tok(d_model), tok(RW_WIDTH), tok(GLA_VW), tok(d_model), tok(d_model),
                      modspec(2), modspec(3), modspec(4), modspec(5)],
            out_specs=[tok(d_model)],
        )(x_hbm, orw_hbm, ogla_hbm, sgr_hbm, sgg_hbm, mod_hbm, mod_hbm, mod_hbm, mod_hbm, o_hbm)

    hbm = pl.BlockSpec(memory_space=pl.ANY)
    vmem = pl.BlockSpec(memory_space=pltpu.VMEM)
    return pl.pallas_call(
        outer,
        in_specs=[hbm] * 6 + [vmem] * 9,
        out_specs=hbm,
        out_shape=jax.ShapeDtypeStruct((n_tok, d_model), F32),
        compiler_params=pltpu.CompilerParams(vmem_limit_bytes=VMEM_LIMIT),
        name="tail",
    )(x2, o_rw, o_gla, sgr, sgg, mod3, wrb, wgb, wmix, win, wout, ln1g, ln1b, ln2g, ln2b)


def kernel(x, c, w_ada, b_ada, w_in, mu_rw, rw_w0, rw_w2, rw_a0, rw_a2, rw_g2, rw_k_k, rw_k_a,
           rw_r_k, rw_gn_g, rw_gn_b, gla_a2, gla_a_b, gla_norm_g, w_rw_branch, w_gla_branch,
           w_mix_out, ln1_g, ln1_b, w_ffn_in, w_ffn_out, ln2_g, ln2_b):
    bsz, seq, d_model = x.shape
    assert w_ada.shape[0] == DEPTH
    assert all(seq % tile == 0 for tile in (INPROJ_TILE, RWKV_TILE, GLA_TILE, TAIL_TILE))
    n_tok = bsz * seq
    l = 0

    wp = _wprep_call(jnp.swapaxes(w_in[l], 0, 1), d_model)
    zeros_lora = jnp.zeros((DECAY_LORA, RW_WIDTH), F32)
    w2p = jnp.concatenate([rw_w2[l], zeros_lora], axis=0).astype(BF16)
    a2p = jnp.concatenate([zeros_lora, rw_a2[l]], axis=0).astype(BF16)
    ga2p = jnp.concatenate(
        [gla_a2[l], jnp.zeros((LANES - GLA_GATE_LORA, GLA_KW), F32)], axis=0).astype(BF16)
    row = lambda a: a.reshape(1, -1)
    hid = jnp.arange(LANES) // RW_HEAD
    bd = (hid[:, None] == hid[None, :]).astype(BF16)
    small = (row(mu_rw[l]), row(rw_w0[l]), w2p, row(rw_a0[l]), a2p, rw_g2[l].astype(BF16),
             row(rw_k_k[l]), row(rw_k_a[l]), row(rw_r_k[l]), ga2p, row(gla_a_b[l]), bd)

    x2 = x.reshape(n_tok, d_model)
    mod = _mod_call(c, w_ada[l], b_ada[l])
    mod3 = mod.reshape(bsz, 1, 6 * d_model)

    (r, k, v, lw, nkk, b, g, bonus, gq, gk, gv, gsg, gla, sgr, sgg) = _inproj_call(
        x2, mod3, wp, small, bsz, seq, d_model)

    o_rw, (wrb, wgb, wmix, win, wout) = _rwkv_call(
        r, k, v, lw, nkk, b, g, bonus, row(rw_gn_g[l]), row(rw_gn_b[l]), bd,
        (w_rw_branch[l], w_gla_branch[l], w_mix_out[l], w_ffn_in[l], w_ffn_out[l]), bsz, seq)
    o_gla = _gla_call(gq, gk, gv, gsg, gla, row(gla_norm_g[l]), bsz, seq)

    out = _tail_call(x2, o_rw, o_gla, sgr, sgg, mod3, wrb, wgb, wmix, win, wout,
                     row(ln1_g[l]), row(ln1_b[l]), row(ln2_g[l]), row(ln2_b[l]), bsz, seq, d_model)
    return out.reshape(bsz, seq, d_model)
```

```python
import functools

import jax
import jax.numpy as jnp
from jax import lax
from jax.experimental import pallas as pl
from jax.experimental.pallas import tpu as pltpu

F32 = jnp.float32
BF16 = jnp.bfloat16

RW_HEAD = 64
RW_HEADS = 8
RW_WIDTH = RW_HEADS * RW_HEAD
DECAY_LORA = 64
AAA_LORA = 64
GATE_LORA = 128
RW_GN_EPS = 64e-5
RW_COLS = 3 * RW_WIDTH + DECAY_LORA + AAA_LORA + GATE_LORA
GLA_HEADS = 4
GLA_DK = 64
GLA_DV = 128
GLA_KW = GLA_HEADS * GLA_DK
GLA_VW = GLA_HEADS * GLA_DV
GLA_GATE_LORA = 16
GLA_TAU = 16.0
GLA_NORM_EPS = 1e-5
GLA_MAIN = 2 * GLA_KW + 2 * GLA_VW
GLA_COLS = GLA_MAIN + GLA_GATE_LORA
LN_EPS = 1e-5
DEPTH = 1
ALPHA = (2.0 * DEPTH) ** 0.25

LANES = 128
MXU_K = 256
CHUNK = 64
MOD_TILE = 1536
WPREP_COLS = 256
INPROJ_TILE = 512
RWKV_UNROLL = 1
RW_GROUP = 2
GLA_UNROLL = 8
RWKV_TILE = 256
RWKV_BATCH = 4
GLA_TILE = 2048
TAIL_TILE = 512
TAIL_SPLIT = 2
FFN_CHUNKS = 2
VMEM_LIMIT = 58 * 1024 * 1024

COL_RW = 0
COL_GLA = RW_COLS
COL_GATE = COL_GLA + GLA_MAIN


def _sigmoid(x):
    return 1.0 / (1.0 + jnp.exp(-x))


def _softplus(x):
    return jnp.maximum(x, 0.0) + jnp.log1p(jnp.exp(-jnp.abs(x)))


def _dot(a, b):
    return jnp.dot(a.astype(BF16), b.astype(BF16), preferred_element_type=F32)


def _dot_nt(a, b):
    return lax.dot_general(a.astype(BF16), b.astype(BF16), (((1,), (1,)), ((), ())),
                           preferred_element_type=F32)


def _split_hi_lo(x):
    hi = x.astype(BF16)
    lo = (x - hi.astype(F32)).astype(BF16)
    return hi, lo


def _dot_exact_rhs(x, w01):
    hi, lo = _split_hi_lo(x)
    if 2 * x.shape[1] <= MXU_K:
        return jnp.dot(jnp.concatenate([hi, lo], axis=1), jnp.concatenate([w01, w01], axis=0),
                       preferred_element_type=F32)
    return (jnp.dot(hi, w01, preferred_element_type=F32)
            + jnp.dot(lo, w01, preferred_element_type=F32))


def _dot_exact_lhs(w01, x):
    hi, lo = _split_hi_lo(x)
    if 2 * x.shape[0] <= MXU_K:
        return jnp.dot(jnp.concatenate([w01, w01], axis=1), jnp.concatenate([hi, lo], axis=0),
                       preferred_element_type=F32)
    return (jnp.dot(w01, hi, preferred_element_type=F32)
            + jnp.dot(w01, lo, preferred_element_type=F32))


def _cumsum_rows(x):
    row = lax.broadcasted_iota(jnp.int32, x.shape, 0)
    shift = 1
    while shift < x.shape[0]:
        x = x + jnp.where(row >= shift, pltpu.roll(x, shift, 0), 0.0)
        shift *= 2
    return x


def _layer_norm(x, eps):
    mu = jnp.mean(x, axis=-1, keepdims=True)
    xc = x - mu
    var = jnp.mean(xc * xc, axis=-1, keepdims=True)
    return xc * lax.rsqrt(var + eps)


def _mod_kernel(c_ref, w_ref, b_ref, o_ref):
    c = c_ref[...]
    sc = c * _sigmoid(c)
    o_ref[...] = _dot(sc, w_ref[...]) + b_ref[...]


def _mod_call(c, w_ada, b_ada):
    bsz, d = c.shape
    n = w_ada.shape[1]
    tn = MOD_TILE
    assert n % tn == 0
    return pl.pallas_call(
        _mod_kernel,
        grid=(n // tn,),
        in_specs=[pl.BlockSpec((bsz, d), lambda j: (0, 0)),
                  pl.BlockSpec((d, tn), lambda j: (0, j)),
                  pl.BlockSpec((1, tn), lambda j: (0, j))],
        out_specs=pl.BlockSpec((bsz, tn), lambda j: (0, j)),
        out_shape=jax.ShapeDtypeStruct((bsz, n), F32),
        compiler_params=pltpu.CompilerParams(dimension_semantics=("arbitrary",),
                                             vmem_limit_bytes=VMEM_LIMIT),
        name="mod",
    )(c, w_ada, b_ada.reshape(1, n))


def _wprep_kernel(w_ref, o_ref, *, d_model):
    main = RW_COLS + GLA_MAIN
    o_ref[:main, :] = w_ref[:main, :].astype(BF16)
    o_ref[main:main + 2 * d_model, :] = w_ref[RW_COLS + GLA_COLS:, :].astype(BF16)
    o_ref[main + 2 * d_model:main + 2 * d_model + GLA_GATE_LORA, :] = (
        w_ref[main:RW_COLS + GLA_COLS, :].astype(BF16))
    o_ref[main + 2 * d_model + GLA_GATE_LORA:, :] = jnp.zeros(
        (LANES - GLA_GATE_LORA, o_ref.shape[1]), BF16)


def _wprep_call(w_in_t, d_model):
    n_in = w_in_t.shape[0]
    n_out = RW_COLS + GLA_MAIN + 2 * d_model + LANES
    cols = WPREP_COLS
    assert d_model % cols == 0
    return pl.pallas_call(
        functools.partial(_wprep_kernel, d_model=d_model),
        grid=(d_model // cols,),
        in_specs=[pl.BlockSpec((n_in, cols), lambda i: (0, i))],
        out_specs=pl.BlockSpec((n_out, cols), lambda i: (0, i)),
        out_shape=jax.ShapeDtypeStruct((n_out, d_model), BF16),
        compiler_params=pltpu.CompilerParams(dimension_semantics=("arbitrary",),
                                             vmem_limit_bytes=VMEM_LIMIT),
        name="wprep",
    )(w_in_t)


def _inproj_kernel(x_ref, shift_ref, scale_ref, wp_ref, mu_ref, w0_ref, w2_ref, a0_ref, a2_ref,
                   g2_ref, kk_ref, ka_ref, rk_ref, ga2_ref, gab_ref, bd_ref,
                   r_out, k_out, v_out, lw_out, nkk_out, b_out, g_out, bonus_out,
                   gq_out, gk_out, gv_out, gsg_out, gla_out, sgr_out, sgg_out,
                   carry_ref, *, tm, d_model):
    t = pl.program_id(1)

    @pl.when(t == 0)
    def _():
        carry_ref[...] = jnp.zeros_like(carry_ref)

    u = (x_ref[...] * (1.0 + scale_ref[...]) + shift_ref[...]).astype(BF16)
    p = _dot_nt(u, wp_ref[COL_RW:COL_RW + RW_COLS, :])
    pg = _dot_nt(u, wp_ref[COL_GLA:COL_GLA + GLA_MAIN, :])

    prev = pltpu.roll(p, 1, 0)
    row = lax.broadcasted_iota(jnp.int32, p.shape, 0)
    prev = jnp.where(row == 0, carry_ref[...], prev)
    carry_ref[...] = p[tm - 1:tm, :]
    ps = p + mu_ref[...] * (prev - p)

    r = ps[:, 0:RW_WIDTH]
    k = ps[:, RW_WIDTH:2 * RW_WIDTH]
    v = ps[:, 2 * RW_WIDTH:3 * RW_WIDTH]
    z = ps[:, 3 * RW_WIDTH:3 * RW_WIDTH + LANES]
    gd = ps[:, 3 * RW_WIDTH + LANES:RW_COLS]

    w_lora = _dot(jnp.tanh(z), w2_ref[...])
    a_lora = _dot(z, a2_ref[...])
    g_out[...] = _dot(_sigmoid(gd), g2_ref[...])
    gates = _dot_nt(u, wp_ref[COL_GATE:, :])

    w = -_softplus(-(w0_ref[...] + w_lora)) - 0.5
    lw_out[...] = -jnp.exp(w)
    a = _sigmoid(a0_ref[...] + a_lora)
    kkv = k * kk_ref[...]
    kp = k * (1.0 + (a - 1.0) * ka_ref[...])
    rkr = r * kp * rk_ref[...]
    bd = bd_ref[...]
    for j in range(RW_WIDTH // LANES):
        sl = slice(j * LANES, (j + 1) * LANES)
        kkj = kkv[:, sl]
        ssq = _dot_exact_rhs(kkj * kkj, bd)
        kkn = kkj / jnp.maximum(jnp.sqrt(ssq), 1e-12)
        nkk_out[:, sl] = -kkn
        b_out[:, sl] = kkn * a[:, sl]
        bonus_out[:, sl] = _dot_exact_rhs(rkr[:, sl], bd) * v[:, sl]
    r_out[...] = r
    k_out[...] = kp
    v_out[...] = v.astype(BF16)

    gq_out[...] = pg[:, 0:GLA_KW] * (GLA_DK ** -0.5)
    gk_out[...] = pg[:, GLA_KW:2 * GLA_KW]
    gv_out[...] = pg[:, 2 * GLA_KW:2 * GLA_KW + GLA_VW].astype(BF16)
    gg = pg[:, 2 * GLA_KW + GLA_VW:GLA_MAIN]
    gsg_out[...] = gg * _sigmoid(gg)
    adg = gates[:, 2 * d_model:2 * d_model + LANES]
    la = _dot(adg, ga2_ref[...]) + gab_ref[...]
    gla_out[...] = -_softplus(-la) / GLA_TAU

    sgr_out[...] = gates[:, :d_model]
    sgg_out[...] = gates[:, d_model:2 * d_model]


def _inproj_call(x2, mod3, wp, small, bsz, seq, d_model):
    tm = INPROJ_TILE
    nt = seq // tm
    n_tok = bsz * seq
    def tok(n):
        return pl.BlockSpec((tm, n), lambda b, t: (b * nt + t, 0))

    def modspec(idx):
        return pl.BlockSpec((1, 1, d_model), lambda b, t: (b, 0, idx))

    outs = [
        (RW_WIDTH, F32), (RW_WIDTH, F32), (RW_WIDTH, BF16), (RW_WIDTH, F32), (RW_WIDTH, F32),
        (RW_WIDTH, F32), (RW_WIDTH, F32), (RW_WIDTH, F32),
        (GLA_KW, F32), (GLA_KW, F32), (GLA_VW, BF16), (GLA_VW, F32), (GLA_KW, F32),
        (d_model, F32), (d_model, F32),
    ]
    nres = 1 + len(small)

    def outer(x_hbm, mod_hbm, *refs):
        resident = refs[:nres]
        out_hbm = refs[nres:nres + len(outs)]
        carry_ref = refs[nres + len(outs)]

        def body(x_ref, sh, sc, *out_refs):
            _inproj_kernel(x_ref, sh.at[0], sc.at[0], *resident, *out_refs, carry_ref,
                           tm=tm, d_model=d_model)

        pltpu.emit_pipeline(
            body, grid=(bsz, nt),
            in_specs=[tok(d_model), modspec(0), modspec(1)],
            out_specs=[tok(n) for n, _ in outs],
        )(x_hbm, mod_hbm, mod_hbm, *out_hbm)

    hbm = pl.BlockSpec(memory_space=pl.ANY)
    vmem = pl.BlockSpec(memory_space=pltpu.VMEM)
    return pl.pallas_call(
        outer,
        in_specs=[hbm, hbm] + [vmem] * nres,
        out_specs=[hbm] * len(outs),
        out_shape=[jax.ShapeDtypeStruct((n_tok, n), dt) for n, dt in outs],
        scratch_shapes=[pltpu.VMEM((1, RW_COLS), F32)],
        compiler_params=pltpu.CompilerParams(vmem_limit_bytes=VMEM_LIMIT),
        name="inproj",
    )(x2, mod3, wp, *small)


def _rwkv_kernel(*refs, nb, tb, unroll, ncast):
    (r_ref, k_ref, v_ref, lw_ref, nkk_ref, b_ref, g_ref, bonus_ref, gng_ref, gnb_ref,
     bd_ref) = refs[:11]
    cast_in = refs[11:11 + ncast]
    o_ref = refs[11 + ncast]
    cast_out = refs[12 + ncast:12 + 2 * ncast]
    h_ref, rm_s, yl_s, hl_s = refs[12 + 2 * ncast:]
    for w_in_ref, w_out_ref in zip(cast_in, cast_out):
        w_out_ref[...] = w_in_ref[...].astype(BF16)

    c = CHUNK
    gw = RW_GROUP * RW_HEAD
    nchunk = tb // c
    ngroup = RW_WIDTH // gw
    t = pl.program_id(1)

    @pl.when(t == 0)
    def _():
        h_ref[...] = jnp.zeros_like(h_ref)

    lane = lax.broadcasted_iota(jnp.int32, (c, gw), 1)
    rowi = lax.broadcasted_iota(jnp.int32, (c, gw), 0)
    head_of_lane = lane // RW_HEAD
    lmask = [head_of_lane == h for h in range(RW_GROUP)]
    strict = (lane % RW_HEAD) < rowi
    incl = (lane % RW_HEAD) <= rowi
    eye = (lane % RW_HEAD) == rowi
    eye_f = eye.astype(F32)
    assert c == RW_HEAD

    def stack(x):
        xb = x.astype(BF16)
        zero = jnp.zeros_like(xb)
        return jnp.concatenate([jnp.where(lmask[h], xb, zero) for h in range(RW_GROUP)], axis=0)

    def block_t(x):
        xt = x.T
        return jnp.concatenate([xt[h * RW_HEAD:(h + 1) * RW_HEAD, :] for h in range(RW_GROUP)],
                               axis=1)

    def local_group(gi, carry):
        units = [(e, u, g) for e in range(nb) for u in range(unroll) for g in range(ngroup)]
        n = len(units)
        idxs = [gi * unroll + u for u in range(unroll)]
        rows = [pl.ds(pl.multiple_of(i * c, c), c) for i in idxs]

        def ld(ref, e, u, g):
            return ref[e, rows[u], g * gw:(g + 1) * gw]

        lw = [ld(lw_ref, e, u, g) for e, u, g in units]
        gcum = [_cumsum_rows(x) for x in lw]
        gend, rt, at, kt, bt, bh, kh, vst = [], [], [], [], [], [], [], []
        for s, (e, u, g) in enumerate(units):
            g_ = gcum[s]
            ge = g_[c - 1:c, :]
            r = ld(r_ref, e, u, g)
            k = ld(k_ref, e, u, g)
            nkk = ld(nkk_ref, e, u, g)
            b = ld(b_ref, e, u, g)
            e_neg = jnp.exp(-g_)
            dk = jnp.exp(ge - g_)
            gend.append(ge)
            rt.append(r * jnp.exp(g_))
            at.append(nkk * jnp.exp(g_ - lw[s]))
            kt.append(k * e_neg)
            bt.append(b * e_neg)
            bh.append(b * dk)
            kh.append(k * dk)
            vst.append(stack(ld(v_ref, e, u, g)))

        x = [_dot_nt(jnp.concatenate([at[s], rt[s]], axis=0),
                     jnp.concatenate([stack(bt[s]), stack(kt[s])], axis=0)) for s in range(n)]
        a_ab = [jnp.where(strict, y[:c, :gw], 0.0) for y in x]
        a_ak = [jnp.where(strict, y[:c, gw:], 0.0) for y in x]
        a_rb = [jnp.where(incl, y[c:, :gw], 0.0) for y in x]
        a_rk = [jnp.where(incl, y[c:, gw:], 0.0) for y in x]
        bht = [block_t(y) for y in bh]
        kht = [block_t(y) for y in kh]

        tp = [eye_f + a for a in a_ab]
        pw = [_dot(a, stack(a)) for a in a_ab]
        for _ in range(4):
            res = [_dot(jnp.concatenate([pw[s], tp[s]], axis=0), stack(pw[s])) for s in range(n)]
            tp = [tp[s] + res[s][c:] for s in range(n)]
            pw = [res[s][:c] for s in range(n)]
        tinv = [tp[s] + _dot(tp[s], stack(pw[s])) for s in range(n)]

        vres = [_dot(jnp.concatenate([a_ak[s], a_rk[s], kht[s]], axis=0), vst[s]) for s in range(n)]
        akv = [y[:c] for y in vres]
        rk_v = [y[c:2 * c] for y in vres]
        h2 = [y[2 * c:] for y in vres]
        tres = [_dot(tinv[s], jnp.concatenate([stack(at[s]), stack(akv[s])], axis=1))
                for s in range(n)]
        ap = [y[:, :gw] for y in tres]
        uloc = [y[:, gw:] for y in tres]
        fres = [_dot(jnp.concatenate([a_rb[s], bht[s]], axis=0),
                     jnp.concatenate([stack(ap[s]), stack(uloc[s])], axis=1)) for s in range(n)]
        for s, (e, u, g) in enumerate(units):
            y = fres[s]
            rm_s[e, idxs[u], g, :c, :] = rt[s] + y[:c, :gw]
            rm_s[e, idxs[u], g, c:, :] = jnp.where(eye, jnp.exp(gend[s]), 0.0) + y[c:, :gw]
            yl_s[e, rows[u], g * gw:(g + 1) * gw] = y[:c, gw:] + rk_v[s]
            hl_s[e, idxs[u], g] = y[c:, gw:] + h2[s]
        return carry

    lax.fori_loop(0, nchunk // unroll, local_group, 0)

    chains = [(e, g) for e in range(nb) for g in range(ngroup)]
    hstate = [h_ref[e, g] for e, g in chains]
    for ci in range(nchunk):
        rows = slice(ci * c, (ci + 1) * c)
        res = [_dot(rm_s[e, ci, g], stack(hstate[j])) for j, (e, g) in enumerate(chains)]
        for j, (e, g) in enumerate(chains):
            lanes = slice(g * gw, (g + 1) * gw)
            yl_s[e, rows, lanes] = res[j][:c] + yl_s[e, rows, lanes]
            hstate[j] = res[j][c:] + hl_s[e, ci, g]
    for j, (e, g) in enumerate(chains):
        h_ref[e, g] = hstate[j]

    bd = bd_ref[...]
    for p in range(RW_WIDTH // LANES):
        lanes = slice(p * LANES, (p + 1) * LANES)
        y = yl_s[:, :, lanes].reshape(nb * tb, LANES)
        mean = _dot_exact_rhs(y, bd) * (1.0 / RW_HEAD)
        yc = y - mean
        var = _dot_exact_rhs(yc * yc, bd) * (1.0 / RW_HEAD)
        yn = yc * lax.rsqrt(var + RW_GN_EPS)
        bonus = bonus_ref[:, :, lanes].reshape(nb * tb, LANES)
        gate = g_ref[:, :, lanes].reshape(nb * tb, LANES)
        out = (yn * gng_ref[:, lanes] + gnb_ref[:, lanes] + bonus) * gate
        o_ref[:, :, lanes] = out.reshape(nb, tb, LANES).astype(o_ref.dtype)


def _rwkv_call(r, k, v, lw, nkk, b, g, bonus, gng, gnb, bd, weights, bsz, seq):
    tb = RWKV_TILE
    nb = RWKV_BATCH
    nt = seq // tb
    gw = RW_GROUP * RW_HEAD
    ngroup = RW_WIDTH // gw
    nchunk = tb // CHUNK
    nsteps = (bsz // nb) * nt
    assert bsz % nb == 0
    assert all(w.shape[0] % (16 * nsteps) == 0 for w in weights)

    def tok():
        return pl.BlockSpec((nb, tb, RW_WIDTH), lambda bb, t: (bb, t, 0))

    def full(a):
        return pl.BlockSpec(a.shape, lambda bb, t: (0,) * a.ndim)

    def wblock(w):
        return pl.BlockSpec((w.shape[0] // nsteps, w.shape[1]), lambda bb, t: (bb * nt + t, 0))

    per_seq = lambda a: a.reshape(bsz, seq, RW_WIDTH)
    out, *cast = pl.pallas_call(
        functools.partial(_rwkv_kernel, nb=nb, tb=tb, unroll=RWKV_UNROLL, ncast=len(weights)),
        grid=(bsz // nb, nt),
        in_specs=([tok() for _ in range(8)] + [full(gng), full(gnb), full(bd)]
                  + [wblock(w) for w in weights]),
        out_specs=[tok()] + [wblock(w) for w in weights],
        out_shape=([jax.ShapeDtypeStruct((bsz, seq, RW_WIDTH), BF16)]
                   + [jax.ShapeDtypeStruct(w.shape, BF16) for w in weights]),
        scratch_shapes=[pltpu.VMEM((nb, ngroup, RW_HEAD, gw), F32),
                        pltpu.VMEM((nb, nchunk, ngroup, CHUNK + RW_HEAD, gw), F32),
                        pltpu.VMEM((nb, tb, RW_WIDTH), F32),
                        pltpu.VMEM((nb, nchunk, ngroup, RW_HEAD, gw), F32)],
        compiler_params=pltpu.CompilerParams(
            dimension_semantics=("arbitrary", "arbitrary"),
            vmem_limit_bytes=VMEM_LIMIT),
        name="rwkv",
    )(*[per_seq(a) for a in (r, k, v, lw, nkk, b, g, bonus)], gng, gnb, bd, *weights)
    return out.reshape(bsz * seq, RW_WIDTH), cast


def _gla_kernel(q_ref, k_ref, v_ref, sg_ref, la_ref, ng_ref, o_ref, st_ref, qs_s, oi_s, kv_s, eb_s,
                *, tb, unroll):
    c = CHUNK
    nchunk = tb // c
    t = pl.program_id(1)

    @pl.when(t == 0)
    def _():
        st_ref[...] = jnp.zeros_like(st_ref)

    lane = lax.broadcasted_iota(jnp.int32, (1, LANES), 1)
    hmask = ((lane < GLA_DK).astype(F32), (lane >= GLA_DK).astype(F32))
    ri = lax.broadcasted_iota(jnp.int32, (c, c), 0)
    ci_ = lax.broadcasted_iota(jnp.int32, (c, c), 1)
    causal = ci_ <= ri
    ltri = causal.astype(BF16)

    def local_group(gi, carry):
        units = [(u, h) for u in range(unroll) for h in range(GLA_HEADS)]
        idxs = [gi * unroll + u for u in range(unroll)]
        rows = [pl.ds(pl.multiple_of(i * c, c), c) for i in idxs]

        def qk_lanes(h):
            return slice((h // 2) * LANES, (h // 2 + 1) * LANES)

        la = [la_ref[rows[u], qk_lanes(h)] for u, h in units]
        bcum = [_dot_exact_lhs(ltri, x) for x in la]
        q_s, k_s, kdec, v = [], [], [], []
        for s, (u, h) in enumerate(units):
            bc = bcum[s]
            blast = bc[c - 1:c, :]
            q = q_ref[rows[u], qk_lanes(h)]
            k = k_ref[rows[u], qk_lanes(h)]
            hm = hmask[h % 2]
            q_s.append((q * jnp.exp(bc) * hm).astype(BF16))
            k_s.append(k * jnp.exp(-bc))
            kdec.append(k * jnp.exp(blast - bc) * hm)
            v.append(v_ref[rows[u], h * LANES:(h + 1) * LANES])
            eb_s[idxs[u], h] = jnp.exp(blast)
        att = [jnp.where(causal, _dot_nt(q_s[s], k_s[s]), 0.0) for s in range(len(units))]
        oi = [_dot(att[s], v[s]) for s in range(len(units))]
        v_t = [x.astype(F32).T.astype(BF16) for x in v]
        kv = [_dot(v_t[s], kdec[s]) for s in range(len(units))]
        for s, (u, h) in enumerate(units):
            qs_s[idxs[u], h] = q_s[s]
            oi_s[rows[u], h * LANES:(h + 1) * LANES] = oi[s]
            kv_s[idxs[u], h] = kv[s]
        return carry

    lax.fori_loop(0, nchunk // unroll, local_group, 0)

    st = [st_ref[h] for h in range(GLA_HEADS)]
    for ci in range(nchunk):
        rows = slice(ci * c, (ci + 1) * c)
        inter = [_dot_nt(qs_s[ci, h], st[h]) for h in range(GLA_HEADS)]
        for h in range(GLA_HEADS):
            lanes = slice(h * LANES, (h + 1) * LANES)
            oi_s[rows, lanes] = oi_s[rows, lanes] + inter[h]
            st[h] = st[h] * eb_s[ci, h] + kv_s[ci, h]
    for h in range(GLA_HEADS):
        st_ref[h] = st[h]

    ng = ng_ref[...]
    for h in range(GLA_HEADS):
        lanes = slice(h * LANES, (h + 1) * LANES)
        o = oi_s[:, lanes]
        o = o * lax.rsqrt(jnp.mean(o * o, axis=-1, keepdims=True) + GLA_NORM_EPS) * ng
        o_ref[:, lanes] = (o * sg_ref[:, lanes]).astype(o_ref.dtype)


def _gla_call(q, k, v, sg, la, ng, bsz, seq):
    tb = GLA_TILE
    nt = seq // tb
    n_tok = bsz * seq
    nchunk = tb // CHUNK

    def tok(n):
        return pl.BlockSpec((tb, n), lambda bb, t: (bb * nt + t, 0))

    def outer(q_hbm, k_hbm, v_hbm, sg_hbm, la_hbm, ng_ref, o_hbm, *scratch):
        def body(q_ref, k_ref, v_ref, sg_ref, la_ref, o_ref):
            _gla_kernel(q_ref, k_ref, v_ref, sg_ref, la_ref, ng_ref, o_ref, *scratch,
                        tb=tb, unroll=GLA_UNROLL)

        pltpu.emit_pipeline(
            body, grid=(bsz, nt),
            in_specs=[tok(GLA_KW), tok(GLA_KW), tok(GLA_VW), tok(GLA_VW), tok(GLA_KW)],
            out_specs=[tok(GLA_VW)],
        )(q_hbm, k_hbm, v_hbm, sg_hbm, la_hbm, o_hbm)

    hbm = pl.BlockSpec(memory_space=pl.ANY)
    return pl.pallas_call(
        outer,
        in_specs=[hbm] * 5 + [pl.BlockSpec(memory_space=pltpu.VMEM)],
        out_specs=hbm,
        out_shape=jax.ShapeDtypeStruct((n_tok, GLA_VW), BF16),
        scratch_shapes=[pltpu.VMEM((GLA_HEADS, GLA_DV, LANES), F32),
                        pltpu.VMEM((nchunk, GLA_HEADS, CHUNK, LANES), BF16),
                        pltpu.VMEM((tb, GLA_VW), F32),
                        pltpu.VMEM((nchunk, GLA_HEADS, GLA_DV, LANES), F32),
                        pltpu.VMEM((nchunk, GLA_HEADS, 1, LANES), F32)],
        compiler_params=pltpu.CompilerParams(vmem_limit_bytes=VMEM_LIMIT),
        name="gla",
    )(q, k, v, sg, la, ng)


def _tail_kernel(x_ref, orw_ref, ogla_ref, sgr_ref, sgg_ref, gate1_ref, shift2_ref, scale2_ref,
                 gate2_ref, wrb_ref, wgb_ref, wmix_ref, win_ref, wout_ref, ln1g_ref, ln1b_ref,
                 ln2g_ref, ln2b_ref, o_ref, *, d_ff):
    hrows = x_ref.shape[0] // TAIL_SPLIT
    rs = [slice(i * hrows, (i + 1) * hrows) for i in range(TAIL_SPLIT)]
    nsp = range(TAIL_SPLIT)
    b_rw = [jnp.dot(orw_ref[s, :], wrb_ref[...], preferred_element_type=F32) for s in rs]
    b_gla = [jnp.dot(ogla_ref[s, :], wgb_ref[...], preferred_element_type=F32) for s in rs]
    mix = [_dot(_sigmoid(sgr_ref[rs[i], :]) * b_rw[i] + _sigmoid(sgg_ref[rs[i], :]) * b_gla[i],
                wmix_ref[...]) for i in nsp]
    x1 = [_layer_norm(ALPHA * x_ref[rs[i], :] + gate1_ref[...] * mix[i], LN_EPS) * ln1g_ref[...]
          + ln1b_ref[...] for i in nsp]
    u = [(x1[i] * (1.0 + scale2_ref[...]) + shift2_ref[...]).astype(BF16) for i in nsp]
    ffn = [jnp.zeros_like(x1[i]) for i in nsp]
    ntile = d_ff // MXU_K
    bounds = [MXU_K * ((ntile * j) // FFN_CHUNKS) for j in range(FFN_CHUNKS)] + [d_ff]
    for lo, hi in zip(bounds[:-1], bounds[1:]):
        hg, hu = [], []
        for i in nsp:
            hg.append(jnp.dot(u[i], win_ref[:, lo:hi], preferred_element_type=F32))
            hu.append(jnp.dot(u[i], win_ref[:, d_ff + lo:d_ff + hi], preferred_element_type=F32))
        act = [hg[i] * _sigmoid(hg[i]) * hu[i] for i in nsp]
        ffn = [ffn[i] + _dot(act[i], wout_ref[lo:hi, :]) for i in nsp]
    for i in nsp:
        y = _layer_norm(ALPHA * x1[i] + gate2_ref[...] * ffn[i], LN_EPS)
        o_ref[rs[i], :] = y * ln2g_ref[...] + ln2b_ref[...]


def _tail_call(x2, o_rw, o_gla, sgr, sgg, mod3, wrb, wgb, wmix, win, wout, ln1g, ln1b, ln2g, ln2b,
               bsz, seq, d_model):
    tm = TAIL_TILE
    nt = seq // tm
    n_tok = bsz * seq
    d_ff = wout.shape[0]
    assert d_ff % MXU_K == 0

    def tok(n):
        return pl.BlockSpec((tm, n), lambda b, t: (b * nt + t, 0))

    def modspec(idx):
        return pl.BlockSpec((1, 1, d_model), lambda b, t: (b, 0, idx))

    def tok3(n):
        return pl.BlockSpec((tm, n), lambda b, t: (b * nt + t, 0), pipeline_mode=pl.Buffered(3))

    def outer(x_hbm, orw_hbm, ogla_hbm, sgr_hbm, sgg_hbm, mod_hbm, wrb_ref, wgb_ref, wmix_ref,
              win_ref, wout_ref, ln1g_ref, ln1b_ref, ln2g_ref, ln2b_ref, o_hbm):
        def body(x_ref, orw_ref, ogla_ref, sgr_ref, sgg_ref, g1, sh2, sc2, g2, o_ref):
            _tail_kernel(x_ref, orw_ref, ogla_ref, sgr_ref, sgg_ref,
                         g1.at[0], sh2.at[0], sc2.at[0], g2.at[0],
                         wrb_ref, wgb_ref, wmix_ref, win_ref, wout_ref,
                         ln1g_ref, ln1b_ref, ln2g_ref, ln2b_ref, o_ref, d_ff=d_ff)

        pltpu.emit_pipeline(
            body, grid=(bsz, nt),
            in_specs=[tok3(d_model), tok3(RW_WIDTH), tok3(GLA_VW), tok3(d_model), tok3(d_model),
                      modspec(2), modspec(3), modspec(4), modspec(5)],
            out_specs=[tok(d_model)],
        )(x_hbm, orw_hbm, ogla_hbm, sgr_hbm, sgg_hbm, mod_hbm, mod_hbm, mod_hbm, mod_hbm, o_hbm)

    hbm = pl.BlockSpec(memory_space=pl.ANY)
    vmem = pl.BlockSpec(memory_space=pltpu.VMEM)
    return pl.pallas_call(
        outer,
        in_specs=[hbm] * 6 + [vmem] * 9,
        out_specs=hbm,
        out_shape=jax.ShapeDtypeStruct((n_tok, d_model), F32),
        compiler_params=pltpu.CompilerParams(vmem_limit_bytes=VMEM_LIMIT),
        name="tail",
    )(x2, o_rw, o_gla, sgr, sgg, mod3, wrb, wgb, wmix, win, wout, ln1g, ln1b, ln2g, ln2b)


def kernel(x, c, w_ada, b_ada, w_in, mu_rw, rw_w0, rw_w2, rw_a0, rw_a2, rw_g2, rw_k_k, rw_k_a,
           rw_r_k, rw_gn_g, rw_gn_b, gla_a2, gla_a_b, gla_norm_g, w_rw_branch, w_gla_branch,
           w_mix_out, ln1_g, ln1_b, w_ffn_in, w_ffn_out, ln2_g, ln2_b):
    bsz, seq, d_model = x.shape
    assert w_ada.shape[0] == DEPTH
    assert all(seq % tile == 0 for tile in (INPROJ_TILE, RWKV_TILE, GLA_TILE, TAIL_TILE))
    n_tok = bsz * seq
    l = 0

    wp = _wprep_call(jnp.swapaxes(w_in[l], 0, 1), d_model)
    zeros_lora = jnp.zeros((DECAY_LORA, RW_WIDTH), F32)
    w2p = jnp.concatenate([rw_w2[l], zeros_lora], axis=0).astype(BF16)
    a2p = jnp.concatenate([zeros_lora, rw_a2[l]], axis=0).astype(BF16)
    ga2p = jnp.concatenate(
        [gla_a2[l], jnp.zeros((LANES - GLA_GATE_LORA, GLA_KW), F32)], axis=0).astype(BF16)
    row = lambda a: a.reshape(1, -1)
    hid = jnp.arange(LANES) // RW_HEAD
    bd = (hid[:, None] == hid[None, :]).astype(BF16)
    small = (row(mu_rw[l]), row(rw_w0[l]), w2p, row(rw_a0[l]), a2p, rw_g2[l].astype(BF16),
             row(rw_k_k[l]), row(rw_k_a[l]), row(rw_r_k[l]), ga2p, row(gla_a_b[l]), bd)

    x2 = x.reshape(n_tok, d_model)
    mod = _mod_call(c, w_ada[l], b_ada[l])
    mod3 = mod.reshape(bsz, 1, 6 * d_model)

    (r, k, v, lw, nkk, b, g, bonus, gq, gk, gv, gsg, gla, sgr, sgg) = _inproj_call(
        x2, mod3, wp, small, bsz, seq, d_model)

    o_rw, (wrb, wgb, wmix, win, wout) = _rwkv_call(
        r, k, v, lw, nkk, b, g, bonus, row(rw_gn_g[l]), row(rw_gn_b[l]), bd,
        (w_rw_branch[l], w_gla_branch[l], w_mix_out[l], w_ffn_in[l], w_ffn_out[l]), bsz, seq)
    o_gla = _gla_call(gq, gk, gv, gsg, gla, row(gla_norm_g[l]), bsz, seq)

    out = _tail_call(x2, o_rw, o_gla, sgr, sgg, mod3, wrb, wgb, wmix, win, wout,
                     row(ln1_g[l]), row(ln1_b[l]), row(ln2_g[l]), row(ln2_b[l]), bsz, seq, d_model)
    return out.reshape(bsz, seq, d_model)
```

```python
import functools

import jax
import jax.numpy as jnp
from jax import lax
from jax.experimental import pallas as pl
from jax.experimental.pallas import tpu as pltpu

F32 = jnp.float32
BF16 = jnp.bfloat16

RW_HEAD = 64
RW_HEADS = 8
RW_WIDTH = RW_HEADS * RW_HEAD
DECAY_LORA = 64
AAA_LORA = 64
GATE_LORA = 128
RW_GN_EPS = 64e-5
RW_COLS = 3 * RW_WIDTH + DECAY_LORA + AAA_LORA + GATE_LORA
GLA_HEADS = 4
GLA_DK = 64
GLA_DV = 128
GLA_KW = GLA_HEADS * GLA_DK
GLA_VW = GLA_HEADS * GLA_DV
GLA_GATE_LORA = 16
GLA_TAU = 16.0
GLA_NORM_EPS = 1e-5
GLA_MAIN = 2 * GLA_KW + 2 * GLA_VW
GLA_COLS = GLA_MAIN + GLA_GATE_LORA
LN_EPS = 1e-5
DEPTH = 1
ALPHA = (2.0 * DEPTH) ** 0.25

LANES = 128
MXU_K = 256
CHUNK = 64
MOD_TILE = 1536
WPREP_COLS = 256
INPROJ_TILE = 512
RWKV_UNROLL = 1
RW_GROUP = 2
GLA_UNROLL = 8
RWKV_TILE = 256
RWKV_BATCH = 4
GLA_TILE = 2048
TAIL_TILE = 512
TAIL_SPLIT = 2
FFN_CHUNKS = 2
VMEM_LIMIT = 58 * 1024 * 1024

COL_RW = 0
COL_GLA = RW_COLS
COL_GATE = COL_GLA + GLA_MAIN


def _sigmoid(x):
    return 1.0 / (1.0 + jnp.exp(-x))


def _softplus(x):
    return jnp.maximum(x, 0.0) + jnp.log1p(jnp.exp(-jnp.abs(x)))


def _dot(a, b):
    return jnp.dot(a.astype(BF16), b.astype(BF16), preferred_element_type=F32)


def _dot_nt(a, b):
    return lax.dot_general(a.astype(BF16), b.astype(BF16), (((1,), (1,)), ((), ())),
                           preferred_element_type=F32)


def _split_hi_lo(x):
    hi = x.astype(BF16)
    lo = (x - hi.astype(F32)).astype(BF16)
    return hi, lo


def _dot_exact_rhs(x, w01):
    hi, lo = _split_hi_lo(x)
    if 2 * x.shape[1] <= MXU_K:
        return jnp.dot(jnp.concatenate([hi, lo], axis=1), jnp.concatenate([w01, w01], axis=0),
                       preferred_element_type=F32)
    return (jnp.dot(hi, w01, preferred_element_type=F32)
            + jnp.dot(lo, w01, preferred_element_type=F32))


def _dot_exact_lhs(w01, x):
    hi, lo = _split_hi_lo(x)
    if 2 * x.shape[0] <= MXU_K:
        return jnp.dot(jnp.concatenate([w01, w01], axis=1), jnp.concatenate([hi, lo], axis=0),
                       preferred_element_type=F32)
    return (jnp.dot(w01, hi, preferred_element_type=F32)
            + jnp.dot(w01, lo, preferred_element_type=F32))


def _cumsum_rows(x):
    row = lax.broadcasted_iota(jnp.int32, x.shape, 0)
    shift = 1
    while shift < x.shape[0]:
        x = x + jnp.where(row >= shift, pltpu.roll(x, shift, 0), 0.0)
        shift *= 2
    return x


def _layer_norm(x, eps):
    mu = jnp.mean(x, axis=-1, keepdims=True)
    xc = x - mu
    var = jnp.mean(xc * xc, axis=-1, keepdims=True)
    return xc * lax.rsqrt(var + eps)


def _mod_kernel(c_ref, w_ref, b_ref, o_ref):
    c = c_ref[...]
    sc = c * _sigmoid(c)
    o_ref[...] = _dot(sc, w_ref[...]) + b_ref[...]


def _mod_call(c, w_ada, b_ada):
    bsz, d = c.shape
    n = w_ada.shape[1]
    tn = MOD_TILE
    assert n % tn == 0
    return pl.pallas_call(
        _mod_kernel,
        grid=(n // tn,),
        in_specs=[pl.BlockSpec((bsz, d), lambda j: (0, 0)),
                  pl.BlockSpec((d, tn), lambda j: (0, j)),
                  pl.BlockSpec((1, tn), lambda j: (0, j))],
        out_specs=pl.BlockSpec((bsz, tn), lambda j: (0, j)),
        out_shape=jax.ShapeDtypeStruct((bsz, n), F32),
        compiler_params=pltpu.CompilerParams(dimension_semantics=("arbitrary",),
                                             vmem_limit_bytes=VMEM_LIMIT),
        name="mod",
    )(c, w_ada, b_ada.reshape(1, n))


def _wprep_kernel(w_ref, o_ref, *, d_model):
    main = RW_COLS + GLA_MAIN
    o_ref[:main, :] = w_ref[:main, :].astype(BF16)
    o_ref[main:main + 2 * d_model, :] = w_ref[RW_COLS + GLA_COLS:, :].astype(BF16)
    o_ref[main + 2 * d_model:main + 2 * d_model + GLA_GATE_LORA, :] = (
        w_ref[main:RW_COLS + GLA_COLS, :].astype(BF16))
    o_ref[main + 2 * d_model + GLA_GATE_LORA:, :] = jnp.zeros(
        (LANES - GLA_GATE_LORA, o_ref.shape[1]), BF16)


def _wprep_call(w_in_t, d_model):
    n_in = w_in_t.shape[0]
    n_out = RW_COLS + GLA_MAIN + 2 * d_model + LANES
    cols = WPREP_COLS
    assert d_model % cols == 0
    return pl.pallas_call(
        functools.partial(_wprep_kernel, d_model=d_model),
        grid=(d_model // cols,),
        in_specs=[pl.BlockSpec((n_in, cols), lambda i: (0, i))],
        out_specs=pl.BlockSpec((n_out, cols), lambda i: (0, i)),
        out_shape=jax.ShapeDtypeStruct((n_out, d_model), BF16),
        compiler_params=pltpu.CompilerParams(dimension_semantics=("arbitrary",),
                                             vmem_limit_bytes=VMEM_LIMIT),
        name="wprep",
    )(w_in_t)


def _inproj_kernel(x_ref, shift_ref, scale_ref, wp_ref, mu_ref, w0_ref, w2_ref, a0_ref, a2_ref,
                   g2_ref, kk_ref, ka_ref, rk_ref, ga2_ref, gab_ref, bd_ref,
                   r_out, k_out, v_out, lw_out, nkk_out, b_out, g_out, bonus_out,
                   gq_out, gk_out, gv_out, gsg_out, gla_out, sgr_out, sgg_out,
                   carry_ref, *, tm, d_model):
    t = pl.program_id(1)

    @pl.when(t == 0)
    def _():
        carry_ref[...] = jnp.zeros_like(carry_ref)

    u = (x_ref[...] * (1.0 + scale_ref[...]) + shift_ref[...]).astype(BF16)
    p = _dot_nt(u, wp_ref[COL_RW:COL_RW + RW_COLS, :])
    pg = _dot_nt(u, wp_ref[COL_GLA:COL_GLA + GLA_MAIN, :])

    prev = pltpu.roll(p, 1, 0)
    row = lax.broadcasted_iota(jnp.int32, p.shape, 0)
    prev = jnp.where(row == 0, carry_ref[...], prev)
    carry_ref[...] = p[tm - 1:tm, :]
    ps = p + mu_ref[...] * (prev - p)

    r = ps[:, 0:RW_WIDTH]
    k = ps[:, RW_WIDTH:2 * RW_WIDTH]
    v = ps[:, 2 * RW_WIDTH:3 * RW_WIDTH]
    z = ps[:, 3 * RW_WIDTH:3 * RW_WIDTH + LANES]
    gd = ps[:, 3 * RW_WIDTH + LANES:RW_COLS]

    w_lora = _dot(jnp.tanh(z), w2_ref[...])
    a_lora = _dot(z, a2_ref[...])
    g_out[...] = _dot(_sigmoid(gd), g2_ref[...])
    gates = _dot_nt(u, wp_ref[COL_GATE:, :])

    w = -_softplus(-(w0_ref[...] + w_lora)) - 0.5
    lw_out[...] = -jnp.exp(w)
    a = _sigmoid(a0_ref[...] + a_lora)
    kkv = k * kk_ref[...]
    kp = k * (1.0 + (a - 1.0) * ka_ref[...])
    rkr = r * kp * rk_ref[...]
    bd = bd_ref[...]
    for j in range(RW_WIDTH // LANES):
        sl = slice(j * LANES, (j + 1) * LANES)
        kkj = kkv[:, sl]
        ssq = _dot_exact_rhs(kkj * kkj, bd)
        kkn = kkj / jnp.maximum(jnp.sqrt(ssq), 1e-12)
        nkk_out[:, sl] = -kkn
        b_out[:, sl] = kkn * a[:, sl]
        bonus_out[:, sl] = _dot_exact_rhs(rkr[:, sl], bd) * v[:, sl]
    r_out[...] = r
    k_out[...] = kp
    v_out[...] = v.astype(BF16)

    gq_out[...] = pg[:, 0:GLA_KW] * (GLA_DK ** -0.5)
    gk_out[...] = pg[:, GLA_KW:2 * GLA_KW]
    gv_out[...] = pg[:, 2 * GLA_KW:2 * GLA_KW + GLA_VW].astype(BF16)
    gg = pg[:, 2 * GLA_KW + GLA_VW:GLA_MAIN]
    gsg_out[...] = gg * _sigmoid(gg)
    adg = gates[:, 2 * d_model:2 * d_model + LANES]
    la = _dot(adg, ga2_ref[...]) + gab_ref[...]
    gla_out[...] = -_softplus(-la) / GLA_TAU

    sgr_out[...] = gates[:, :d_model]
    sgg_out[...] = gates[:, d_model:2 * d_model]


def _inproj_call(x2, mod3, wp, small, bsz, seq, d_model):
    tm = INPROJ_TILE
    nt = seq // tm
    n_tok = bsz * seq
    def tok(n):
        return pl.BlockSpec((tm, n), lambda b, t: (b * nt + t, 0))

    def modspec(idx):
        return pl.BlockSpec((1, 1, d_model), lambda b, t: (b, 0, idx))

    outs = [
        (RW_WIDTH, F32), (RW_WIDTH, F32), (RW_WIDTH, BF16), (RW_WIDTH, F32), (RW_WIDTH, F32),
        (RW_WIDTH, F32), (RW_WIDTH, F32), (RW_WIDTH, F32),
        (GLA_KW, F32), (GLA_KW, F32), (GLA_VW, BF16), (GLA_VW, F32), (GLA_KW, F32),
        (d_model, F32), (d_model, F32),
    ]
    nres = 1 + len(small)

    def outer(x_hbm, mod_hbm, *refs):
        resident = refs[:nres]
        out_hbm = refs[nres:nres + len(outs)]
        carry_ref = refs[nres + len(outs)]

        def body(x_ref, sh, sc, *out_refs):
            _inproj_kernel(x_ref, sh.at[0], sc.at[0], *resident, *out_refs, carry_ref,
                           tm=tm, d_model=d_model)

        pltpu.emit_pipeline(
            body, grid=(bsz, nt),
            in_specs=[tok(d_model), modspec(0), modspec(1)],
            out_specs=[tok(n) for n, _ in outs],
        )(x_hbm, mod_hbm, mod_hbm, *out_hbm)

    hbm = pl.BlockSpec(memory_space=pl.ANY)
    vmem = pl.BlockSpec(memory_space=pltpu.VMEM)
    return pl.pallas_call(
        outer,
        in_specs=[hbm, hbm] + [vmem] * nres,
        out_specs=[hbm] * len(outs),
        out_shape=[jax.ShapeDtypeStruct((n_tok, n), dt) for n, dt in outs],
        scratch_shapes=[pltpu.VMEM((1, RW_COLS), F32)],
        compiler_params=pltpu.CompilerParams(vmem_limit_bytes=VMEM_LIMIT),
        name="inproj",
    )(x2, mod3, wp, *small)


def _rwkv_kernel(*refs, nb, tb, unroll, ncast):
    (r_ref, k_ref, v_ref, lw_ref, nkk_ref, b_ref, g_ref, bonus_ref, gng_ref, gnb_ref,
     bd_ref) = refs[:11]
    cast_in = refs[11:11 + ncast]
    o_ref = refs[11 + ncast]
    cast_out = refs[12 + ncast:12 + 2 * ncast]
    h_ref, rm_s, yl_s, hl_s = refs[12 + 2 * ncast:]
    for w_in_ref, w_out_ref in zip(cast_in, cast_out):
        w_out_ref[...] = w_in_ref[...].astype(BF16)

    c = CHUNK
    gw = RW_GROUP * RW_HEAD
    nchunk = tb // c
    ngroup = RW_WIDTH // gw
    t = pl.program_id(1)

    @pl.when(t == 0)
    def _():
        h_ref[...] = jnp.zeros_like(h_ref)

    lane = lax.broadcasted_iota(jnp.int32, (c, gw), 1)
    rowi = lax.broadcasted_iota(jnp.int32, (c, gw), 0)
    head_of_lane = lane // RW_HEAD
    lmask = [head_of_lane == h for h in range(RW_GROUP)]
    strict = (lane % RW_HEAD) < rowi
    incl = (lane % RW_HEAD) <= rowi
    eye = (lane % RW_HEAD) == rowi
    eye_f = eye.astype(F32)
    assert c == RW_HEAD

    def stack(x):
        xb = x.astype(BF16)
        zero = jnp.zeros_like(xb)
        return jnp.concatenate([jnp.where(lmask[h], xb, zero) for h in range(RW_GROUP)], axis=0)

    def block_t(x):
        xt = x.T
        return jnp.concatenate([xt[h * RW_HEAD:(h + 1) * RW_HEAD, :] for h in range(RW_GROUP)],
                               axis=1)

    def local_group(gi, carry):
        units = [(e, u, g) for e in range(nb) for u in range(unroll) for g in range(ngroup)]
        n = len(units)
        idxs = [gi * unroll + u for u in range(unroll)]
        rows = [pl.ds(pl.multiple_of(i * c, c), c) for i in idxs]

        def ld(ref, e, u, g):
            return ref[e, rows[u], g * gw:(g + 1) * gw]

        lw = [ld(lw_ref, e, u, g) for e, u, g in units]
        gcum = [_cumsum_rows(x) for x in lw]
        gend, rt, at, kt, bt, bh, kh, vst = [], [], [], [], [], [], [], []
        for s, (e, u, g) in enumerate(units):
            g_ = gcum[s]
            ge = g_[c - 1:c, :]
            r = ld(r_ref, e, u, g)
            k = ld(k_ref, e, u, g)
            nkk = ld(nkk_ref, e, u, g)
            b = ld(b_ref, e, u, g)
            e_neg = jnp.exp(-g_)
            dk = jnp.exp(ge - g_)
            gend.append(ge)
            rt.append(r * jnp.exp(g_))
            at.append(nkk * jnp.exp(g_ - lw[s]))
            kt.append(k * e_neg)
            bt.append(b * e_neg)
            bh.append(b * dk)
            kh.append(k * dk)
            vst.append(stack(ld(v_ref, e, u, g)))

        x = [_dot_nt(jnp.concatenate([at[s], rt[s]], axis=0),
                     jnp.concatenate([stack(bt[s]), stack(kt[s])], axis=0)) for s in range(n)]
        a_ab = [jnp.where(strict, y[:c, :gw], 0.0) for y in x]
        a_ak = [jnp.where(strict, y[:c, gw:], 0.0) for y in x]
        a_rb = [jnp.where(incl, y[c:, :gw], 0.0) for y in x]
        a_rk = [jnp.where(incl, y[c:, gw:], 0.0) for y in x]
        bht = [block_t(y) for y in bh]
        kht = [block_t(y) for y in kh]

        tp = [eye_f + a for a in a_ab]
        pw = [_dot(a, stack(a)) for a in a_ab]
        for _ in range(4):
            res = [_dot(jnp.concatenate([pw[s], tp[s]], axis=0), stack(pw[s])) for s in range(n)]
            tp = [tp[s] + res[s][c:] for s in range(n)]
            pw = [res[s][:c] for s in range(n)]
        tinv = [tp[s] + _dot(tp[s], stack(pw[s])) for s in range(n)]

        vres = [_dot(jnp.concatenate([a_ak[s], a_rk[s], kht[s]], axis=0), vst[s]) for s in range(n)]
        akv = [y[:c] for y in vres]
        rk_v = [y[c:2 * c] for y in vres]
        h2 = [y[2 * c:] for y in vres]
        tres = [_dot(tinv[s], jnp.concatenate([stack(at[s]), stack(akv[s])], axis=1))
                for s in range(n)]
        ap = [y[:, :gw] for y in tres]
        uloc = [y[:, gw:] for y in tres]
        fres = [_dot(jnp.concatenate([a_rb[s], bht[s]], axis=0),
                     jnp.concatenate([stack(ap[s]), stack(uloc[s])], axis=1)) for s in range(n)]
        for s, (e, u, g) in enumerate(units):
            y = fres[s]
            rm_s[e, idxs[u], g, :c, :] = rt[s] + y[:c, :gw]
            rm_s[e, idxs[u], g, c:, :] = jnp.where(eye, jnp.exp(gend[s]), 0.0) + y[c:, :gw]
            yl_s[e, rows[u], g * gw:(g + 1) * gw] = y[:c, gw:] + rk_v[s]
            hl_s[e, idxs[u], g] = y[c:, gw:] + h2[s]
        return carry

    lax.fori_loop(0, nchunk // unroll, local_group, 0)

    chains = [(e, g) for e in range(nb) for g in range(ngroup)]
    hstate = [h_ref[e, g] for e, g in chains]
    for ci in range(nchunk):
        rows = slice(ci * c, (ci + 1) * c)
        res = [_dot(rm_s[e, ci, g], stack(hstate[j])) for j, (e, g) in enumerate(chains)]
        for j, (e, g) in enumerate(chains):
            lanes = slice(g * gw, (g + 1) * gw)
            yl_s[e, rows, lanes] = res[j][:c] + yl_s[e, rows, lanes]
            hstate[j] = res[j][c:] + hl_s[e, ci, g]
    for j, (e, g) in enumerate(chains):
        h_ref[e, g] = hstate[j]

    bd_mean = bd_ref[...] * (1.0 / RW_HEAD)
    for p in range(RW_WIDTH // LANES):
        lanes = slice(p * LANES, (p + 1) * LANES)
        y = yl_s[:, :, lanes].reshape(nb * tb, LANES)
        mean = _dot_exact_rhs(y, bd_mean)
        yc = y - mean
        var = _dot_exact_rhs(yc * yc, bd_mean)
        yn = yc * lax.rsqrt(var + RW_GN_EPS)
        bonus = bonus_ref[:, :, lanes].reshape(nb * tb, LANES)
        gate = g_ref[:, :, lanes].reshape(nb * tb, LANES)
        out = (yn * gng_ref[:, lanes] + gnb_ref[:, lanes] + bonus) * gate
        o_ref[:, :, lanes] = out.reshape(nb, tb, LANES).astype(o_ref.dtype)


def _rwkv_call(r, k, v, lw, nkk, b, g, bonus, gng, gnb, bd, weights, bsz, seq):
    tb = RWKV_TILE
    nb = RWKV_BATCH
    nt = seq // tb
    gw = RW_GROUP * RW_HEAD
    ngroup = RW_WIDTH // gw
    nchunk = tb // CHUNK
    nsteps = (bsz // nb) * nt
    assert bsz % nb == 0
    assert all(w.shape[0] % (16 * nsteps) == 0 for w in weights)

    def tok():
        return pl.BlockSpec((nb, tb, RW_WIDTH), lambda bb, t: (bb, t, 0))

    def full(a):
        return pl.BlockSpec(a.shape, lambda bb, t: (0,) * a.ndim)

    def wblock(w):
        return pl.BlockSpec((w.shape[0] // nsteps, w.shape[1]), lambda bb, t: (bb * nt + t, 0))

    per_seq = lambda a: a.reshape(bsz, seq, RW_WIDTH)
    out, *cast = pl.pallas_call(
        functools.partial(_rwkv_kernel, nb=nb, tb=tb, unroll=RWKV_UNROLL, ncast=len(weights)),
        grid=(bsz // nb, nt),
        in_specs=([tok() for _ in range(8)] + [full(gng), full(gnb), full(bd)]
                  + [wblock(w) for w in weights]),
        out_specs=[tok()] + [wblock(w) for w in weights],
        out_shape=([jax.ShapeDtypeStruct((bsz, seq, RW_WIDTH), BF16)]
                   + [jax.ShapeDtypeStruct(w.shape, BF16) for w in weights]),
        scratch_shapes=[pltpu.VMEM((nb, ngroup, RW_HEAD, gw), F32),
                        pltpu.VMEM((nb, nchunk, ngroup, CHUNK + RW_HEAD, gw), F32),
                        pltpu.VMEM((nb, tb, RW_WIDTH), F32),
                        pltpu.VMEM((nb, nchunk, ngroup, RW_HEAD, gw), F32)],
        compiler_params=pltpu.CompilerParams(
            dimension_semantics=("arbitrary", "arbitrary"),
            vmem_limit_bytes=VMEM_LIMIT),
        name="rwkv",
    )(*[per_seq(a) for a in (r, k, v, lw, nkk, b, g, bonus)], gng, gnb, bd, *weights)
    return out.reshape(bsz * seq, RW_WIDTH), cast


def _gla_kernel(q_ref, k_ref, v_ref, sg_ref, la_ref, ng_ref, o_ref, st_ref, qs_s, oi_s, kv_s, eb_s,
                *, tb, unroll):
    c = CHUNK
    nchunk = tb // c
    t = pl.program_id(1)

    @pl.when(t == 0)
    def _():
        st_ref[...] = jnp.zeros_like(st_ref)

    lane = lax.broadcasted_iota(jnp.int32, (1, LANES), 1)
    hmask = ((lane < GLA_DK).astype(F32), (lane >= GLA_DK).astype(F32))
    ri = lax.broadcasted_iota(jnp.int32, (c, c), 0)
    ci_ = lax.broadcasted_iota(jnp.int32, (c, c), 1)
    causal = ci_ <= ri
    ltri = causal.astype(BF16)

    def local_group(gi, carry):
        units = [(u, h) for u in range(unroll) for h in range(GLA_HEADS)]
        idxs = [gi * unroll + u for u in range(unroll)]
        rows = [pl.ds(pl.multiple_of(i * c, c), c) for i in idxs]

        def qk_lanes(h):
            return slice((h // 2) * LANES, (h // 2 + 1) * LANES)

        la = [la_ref[rows[u], qk_lanes(h)] for u, h in units]
        bcum = [_dot_exact_lhs(ltri, x) for x in la]
        q_s, k_s, kdec, v = [], [], [], []
        for s, (u, h) in enumerate(units):
            bc = bcum[s]
            blast = bc[c - 1:c, :]
            q = q_ref[rows[u], qk_lanes(h)]
            k = k_ref[rows[u], qk_lanes(h)]
            hm = hmask[h % 2]
            q_s.append((q * jnp.exp(bc) * hm).astype(BF16))
            k_s.append(k * jnp.exp(-bc))
            kdec.append(k * jnp.exp(blast - bc) * hm)
            v.append(v_ref[rows[u], h * LANES:(h + 1) * LANES])
            eb_s[idxs[u], h] = jnp.exp(blast)
        att = [jnp.where(causal, _dot_nt(q_s[s], k_s[s]), 0.0) for s in range(len(units))]
        oi = [_dot(att[s], v[s]) for s in range(len(units))]
        v_t = [x.astype(F32).T.astype(BF16) for x in v]
        kv = [_dot(v_t[s], kdec[s]) for s in range(len(units))]
        for s, (u, h) in enumerate(units):
            qs_s[idxs[u], h] = q_s[s]
            oi_s[rows[u], h * LANES:(h + 1) * LANES] = oi[s]
            kv_s[idxs[u], h] = kv[s]
        return carry

    lax.fori_loop(0, nchunk // unroll, local_group, 0)

    st = [st_ref[h] for h in range(GLA_HEADS)]
    for ci in range(nchunk):
        rows = slice(ci * c, (ci + 1) * c)
        inter = [_dot_nt(qs_s[ci, h], st[h]) for h in range(GLA_HEADS)]
        for h in range(GLA_HEADS):
            lanes = slice(h * LANES, (h + 1) * LANES)
            oi_s[rows, lanes] = oi_s[rows, lanes] + inter[h]
            st[h] = st[h] * eb_s[ci, h] + kv_s[ci, h]
    for h in range(GLA_HEADS):
        st_ref[h] = st[h]

    ng = ng_ref[...]
    for h in range(GLA_HEADS):
        lanes = slice(h * LANES, (h + 1) * LANES)
        o = oi_s[:, lanes]
        o = o * lax.rsqrt(jnp.mean(o * o, axis=-1, keepdims=True) + GLA_NORM_EPS) * ng
        o_ref[:, lanes] = (o * sg_ref[:, lanes]).astype(o_ref.dtype)


def _gla_call(q, k, v, sg, la, ng, bsz, seq):
    tb = GLA_TILE
    nt = seq // tb
    n_tok = bsz * seq
    nchunk = tb // CHUNK

    def tok(n):
        return pl.BlockSpec((tb, n), lambda bb, t: (bb * nt + t, 0))

    def outer(q_hbm, k_hbm, v_hbm, sg_hbm, la_hbm, ng_ref, o_hbm, *scratch):
        def body(q_ref, k_ref, v_ref, sg_ref, la_ref, o_ref):
            _gla_kernel(q_ref, k_ref, v_ref, sg_ref, la_ref, ng_ref, o_ref, *scratch,
                        tb=tb, unroll=GLA_UNROLL)

        pltpu.emit_pipeline(
            body, grid=(bsz, nt),
            in_specs=[tok(GLA_KW), tok(GLA_KW), tok(GLA_VW), tok(GLA_VW), tok(GLA_KW)],
            out_specs=[tok(GLA_VW)],
        )(q_hbm, k_hbm, v_hbm, sg_hbm, la_hbm, o_hbm)

    hbm = pl.BlockSpec(memory_space=pl.ANY)
    return pl.pallas_call(
        outer,
        in_specs=[hbm] * 5 + [pl.BlockSpec(memory_space=pltpu.VMEM)],
        out_specs=hbm,
        out_shape=jax.ShapeDtypeStruct((n_tok, GLA_VW), BF16),
        scratch_shapes=[pltpu.VMEM((GLA_HEADS, GLA_DV, LANES), F32),
                        pltpu.VMEM((nchunk, GLA_HEADS, CHUNK, LANES), BF16),
                        pltpu.VMEM((tb, GLA_VW), F32),
                        pltpu.VMEM((nchunk, GLA_HEADS, GLA_DV, LANES), F32),
                        pltpu.VMEM((nchunk, GLA_HEADS, 1, LANES), F32)],
        compiler_params=pltpu.CompilerParams(vmem_limit_bytes=VMEM_LIMIT),
        name="gla",
    )(q, k, v, sg, la, ng)


def _tail_kernel(x_ref, orw_ref, ogla_ref, sgr_ref, sgg_ref, gate1_ref, shift2_ref, scale2_ref,
                 gate2_ref, wrb_ref, wgb_ref, wmix_ref, win_ref, wout_ref, ln1g_ref, ln1b_ref,
                 ln2g_ref, ln2b_ref, o_ref, *, d_ff):
    hrows = x_ref.shape[0] // TAIL_SPLIT
    rs = [slice(i * hrows, (i + 1) * hrows) for i in range(TAIL_SPLIT)]
    nsp = range(TAIL_SPLIT)
    b_rw = [jnp.dot(orw_ref[s, :], wrb_ref[...], preferred_element_type=F32) for s in rs]
    b_gla = [jnp.dot(ogla_ref[s, :], wgb_ref[...], preferred_element_type=F32) for s in rs]
    mix = [_dot(_sigmoid(sgr_ref[rs[i], :]) * b_rw[i] + _sigmoid(sgg_ref[rs[i], :]) * b_gla[i],
                wmix_ref[...]) for i in nsp]
    x1 = [_layer_norm(ALPHA * x_ref[rs[i], :] + gate1_ref[...] * mix[i], LN_EPS) * ln1g_ref[...]
          + ln1b_ref[...] for i in nsp]
    u = [(x1[i] * (1.0 + scale2_ref[...]) + shift2_ref[...]).astype(BF16) for i in nsp]
    ffn = [jnp.zeros_like(x1[i]) for i in nsp]
    ntile = d_ff // MXU_K
    bounds = [MXU_K * ((ntile * j) // FFN_CHUNKS) for j in range(FFN_CHUNKS)] + [d_ff]
    for lo, hi in zip(bounds[:-1], bounds[1:]):
        hg, hu = [], []
        for i in nsp:
            hg.append(jnp.dot(u[i], win_ref[:, lo:hi], preferred_element_type=F32))
            hu.append(jnp.dot(u[i], win_ref[:, d_ff + lo:d_ff + hi], preferred_element_type=F32))
        act = [hg[i] * _sigmoid(hg[i]) * hu[i] for i in nsp]
        ffn = [ffn[i] + _dot(act[i], wout_ref[lo:hi, :]) for i in nsp]
    for i in nsp:
        y = _layer_norm(ALPHA * x1[i] + gate2_ref[...] * ffn[i], LN_EPS)
        o_ref[rs[i], :] = y * ln2g_ref[...] + ln2b_ref[...]


def _tail_call(x2, o_rw, o_gla, sgr, sgg, mod3, wrb, wgb, wmix, win, wout, ln1g, ln1b, ln2g, ln2b,
               bsz, seq, d_model):
    tm = TAIL_TILE
    nt = seq // tm
    n_tok = bsz * seq
    d_ff = wout.shape[0]
    assert d_ff % MXU_K == 0

    def tok(n):
        return pl.BlockSpec((tm, n), lambda b, t: (b * nt + t, 0))

    def modspec(idx):
        return pl.BlockSpec((1, 1, d_model), lambda b, t: (b, 0, idx))

    def outer(x_hbm, orw_hbm, ogla_hbm, sgr_hbm, sgg_hbm, mod_hbm, wrb_ref, wgb_ref, wmix_ref,
              win_ref, wout_ref, ln1g_ref, ln1b_ref, ln2g_ref, ln2b_ref, o_hbm):
        def body(x_ref, orw_ref, ogla_ref, sgr_ref, sgg_ref, g1, sh2, sc2, g2, o_ref):
            _tail_kernel(x_ref, orw_ref, ogla_ref, sgr_ref, sgg_ref,
                         g1.at[0], sh2.at[0], sc2.at[0], g2.at[0],
                         wrb_ref, wgb_ref, wmix_ref, win_ref, wout_ref,
                         ln1g_ref, ln1b_ref, ln2g_ref, ln2b_ref, o_ref, d_ff=d_ff)

        pltpu.emit_pipeline(
            body, grid=(bsz, nt),
            in_specs=[tok(d_model), tok(RW_WIDTH), tok(GLA_VW), tok(d_model), tok(d_model),
                      modspec(2), modspec(3), modspec(4), modspec(5)],
            out_specs=[tok(d_model)],
        )(x_hbm, orw_hbm, ogla_hbm, sgr_hbm, sgg_hbm, mod_hbm, mod_hbm, mod_hbm, mod_hbm, o_hbm)

    hbm = pl.BlockSpec(memory_space=pl.ANY)
    vmem = pl.BlockSpec(memory_space=pltpu.VMEM)
    return pl.pallas_call(
        outer,
        in_specs=[hbm] * 6 + [vmem] * 9,
        out_specs=hbm,
        out_shape=jax.ShapeDtypeStruct((n_tok, d_model), F32),
        compiler_params=pltpu.CompilerParams(vmem_limit_bytes=VMEM_LIMIT),
        name="tail",
    )(x2, o_rw, o_gla, sgr, sgg, mod3, wrb, wgb, wmix, win, wout, ln1g, ln1b, ln2g, ln2b)


def kernel(x, c, w_ada, b_ada, w_in, mu_rw, rw_w0, rw_w2, rw_a0, rw_a2, rw_g2, rw_k_k, rw_k_a,
           rw_r_k, rw_gn_g, rw_gn_b, gla_a2, gla_a_b, gla_norm_g, w_rw_branch, w_gla_branch,
           w_mix_out, ln1_g, ln1_b, w_ffn_in, w_ffn_out, ln2_g, ln2_b):
    bsz, seq, d_model = x.shape
    assert w_ada.shape[0] == DEPTH
    assert all(seq % tile == 0 for tile in (INPROJ_TILE, RWKV_TILE, GLA_TILE, TAIL_TILE))
    n_tok = bsz * seq
    l = 0

    wp = _wprep_call(jnp.swapaxes(w_in[l], 0, 1), d_model)
    zeros_lora = jnp.zeros((DECAY_LORA, RW_WIDTH), F32)
    w2p = jnp.concatenate([rw_w2[l], zeros_lora], axis=0).astype(BF16)
    a2p = jnp.concatenate([zeros_lora, rw_a2[l]], axis=0).astype(BF16)
    ga2p = jnp.concatenate(
        [gla_a2[l], jnp.zeros((LANES - GLA_GATE_LORA, GLA_KW), F32)], axis=0).astype(BF16)
    row = lambda a: a.reshape(1, -1)
    hid = jnp.arange(LANES) // RW_HEAD
    bd = (hid[:, None] == hid[None, :]).astype(BF16)
    small = (row(mu_rw[l]), row(rw_w0[l]), w2p, row(rw_a0[l]), a2p, rw_g2[l].astype(BF16),
             row(rw_k_k[l]), row(rw_k_a[l]), row(rw_r_k[l]), ga2p, row(gla_a_b[l]), bd)

    x2 = x.reshape(n_tok, d_model)
    mod = _mod_call(c, w_ada[l], b_ada[l])
    mod3 = mod.reshape(bsz, 1, 6 * d_model)

    (r, k, v, lw, nkk, b, g, bonus, gq, gk, gv, gsg, gla, sgr, sgg) = _inproj_call(
        x2, mod3, wp, small, bsz, seq, d_model)

    o_rw, (wrb, wgb, wmix, win, wout) = _rwkv_call(
        r, k, v, lw, nkk, b, g, bonus, row(rw_gn_g[l]), row(rw_gn_b[l]), bd,
        (w_rw_branch[l], w_gla_branch[l], w_mix_out[l], w_ffn_in[l], w_ffn_out[l]), bsz, seq)
    o_gla = _gla_call(gq, gk, gv, gsg, gla, row(gla_norm_g[l]), bsz, seq)

    out = _tail_call(x2, o_rw, o_gla, sgr, sgg, mod3, wrb, wgb, wmix, win, wout,
                     row(ln1_g[l]), row(ln1_b[l]), row(ln2_g[l]), row(ln2_b[l]), bsz, seq, d_model)
    return out.reshape(bsz, seq, d_model)
```
